```python
import jax, jax.numpy as jnp
from jax import lax
import numpy as np

D_MODEL = 2048
BATCH = 4
SEQ = 4096
DEPTH = 1

GRID_W = 64
CTX_LEN = 256
HEAD_DIM = 128
ATT_HEADS = 8
ATT_KV_HEADS = 2
ATT_GROUP = ATT_HEADS // ATT_KV_HEADS
RET_HEADS = 4
RET_QK_DIM = 128
RET_V_DIM = 256
ROPE_THETA = 10000.0
Q_BLOCK = 128
RET_CHUNK = 128
N_EXPERTS = 16
EXPERT_FF = 2048
EC_FACTOR = 2
EPS = 1e-6

ATT_W = ATT_HEADS * HEAD_DIM
ATT_KV_W = ATT_KV_HEADS * HEAD_DIM
RET_QK_W = RET_HEADS * RET_QK_DIM
RET_V_W = RET_HEADS * RET_V_DIM
Q_SIZES = (ATT_W, RET_QK_W, RET_V_W, D_MODEL, D_MODEL)
KV_SIZES = (ATT_KV_W, ATT_KV_W, RET_QK_W, RET_V_W)
Q_WIDTH = ATT_W + RET_QK_W + RET_V_W + 2 * D_MODEL
KV_WIDTH = 2 * ATT_KV_W + RET_QK_W + RET_V_W
IN_WIDTH = Q_WIDTH + KV_WIDTH

kernel_name = 'hybrid_gqa_retention_ec_moe_prefix_dit'


def _points(sizes):
    return [int(p) for p in np.cumsum(sizes)[:-1]]


def rms_norm(x, g):
    xf = x.astype(jnp.float32)
    y = xf * lax.rsqrt(jnp.mean(xf * xf, axis=-1, keepdims=True) + EPS)
    return (y * g.astype(jnp.float32)).astype(x.dtype)


def modulate(x, shift, scale):
    return x * (1.0 + scale) + shift


def axial_rope_tables(n_tokens):
    rows = n_tokens // GRID_W
    r, cl = jnp.meshgrid(jnp.arange(rows), jnp.arange(GRID_W), indexing='ij')
    r = r.reshape(-1).astype(jnp.float32)
    cl = cl.reshape(-1).astype(jnp.float32)
    quarter = HEAD_DIM // 4
    inv = ROPE_THETA ** (-jnp.arange(quarter, dtype=jnp.float32) / quarter)
    ang_r = r[:, None] * inv
    ang_c = cl[:, None] * inv
    ang = jnp.concatenate([ang_r, ang_r, ang_c, ang_c], axis=-1)
    return jnp.cos(ang), jnp.sin(ang)


def apply_rope(x, cos, sin):
    x4 = x.reshape(x.shape[:-1] + (2, 2, HEAD_DIM // 4))
    rot = jnp.stack([-x4[..., 1, :], x4[..., 0, :]], axis=-2).reshape(x.shape)
    out = x.astype(jnp.float32) * cos[:, None, :] + rot.astype(jnp.float32) * sin[:, None, :]
    return out.astype(x.dtype)


def q_heads(z_q, q_norm_g, cos, sin):
    B, T, _ = z_q.shape
    q_a, q_r, g_r, gate_a, gate_r = jnp.split(z_q, _points(Q_SIZES), axis=-1)
    q_a = rms_norm(q_a.reshape(B, T, ATT_HEADS, HEAD_DIM), q_norm_g)
    q_r = q_r.reshape(B, T, RET_HEADS, RET_QK_DIM)
    if cos is not None:
        q_a = apply_rope(q_a, cos, sin)
        q_r = apply_rope(q_r, cos, sin)
    q_r = q_r.transpose(0, 2, 1, 3).astype(jnp.float32)
    return q_a, q_r, g_r, gate_a, gate_r


def kv_heads(z_kv, k_norm_g, cos, sin):
    B, T, _ = z_kv.shape
    k_a, v_a, k_r, v_r = jnp.split(z_kv, _points(KV_SIZES), axis=-1)
    k_a = rms_norm(k_a.reshape(B, T, ATT_KV_HEADS, HEAD_DIM), k_norm_g)
    v_a = v_a.reshape(B, T, ATT_KV_HEADS, HEAD_DIM)
    k_r = k_r.reshape(B, T, RET_HEADS, RET_QK_DIM) * (RET_QK_DIM ** -0.5)
    if cos is not None:
        k_a = apply_rope(k_a, cos, sin)
        k_r = apply_rope(k_r, cos, sin)
    k_r = k_r.transpose(0, 2, 1, 3).astype(jnp.float32)
    v_r = v_r.reshape(B, T, RET_HEADS, RET_V_DIM).transpose(0, 2, 1, 3).astype(jnp.float32)
    return k_a, v_a, k_r, v_r


def latent_attention(q, k, v, k_ctx, v_ctx):
    B, T, _, _ = q.shape
    nb = T // Q_BLOCK
    keys = jnp.concatenate([k_ctx, k], axis=1).transpose(0, 2, 1, 3)
    vals = jnp.concatenate([v_ctx, v], axis=1).transpose(0, 2, 1, 3)
    qb = q.reshape(B, nb, Q_BLOCK, ATT_KV_HEADS, ATT_GROUP, HEAD_DIM).transpose(1, 0, 3, 4, 2, 5)
    scale = HEAD_DIM ** -0.5

    def block(qi):
        s = jnp.einsum('bhgqd,bhkd->bhgqk', qi, keys).astype(jnp.float32) * scale
        p = jax.nn.softmax(s, axis=-1).astype(vals.dtype)
        return jnp.einsum('bhgqk,bhkd->bhgqd', p, vals)

    o = lax.map(block, qb)
    return o.transpose(1, 0, 4, 2, 3, 5).reshape(B, T, ATT_W)


def context_attention(q, k, v):
    B, L, _, _ = q.shape
    qg = q.reshape(B, L, ATT_KV_HEADS, ATT_GROUP, HEAD_DIM)
    s = jnp.einsum('blhgd,bmhd->bhglm', qg, k).astype(jnp.float32) * (HEAD_DIM ** -0.5)
    p = jax.nn.softmax(s, axis=-1).astype(v.dtype)
    return jnp.einsum('bhglm,bmhd->blhgd', p, v).reshape(B, L, ATT_W)


def retention_chunked(q, k, v, log_g, s0):
    B, H, T, dk = q.shape
    dv = v.shape[-1]
    C = RET_CHUNK
    n = T // C
    i = jnp.arange(C, dtype=jnp.float32)
    diff = i[:, None] - i[None, :]
    mask = jnp.where(diff >= 0, jnp.exp(jnp.maximum(diff, 0.0)[None] * log_g[:, None, None]), 0.0)
    qc = q.reshape(B, H, n, C, dk)
    kc = k.reshape(B, H, n, C, dk)
    vc = v.reshape(B, H, n, C, dv)
    s = jnp.einsum('bhnid,bhnjd->bhnij', qc, kc) * mask[None, :, None]
    o_inner = jnp.einsum('bhnij,bhnjv->bhniv', s, vc)
    w_k = jnp.exp((C - 1 - i)[None] * log_g[:, None])
    u = jnp.einsum('bhnjd,bhnjv->nbhdv', kc * w_k[None, :, None, :, None], vc)
    g_chunk = jnp.exp(C * log_g)[None, :, None, None]

    def step(state, u_n):
        return g_chunk * state + u_n, state

    s_final, s_prev = lax.scan(step, s0, u)
    w_q = jnp.exp((i + 1.0)[None] * log_g[:, None])
    o_cross = jnp.einsum('bhnid,nbhdv->bhniv', qc * w_q[None, :, None, :, None], s_prev)
    return (o_inner + o_cross).reshape(B, H, T, dv), s_final


def retention_final_state(k, v, log_g):
    L = k.shape[2]
    age = jnp.arange(L - 1, -1, -1, dtype=jnp.float32)
    w = jnp.exp(age[None] * log_g[:, None])
    return jnp.einsum('bhld,bhlv->bhdv', k * w[None, :, :, None], v)


def bidir_retention(q, k, v, log_f, log_b, s_f, s_b):
    o_f, _ = retention_chunked(q, k, v, log_f, s_f)
    o_b, _ = retention_chunked(jnp.flip(q, 2), jnp.flip(k, 2), jnp.flip(v, 2), log_b, s_b)
    return o_f + jnp.flip(o_b, 2)


def retention_output(o, g_r, gn_g):
    B, H, T, dv = o.shape
    mu = jnp.mean(o, axis=-1, keepdims=True)
    var = jnp.mean(jnp.square(o - mu), axis=-1, keepdims=True)
    y = ((o - mu) * lax.rsqrt(var + EPS)).transpose(0, 2, 1, 3).reshape(B, T, H * dv)
    y = y * gn_g.astype(jnp.float32) * jax.nn.silu(g_r.astype(jnp.float32))
    return y.astype(g_r.dtype)


def merge_branches(o_att, o_ret, gate_a, gate_r, w_o_att, w_o_ret, w_out):
    y = jax.nn.sigmoid(gate_a) * (o_att @ w_o_att) + jax.nn.sigmoid(gate_r) * (o_ret @ w_o_ret)
    return y @ w_out


def expert_choice_ffn(h, w_router, w_gate, w_up, w_down):
    B, T, D = h.shape
    cap = EC_FACTOR * T // N_EXPERTS
    aff = jax.nn.softmax((h @ w_router).astype(jnp.float32), axis=-1)
    top_w, top_i = lax.top_k(aff.transpose(0, 2, 1), cap)
    xg = jax.vmap(lambda hb, ib: hb[ib])(h, top_i)
    a = jnp.einsum('becd,edf->becf', xg, w_gate)
    b = jnp.einsum('becd,edf->becf', xg, w_up)
    y = jnp.einsum('becf,efd->becd', jax.nn.silu(a) * b, w_down) * top_w[..., None].astype(h.dtype)
    return jax.vmap(lambda yb, ib: jnp.zeros((T, D), h.dtype).at[ib.reshape(-1)].add(yb.reshape(-1, D)))(y, top_i)


def setup_inputs(seed: int = 0) -> dict:
    key = jax.random.key(seed)
    ks = jax.random.split(key, 24)
    f32 = jnp.float32

    def nrm(k, shape, s):
        return jax.random.normal(k, shape, f32) * s

    def gain(k, shape):
        return 1.0 + 0.01 * jax.random.normal(k, shape, f32)

    gamma0 = 1.0 - 2.0 ** (-5.0 - np.arange(RET_HEADS))
    p0 = np.log(np.expm1(-np.log(gamma0))).astype(np.float32)
    return {
        'x': nrm(ks[0], (BATCH, SEQ, D_MODEL), 1.0),
        'c': nrm(ks[1], (BATCH, D_MODEL), 1.0),
        'ctx': nrm(ks[2], (BATCH, CTX_LEN, D_MODEL), 1.0),
        'c_ctx': nrm(ks[3], (D_MODEL,), 1.0),
        'w_mod': nrm(ks[4], (DEPTH, D_MODEL, 6 * D_MODEL), 0.5 * D_MODEL ** -0.5),
        'b_mod': nrm(ks[5], (DEPTH, 6 * D_MODEL), 0.01),
        'pre_norm1': gain(ks[6], (DEPTH, D_MODEL)),
        'post_norm1': gain(ks[7], (DEPTH, D_MODEL)),
        'pre_norm2': gain(ks[8], (DEPTH, D_MODEL)),
        'post_norm2': gain(ks[9], (DEPTH, D_MODEL)),
        'w_in': nrm(ks[10], (DEPTH, D_MODEL, IN_WIDTH), D_MODEL ** -0.5),
        'q_norm': gain(ks[11], (DEPTH, HEAD_DIM)),
        'k_norm': gain(ks[12], (DEPTH, HEAD_DIM)),
        'ret_decay': jnp.asarray(p0)[None, None, :] + nrm(ks[13], (DEPTH, 2, RET_HEADS), 0.05),
        'ret_gn': gain(ks[14], (DEPTH, RET_V_W)),
        'w_o_att': nrm(ks[15], (DEPTH, ATT_W, D_MODEL), ATT_W ** -0.5),
        'w_o_ret': nrm(ks[16], (DEPTH, RET_V_W, D_MODEL), RET_V_W ** -0.5),
        'w_out': nrm(ks[17], (DEPTH, D_MODEL, D_MODEL), D_MODEL ** -0.5),
        'w_router': nrm(ks[18], (DEPTH, D_MODEL, N_EXPERTS), D_MODEL ** -0.5),
        'w_gate': nrm(ks[19], (DEPTH, N_EXPERTS, D_MODEL, EXPERT_FF), D_MODEL ** -0.5),
        'w_up': nrm(ks[20], (DEPTH, N_EXPERTS, D_MODEL, EXPERT_FF), D_MODEL ** -0.5),
        'w_down': nrm(ks[21], (DEPTH, N_EXPERTS, EXPERT_FF, D_MODEL), EXPERT_FF ** -0.5),
    }


def reference(x, c, ctx, c_ctx, w_mod, b_mod, pre_norm1, post_norm1, pre_norm2, post_norm2,
              w_in, q_norm, k_norm, ret_decay, ret_gn, w_o_att, w_o_ret, w_out,
              w_router, w_gate, w_up, w_down):
    B, T, _ = x.shape
    cos, sin = axial_rope_tables(T)
    xc = ctx
    for layer in range(DEPTH):
        last = layer == DEPTH - 1
        mod = jax.nn.silu(c) @ w_mod[layer] + b_mod[layer]
        sh1, sc1, gt1, sh2, sc2, gt2 = [m[:, None, :] for m in jnp.split(mod, 6, axis=-1)]
        mod_c = jax.nn.silu(c_ctx) @ w_mod[layer] + b_mod[layer]
        csh1, csc1, cgt1, csh2, csc2, cgt2 = jnp.split(mod_c, 6, axis=-1)
        log_f = -jax.nn.softplus(ret_decay[layer, 0].astype(jnp.float32))
        log_b = -jax.nn.softplus(ret_decay[layer, 1].astype(jnp.float32))

        h = modulate(rms_norm(x, pre_norm1[layer]), sh1, sc1)
        hc = modulate(rms_norm(xc, pre_norm1[layer]), csh1, csc1)
        kc_a, vc_a, kc_r, vc_r = kv_heads(hc @ w_in[layer][:, Q_WIDTH:], k_norm[layer], None, None)
        sc_f = retention_final_state(kc_r, vc_r, log_f)
        sc_b = retention_final_state(jnp.flip(kc_r, 2), jnp.flip(vc_r, 2), log_b)
        z = h @ w_in[layer]
        q_a, q_r, g_r, gate_a, gate_r = q_heads(z[..., :Q_WIDTH], q_norm[layer], cos, sin)
        k_a, v_a, k_r, v_r = kv_heads(z[..., Q_WIDTH:], k_norm[layer], cos, sin)
        o_att = latent_attention(q_a, k_a, v_a, kc_a, vc_a)
        o_ret = retention_output(bidir_retention(q_r, k_r, v_r, log_f, log_b, sc_f, sc_b), g_r, ret_gn[layer])
        y = merge_branches(o_att, o_ret, gate_a, gate_r, w_o_att[layer], w_o_ret[layer], w_out[layer])
        x = x + gt1 * rms_norm(y, post_norm1[layer])
        if not last:
            qc_a, qc_r, gc_r, gatec_a, gatec_r = q_heads(hc @ w_in[layer][:, :Q_WIDTH], q_norm[layer], None, None)
            oc_att = context_attention(qc_a, kc_a, vc_a)
            zero_state = jnp.zeros_like(sc_f)
            oc_ret = retention_output(bidir_retention(qc_r, kc_r, vc_r, log_f, log_b, zero_state, zero_state),
                                      gc_r, ret_gn[layer])
            yc = merge_branches(oc_att, oc_ret, gatec_a, gatec_r, w_o_att[layer], w_o_ret[layer], w_out[layer])
            xc = xc + cgt1 * rms_norm(yc, post_norm1[layer])

        h2 = modulate(rms_norm(x, pre_norm2[layer]), sh2, sc2)
        y2 = expert_choice_ffn(h2, w_router[layer], w_gate[layer], w_up[layer], w_down[layer])
        x = x + gt2 * rms_norm(y2, post_norm2[layer])
        if not last:
            hc2 = modulate(rms_norm(xc, pre_norm2[layer]), csh2, csc2)
            yc2 = expert_choice_ffn(hc2, w_router[layer], w_gate[layer], w_up[layer], w_down[layer])
            xc = xc + cgt2 * rms_norm(yc2, post_norm2[layer])
    return x
```

```python
import functools

import jax
import jax.numpy as jnp
from jax import lax
from jax.experimental import pallas as pl
from jax.experimental.pallas import tpu as pltpu

F32 = jnp.float32
BF16 = jnp.bfloat16

GRID_W = 64
HEAD_DIM = 128
ATT_HEADS = 8
ATT_KV_HEADS = 2
ATT_GROUP = ATT_HEADS // ATT_KV_HEADS
RET_HEADS = 4
RET_QK_DIM = 128
RET_V_DIM = 256
ROPE_THETA = 10000.0
N_EXPERTS = 16
EC_FACTOR = 2
EPS = 1e-6

ATT_W = ATT_HEADS * HEAD_DIM
ATT_KV_W = ATT_KV_HEADS * HEAD_DIM
RET_QK_W = RET_HEADS * RET_QK_DIM
RET_V_W = RET_HEADS * RET_V_DIM

LANES = 128
VMEM_LIMIT = 56 * 1024 * 1024
RET_CHUNK = 256
MIN_NORMAL_F32_BITS = 0x00800000


def _blk(pref, n):
    return pref if n % pref == 0 else n


def _params(*sem):
    return pltpu.CompilerParams(dimension_semantics=sem, vmem_limit_bytes=VMEM_LIMIT)


def _sigmoid(x):
    return 1.0 / (1.0 + jnp.exp(-x))


def _mod_kernel(s_ref, w_ref, b_ref, o_ref):
    s = s_ref[...]
    s = s * _sigmoid(s)
    o_ref[...] = jnp.dot(s, w_ref[...], precision=lax.Precision.HIGHEST,
                         preferred_element_type=F32) + b_ref[...]


def _mod_vectors(cs, w_mod, b_mod):
    rows, d = cs.shape
    n = w_mod.shape[1]
    tn = _blk(1024, n)
    return pl.pallas_call(
        _mod_kernel,
        grid=(n // tn,),
        in_specs=[pl.BlockSpec((rows, d), lambda j: (0, 0)),
                  pl.BlockSpec((d, tn), lambda j: (0, j)),
                  pl.BlockSpec((1, tn), lambda j: (0, j))],
        out_specs=pl.BlockSpec((rows, tn), lambda j: (0, j)),
        out_shape=jax.ShapeDtypeStruct((rows, n), F32),
        compiler_params=_params("parallel"),
        name="mod",
    )(cs, w_mod, b_mod.reshape(1, n))


def _prenorm_kernel(x_ref, g_ref, sh_ref, sc_ref, o_ref):
    x = x_ref[0]
    y = x * lax.rsqrt(jnp.mean(x * x, axis=-1, keepdims=True) + EPS) * g_ref[...]
    o_ref[0] = (y * (1.0 + sc_ref[0]) + sh_ref[0]).astype(o_ref.dtype)


def _prenorm(x, gain, shift, scale):
    b, t, d = x.shape
    tm = _blk(512, t)
    return pl.pallas_call(
        _prenorm_kernel,
        grid=(b, t // tm),
        in_specs=[pl.BlockSpec((1, tm, d), lambda i, j: (i, j, 0)),
                  pl.BlockSpec((1, d), lambda i, j: (0, 0)),
                  pl.BlockSpec((1, 1, d), lambda i, j: (i, 0, 0)),
                  pl.BlockSpec((1, 1, d), lambda i, j: (i, 0, 0))],
        out_specs=pl.BlockSpec((1, tm, d), lambda i, j: (i, j, 0)),
        out_shape=jax.ShapeDtypeStruct((b, t, d), BF16),
        compiler_params=_params("parallel", "parallel"),
        name="prenorm",
    )(x, gain.reshape(1, d), shift, scale)


def _proj_kernel(*refs, mode, pre_scale):
    if mode == "plain":
        h_ref, w_ref, o_ref = refs
    elif mode == "rope":
        h_ref, w_ref, cos_ref, sin_ref, o_ref = refs
    else:
        h_ref, w_ref, gain_ref, cos_ref, sin_ref, o_ref = refs
    acc = jnp.dot(h_ref[...], w_ref[...].astype(BF16), preferred_element_type=F32)
    if mode == "plain":
        o_ref[...] = acc.astype(o_ref.dtype)
        return
    tm, tn = acc.shape
    cos = cos_ref[...]
    sin = sin_ref[...]
    lane = lax.broadcasted_iota(jnp.int32, (tm, HEAD_DIM), 1)
    first = (lane % (HEAD_DIM // 2)) < (HEAD_DIM // 4)
    for hh in range(tn // HEAD_DIM):
        xh = acc[:, hh * HEAD_DIM:(hh + 1) * HEAD_DIM]
        if mode == "norm_rope":
            xh = xh * lax.rsqrt(jnp.mean(xh * xh, axis=-1, keepdims=True) + EPS) * gain_ref[...]
        elif pre_scale != 1.0:
            xh = xh * pre_scale
        rot = jnp.where(first,
                        pltpu.roll(xh, HEAD_DIM - HEAD_DIM // 4, 1),
                        pltpu.roll(xh, HEAD_DIM // 4, 1))
        o_ref[:, hh * HEAD_DIM:(hh + 1) * HEAD_DIM] = (xh * cos + rot * sin).astype(o_ref.dtype)


def _proj(h, w_in, col0, width, *, mode, out_dtype, tn, rope=None, gain=None, pre_scale=1.0):
    r, d = h.shape
    tm = _blk(2048, r)
    assert col0 % tn == 0 and width % tn == 0
    c0 = col0 // tn
    in_specs = [pl.BlockSpec((tm, d), lambda i, j: (i, 0)),
                pl.BlockSpec((d, tn), lambda i, j: (0, c0 + j))]
    args = [h, w_in]
    if mode == "norm_rope":
        in_specs.append(pl.BlockSpec((1, HEAD_DIM), lambda i, j: (0, 0)))
        args.append(gain.reshape(1, HEAD_DIM))
    if mode != "plain":
        cos, sin = rope
        assert cos.shape == (r, HEAD_DIM)
        in_specs += [pl.BlockSpec((tm, HEAD_DIM), lambda i, j: (i, 0)),
                     pl.BlockSpec((tm, HEAD_DIM), lambda i, j: (i, 0))]
        args += [cos, sin]
    return pl.pallas_call(
        functools.partial(_proj_kernel, mode=mode, pre_scale=pre_scale),
        grid=(r // tm, width // tn),
        in_specs=in_specs,
        out_specs=pl.BlockSpec((tm, tn), lambda i, j: (i, j)),
        out_shape=jax.ShapeDtypeStruct((r, width), out_dtype),
        compiler_params=_params("parallel", "parallel"),
        name="proj_" + mode,
    )(*args)


def _ctx_state_kernel(dec_ref, k_ref, v_ref, sf_ref, sb_ref):
    hh = pl.program_id(1)
    lf = dec_ref[0, hh]
    lb = dec_ref[1, hh]
    k = k_ref[0].astype(F32)
    v = v_ref[0]
    n = k.shape[0]
    pos = lax.broadcasted_iota(jnp.int32, k.shape, 0).astype(F32)
    kf = (k * jnp.exp((n - 1.0 - pos) * lf)).T.astype(BF16)
    kb = (k * jnp.exp(pos * lb)).T.astype(BF16)
    sf_ref[0, 0] = jnp.dot(kf, v, preferred_element_type=F32)
    sb_ref[0, 0] = jnp.dot(kb, v, preferred_element_type=F32)


def _ctx_states(dec, k_r, v_r):
    b, n, _ = k_r.shape
    spec_s = pl.BlockSpec((1, 1, RET_QK_DIM, RET_V_DIM), lambda i, j: (i, j, 0, 0))
    shape_s = jax.ShapeDtypeStruct((b, RET_HEADS, RET_QK_DIM, RET_V_DIM), F32)
    return pl.pallas_call(
        _ctx_state_kernel,
        grid=(b, RET_HEADS),
        in_specs=[pl.BlockSpec(memory_space=pltpu.SMEM),
                  pl.BlockSpec((1, n, RET_QK_DIM), lambda i, j: (i, 0, j)),
                  pl.BlockSpec((1, n, RET_V_DIM), lambda i, j: (i, 0, j))],
        out_specs=(spec_s, spec_s),
        out_shape=(shape_s, shape_s),
        compiler_params=_params("parallel", "parallel"),
        name="ctx_state",
    )(dec, k_r, v_r)


def _attn_kernel(q_ref, kt_ref, v_ref, o_ref, *, scale):
    kt = kt_ref[0, 0]
    v = v_ref[0]
    for g in range(ATT_GROUP):
        q = q_ref[0, :, g * HEAD_DIM:(g + 1) * HEAD_DIM]
        s = jnp.dot(q, kt, preferred_element_type=F32)
        m = jnp.max(s, axis=-1, keepdims=True)
        p = jnp.exp((s - m) * scale)
        l = jnp.sum(p, axis=-1, keepdims=True)
        o = jnp.dot(p.astype(BF16), v, preferred_element_type=F32)
        o_ref[0, :, g * HEAD_DIM:(g + 1) * HEAD_DIM] = (o / l).astype(o_ref.dtype)


def _attention(q, kt, v):
    b, t, _ = q.shape
    s = kt.shape[-1]
    tq = _blk(256, t)
    gw = ATT_GROUP * HEAD_DIM
    return pl.pallas_call(
        functools.partial(_attn_kernel, scale=HEAD_DIM ** -0.5),
        grid=(b, ATT_KV_HEADS, t // tq),
        in_specs=[pl.BlockSpec((1, tq, gw), lambda i, j, n: (i, n, j)),
                  pl.BlockSpec((1, 1, HEAD_DIM, s), lambda i, j, n: (i, j, 0, 0)),
                  pl.BlockSpec((1, s, HEAD_DIM), lambda i, j, n: (i, 0, j))],
        out_specs=pl.BlockSpec((1, tq, gw), lambda i, j, n: (i, n, j)),
        out_shape=jax.ShapeDtypeStruct((b, t, ATT_W), BF16),
        compiler_params=_params("parallel", "parallel", "parallel"),
        name="attention",
    )(q, kt, v)


def _ret_kernel(dec_ref, qf_ref, kf_ref, vf_ref, qb_ref, kb_ref, vb_ref, s0f_ref, s0b_ref,
                of_ref, ob_ref, sf_ref, sb_ref, mf_ref, mb_ref):
    hh = pl.program_id(1)
    n = pl.program_id(2)
    lf = dec_ref[0, hh]
    lb = dec_ref[1, hh]
    c = qf_ref.shape[1]

    @pl.when(n == 0)
    def _():
        sf_ref[...] = s0f_ref[0, 0]
        sb_ref[...] = s0b_ref[0, 0]
        ri = lax.broadcasted_iota(jnp.int32, (c, c), 0)
        ci = lax.broadcasted_iota(jnp.int32, (c, c), 1)
        d = (ri - ci).astype(F32)
        mf_ref[...] = jnp.where(d >= 0, jnp.exp(jnp.maximum(d, 0.0) * lf), 0.0)
        mb_ref[...] = jnp.where(d <= 0, jnp.exp(jnp.maximum(-d, 0.0) * lb), 0.0)

    nt = (((1,), (1,)), ((), ()))
    row_v = lax.broadcasted_iota(jnp.int32, (c, RET_V_DIM), 0).astype(F32)
    row_k = lax.broadcasted_iota(jnp.int32, (c, RET_QK_DIM), 0).astype(F32)

    def sweep(q_ref, k_ref, v_ref, o_ref, s_ref, m_ref, lg, wq_age, wk_age):
        q = q_ref[0]
        k = k_ref[0]
        v = v_ref[0]
        s = lax.dot_general(q, k, nt, preferred_element_type=F32) * m_ref[...]
        o_in = jnp.dot(s.astype(BF16), v, preferred_element_type=F32)
        state = s_ref[...]
        o_x = jnp.dot(q, state.astype(BF16), preferred_element_type=F32) * jnp.exp(wq_age * lg)
        o_ref[0] = o_in + o_x
        kw = (k.astype(F32) * jnp.exp(wk_age * lg)).T.astype(BF16)
        g_chunk = jnp.exp(jnp.full((1, RET_V_DIM), c * lg, F32))
        s_ref[...] = g_chunk * state + jnp.dot(kw, v, preferred_element_type=F32)

    sweep(qf_ref, kf_ref, vf_ref, of_ref, sf_ref, mf_ref, lf, row_v + 1.0, c - 1.0 - row_k)
    sweep(qb_ref, kb_ref, vb_ref, ob_ref, sb_ref, mb_ref, lb, c - row_v, row_k)


def _retention(dec, q, k, v, s0f, s0b):
    b, t, _ = q.shape
    c = _blk(RET_CHUNK, t)
    nc = t // c
    fwd = lambda i, j, n: (i, n, j)
    bwd = lambda i, j, n: (i, nc - 1 - n, j)
    spec_s = pl.BlockSpec((1, 1, RET_QK_DIM, RET_V_DIM), lambda i, j, n: (i, j, 0, 0))
    out_shape = jax.ShapeDtypeStruct((b, t, RET_V_W), F32)
    return pl.pallas_call(
        _ret_kernel,
        grid=(b, RET_HEADS, nc),
        in_specs=[pl.BlockSpec(memory_space=pltpu.SMEM),
                  pl.BlockSpec((1, c, RET_QK_DIM), fwd),
                  pl.BlockSpec((1, c, RET_QK_DIM), fwd),
                  pl.BlockSpec((1, c, RET_V_DIM), fwd),
                  pl.BlockSpec((1, c, RET_QK_DIM), bwd),
                  pl.BlockSpec((1, c, RET_QK_DIM), bwd),
                  pl.BlockSpec((1, c, RET_V_DIM), bwd),
                  spec_s, spec_s],
        out_specs=(pl.BlockSpec((1, c, RET_V_DIM), fwd), pl.BlockSpec((1, c, RET_V_DIM), bwd)),
        out_shape=(out_shape, out_shape),
        scratch_shapes=[pltpu.VMEM((RET_QK_DIM, RET_V_DIM), F32),
                        pltpu.VMEM((RET_QK_DIM, RET_V_DIM), F32),
                        pltpu.VMEM((c, c), F32),
                        pltpu.VMEM((c, c), F32)],
        compiler_params=_params("parallel", "parallel", "arbitrary"),
        name="retention",
    )(dec, q, k, v, q, k, v, s0f, s0b)


def _merge1_kernel(oa_ref, of_ref, ob_ref, g_ref, ga_ref, gr_ref, gn_ref, wa_ref, wr_ref,
                   y_ref, oret_ref):
    @pl.when(pl.program_id(1) == 0)
    def _():
        o = of_ref[...] + ob_ref[...]
        for hh in range(RET_HEADS):
            sl = slice(hh * RET_V_DIM, (hh + 1) * RET_V_DIM)
            oh = o[:, sl]
            dlt = oh - jnp.mean(oh, axis=-1, keepdims=True)
            yh = dlt * lax.rsqrt(jnp.mean(dlt * dlt, axis=-1, keepdims=True) + EPS)
            g = g_ref[:, sl]
            oret_ref[:, sl] = (yh * gn_ref[:, sl] * (g * _sigmoid(g))).astype(BF16)

    ya = jnp.dot(oa_ref[...], wa_ref[...], preferred_element_type=F32)
    yr = jnp.dot(oret_ref[...], wr_ref[...], preferred_element_type=F32)
    y_ref[...] = (_sigmoid(ga_ref[...]) * ya + _sigmoid(gr_ref[...]) * yr).astype(y_ref.dtype)


def _merge1(o_att, o_f, o_b, zg, gn, wa, wr):
    r = o_att.shape[0]
    d = wa.shape[1]
    tm = _blk(512, r)
    tn = _blk(1024, d)
    ga0 = RET_V_W // tn
    gr0 = (RET_V_W + d) // tn
    row = lambda i, j: (i, 0)
    return pl.pallas_call(
        _merge1_kernel,
        grid=(r // tm, d // tn),
        in_specs=[pl.BlockSpec((tm, ATT_W), row),
                  pl.BlockSpec((tm, RET_V_W), row),
                  pl.BlockSpec((tm, RET_V_W), row),
                  pl.BlockSpec((tm, RET_V_W), row),
                  pl.BlockSpec((tm, tn), lambda i, j: (i, ga0 + j)),
                  pl.BlockSpec((tm, tn), lambda i, j: (i, gr0 + j)),
                  pl.BlockSpec((1, RET_V_W), lambda i, j: (0, 0)),
                  pl.BlockSpec((ATT_W, tn), lambda i, j: (0, j)),
                  pl.BlockSpec((RET_V_W, tn), lambda i, j: (0, j))],
        out_specs=pl.BlockSpec((tm, tn), lambda i, j: (i, j)),
        out_shape=jax.ShapeDtypeStruct((r, d), BF16),
        scratch_shapes=[pltpu.VMEM((tm, RET_V_W), BF16)],
        compiler_params=_params("parallel", "arbitrary"),
        name="merge1",
    )(o_att, o_f, o_b, zg, zg, zg, gn.reshape(1, RET_V_W), wa, wr)


def _merge2_kernel(y_ref, w_ref, x_ref, gt_ref, pn1_ref, pn2_ref, sh_ref, sc_ref, wr_ref,
                   x1_ref, h2_ref, aff_ref):
    yy = jnp.dot(y_ref[0], w_ref[...], preferred_element_type=F32)
    r = yy * lax.rsqrt(jnp.mean(yy * yy, axis=-1, keepdims=True) + EPS) * pn1_ref[...]
    x1 = x_ref[0] + gt_ref[0] * r
    x1_ref[0] = x1
    h2 = x1 * lax.rsqrt(jnp.mean(x1 * x1, axis=-1, keepdims=True) + EPS) * pn2_ref[...]
    h2 = h2 * (1.0 + sc_ref[0]) + sh_ref[0]
    h2_ref[0] = h2.astype(h2_ref.dtype)
    logits = jnp.dot(h2, wr_ref[...], precision=lax.Precision.HIGHEST, preferred_element_type=F32)
    e = jnp.exp(logits - jnp.max(logits, axis=-1, keepdims=True))
    aff_ref[0] = e / jnp.sum(e, axis=-1, keepdims=True)


def _merge2(y, w_out, x, gt1, pn1, pn2, sh2, sc2, w_router):
    b, t, d = x.shape
    tm = _blk(256, t)
    ne = w_router.shape[1]
    tile = lambda i, j: (i, j, 0)
    vec = lambda i, j: (0, 0)
    per_b = lambda i, j: (i, 0, 0)
    return pl.pallas_call(
        _merge2_kernel,
        grid=(b, t // tm),
        in_specs=[pl.BlockSpec((1, tm, d), tile),
                  pl.BlockSpec((d, d), vec),
                  pl.BlockSpec((1, tm, d), tile),
                  pl.BlockSpec((1, 1, d), per_b),
                  pl.BlockSpec((1, d), vec),
                  pl.BlockSpec((1, d), vec),
                  pl.BlockSpec((1, 1, d), per_b),
                  pl.BlockSpec((1, 1, d), per_b),
                  pl.BlockSpec((d, ne), vec)],
        out_specs=(pl.BlockSpec((1, tm, d), tile),
                   pl.BlockSpec((1, tm, d), tile),
                   pl.BlockSpec((1, tm, ne), tile)),
        out_shape=(jax.ShapeDtypeStruct((b, t, d), F32),
                   jax.ShapeDtypeStruct((b, t, d), BF16),
                   jax.ShapeDtypeStruct((b, t, ne), F32)),
        compiler_params=_params("parallel", "parallel"),
        name="merge2",
    )(y.reshape(b, t, d), w_out, x, gt1, pn1.reshape(1, d), pn2.reshape(1, d), sh2, sc2, w_router)


def _lane_cumsum(x, tri):
    off = jnp.zeros((x.shape[0], 1), F32)
    parts = []
    for cidx in range(x.shape[1] // LANES):
        xc = x[:, cidx * LANES:(cidx + 1) * LANES].astype(BF16)
        cs = jnp.dot(xc, tri, preferred_element_type=F32) + off
        parts.append(cs)
        off = cs[:, LANES - 1:LANES]
    return jnp.concatenate(parts, axis=1)


def _topk_kernel(a_ref, pos_ref, *, cap):
    a = a_ref[0]
    bits = jnp.zeros((a.shape[0], 1), jnp.int32)
    for bit in range(30, -1, -1):
        cand = bits | (1 << bit)
        cnt = jnp.sum(jnp.where(a >= lax.bitcast_convert_type(cand, F32), 1.0, 0.0), axis=1, keepdims=True)
        bits = jnp.where(cnt >= cap, cand, bits)
    bits = jnp.where(bits < MIN_NORMAL_F32_BITS, 0, bits)
    thr = lax.bitcast_convert_type(bits, F32)
    ri = lax.broadcasted_iota(jnp.int32, (LANES, LANES), 0)
    ci = lax.broadcasted_iota(jnp.int32, (LANES, LANES), 1)
    tri = jnp.where(ri <= ci, 1.0, 0.0).astype(BF16)
    gt = a > thr
    eq = a == thr
    n_gt = jnp.sum(jnp.where(gt, 1.0, 0.0), axis=1, keepdims=True)
    eq_rank = _lane_cumsum(jnp.where(eq, 1.0, 0.0), tri)
    sel = jnp.where(gt, 1.0, jnp.where(eq, jnp.where(eq_rank <= cap - n_gt, 1.0, 0.0), 0.0))
    pos_ref[0] = _lane_cumsum(sel, tri) * sel


def _topk_slots(aff_t, cap):
    b, ne, t = aff_t.shape
    return pl.pallas_call(
        functools.partial(_topk_kernel, cap=cap),
        grid=(b,),
        in_specs=[pl.BlockSpec((1, ne, t), lambda i: (i, 0, 0))],
        out_specs=pl.BlockSpec((1, ne, t), lambda i: (i, 0, 0)),
        out_shape=jax.ShapeDtypeStruct((b, ne, t), F32),
        compiler_params=_params("parallel"),
        name="topk",
    )(aff_t)


def _gather_kernel(pos_ref, h_ref, o_ref):
    cap = o_ref.shape[1]
    t = h_ref.shape[1]
    tc = _blk(1024, t)
    slot = lax.broadcasted_iota(jnp.int32, (cap, tc), 0).astype(F32) + 1.0
    acc = None
    for cidx in range(t // tc):
        row = pos_ref[0, 0, :, cidx * tc:(cidx + 1) * tc]
        onehot = jnp.where(row == slot, 1.0, 0.0).astype(BF16)
        part = jnp.dot(onehot, h_ref[0, cidx * tc:(cidx + 1) * tc, :], preferred_element_type=F32)
        acc = part if acc is None else acc + part
    o_ref[0] = acc.astype(o_ref.dtype)


def _gather(pos, h2, cap):
    b, ne, t = pos.shape
    d = h2.shape[-1]
    dn = _blk(1024, d)
    return pl.pallas_call(
        _gather_kernel,
        grid=(b, d // dn, ne),
        in_specs=[pl.BlockSpec((1, 1, 1, t), lambda i, n, e: (i, e, 0, 0)),
                  pl.BlockSpec((1, t, dn), lambda i, n, e: (i, 0, n))],
        out_specs=pl.BlockSpec((1, cap, dn), lambda i, n, e: (e, i, n)),
        out_shape=jax.ShapeDtypeStruct((ne, b * cap, d), BF16),
        compiler_params=_params("parallel", "parallel", "parallel"),
        name="gather",
    )(pos.reshape(b, ne, 1, t), h2)


def _ffn_kernel(x_ref, wg_ref, wu_ref, wd_ref, o_ref, acc_ref):
    f = pl.program_id(2)
    x = x_ref[0]
    a = jnp.dot(x, wg_ref[0].astype(BF16), preferred_element_type=F32)
    u = jnp.dot(x, wu_ref[0].astype(BF16), preferred_element_type=F32)
    hmid = (a * _sigmoid(a) * u).astype(BF16)
    part = jnp.dot(hmid, wd_ref[0].astype(BF16), preferred_element_type=F32)

    @pl.when(f == 0)
    def _():
        acc_ref[...] = part

    @pl.when(f > 0)
    def _():
        acc_ref[...] += part

    @pl.when(f == pl.num_programs(2) - 1)
    def _():
        o_ref[0] = acc_ref[...].astype(o_ref.dtype)


def _ffn(xg, w_gate, w_up, w_down):
    ne, rows, d = xg.shape
    ff = w_gate.shape[-1]
    tm = _blk(1024, rows)
    tf = _blk(256, ff)
    return pl.pallas_call(
        _ffn_kernel,
        grid=(ne, rows // tm, ff // tf),
        in_specs=[pl.BlockSpec((1, tm, d), lambda e, m, f: (e, m, 0)),
                  pl.BlockSpec((1, d, tf), lambda e, m, f: (e, 0, f)),
                  pl.BlockSpec((1, d, tf), lambda e, m, f: (e, 0, f)),
                  pl.BlockSpec((1, tf, d), lambda e, m, f: (e, f, 0))],
        out_specs=pl.BlockSpec((1, tm, d), lambda e, m, f: (e, m, 0)),
        out_shape=jax.ShapeDtypeStruct((ne, rows, d), BF16),
        scratch_shapes=[pltpu.VMEM((tm, d), F32)],
        compiler_params=_params("parallel", "parallel", "arbitrary"),
        name="ffn",
    )(xg, w_gate, w_up, w_down)


def _combine_kernel(pos_ref, aff_ref, y_ref, x1_ref, gt_ref, pn_ref, o_ref, acc_ref):
    e = pl.program_id(2)
    tt, ne = pos_ref.shape[1], pos_ref.shape[2]
    cap = y_ref.shape[1]
    pick = lax.broadcasted_iota(jnp.int32, (tt, ne), 1) == e
    pcol = jnp.sum(jnp.where(pick, pos_ref[0], 0.0), axis=1, keepdims=True)
    acol = jnp.sum(jnp.where(pick, aff_ref[0], 0.0), axis=1, keepdims=True)
    slot = lax.broadcasted_iota(jnp.int32, (tt, cap), 1).astype(F32) + 1.0
    onehot = jnp.where(pcol == slot, 1.0, 0.0).astype(BF16)
    z = jnp.dot(onehot, y_ref[0], preferred_element_type=F32) * acol

    @pl.when(e == 0)
    def _():
        acc_ref[...] = z

    @pl.when(e > 0)
    def _():
        acc_ref[...] += z

    @pl.when(e == pl.num_programs(2) - 1)
    def _():
        y2 = acc_ref[...]
        r = y2 * lax.rsqrt(jnp.mean(y2 * y2, axis=-1, keepdims=True) + EPS) * pn_ref[...]
        o_ref[0] = x1_ref[0] + gt_ref[0] * r


def _combine(pos_t, aff, y, x1, gt2, pn2, cap):
    b, t, d = x1.shape
    ne = aff.shape[-1]
    tt = _blk(512, t)
    return pl.pallas_call(
        _combine_kernel,
        grid=(b, t // tt, ne),
        in_specs=[pl.BlockSpec((1, tt, ne), lambda i, j, e: (i, j, 0)),
                  pl.BlockSpec((1, tt, ne), lambda i, j, e: (i, j, 0)),
                  pl.BlockSpec((1, cap, d), lambda i, j, e: (e, i, 0)),
                  pl.BlockSpec((1, tt, d), lambda i, j, e: (i, j, 0)),
                  pl.BlockSpec((1, 1, d), lambda i, j, e: (i, 0, 0)),
                  pl.BlockSpec((1, d), lambda i, j, e: (0, 0))],
        out_specs=pl.BlockSpec((1, tt, d), lambda i, j, e: (i, j, 0)),
        out_shape=jax.ShapeDtypeStruct((b, t, d), F32),
        scratch_shapes=[pltpu.VMEM((tt, d), F32)],
        compiler_params=_params("parallel", "parallel", "arbitrary"),
        name="combine",
    )(pos_t, aff, y, x1, gt2, pn2.reshape(1, d))


def _rope_tables(t):
    pos = jnp.arange(t)
    r = (pos // GRID_W).astype(F32)
    cl = (pos % GRID_W).astype(F32)
    quarter = HEAD_DIM // 4
    inv = ROPE_THETA ** (-jnp.arange(quarter, dtype=F32) / quarter)
    ang_r = r[:, None] * inv
    ang_c = cl[:, None] * inv
    ang = jnp.concatenate([ang_r, ang_r, ang_c, ang_c], axis=-1)
    sign = jnp.where((jnp.arange(HEAD_DIM) % (HEAD_DIM // 2)) < quarter, -1.0, 1.0).astype(F32)
    return jnp.cos(ang), jnp.sin(ang) * sign


def _kv_heads(h, w_in, q_width, k_norm, rope):
    k_a = _proj(h, w_in, q_width, ATT_KV_W, mode="norm_rope", out_dtype=BF16, tn=ATT_KV_W,
                rope=rope, gain=k_norm)
    v_a = _proj(h, w_in, q_width + ATT_KV_W, ATT_KV_W, mode="plain", out_dtype=BF16, tn=ATT_KV_W)
    k_r = _proj(h, w_in, q_width + 2 * ATT_KV_W, RET_QK_W, mode="rope", out_dtype=BF16, tn=RET_QK_W,
                rope=rope, pre_scale=RET_QK_DIM ** -0.5)
    v_r = _proj(h, w_in, q_width + 2 * ATT_KV_W + RET_QK_W, RET_V_W, mode="plain", out_dtype=BF16,
                tn=512)
    return k_a, v_a, k_r, v_r


def kernel(x, c, ctx, c_ctx, w_mod, b_mod, pre_norm1, post_norm1, pre_norm2, post_norm2, w_in, q_norm,
           k_norm, ret_decay, ret_gn, w_o_att, w_o_ret, w_out, w_router, w_gate, w_up, w_down):
    b, t, d = x.shape
    n_ctx = ctx.shape[1]
    depth = w_mod.shape[0]
    q_width = ATT_W + RET_QK_W + RET_V_W + 2 * d
    cap = EC_FACTOR * t // N_EXPERTS
    rope_lat = tuple(jnp.tile(tbl, (b, 1)) for tbl in _rope_tables(t))
    rope_ctx = (jnp.ones((b * n_ctx, HEAD_DIM), F32), jnp.zeros((b * n_ctx, HEAD_DIM), F32))
    cs = jnp.zeros((8, d), F32).at[:b].set(c).at[b].set(c_ctx)
    xc = ctx
    for layer in range(depth):
        assert layer == depth - 1, "context-stream update between layers is not implemented"
        mod = _mod_vectors(cs, w_mod[layer], b_mod[layer])
        sh1, sc1, gt1, sh2, sc2, gt2 = [m[:b, None, :] for m in jnp.split(mod, 6, axis=-1)]
        csh1, csc1 = [jnp.broadcast_to(m[b][None, None, :], (b, 1, d)) for m in jnp.split(mod, 6, axis=-1)[:2]]
        dec = -jax.nn.softplus(ret_decay[layer].astype(F32))
        wl = w_in[layer]

        hc = _prenorm(xc, pre_norm1[layer], csh1, csc1).reshape(b * n_ctx, d)
        kc_a, vc_a, kc_r, vc_r = _kv_heads(hc, wl, q_width, k_norm[layer], rope_ctx)
        s0f, s0b = _ctx_states(dec, kc_r.reshape(b, n_ctx, RET_QK_W), vc_r.reshape(b, n_ctx, RET_V_W))

        h = _prenorm(x, pre_norm1[layer], sh1, sc1).reshape(b * t, d)
        q_a = _proj(h, wl, 0, ATT_W, mode="norm_rope", out_dtype=BF16, tn=512, rope=rope_lat,
                    gain=q_norm[layer])
        q_r = _proj(h, wl, ATT_W, RET_QK_W, mode="rope", out_dtype=BF16, tn=512, rope=rope_lat)
        zg = _proj(h, wl, ATT_W + RET_QK_W, RET_V_W + 2 * d, mode="plain", out_dtype=F32, tn=512)
        k_a, v_a, k_r, v_r = _kv_heads(h, wl, q_width, k_norm[layer], rope_lat)

        keys = jnp.concatenate([kc_a.reshape(b, n_ctx, ATT_KV_W), k_a.reshape(b, t, ATT_KV_W)], axis=1)
        vals = jnp.concatenate([vc_a.reshape(b, n_ctx, ATT_KV_W), v_a.reshape(b, t, ATT_KV_W)], axis=1)
        kt = keys.reshape(b, n_ctx + t, ATT_KV_HEADS, HEAD_DIM).transpose(0, 2, 3, 1)
        o_att = _attention(q_a.reshape(b, t, ATT_W), kt, vals)

        o_f, o_b = _retention(dec, q_r.reshape(b, t, RET_QK_W), k_r.reshape(b, t, RET_QK_W),
                              v_r.reshape(b, t, RET_V_W), s0f, s0b)

        y = _merge1(o_att.reshape(b * t, ATT_W), o_f.reshape(b * t, RET_V_W), o_b.reshape(b * t, RET_V_W),
                    zg, ret_gn[layer], w_o_att[layer].astype(BF16), w_o_ret[layer].astype(BF16))
        x1, h2, aff = _merge2(y, w_out[layer].astype(BF16), x, gt1, post_norm1[layer], pre_norm2[layer],
                              sh2, sc2, w_router[layer])

        pos = _topk_slots(aff.transpose(0, 2, 1), cap)
        xg = _gather(pos, h2, cap)
        yg = _ffn(xg, w_gate[layer], w_up[layer], w_down[layer])
        x = _combine(pos.transpose(0, 2, 1), aff, yg, x1, gt2, post_norm2[layer], cap)
    return x
```

```python
import functools

import jax
import jax.numpy as jnp
from jax import lax
from jax.experimental import pallas as pl
from jax.experimental.pallas import tpu as pltpu

F32 = jnp.float32
BF16 = jnp.bfloat16

GRID_W = 64
HEAD_DIM = 128
ATT_HEADS = 8
ATT_KV_HEADS = 2
ATT_GROUP = ATT_HEADS // ATT_KV_HEADS
RET_HEADS = 4
RET_QK_DIM = 128
RET_V_DIM = 256
ROPE_THETA = 10000.0
N_EXPERTS = 16
EC_FACTOR = 2
EPS = 1e-6

ATT_W = ATT_HEADS * HEAD_DIM
ATT_KV_W = ATT_KV_HEADS * HEAD_DIM
RET_QK_W = RET_HEADS * RET_QK_DIM
RET_V_W = RET_HEADS * RET_V_DIM

LANES = 128
VMEM_LIMIT = 56 * 1024 * 1024
RET_CHUNK = 256
MIN_NORMAL_F32_BITS = 0x00800000
LOG2_E = 1.4426950408889634
TOK_TILE = 256
SLOT_BLK = 64
GATHER_WIN = SLOT_BLK + 16


def _blk(pref, n):
    return pref if n % pref == 0 else n


def _params(*sem):
    return pltpu.CompilerParams(dimension_semantics=sem, vmem_limit_bytes=VMEM_LIMIT)


def _sigmoid(x):
    return 0.5 * jnp.tanh(0.5 * x) + 0.5


def _mod_kernel(s_ref, w_ref, b_ref, o_ref):
    s = s_ref[...]
    s = s * _sigmoid(s)
    o_ref[...] = jnp.dot(s, w_ref[...], precision=lax.Precision.HIGHEST,
                         preferred_element_type=F32) + b_ref[...]


def _mod_vectors(cs, w_mod, b_mod):
    rows, d = cs.shape
    n = w_mod.shape[1]
    tn = _blk(1024, n)
    return pl.pallas_call(
        _mod_kernel,
        grid=(n // tn,),
        in_specs=[pl.BlockSpec((rows, d), lambda j: (0, 0)),
                  pl.BlockSpec((d, tn), lambda j: (0, j)),
                  pl.BlockSpec((1, tn), lambda j: (0, j))],
        out_specs=pl.BlockSpec((rows, tn), lambda j: (0, j)),
        out_shape=jax.ShapeDtypeStruct((rows, n), F32),
        compiler_params=_params("parallel"),
        name="mod",
    )(cs, w_mod, b_mod.reshape(1, n))


def _prenorm_kernel(x_ref, g_ref, sh_ref, sc_ref, o_ref):
    x = x_ref[0]
    y = x * lax.rsqrt(jnp.mean(x * x, axis=-1, keepdims=True) + EPS) * g_ref[...]
    o_ref[0] = (y * (1.0 + sc_ref[0]) + sh_ref[0]).astype(o_ref.dtype)


def _prenorm(x, gain, shift, scale):
    b, t, d = x.shape
    tm = _blk(512, t)
    return pl.pallas_call(
        _prenorm_kernel,
        grid=(b, t // tm),
        in_specs=[pl.BlockSpec((1, tm, d), lambda i, j: (i, j, 0)),
                  pl.BlockSpec((1, d), lambda i, j: (0, 0)),
                  pl.BlockSpec((1, 1, d), lambda i, j: (i, 0, 0)),
                  pl.BlockSpec((1, 1, d), lambda i, j: (i, 0, 0))],
        out_specs=pl.BlockSpec((1, tm, d), lambda i, j: (i, j, 0)),
        out_shape=jax.ShapeDtypeStruct((b, t, d), BF16),
        compiler_params=_params("parallel", "parallel"),
        name="prenorm",
    )(x, gain.reshape(1, d), shift, scale)


def _proj_kernel(*refs, mode, pre_scale):
    if mode == "plain":
        h_ref, w_ref, o_ref = refs
    elif mode == "rope":
        h_ref, w_ref, cos_ref, sin_ref, o_ref = refs
    else:
        h_ref, w_ref, gain_ref, cos_ref, sin_ref, o_ref = refs
    acc = jnp.dot(h_ref[...], w_ref[...].astype(BF16), preferred_element_type=F32)
    if mode == "plain":
        o_ref[...] = acc.astype(o_ref.dtype)
        return
    tm, tn = acc.shape
    cos = cos_ref[...]
    sin = sin_ref[...]
    lane = lax.broadcasted_iota(jnp.int32, (tm, HEAD_DIM), 1)
    first = (lane % (HEAD_DIM // 2)) < (HEAD_DIM // 4)
    for hh in range(tn // HEAD_DIM):
        xh = acc[:, hh * HEAD_DIM:(hh + 1) * HEAD_DIM]
        if mode == "norm_rope":
            xh = xh * lax.rsqrt(jnp.mean(xh * xh, axis=-1, keepdims=True) + EPS) * gain_ref[...]
        elif pre_scale != 1.0:
            xh = xh * pre_scale
        rot = jnp.where(first,
                        pltpu.roll(xh, HEAD_DIM - HEAD_DIM // 4, 1),
                        pltpu.roll(xh, HEAD_DIM // 4, 1))
        o_ref[:, hh * HEAD_DIM:(hh + 1) * HEAD_DIM] = (xh * cos + rot * sin).astype(o_ref.dtype)


def _proj(h, w_in, col0, width, *, mode, out_dtype, tn, rope=None, gain=None, pre_scale=1.0):
    r, d = h.shape
    tm = _blk(2048, r)
    assert col0 % tn == 0 and width % tn == 0
    c0 = col0 // tn
    in_specs = [pl.BlockSpec((tm, d), lambda i, j: (i, 0)),
                pl.BlockSpec((d, tn), lambda i, j: (0, c0 + j))]
    args = [h, w_in]
    if mode == "norm_rope":
        in_specs.append(pl.BlockSpec((1, HEAD_DIM), lambda i, j: (0, 0)))
        args.append(gain.reshape(1, HEAD_DIM))
    if mode != "plain":
        cos, sin = rope
        assert cos.shape == (r, HEAD_DIM)
        in_specs += [pl.BlockSpec((tm, HEAD_DIM), lambda i, j: (i, 0)),
                     pl.BlockSpec((tm, HEAD_DIM), lambda i, j: (i, 0))]
        args += [cos, sin]
    return pl.pallas_call(
        functools.partial(_proj_kernel, mode=mode, pre_scale=pre_scale),
        grid=(r // tm, width // tn),
        in_specs=in_specs,
        out_specs=pl.BlockSpec((tm, tn), lambda i, j: (i, j)),
        out_shape=jax.ShapeDtypeStruct((r, width), out_dtype),
        compiler_params=_params("parallel", "parallel"),
        name="proj_" + mode,
    )(*args)


def _ctx_state_kernel(dec_ref, k_ref, v_ref, sf_ref, sb_ref):
    hh = pl.program_id(1)
    lf = dec_ref[0, hh]
    lb = dec_ref[1, hh]
    k = k_ref[0].astype(F32)
    v = v_ref[0]
    n = k.shape[0]
    pos = lax.broadcasted_iota(jnp.int32, k.shape, 0).astype(F32)
    kf = (k * jnp.exp((n - 1.0 - pos) * lf)).T.astype(BF16)
    kb = (k * jnp.exp(pos * lb)).T.astype(BF16)
    sf_ref[0, 0] = jnp.dot(kf, v, preferred_element_type=F32)
    sb_ref[0, 0] = jnp.dot(kb, v, preferred_element_type=F32)


def _ctx_states(dec, k_r, v_r):
    b, n, _ = k_r.shape
    spec_s = pl.BlockSpec((1, 1, RET_QK_DIM, RET_V_DIM), lambda i, j: (i, j, 0, 0))
    shape_s = jax.ShapeDtypeStruct((b, RET_HEADS, RET_QK_DIM, RET_V_DIM), F32)
    return pl.pallas_call(
        _ctx_state_kernel,
        grid=(b, RET_HEADS),
        in_specs=[pl.BlockSpec(memory_space=pltpu.SMEM),
                  pl.BlockSpec((1, n, RET_QK_DIM), lambda i, j: (i, 0, j)),
                  pl.BlockSpec((1, n, RET_V_DIM), lambda i, j: (i, 0, j))],
        out_specs=(spec_s, spec_s),
        out_shape=(shape_s, shape_s),
        compiler_params=_params("parallel", "parallel"),
        name="ctx_state",
    )(dec, k_r, v_r)


def _attn_kernel(q_ref, kt_ref, v_ref, o_ref, *, scale):
    kt = kt_ref[0, 0]
    v = v_ref[0]
    for g in range(ATT_GROUP):
        q = q_ref[0, :, g * HEAD_DIM:(g + 1) * HEAD_DIM]
        s = jnp.dot(q, kt, preferred_element_type=F32)
        m = jnp.max(s, axis=-1, keepdims=True)
        p = jnp.exp2((s - m) * (scale * LOG2_E))
        l = jnp.sum(p, axis=-1, keepdims=True)
        o = jnp.dot(p.astype(BF16), v, preferred_element_type=F32)
        o_ref[0, :, g * HEAD_DIM:(g + 1) * HEAD_DIM] = (o / l).astype(o_ref.dtype)


def _attention(q, kt, v):
    b, t, _ = q.shape
    s = kt.shape[-1]
    tq = _blk(256, t)
    gw = ATT_GROUP * HEAD_DIM
    return pl.pallas_call(
        functools.partial(_attn_kernel, scale=HEAD_DIM ** -0.5),
        grid=(b, ATT_KV_HEADS, t // tq),
        in_specs=[pl.BlockSpec((1, tq, gw), lambda i, j, n: (i, n, j)),
                  pl.BlockSpec((1, 1, HEAD_DIM, s), lambda i, j, n: (i, j, 0, 0)),
                  pl.BlockSpec((1, s, HEAD_DIM), lambda i, j, n: (i, 0, j))],
        out_specs=pl.BlockSpec((1, tq, gw), lambda i, j, n: (i, n, j)),
        out_shape=jax.ShapeDtypeStruct((b, t, ATT_W), BF16),
        compiler_params=_params("parallel", "parallel", "parallel"),
        name="attention",
    )(q, kt, v)


def _ret_kernel(dec_ref, qf_ref, kf_ref, vf_ref, qb_ref, kb_ref, vb_ref, s0f_ref, s0b_ref,
                of_ref, ob_ref, sf_ref, sb_ref, mf_ref, mb_ref):
    hh = pl.program_id(1)
    n = pl.program_id(2)
    lf = dec_ref[0, hh]
    lb = dec_ref[1, hh]
    c = qf_ref.shape[1]

    @pl.when(n == 0)
    def _():
        sf_ref[...] = s0f_ref[0, 0]
        sb_ref[...] = s0b_ref[0, 0]
        ri = lax.broadcasted_iota(jnp.int32, (c, c), 0)
        ci = lax.broadcasted_iota(jnp.int32, (c, c), 1)
        d = (ri - ci).astype(F32)
        mf_ref[...] = jnp.where(d >= 0, jnp.exp(jnp.maximum(d, 0.0) * lf), 0.0)
        mb_ref[...] = jnp.where(d <= 0, jnp.exp(jnp.maximum(-d, 0.0) * lb), 0.0)

    nt = (((1,), (1,)), ((), ()))
    row_v = lax.broadcasted_iota(jnp.int32, (c, RET_V_DIM), 0).astype(F32)
    row_k = lax.broadcasted_iota(jnp.int32, (c, RET_QK_DIM), 0).astype(F32)

    def sweep(q_ref, k_ref, v_ref, o_ref, s_ref, m_ref, lg, wq_age, wk_age):
        q = q_ref[0]
        k = k_ref[0]
        v = v_ref[0]
        s = lax.dot_general(q, k, nt, preferred_element_type=F32) * m_ref[...]
        o_in = jnp.dot(s.astype(BF16), v, preferred_element_type=F32)
        state = s_ref[...]
        o_x = jnp.dot(q, state.astype(BF16), preferred_element_type=F32) * jnp.exp(wq_age * lg)
        o_ref[0] = o_in + o_x
        kw = (k.astype(F32) * jnp.exp(wk_age * lg)).T.astype(BF16)
        g_chunk = jnp.exp(jnp.full((1, RET_V_DIM), c * lg, F32))
        s_ref[...] = g_chunk * state + jnp.dot(kw, v, preferred_element_type=F32)

    sweep(qf_ref, kf_ref, vf_ref, of_ref, sf_ref, mf_ref, lf, row_v + 1.0, c - 1.0 - row_k)
    sweep(qb_ref, kb_ref, vb_ref, ob_ref, sb_ref, mb_ref, lb, c - row_v, row_k)


def _retention(dec, q, k, v, s0f, s0b):
    b, t, _ = q.shape
    c = _blk(RET_CHUNK, t)
    nc = t // c
    fwd = lambda i, j, n: (i, n, j)
    bwd = lambda i, j, n: (i, nc - 1 - n, j)
    spec_s = pl.BlockSpec((1, 1, RET_QK_DIM, RET_V_DIM), lambda i, j, n: (i, j, 0, 0))
    out_shape = jax.ShapeDtypeStruct((b, t, RET_V_W), F32)
    return pl.pallas_call(
        _ret_kernel,
        grid=(b, RET_HEADS, nc),
        in_specs=[pl.BlockSpec(memory_space=pltpu.SMEM),
                  pl.BlockSpec((1, c, RET_QK_DIM), fwd),
                  pl.BlockSpec((1, c, RET_QK_DIM), fwd),
                  pl.BlockSpec((1, c, RET_V_DIM), fwd),
                  pl.BlockSpec((1, c, RET_QK_DIM), bwd),
                  pl.BlockSpec((1, c, RET_QK_DIM), bwd),
                  pl.BlockSpec((1, c, RET_V_DIM), bwd),
                  spec_s, spec_s],
        out_specs=(pl.BlockSpec((1, c, RET_V_DIM), fwd), pl.BlockSpec((1, c, RET_V_DIM), bwd)),
        out_shape=(out_shape, out_shape),
        scratch_shapes=[pltpu.VMEM((RET_QK_DIM, RET_V_DIM), F32),
                        pltpu.VMEM((RET_QK_DIM, RET_V_DIM), F32),
                        pltpu.VMEM((c, c), F32),
                        pltpu.VMEM((c, c), F32)],
        compiler_params=_params("parallel", "parallel", "arbitrary"),
        name="retention",
    )(dec, q, k, v, q, k, v, s0f, s0b)


def _merge1_kernel(oa_ref, of_ref, ob_ref, g_ref, ga_ref, gr_ref, gn_ref, wa_ref, wr_ref,
                   y_ref, oret_ref):
    @pl.when(pl.program_id(1) == 0)
    def _():
        o = of_ref[...] + ob_ref[...]
        for hh in range(RET_HEADS):
            sl = slice(hh * RET_V_DIM, (hh + 1) * RET_V_DIM)
            oh = o[:, sl]
            dlt = oh - jnp.mean(oh, axis=-1, keepdims=True)
            yh = dlt * lax.rsqrt(jnp.mean(dlt * dlt, axis=-1, keepdims=True) + EPS)
            g = g_ref[:, sl]
            oret_ref[:, sl] = (yh * gn_ref[:, sl] * (g * _sigmoid(g))).astype(BF16)

    ya = jnp.dot(oa_ref[...], wa_ref[...], preferred_element_type=F32)
    yr = jnp.dot(oret_ref[...], wr_ref[...], preferred_element_type=F32)
    y_ref[...] = (_sigmoid(ga_ref[...]) * ya + _sigmoid(gr_ref[...]) * yr).astype(y_ref.dtype)


def _merge1(o_att, o_f, o_b, zg, gn, wa, wr):
    r = o_att.shape[0]
    d = wa.shape[1]
    tm = _blk(512, r)
    tn = _blk(1024, d)
    ga0 = RET_V_W // tn
    gr0 = (RET_V_W + d) // tn
    row = lambda i, j: (i, 0)
    return pl.pallas_call(
        _merge1_kernel,
        grid=(r // tm, d // tn),
        in_specs=[pl.BlockSpec((tm, ATT_W), row),
                  pl.BlockSpec((tm, RET_V_W), row),
                  pl.BlockSpec((tm, RET_V_W), row),
                  pl.BlockSpec((tm, RET_V_W), row),
                  pl.BlockSpec((tm, tn), lambda i, j: (i, ga0 + j)),
                  pl.BlockSpec((tm, tn), lambda i, j: (i, gr0 + j)),
                  pl.BlockSpec((1, RET_V_W), lambda i, j: (0, 0)),
                  pl.BlockSpec((ATT_W, tn), lambda i, j: (0, j)),
                  pl.BlockSpec((RET_V_W, tn), lambda i, j: (0, j))],
        out_specs=pl.BlockSpec((tm, tn), lambda i, j: (i, j)),
        out_shape=jax.ShapeDtypeStruct((r, d), BF16),
        scratch_shapes=[pltpu.VMEM((tm, RET_V_W), BF16)],
        compiler_params=_params("parallel", "arbitrary"),
        name="merge1",
    )(o_att, o_f, o_b, zg, zg, zg, gn.reshape(1, RET_V_W), wa, wr)


def _merge2_kernel(y_ref, w_ref, x_ref, gt_ref, pn1_ref, pn2_ref, sh_ref, sc_ref, wr_ref,
                   x1_ref, h2_ref, aff_ref):
    yy = jnp.dot(y_ref[0], w_ref[...], preferred_element_type=F32)
    r = yy * lax.rsqrt(jnp.mean(yy * yy, axis=-1, keepdims=True) + EPS) * pn1_ref[...]
    x1 = x_ref[0] + gt_ref[0] * r
    x1_ref[0] = x1
    h2 = x1 * lax.rsqrt(jnp.mean(x1 * x1, axis=-1, keepdims=True) + EPS) * pn2_ref[...]
    h2 = h2 * (1.0 + sc_ref[0]) + sh_ref[0]
    h_hi = h2.astype(BF16)
    h2_ref[0] = h_hi
    h_lo = (h2 - h_hi.astype(F32)).astype(BF16)
    wr = wr_ref[...]
    w_hi = wr.astype(BF16)
    w_lo = (wr - w_hi.astype(F32)).astype(BF16)
    ne = wr.shape[1]
    r_hi = jnp.dot(h_hi, jnp.concatenate([w_hi, w_lo], axis=1), preferred_element_type=F32)
    logits = r_hi[:, :ne] + r_hi[:, ne:] + jnp.dot(h_lo, w_hi, preferred_element_type=F32)
    e = jnp.exp(logits - jnp.max(logits, axis=-1, keepdims=True))
    aff_ref[0] = e / jnp.sum(e, axis=-1, keepdims=True)


def _merge2(y, w_out, x, gt1, pn1, pn2, sh2, sc2, w_router):
    b, t, d = x.shape
    tm = _blk(512, t)
    ne = w_router.shape[1]
    tile = lambda i, j: (i, j, 0)
    vec = lambda i, j: (0, 0)
    per_b = lambda i, j: (i, 0, 0)
    return pl.pallas_call(
        _merge2_kernel,
        grid=(b, t // tm),
        in_specs=[pl.BlockSpec((1, tm, d), tile),
                  pl.BlockSpec((d, d), vec),
                  pl.BlockSpec((1, tm, d), tile),
                  pl.BlockSpec((1, 1, d), per_b),
                  pl.BlockSpec((1, d), vec),
                  pl.BlockSpec((1, d), vec),
                  pl.BlockSpec((1, 1, d), per_b),
                  pl.BlockSpec((1, 1, d), per_b),
                  pl.BlockSpec((d, ne), vec)],
        out_specs=(pl.BlockSpec((1, tm, d), tile),
                   pl.BlockSpec((1, tm, d), tile),
                   pl.BlockSpec((1, tm, ne), tile)),
        out_shape=(jax.ShapeDtypeStruct((b, t, d), F32),
                   jax.ShapeDtypeStruct((b, t, d), BF16),
                   jax.ShapeDtypeStruct((b, t, ne), F32)),
        compiler_params=_params("parallel", "parallel"),
        name="merge2",
    )(y.reshape(b, t, d), w_out, x, gt1, pn1.reshape(1, d), pn2.reshape(1, d), sh2, sc2, w_router)


def _lane_cumsum(x, tri):
    off = jnp.zeros((x.shape[0], 1), F32)
    parts = []
    for cidx in range(x.shape[1] // LANES):
        xc = x[:, cidx * LANES:(cidx + 1) * LANES].astype(BF16)
        cs = jnp.dot(xc, tri, preferred_element_type=F32) + off
        parts.append(cs)
        off = cs[:, LANES - 1:LANES]
    return jnp.concatenate(parts, axis=1)


def _topk_kernel(a_ref, pos_ref, fill_ref, *, cap):
    a = a_ref[0]
    bits = jnp.zeros((a.shape[0], 1), jnp.int32)
    for bit in range(30, -1, -1):
        cand = bits | (1 << bit)
        cnt = jnp.sum(jnp.where(a >= lax.bitcast_convert_type(cand, F32), 1.0, 0.0), axis=1, keepdims=True)
        bits = jnp.where(cnt >= cap, cand, bits)
    bits = jnp.where(bits < MIN_NORMAL_F32_BITS, 0, bits)
    thr = lax.bitcast_convert_type(bits, F32)
    ri = lax.broadcasted_iota(jnp.int32, (LANES, LANES), 0)
    ci = lax.broadcasted_iota(jnp.int32, (LANES, LANES), 1)
    tri = jnp.where(ri <= ci, 1.0, 0.0).astype(BF16)
    gt = a > thr
    eq = a == thr
    n_gt = jnp.sum(jnp.where(gt, 1.0, 0.0), axis=1, keepdims=True)
    eq_rank = _lane_cumsum(jnp.where(eq, 1.0, 0.0), tri)
    sel = jnp.where(gt, 1.0, jnp.where(eq, jnp.where(eq_rank <= cap - n_gt, 1.0, 0.0), 0.0))
    filled = _lane_cumsum(sel, tri)
    pos_ref[0] = filled * sel
    fill_ref[0] = filled


def _topk_slots(aff_t, cap):
    b, ne, t = aff_t.shape
    spec = pl.BlockSpec((1, ne, t), lambda i: (i, 0, 0))
    shape = jax.ShapeDtypeStruct((b, ne, t), F32)
    return pl.pallas_call(
        functools.partial(_topk_kernel, cap=cap),
        grid=(b,),
        in_specs=[spec],
        out_specs=(spec, spec),
        out_shape=(shape, shape),
        compiler_params=_params("parallel"),
        name="topk",
    )(aff_t)


def _gather_kernel(pos_ref, h_ref, o_ref):
    cap = o_ref.shape[1]
    t = h_ref.shape[1]
    tc = _blk(1024, t)
    slot = lax.broadcasted_iota(jnp.int32, (cap, tc), 0).astype(F32) + 1.0
    acc = None
    for cidx in range(t // tc):
        row = pos_ref[0, 0, :, cidx * tc:(cidx + 1) * tc]
        onehot = jnp.where(row == slot, 1.0, 0.0).astype(BF16)
        part = jnp.dot(onehot, h_ref[0, cidx * tc:(cidx + 1) * tc, :], preferred_element_type=F32)
        acc = part if acc is None else acc + part
    o_ref[0] = acc.astype(o_ref.dtype)


def _gather(pos, h2, cap):
    b, ne, t = pos.shape
    d = h2.shape[-1]
    dn = _blk(1024, d)
    return pl.pallas_call(
        _gather_kernel,
        grid=(b, d // dn, ne),
        in_specs=[pl.BlockSpec((1, 1, 1, t), lambda i, n, e: (i, e, 0, 0)),
                  pl.BlockSpec((1, t, dn), lambda i, n, e: (i, 0, n))],
        out_specs=pl.BlockSpec((1, cap, dn), lambda i, n, e: (e, i, n)),
        out_shape=jax.ShapeDtypeStruct((ne, b * cap, d), BF16),
        compiler_params=_params("parallel", "parallel", "parallel"),
        name="gather",
    )(pos.reshape(b, ne, 1, t), h2)


def _gather_win_kernel(w0_ref, pos_ref, h_ref, o_ref, acc_ref, lhs_ref):
    b = pl.program_id(0)
    ne, cap = o_ref.shape[0], o_ref.shape[1]
    nt = h_ref.shape[1] // TOK_TILE
    slot1 = lax.broadcasted_iota(jnp.int32, (GATHER_WIN, TOK_TILE), 0) + 1

    @pl.when(pl.program_id(1) == 0)
    def _():
        def build(k, carry):
            for e in range(ne):
                w0 = w0_ref[(b * ne + e) * nt + k]
                row = pos_ref[0, pl.ds(e * nt + k, 1), :]
                onehot = jnp.where(row == (slot1 + w0).astype(F32), 1.0, 0.0)
                lhs_ref[k, e * GATHER_WIN:(e + 1) * GATHER_WIN, :] = onehot.astype(BF16)
            return carry
        lax.fori_loop(0, nt, build, 0)

    acc_ref[...] = jnp.zeros_like(acc_ref)

    def tile(k, carry):
        t0 = pl.multiple_of(k * TOK_TILE, TOK_TILE)
        res = jnp.dot(lhs_ref[k], h_ref[0, pl.ds(t0, TOK_TILE), :], preferred_element_type=F32)
        for e in range(ne):
            w0 = pl.multiple_of(w0_ref[(b * ne + e) * nt + k], 8)
            acc_ref[e, pl.ds(w0, GATHER_WIN), :] += res[e * GATHER_WIN:(e + 1) * GATHER_WIN]
        return carry
    lax.fori_loop(0, nt, tile, 0)
    o_ref[...] = acc_ref[:, :cap, :].astype(o_ref.dtype)


def _gather_win(w0, pos, h2, cap):
    b, ne, t = pos.shape
    d = h2.shape[-1]
    nt = t // TOK_TILE
    dn = _blk(256, d)
    return pl.pallas_call(
        _gather_win_kernel,
        grid_spec=pltpu.PrefetchScalarGridSpec(
            num_scalar_prefetch=1,
            grid=(b, d // dn),
            in_specs=[pl.BlockSpec((1, ne * nt, TOK_TILE), lambda i, n, w: (i, 0, 0)),
                      pl.BlockSpec((1, t, dn), lambda i, n, w: (i, 0, n))],
            out_specs=pl.BlockSpec((ne, cap, dn), lambda i, n, w: (0, i, n)),
            scratch_shapes=[pltpu.VMEM((ne, cap + GATHER_WIN, dn), F32),
                            pltpu.VMEM((nt, ne * GATHER_WIN, TOK_TILE), BF16)]),
        out_shape=jax.ShapeDtypeStruct((ne, b * cap, d), BF16),
        compiler_params=_params("parallel", "arbitrary"),
        name="gather_win",
    )(w0, pos.reshape(b, ne * nt, TOK_TILE), h2)


def _ffn_kernel(x_ref, wg_ref, wu_ref, wd_ref, o_ref, h_ref, wgu_ref, *, nf):
    j = pl.program_id(2)
    tf = wg_ref.shape[2]

    @pl.when(j < nf)
    def _():
        wgu_ref[:, :tf] = wg_ref[0].astype(BF16)
        wgu_ref[:, tf:] = wu_ref[0].astype(BF16)
        au = jnp.dot(x_ref[0], wgu_ref[...], preferred_element_type=F32)
        a = au[:, :tf]
        h_ref[j] = (a * _sigmoid(a) * au[:, tf:]).astype(BF16)

    @pl.when(j >= nf)
    def _():
        acc = None
        for cidx in range(nf):
            part = jnp.dot(h_ref[cidx], wd_ref[0, cidx * tf:(cidx + 1) * tf, :].astype(BF16),
                           preferred_element_type=F32)
            acc = part if acc is None else acc + part
        o_ref[0] = acc.astype(o_ref.dtype)


def _ffn(xg, w_gate, w_up, w_down):
    ne, rows, d = xg.shape
    ff = w_gate.shape[-1]
    tm = _blk(1024, rows)
    tf = _blk(512, ff)
    tn = _blk(512, d)
    nf = ff // tf
    return pl.pallas_call(
        functools.partial(_ffn_kernel, nf=nf),
        grid=(ne, rows // tm, nf + d // tn),
        in_specs=[pl.BlockSpec((1, tm, d), lambda e, m, j: (e, m, 0)),
                  pl.BlockSpec((1, d, tf), lambda e, m, j: (e, 0, jnp.minimum(j, nf - 1))),
                  pl.BlockSpec((1, d, tf), lambda e, m, j: (e, 0, jnp.minimum(j, nf - 1))),
                  pl.BlockSpec((1, ff, tn), lambda e, m, j: (e, 0, jnp.maximum(j - nf, 0)))],
        out_specs=pl.BlockSpec((1, tm, tn), lambda e, m, j: (e, m, jnp.maximum(j - nf, 0))),
        out_shape=jax.ShapeDtypeStruct((ne, rows, d), BF16),
        scratch_shapes=[pltpu.VMEM((nf, tm, tf), BF16), pltpu.VMEM((d, 2 * tf), BF16)],
        compiler_params=_params("parallel", "parallel", "arbitrary"),
        name="ffn",
    )(xg, w_gate, w_up, w_down)


def _combine_kernel(pos_ref, aff_ref, y_ref, x1_ref, gt_ref, pn_ref, o_ref, acc_ref):
    e = pl.program_id(2)
    tt, ne = pos_ref.shape[1], pos_ref.shape[2]
    cap = y_ref.shape[1]
    pick = lax.broadcasted_iota(jnp.int32, (tt, ne), 1) == e
    pcol = jnp.sum(jnp.where(pick, pos_ref[0], 0.0), axis=1, keepdims=True)
    acol = jnp.sum(jnp.where(pick, aff_ref[0], 0.0), axis=1, keepdims=True)
    slot = lax.broadcasted_iota(jnp.int32, (tt, cap), 1).astype(F32) + 1.0
    onehot = jnp.where(pcol == slot, 1.0, 0.0).astype(BF16)
    z = jnp.dot(onehot, y_ref[0], preferred_element_type=F32) * acol

    @pl.when(e == 0)
    def _():
        acc_ref[...] = z

    @pl.when(e > 0)
    def _():
        acc_ref[...] += z

    @pl.when(e == pl.num_programs(2) - 1)
    def _():
        y2 = acc_ref[...]
        r = y2 * lax.rsqrt(jnp.mean(y2 * y2, axis=-1, keepdims=True) + EPS) * pn_ref[...]
        o_ref[0] = x1_ref[0] + gt_ref[0] * r


def _combine(pos_t, aff, y, x1, gt2, pn2, cap):
    b, t, d = x1.shape
    ne = aff.shape[-1]
    tt = _blk(512, t)
    return pl.pallas_call(
        _combine_kernel,
        grid=(b, t // tt, ne),
        in_specs=[pl.BlockSpec((1, tt, ne), lambda i, j, e: (i, j, 0)),
                  pl.BlockSpec((1, tt, ne), lambda i, j, e: (i, j, 0)),
                  pl.BlockSpec((1, cap, d), lambda i, j, e: (e, i, 0)),
                  pl.BlockSpec((1, tt, d), lambda i, j, e: (i, j, 0)),
                  pl.BlockSpec((1, 1, d), lambda i, j, e: (i, 0, 0)),
                  pl.BlockSpec((1, d), lambda i, j, e: (0, 0))],
        out_specs=pl.BlockSpec((1, tt, d), lambda i, j, e: (i, j, 0)),
        out_shape=jax.ShapeDtypeStruct((b, t, d), F32),
        scratch_shapes=[pltpu.VMEM((tt, d), F32)],
        compiler_params=_params("parallel", "parallel", "arbitrary"),
        name="combine",
    )(pos_t, aff, y, x1, gt2, pn2.reshape(1, d))


def _combine_win_kernel(blk_ref, base_ref, pos_ref, aff_ref, *refs):
    del blk_ref
    ne = pos_ref.shape[2]
    y_refs = refs[:2 * ne]
    x1_ref, gt_ref, pn_ref, o_ref = refs[2 * ne:]
    b = pl.program_id(0)
    k = pl.program_id(1)
    nt = pl.num_programs(1)
    lane = lax.broadcasted_iota(jnp.int32, (TOK_TILE, 2 * SLOT_BLK), 1)
    second = lane >= SLOT_BLK
    pos = pos_ref[0]
    aff = aff_ref[0]
    cols = []
    for e in range(ne):
        at = ((b * ne + e) * nt + k) * 2
        slot = jnp.where(second, lane + (base_ref[at + 1] + 1 - SLOT_BLK), lane + (base_ref[at] + 1))
        cols.append(jnp.where(pos[:, e:e + 1] == slot.astype(F32), aff[:, e:e + 1], 0.0).astype(BF16))
    ywin = jnp.concatenate([r[...] for r in y_refs], axis=0)
    y2 = jnp.dot(jnp.concatenate(cols, axis=1), ywin, preferred_element_type=F32)
    r = y2 * lax.rsqrt(jnp.mean(y2 * y2, axis=-1, keepdims=True) + EPS) * pn_ref[...]
    o_ref[0] = x1_ref[0] + gt_ref[0] * r


def _combine_win(blk, base, pos_t, aff, y, x1, gt2, pn2):
    b, t, d = x1.shape
    ne = aff.shape[-1]
    nt = t // TOK_TILE

    def y_spec(e, j):
        return pl.BlockSpec((SLOT_BLK, d),
                            lambda i, k, blk_ref, base_ref: (blk_ref[((i * ne + e) * nt + k) * 2 + j], 0))

    tile = lambda i, k, blk_ref, base_ref: (i, k, 0)
    y2d = y.reshape(-1, d)
    return pl.pallas_call(
        _combine_win_kernel,
        grid_spec=pltpu.PrefetchScalarGridSpec(
            num_scalar_prefetch=2,
            grid=(b, nt),
            in_specs=[pl.BlockSpec((1, TOK_TILE, ne), tile),
                      pl.BlockSpec((1, TOK_TILE, ne), tile)]
                     + [y_spec(e, j) for e in range(ne) for j in range(2)]
                     + [pl.BlockSpec((1, TOK_TILE, d), tile),
                        pl.BlockSpec((1, 1, d), lambda i, k, blk_ref, base_ref: (i, 0, 0)),
                        pl.BlockSpec((1, d), lambda i, k, blk_ref, base_ref: (0, 0))],
            out_specs=pl.BlockSpec((1, TOK_TILE, d), tile)),
        out_shape=jax.ShapeDtypeStruct((b, t, d), F32),
        compiler_params=_params("parallel", "parallel"),
        name="combine_win",
    )(blk, base, pos_t, aff, *([y2d] * (2 * ne)), x1, gt2, pn2.reshape(1, d))


def _routing_tables(fill, cap):
    b, ne, t = fill.shape
    nblk = cap // SLOT_BLK
    ends = fill[:, :, TOK_TILE - 1::TOK_TILE].astype(jnp.int32)
    starts = jnp.concatenate([jnp.zeros((b, ne, 1), jnp.int32), ends[:, :, :-1]], axis=-1)
    sparse = jnp.all(ends - starts <= SLOT_BLK)
    first = jnp.minimum(starts // SLOT_BLK, nblk - 1)
    second = first + 1
    group = (jnp.arange(ne)[None, :, None] * b + jnp.arange(b)[:, None, None]) * nblk
    blk = jnp.stack([group + first, group + jnp.minimum(second, nblk - 1)], axis=-1)
    base = jnp.stack([first * SLOT_BLK, jnp.where(second < nblk, second * SLOT_BLK, -(1 << 20))], axis=-1)
    return sparse, (starts // 8 * 8).reshape(-1), blk.reshape(-1), base.reshape(-1)


def _rope_tables(t):
    pos = jnp.arange(t)
    r = (pos // GRID_W).astype(F32)
    cl = (pos % GRID_W).astype(F32)
    quarter = HEAD_DIM // 4
    inv = ROPE_THETA ** (-jnp.arange(quarter, dtype=F32) / quarter)
    ang_r = r[:, None] * inv
    ang_c = cl[:, None] * inv
    ang = jnp.concatenate([ang_r, ang_r, ang_c, ang_c], axis=-1)
    sign = jnp.where((jnp.arange(HEAD_DIM) % (HEAD_DIM // 2)) < quarter, -1.0, 1.0).astype(F32)
    return jnp.cos(ang), jnp.sin(ang) * sign


def _kv_heads(h, w_in, q_width, k_norm, rope):
    k_a = _proj(h, w_in, q_width, ATT_KV_W, mode="norm_rope", out_dtype=BF16, tn=ATT_KV_W,
                rope=rope, gain=k_norm)
    v_a = _proj(h, w_in, q_width + ATT_KV_W, ATT_KV_W, mode="plain", out_dtype=BF16, tn=ATT_KV_W)
    k_r = _proj(h, w_in, q_width + 2 * ATT_KV_W, RET_QK_W, mode="rope", out_dtype=BF16, tn=RET_QK_W,
                rope=rope, pre_scale=RET_QK_DIM ** -0.5)
    v_r = _proj(h, w_in, q_width + 2 * ATT_KV_W + RET_QK_W, RET_V_W, mode="plain", out_dtype=BF16,
                tn=512)
    return k_a, v_a, k_r, v_r


def kernel(x, c, ctx, c_ctx, w_mod, b_mod, pre_norm1, post_norm1, pre_norm2, post_norm2, w_in, q_norm,
           k_norm, ret_decay, ret_gn, w_o_att, w_o_ret, w_out, w_router, w_gate, w_up, w_down):
    b, t, d = x.shape
    n_ctx = ctx.shape[1]
    depth = w_mod.shape[0]
    q_width = ATT_W + RET_QK_W + RET_V_W + 2 * d
    cap = EC_FACTOR * t // N_EXPERTS
    rope_lat = tuple(jnp.tile(tbl, (b, 1)) for tbl in _rope_tables(t))
    rope_ctx = (jnp.ones((b * n_ctx, HEAD_DIM), F32), jnp.zeros((b * n_ctx, HEAD_DIM), F32))
    cs = jnp.zeros((8, d), F32).at[:b].set(c).at[b].set(c_ctx)
    xc = ctx
    for layer in range(depth):
        assert layer == depth - 1, "context-stream update between layers is not implemented"
        mod = _mod_vectors(cs, w_mod[layer], b_mod[layer])
        sh1, sc1, gt1, sh2, sc2, gt2 = [m[:b, None, :] for m in jnp.split(mod, 6, axis=-1)]
        csh1, csc1 = [jnp.broadcast_to(m[b][None, None, :], (b, 1, d)) for m in jnp.split(mod, 6, axis=-1)[:2]]
        dec = -jax.nn.softplus(ret_decay[layer].astype(F32))
        wl = w_in[layer]

        hc = _prenorm(xc, pre_norm1[layer], csh1, csc1).reshape(b * n_ctx, d)
        kc_a, vc_a, kc_r, vc_r = _kv_heads(hc, wl, q_width, k_norm[layer], rope_ctx)
        s0f, s0b = _ctx_states(dec, kc_r.reshape(b, n_ctx, RET_QK_W), vc_r.reshape(b, n_ctx, RET_V_W))

        h = _prenorm(x, pre_norm1[layer], sh1, sc1).reshape(b * t, d)
        q_a = _proj(h, wl, 0, ATT_W, mode="norm_rope", out_dtype=BF16, tn=512, rope=rope_lat,
                    gain=q_norm[layer])
        q_r = _proj(h, wl, ATT_W, RET_QK_W, mode="rope", out_dtype=BF16, tn=512, rope=rope_lat)
        zg = _proj(h, wl, ATT_W + RET_QK_W, RET_V_W + 2 * d, mode="plain", out_dtype=F32, tn=512)
        k_a, v_a, k_r, v_r = _kv_heads(h, wl, q_width, k_norm[layer], rope_lat)

        keys = jnp.concatenate([kc_a.reshape(b, n_ctx, ATT_KV_W), k_a.reshape(b, t, ATT_KV_W)], axis=1)
        vals = jnp.concatenate([vc_a.reshape(b, n_ctx, ATT_KV_W), v_a.reshape(b, t, ATT_KV_W)], axis=1)
        kt = keys.reshape(b, n_ctx + t, ATT_KV_HEADS, HEAD_DIM).transpose(0, 2, 3, 1)
        o_att = _attention(q_a.reshape(b, t, ATT_W), kt, vals)

        o_f, o_b = _retention(dec, q_r.reshape(b, t, RET_QK_W), k_r.reshape(b, t, RET_QK_W),
                              v_r.reshape(b, t, RET_V_W), s0f, s0b)

        y = _merge1(o_att.reshape(b * t, ATT_W), o_f.reshape(b * t, RET_V_W), o_b.reshape(b * t, RET_V_W),
                    zg, ret_gn[layer], w_o_att[layer].astype(BF16), w_o_ret[layer].astype(BF16))
        x1, h2, aff = _merge2(y, w_out[layer].astype(BF16), x, gt1, post_norm1[layer], pre_norm2[layer],
                              sh2, sc2, w_router[layer])

        pos, fill = _topk_slots(aff.transpose(0, 2, 1), cap)
        pos_t = pos.transpose(0, 2, 1)
        sparse, w0, blk, base = _routing_tables(fill, cap)
        xg = lax.cond(sparse,
                      lambda: _gather_win(w0, pos, h2, cap),
                      lambda: _gather(pos, h2, cap))
        yg = _ffn(xg, w_gate[layer], w_up[layer], w_down[layer])
        x = lax.cond(sparse,
                     lambda: _combine_win(blk, base, pos_t, aff, yg, x1, gt2, post_norm2[layer]),
                     lambda: _combine(pos_t, aff, yg, x1, gt2, post_norm2[layer], cap))
    return x
```

```python
import functools

import jax
import jax.numpy as jnp
import numpy as np
from jax import lax
from jax.experimental import pallas as pl
from jax.experimental.pallas import tpu as pltpu

F32 = jnp.float32
BF16 = jnp.bfloat16

GRID_W = 64
HEAD_DIM = 128
ATT_HEADS = 8
ATT_KV_HEADS = 2
ATT_GROUP = ATT_HEADS // ATT_KV_HEADS
RET_HEADS = 4
RET_QK_DIM = 128
RET_V_DIM = 256
ROPE_THETA = 10000.0
N_EXPERTS = 16
EC_FACTOR = 2
EPS = 1e-6

ATT_W = ATT_HEADS * HEAD_DIM
ATT_KV_W = ATT_KV_HEADS * HEAD_DIM
RET_QK_W = RET_HEADS * RET_QK_DIM
RET_V_W = RET_HEADS * RET_V_DIM

LANES = 128
VMEM_LIMIT = 56 * 1024 * 1024
RET_CHUNK = 256
MIN_NORMAL_F32_BITS = 0x00800000
LOG2_E = 1.4426950408889634
TOK_TILE = 256
SLOT_BLK = 64
GATHER_WIN = SLOT_BLK + 16


def _blk(pref, n):
    return pref if n % pref == 0 else n


def _params(*sem):
    return pltpu.CompilerParams(dimension_semantics=sem, vmem_limit_bytes=VMEM_LIMIT)


def _sigmoid(x):
    return 0.5 * jnp.tanh(0.5 * x) + 0.5


def _mod_kernel(s_ref, w_ref, b_ref, o_ref):
    s = s_ref[...]
    s = s * _sigmoid(s)
    o_ref[...] = jnp.dot(s, w_ref[...], precision=lax.Precision.HIGHEST,
                         preferred_element_type=F32) + b_ref[...]


def _mod_vectors(cs, w_mod, b_mod):
    rows, d = cs.shape
    n = w_mod.shape[1]
    tn = _blk(1024, n)
    return pl.pallas_call(
        _mod_kernel,
        grid=(n // tn,),
        in_specs=[pl.BlockSpec((rows, d), lambda j: (0, 0)),
                  pl.BlockSpec((d, tn), lambda j: (0, j)),
                  pl.BlockSpec((1, tn), lambda j: (0, j))],
        out_specs=pl.BlockSpec((rows, tn), lambda j: (0, j)),
        out_shape=jax.ShapeDtypeStruct((rows, n), F32),
        compiler_params=_params("parallel"),
        name="mod",
    )(cs, w_mod, b_mod.reshape(1, n))


def _prenorm_kernel(x_ref, g_ref, sh_ref, sc_ref, o_ref):
    x = x_ref[0]
    y = x * lax.rsqrt(jnp.mean(x * x, axis=-1, keepdims=True) + EPS) * g_ref[...]
    o_ref[0] = (y * (1.0 + sc_ref[0]) + sh_ref[0]).astype(o_ref.dtype)


def _prenorm(x, gain, shift, scale):
    b, t, d = x.shape
    tm = _blk(512, t)
    return pl.pallas_call(
        _prenorm_kernel,
        grid=(b, t // tm),
        in_specs=[pl.BlockSpec((1, tm, d), lambda i, j: (i, j, 0)),
                  pl.BlockSpec((1, d), lambda i, j: (0, 0)),
                  pl.BlockSpec((1, 1, d), lambda i, j: (i, 0, 0)),
                  pl.BlockSpec((1, 1, d), lambda i, j: (i, 0, 0))],
        out_specs=pl.BlockSpec((1, tm, d), lambda i, j: (i, j, 0)),
        out_shape=jax.ShapeDtypeStruct((b, t, d), BF16),
        compiler_params=_params("parallel", "parallel"),
        name="prenorm",
    )(x, gain.reshape(1, d), shift, scale)


def _split_bf16(v):
    hi = v.astype(BF16)
    return [hi, (v - hi.astype(F32)).astype(BF16)]


def _rotate_half_matrix():
    quarter = HEAD_DIM // 4
    r = np.zeros((HEAD_DIM, HEAD_DIM), np.float32)
    for i in range(HEAD_DIM):
        if i % (2 * quarter) < quarter:
            r[i + quarter, i] = -1.0
        else:
            r[i - quarter, i] = 1.0
    return r


def _head_mix_matrix(mode):
    rot = _rotate_half_matrix()
    if mode == "rope":
        return jnp.asarray(np.concatenate([rot, rot], axis=0), BF16)
    ones = np.ones((HEAD_DIM, HEAD_DIM), np.float32)
    zero = np.zeros((HEAD_DIM, HEAD_DIM), np.float32)
    return jnp.asarray(np.block([[ones, zero], [ones, zero], [zero, rot], [zero, rot]]), BF16)


def _proj_kernel(*refs, mode, pre_scale):
    if mode == "plain":
        h_ref, w_ref, o_ref = refs
    elif mode == "rope":
        h_ref, w_ref, mix_ref, cos_ref, sin_ref, o_ref = refs
    else:
        h_ref, w_ref, gain_ref, mix_ref, cos_ref, sin_ref, o_ref = refs
    acc = jnp.dot(h_ref[...], w_ref[...].astype(BF16), preferred_element_type=F32)
    if mode == "plain":
        o_ref[...] = acc.astype(o_ref.dtype)
        return
    cos = cos_ref[...]
    sin = sin_ref[...]
    for hh in range(acc.shape[1] // HEAD_DIM):
        xh = acc[:, hh * HEAD_DIM:(hh + 1) * HEAD_DIM]
        if mode == "norm_rope":
            xg = xh * gain_ref[...]
            mixed = jnp.dot(jnp.concatenate(_split_bf16(xh * xh) + _split_bf16(xg), axis=1), mix_ref[...],
                            preferred_element_type=F32)
            inv = lax.rsqrt(mixed[:, :HEAD_DIM] * (1.0 / HEAD_DIM) + EPS)
            out = (xg * cos + mixed[:, HEAD_DIM:] * sin) * inv
        else:
            xs = xh * pre_scale if pre_scale != 1.0 else xh
            rot = jnp.dot(jnp.concatenate(_split_bf16(xs), axis=1), mix_ref[...], preferred_element_type=F32)
            out = xs * cos + rot * sin
        o_ref[:, hh * HEAD_DIM:(hh + 1) * HEAD_DIM] = out.astype(o_ref.dtype)


def _proj(h, w_in, col0, width, *, mode, out_dtype, tn, rope=None, gain=None, pre_scale=1.0):
    r, d = h.shape
    tm = _blk(2048, r)
    assert col0 % tn == 0 and width % tn == 0
    c0 = col0 // tn
    in_specs = [pl.BlockSpec((tm, d), lambda i, j: (i, 0)),
                pl.BlockSpec((d, tn), lambda i, j: (0, c0 + j))]
    args = [h, w_in]
    if mode == "norm_rope":
        in_specs.append(pl.BlockSpec((1, HEAD_DIM), lambda i, j: (0, 0)))
        args.append(gain.reshape(1, HEAD_DIM))
    if mode != "plain":
        cos, sin = rope
        assert cos.shape == (r, HEAD_DIM)
        mix = _head_mix_matrix(mode)
        in_specs.append(pl.BlockSpec(mix.shape, lambda i, j: (0, 0)))
        args.append(mix)
        in_specs += [pl.BlockSpec((tm, HEAD_DIM), lambda i, j: (i, 0)),
                     pl.BlockSpec((tm, HEAD_DIM), lambda i, j: (i, 0))]
        args += [cos, sin]
    return pl.pallas_call(
        functools.partial(_proj_kernel, mode=mode, pre_scale=pre_scale),
        grid=(r // tm, width // tn),
        in_specs=in_specs,
        out_specs=pl.BlockSpec((tm, tn), lambda i, j: (i, j)),
        out_shape=jax.ShapeDtypeStruct((r, width), out_dtype),
        compiler_params=_params("parallel", "parallel"),
        name="proj_" + mode,
    )(*args)


def _ctx_state_kernel(dec_ref, k_ref, v_ref, sf_ref, sb_ref):
    hh = pl.program_id(1)
    lf = dec_ref[0, hh]
    lb = dec_ref[1, hh]
    k = k_ref[0].astype(F32)
    v = v_ref[0]
    n = k.shape[0]
    pos = lax.broadcasted_iota(jnp.int32, k.shape, 0).astype(F32)
    kf = (k * jnp.exp((n - 1.0 - pos) * lf)).T.astype(BF16)
    kb = (k * jnp.exp(pos * lb)).T.astype(BF16)
    sf_ref[0, 0] = jnp.dot(kf, v, preferred_element_type=F32)
    sb_ref[0, 0] = jnp.dot(kb, v, preferred_element_type=F32)


def _ctx_states(dec, k_r, v_r):
    b, n, _ = k_r.shape
    spec_s = pl.BlockSpec((1, 1, RET_QK_DIM, RET_V_DIM), lambda i, j: (i, j, 0, 0))
    shape_s = jax.ShapeDtypeStruct((b, RET_HEADS, RET_QK_DIM, RET_V_DIM), F32)
    return pl.pallas_call(
        _ctx_state_kernel,
        grid=(b, RET_HEADS),
        in_specs=[pl.BlockSpec(memory_space=pltpu.SMEM),
                  pl.BlockSpec((1, n, RET_QK_DIM), lambda i, j: (i, 0, j)),
                  pl.BlockSpec((1, n, RET_V_DIM), lambda i, j: (i, 0, j))],
        out_specs=(spec_s, spec_s),
        out_shape=(shape_s, shape_s),
        compiler_params=_params("parallel", "parallel"),
        name="ctx_state",
    )(dec, k_r, v_r)


def _attn_kernel(q_ref, kt_ref, v_ref, o_ref, *, scale):
    kt = kt_ref[0, 0]
    v = v_ref[0]
    for g in range(ATT_GROUP):
        q = q_ref[0, :, g * HEAD_DIM:(g + 1) * HEAD_DIM]
        s = jnp.dot(q, kt, preferred_element_type=F32)
        m = jnp.max(s, axis=-1, keepdims=True)
        p = jnp.exp2((s - m) * (scale * LOG2_E))
        l = jnp.sum(p, axis=-1, keepdims=True)
        o = jnp.dot(p.astype(BF16), v, preferred_element_type=F32)
        o_ref[0, :, g * HEAD_DIM:(g + 1) * HEAD_DIM] = (o / l).astype(o_ref.dtype)


def _attention(q, kt, v):
    b, t, _ = q.shape
    s = kt.shape[-1]
    tq = _blk(256, t)
    gw = ATT_GROUP * HEAD_DIM
    return pl.pallas_call(
        functools.partial(_attn_kernel, scale=HEAD_DIM ** -0.5),
        grid=(b, ATT_KV_HEADS, t // tq),
        in_specs=[pl.BlockSpec((1, tq, gw), lambda i, j, n: (i, n, j)),
                  pl.BlockSpec((1, 1, HEAD_DIM, s), lambda i, j, n: (i, j, 0, 0)),
                  pl.BlockSpec((1, s, HEAD_DIM), lambda i, j, n: (i, 0, j))],
        out_specs=pl.BlockSpec((1, tq, gw), lambda i, j, n: (i, n, j)),
        out_shape=jax.ShapeDtypeStruct((b, t, ATT_W), BF16),
        compiler_params=_params("parallel", "parallel", "parallel"),
        name="attention",
    )(q, kt, v)


def _ret_kernel(dec_ref, qf_ref, kf_ref, vf_ref, qb_ref, kb_ref, vb_ref, s0f_ref, s0b_ref,
                of_ref, ob_ref, sf_ref, sb_ref, mf_ref, mb_ref):
    n = pl.program_id(1)
    c = qf_ref.shape[1]

    @pl.when(n == 0)
    def _():
        sf_ref[...] = s0f_ref[0]
        sb_ref[...] = s0b_ref[0]
        ri = lax.broadcasted_iota(jnp.int32, (c, c), 0)
        ci = lax.broadcasted_iota(jnp.int32, (c, c), 1)
        d = (ri - ci).astype(F32)
        for hh in range(RET_HEADS):
            mf_ref[hh] = jnp.where(d >= 0, jnp.exp(jnp.maximum(d, 0.0) * dec_ref[0, hh]), 0.0)
            mb_ref[hh] = jnp.where(d <= 0, jnp.exp(jnp.maximum(-d, 0.0) * dec_ref[1, hh]), 0.0)

    nt = (((1,), (1,)), ((), ()))
    row_v = lax.broadcasted_iota(jnp.int32, (c, RET_V_DIM), 0).astype(F32)
    row_k = lax.broadcasted_iota(jnp.int32, (c, RET_QK_DIM), 0).astype(F32)

    def sweep(q_ref, k_ref, v_ref, o_ref, s_ref, m_ref, hh, lg, wq_age, wk_age):
        qk = slice(hh * RET_QK_DIM, (hh + 1) * RET_QK_DIM)
        vv = slice(hh * RET_V_DIM, (hh + 1) * RET_V_DIM)
        q = q_ref[0, :, qk]
        k = k_ref[0, :, qk]
        v = v_ref[0, :, vv]
        s = lax.dot_general(q, k, nt, preferred_element_type=F32) * m_ref[hh]
        o_in = jnp.dot(s.astype(BF16), v, preferred_element_type=F32)
        state = s_ref[hh]
        o_x = jnp.dot(q, state.astype(BF16), preferred_element_type=F32) * jnp.exp(wq_age * lg)
        o_ref[0, :, vv] = o_in + o_x
        kw = (k.astype(F32) * jnp.exp(wk_age * lg)).T.astype(BF16)
        g_chunk = jnp.exp(jnp.full((1, RET_V_DIM), c * lg, F32))
        s_ref[hh] = g_chunk * state + jnp.dot(kw, v, preferred_element_type=F32)

    for hh in range(RET_HEADS):
        sweep(qf_ref, kf_ref, vf_ref, of_ref, sf_ref, mf_ref, hh, dec_ref[0, hh], row_v + 1.0, c - 1.0 - row_k)
        sweep(qb_ref, kb_ref, vb_ref, ob_ref, sb_ref, mb_ref, hh, dec_ref[1, hh], c - row_v, row_k)


def _retention(dec, q, k, v, s0f, s0b):
    b, t, _ = q.shape
    c = _blk(RET_CHUNK, t)
    nc = t // c
    fwd = lambda i, n: (i, n, 0)
    bwd = lambda i, n: (i, nc - 1 - n, 0)
    spec_s = pl.BlockSpec((1, RET_HEADS, RET_QK_DIM, RET_V_DIM), lambda i, n: (i, 0, 0, 0))
    out_shape = jax.ShapeDtypeStruct((b, t, RET_V_W), F32)
    return pl.pallas_call(
        _ret_kernel,
        grid=(b, nc),
        in_specs=[pl.BlockSpec(memory_space=pltpu.SMEM),
                  pl.BlockSpec((1, c, RET_QK_W), fwd),
                  pl.BlockSpec((1, c, RET_QK_W), fwd),
                  pl.BlockSpec((1, c, RET_V_W), fwd),
                  pl.BlockSpec((1, c, RET_QK_W), bwd),
                  pl.BlockSpec((1, c, RET_QK_W), bwd),
                  pl.BlockSpec((1, c, RET_V_W), bwd),
                  spec_s, spec_s],
        out_specs=(pl.BlockSpec((1, c, RET_V_W), fwd), pl.BlockSpec((1, c, RET_V_W), bwd)),
        out_shape=(out_shape, out_shape),
        scratch_shapes=[pltpu.VMEM((RET_HEADS, RET_QK_DIM, RET_V_DIM), F32),
                        pltpu.VMEM((RET_HEADS, RET_QK_DIM, RET_V_DIM), F32),
                        pltpu.VMEM((RET_HEADS, c, c), F32),
                        pltpu.VMEM((RET_HEADS, c, c), F32)],
        compiler_params=_params("parallel", "arbitrary"),
        name="retention",
    )(dec, q, k, v, q, k, v, s0f, s0b)


def _merge1_kernel(oa_ref, of_ref, ob_ref, g_ref, ga_ref, gr_ref, gn_ref, wa_ref, wr_ref,
                   y_ref, oret_ref):
    @pl.when(pl.program_id(1) == 0)
    def _():
        o = of_ref[...] + ob_ref[...]
        for hh in range(RET_HEADS):
            sl = slice(hh * RET_V_DIM, (hh + 1) * RET_V_DIM)
            oh = o[:, sl]
            dlt = oh - jnp.mean(oh, axis=-1, keepdims=True)
            yh = dlt * lax.rsqrt(jnp.mean(dlt * dlt, axis=-1, keepdims=True) + EPS)
            g = g_ref[:, sl]
            oret_ref[:, sl] = (yh * gn_ref[:, sl] * (g * _sigmoid(g))).astype(BF16)

    ya = jnp.dot(oa_ref[...], wa_ref[...], preferred_element_type=F32)
    yr = jnp.dot(oret_ref[...], wr_ref[...], preferred_element_type=F32)
    y_ref[...] = (_sigmoid(ga_ref[...]) * ya + _sigmoid(gr_ref[...]) * yr).astype(y_ref.dtype)


def _merge1(o_att, o_f, o_b, zg, gn, wa, wr):
    r = o_att.shape[0]
    d = wa.shape[1]
    tm = _blk(512, r)
    tn = _blk(1024, d)
    ga0 = RET_V_W // tn
    gr0 = (RET_V_W + d) // tn
    row = lambda i, j: (i, 0)
    return pl.pallas_call(
        _merge1_kernel,
        grid=(r // tm, d // tn),
        in_specs=[pl.BlockSpec((tm, ATT_W), row),
                  pl.BlockSpec((tm, RET_V_W), row),
                  pl.BlockSpec((tm, RET_V_W), row),
                  pl.BlockSpec((tm, RET_V_W), row),
                  pl.BlockSpec((tm, tn), lambda i, j: (i, ga0 + j)),
                  pl.BlockSpec((tm, tn), lambda i, j: (i, gr0 + j)),
                  pl.BlockSpec((1, RET_V_W), lambda i, j: (0, 0)),
                  pl.BlockSpec((ATT_W, tn), lambda i, j: (0, j)),
                  pl.BlockSpec((RET_V_W, tn), lambda i, j: (0, j))],
        out_specs=pl.BlockSpec((tm, tn), lambda i, j: (i, j)),
        out_shape=jax.ShapeDtypeStruct((r, d), BF16),
        scratch_shapes=[pltpu.VMEM((tm, RET_V_W), BF16)],
        compiler_params=_params("parallel", "arbitrary"),
        name="merge1",
    )(o_att, o_f, o_b, zg, zg, zg, gn.reshape(1, RET_V_W), wa, wr)


def _merge2_kernel(y_ref, w_ref, x_ref, gt_ref, pn1_ref, pn2_ref, sh_ref, sc_ref, wr_ref,
                   x1_ref, h2_ref, aff_ref):
    yy = jnp.dot(y_ref[0], w_ref[...], preferred_element_type=F32)
    r = yy * lax.rsqrt(jnp.mean(yy * yy, axis=-1, keepdims=True) + EPS) * pn1_ref[...]
    x1 = x_ref[0] + gt_ref[0] * r
    x1_ref[0] = x1
    h2 = x1 * lax.rsqrt(jnp.mean(x1 * x1, axis=-1, keepdims=True) + EPS) * pn2_ref[...]
    h2 = h2 * (1.0 + sc_ref[0]) + sh_ref[0]
    h_hi = h2.astype(BF16)
    h2_ref[0] = h_hi
    h_lo = (h2 - h_hi.astype(F32)).astype(BF16)
    wr = wr_ref[...]
    w_hi = wr.astype(BF16)
    w_lo = (wr - w_hi.astype(F32)).astype(BF16)
    ne = wr.shape[1]
    r_hi = jnp.dot(h_hi, jnp.concatenate([w_hi, w_lo], axis=1), preferred_element_type=F32)
    logits = r_hi[:, :ne] + r_hi[:, ne:] + jnp.dot(h_lo, w_hi, preferred_element_type=F32)
    e = jnp.exp(logits - jnp.max(logits, axis=-1, keepdims=True))
    aff_ref[0] = e / jnp.sum(e, axis=-1, keepdims=True)


def _merge2(y, w_out, x, gt1, pn1, pn2, sh2, sc2, w_router):
    b, t, d = x.shape
    tm = _blk(512, t)
    ne = w_router.shape[1]
    tile = lambda i, j: (i, j, 0)
    vec = lambda i, j: (0, 0)
    per_b = lambda i, j: (i, 0, 0)
    return pl.pallas_call(
        _merge2_kernel,
        grid=(b, t // tm),
        in_specs=[pl.BlockSpec((1, tm, d), tile),
                  pl.BlockSpec((d, d), vec),
                  pl.BlockSpec((1, tm, d), tile),
                  pl.BlockSpec((1, 1, d), per_b),
                  pl.BlockSpec((1, d), vec),
                  pl.BlockSpec((1, d), vec),
                  pl.BlockSpec((1, 1, d), per_b),
                  pl.BlockSpec((1, 1, d), per_b),
                  pl.BlockSpec((d, ne), vec)],
        out_specs=(pl.BlockSpec((1, tm, d), tile),
                   pl.BlockSpec((1, tm, d), tile),
                   pl.BlockSpec((1, tm, ne), tile)),
        out_shape=(jax.ShapeDtypeStruct((b, t, d), F32),
                   jax.ShapeDtypeStruct((b, t, d), BF16),
                   jax.ShapeDtypeStruct((b, t, ne), F32)),
        compiler_params=_params("parallel", "parallel"),
        name="merge2",
    )(y.reshape(b, t, d), w_out, x, gt1, pn1.reshape(1, d), pn2.reshape(1, d), sh2, sc2, w_router)


def _lane_cumsum(x, tri):
    off = jnp.zeros((x.shape[0], 1), F32)
    parts = []
    for cidx in range(x.shape[1] // LANES):
        xc = x[:, cidx * LANES:(cidx + 1) * LANES].astype(BF16)
        cs = jnp.dot(xc, tri, preferred_element_type=F32) + off
        parts.append(cs)
        off = cs[:, LANES - 1:LANES]
    return jnp.concatenate(parts, axis=1)


def _topk_kernel(a_ref, pos_ref, fill_ref, *, cap):
    a = a_ref[0]
    bits = jnp.zeros((a.shape[0], 1), jnp.int32)
    for bit in range(30, -1, -1):
        cand = bits | (1 << bit)
        cnt = jnp.sum(jnp.where(a >= lax.bitcast_convert_type(cand, F32), 1.0, 0.0), axis=1, keepdims=True)
        bits = jnp.where(cnt >= cap, cand, bits)
    bits = jnp.where(bits < MIN_NORMAL_F32_BITS, 0, bits)
    thr = lax.bitcast_convert_type(bits, F32)
    ri = lax.broadcasted_iota(jnp.int32, (LANES, LANES), 0)
    ci = lax.broadcasted_iota(jnp.int32, (LANES, LANES), 1)
    tri = jnp.where(ri <= ci, 1.0, 0.0).astype(BF16)
    gt = a > thr
    eq = a == thr
    n_gt = jnp.sum(jnp.where(gt, 1.0, 0.0), axis=1, keepdims=True)
    eq_rank = _lane_cumsum(jnp.where(eq, 1.0, 0.0), tri)
    sel = jnp.where(gt, 1.0, jnp.where(eq, jnp.where(eq_rank <= cap - n_gt, 1.0, 0.0), 0.0))
    filled = _lane_cumsum(sel, tri)
    pos_ref[0] = filled * sel
    fill_ref[0] = filled


def _topk_slots(aff_t, cap):
    b, ne, t = aff_t.shape
    spec = pl.BlockSpec((1, ne, t), lambda i: (i, 0, 0))
    shape = jax.ShapeDtypeStruct((b, ne, t), F32)
    return pl.pallas_call(
        functools.partial(_topk_kernel, cap=cap),
        grid=(b,),
        in_specs=[spec],
        out_specs=(spec, spec),
        out_shape=(shape, shape),
        compiler_params=_params("parallel"),
        name="topk",
    )(aff_t)


def _gather_kernel(pos_ref, h_ref, o_ref):
    cap = o_ref.shape[1]
    t = h_ref.shape[1]
    tc = _blk(1024, t)
    slot = lax.broadcasted_iota(jnp.int32, (cap, tc), 0).astype(F32) + 1.0
    acc = None
    for cidx in range(t // tc):
        row = pos_ref[0, 0, :, cidx * tc:(cidx + 1) * tc]
        onehot = jnp.where(row == slot, 1.0, 0.0).astype(BF16)
        part = jnp.dot(onehot, h_ref[0, cidx * tc:(cidx + 1) * tc, :], preferred_element_type=F32)
        acc = part if acc is None else acc + part
    o_ref[0] = acc.astype(o_ref.dtype)


def _gather(pos, h2, cap):
    b, ne, t = pos.shape
    d = h2.shape[-1]
    dn = _blk(1024, d)
    return pl.pallas_call(
        _gather_kernel,
        grid=(b, d // dn, ne),
        in_specs=[pl.BlockSpec((1, 1, 1, t), lambda i, n, e: (i, e, 0, 0)),
                  pl.BlockSpec((1, t, dn), lambda i, n, e: (i, 0, n))],
        out_specs=pl.BlockSpec((1, cap, dn), lambda i, n, e: (e, i, n)),
        out_shape=jax.ShapeDtypeStruct((ne, b * cap, d), BF16),
        compiler_params=_params("parallel", "parallel", "parallel"),
        name="gather",
    )(pos.reshape(b, ne, 1, t), h2)


def _gather_win_kernel(w0_ref, pos_ref, h_ref, o_ref, acc_ref, lhs_ref):
    b = pl.program_id(0)
    ne, cap = o_ref.shape[0], o_ref.shape[1]
    nt = h_ref.shape[1] // TOK_TILE
    slot1 = lax.broadcasted_iota(jnp.int32, (GATHER_WIN, TOK_TILE), 0) + 1

    @pl.when(pl.program_id(1) == 0)
    def _():
        def build(k, carry):
            for e in range(ne):
                w0 = w0_ref[(b * ne + e) * nt + k]
                row = pos_ref[0, pl.ds(e * nt + k, 1), :]
                onehot = jnp.where(row == (slot1 + w0).astype(F32), 1.0, 0.0)
                lhs_ref[k, e * GATHER_WIN:(e + 1) * GATHER_WIN, :] = onehot.astype(BF16)
            return carry
        lax.fori_loop(0, nt, build, 0)

    acc_ref[...] = jnp.zeros_like(acc_ref)

    def tile(k, carry):
        t0 = pl.multiple_of(k * TOK_TILE, TOK_TILE)
        res = jnp.dot(lhs_ref[k], h_ref[0, pl.ds(t0, TOK_TILE), :], preferred_element_type=F32)
        for e in range(ne):
            w0 = pl.multiple_of(w0_ref[(b * ne + e) * nt + k], 8)
            acc_ref[e, pl.ds(w0, GATHER_WIN), :] += res[e * GATHER_WIN:(e + 1) * GATHER_WIN]
        return carry
    lax.fori_loop(0, nt, tile, 0)
    o_ref[...] = acc_ref[:, :cap, :].astype(o_ref.dtype)


def _gather_win(w0, pos, h2, cap):
    b, ne, t = pos.shape
    d = h2.shape[-1]
    nt = t // TOK_TILE
    dn = _blk(256, d)
    return pl.pallas_call(
        _gather_win_kernel,
        grid_spec=pltpu.PrefetchScalarGridSpec(
            num_scalar_prefetch=1,
            grid=(b, d // dn),
            in_specs=[pl.BlockSpec((1, ne * nt, TOK_TILE), lambda i, n, w: (i, 0, 0)),
                      pl.BlockSpec((1, t, dn), lambda i, n, w: (i, 0, n))],
            out_specs=pl.BlockSpec((ne, cap, dn), lambda i, n, w: (0, i, n)),
            scratch_shapes=[pltpu.VMEM((ne, cap + GATHER_WIN, dn), F32),
                            pltpu.VMEM((nt, ne * GATHER_WIN, TOK_TILE), BF16)]),
        out_shape=jax.ShapeDtypeStruct((ne, b * cap, d), BF16),
        compiler_params=_params("parallel", "arbitrary"),
        name="gather_win",
    )(w0, pos.reshape(b, ne * nt, TOK_TILE), h2)


def _ffn_kernel(x_ref, wg_ref, wu_ref, wd_ref, o_ref, h_ref, wgu_ref, *, nf):
    j = pl.program_id(2)
    tf = wg_ref.shape[2]

    @pl.when(j < nf)
    def _():
        wgu_ref[:, :tf] = wg_ref[0].astype(BF16)
        wgu_ref[:, tf:] = wu_ref[0].astype(BF16)
        au = jnp.dot(x_ref[0], wgu_ref[...], preferred_element_type=F32)
        a = au[:, :tf]
        h_ref[j] = (a * _sigmoid(a) * au[:, tf:]).astype(BF16)

    @pl.when(j >= nf)
    def _():
        acc = None
        for cidx in range(nf):
            part = jnp.dot(h_ref[cidx], wd_ref[0, cidx * tf:(cidx + 1) * tf, :].astype(BF16),
                           preferred_element_type=F32)
            acc = part if acc is None else acc + part
        o_ref[0] = acc.astype(o_ref.dtype)


def _ffn(xg, w_gate, w_up, w_down):
    ne, rows, d = xg.shape
    ff = w_gate.shape[-1]
    tm = _blk(2048, rows)
    tf = _blk(256, ff)
    tn = _blk(256, d)
    nf = ff // tf
    return pl.pallas_call(
        functools.partial(_ffn_kernel, nf=nf),
        grid=(ne, rows // tm, nf + d // tn),
        in_specs=[pl.BlockSpec((1, tm, d), lambda e, m, j: (e, m, 0)),
                  pl.BlockSpec((1, d, tf), lambda e, m, j: (e, 0, jnp.minimum(j, nf - 1))),
                  pl.BlockSpec((1, d, tf), lambda e, m, j: (e, 0, jnp.minimum(j, nf - 1))),
                  pl.BlockSpec((1, ff, tn), lambda e, m, j: (e, 0, jnp.maximum(j - nf, 0)))],
        out_specs=pl.BlockSpec((1, tm, tn), lambda e, m, j: (e, m, jnp.maximum(j - nf, 0))),
        out_shape=jax.ShapeDtypeStruct((ne, rows, d), BF16),
        scratch_shapes=[pltpu.VMEM((nf, tm, tf), BF16), pltpu.VMEM((d, 2 * tf), BF16)],
        compiler_params=_params("parallel", "parallel", "arbitrary"),
        name="ffn",
    )(xg, w_gate, w_up, w_down)


def _combine_kernel(pos_ref, aff_ref, y_ref, x1_ref, gt_ref, pn_ref, o_ref, acc_ref):
    e = pl.program_id(2)
    tt, ne = pos_ref.shape[1], pos_ref.shape[2]
    cap = y_ref.shape[1]
    pick = lax.broadcasted_iota(jnp.int32, (tt, ne), 1) == e
    pcol = jnp.sum(jnp.where(pick, pos_ref[0], 0.0), axis=1, keepdims=True)
    acol = jnp.sum(jnp.where(pick, aff_ref[0], 0.0), axis=1, keepdims=True)
    slot = lax.broadcasted_iota(jnp.int32, (tt, cap), 1).astype(F32) + 1.0
    onehot = jnp.where(pcol == slot, 1.0, 0.0).astype(BF16)
    z = jnp.dot(onehot, y_ref[0], preferred_element_type=F32) * acol

    @pl.when(e == 0)
    def _():
        acc_ref[...] = z

    @pl.when(e > 0)
    def _():
        acc_ref[...] += z

    @pl.when(e == pl.num_programs(2) - 1)
    def _():
        y2 = acc_ref[...]
        r = y2 * lax.rsqrt(jnp.mean(y2 * y2, axis=-1, keepdims=True) + EPS) * pn_ref[...]
        o_ref[0] = x1_ref[0] + gt_ref[0] * r


def _combine(pos_t, aff, y, x1, gt2, pn2, cap):
    b, t, d = x1.shape
    ne = aff.shape[-1]
    tt = _blk(512, t)
    return pl.pallas_call(
        _combine_kernel,
        grid=(b, t // tt, ne),
        in_specs=[pl.BlockSpec((1, tt, ne), lambda i, j, e: (i, j, 0)),
                  pl.BlockSpec((1, tt, ne), lambda i, j, e: (i, j, 0)),
                  pl.BlockSpec((1, cap, d), lambda i, j, e: (e, i, 0)),
                  pl.BlockSpec((1, tt, d), lambda i, j, e: (i, j, 0)),
                  pl.BlockSpec((1, 1, d), lambda i, j, e: (i, 0, 0)),
                  pl.BlockSpec((1, d), lambda i, j, e: (0, 0))],
        out_specs=pl.BlockSpec((1, tt, d), lambda i, j, e: (i, j, 0)),
        out_shape=jax.ShapeDtypeStruct((b, t, d), F32),
        scratch_shapes=[pltpu.VMEM((tt, d), F32)],
        compiler_params=_params("parallel", "parallel", "arbitrary"),
        name="combine",
    )(pos_t, aff, y, x1, gt2, pn2.reshape(1, d))


def _combine_win_kernel(blk_ref, base_ref, pos_ref, aff_ref, *refs):
    del blk_ref
    ne = pos_ref.shape[2]
    y_refs = refs[:2 * ne]
    x1_ref, gt_ref, pn_ref, o_ref = refs[2 * ne:]
    b = pl.program_id(0)
    k = pl.program_id(1)
    nt = pl.num_programs(1)
    lane = lax.broadcasted_iota(jnp.int32, (TOK_TILE, 2 * SLOT_BLK), 1)
    second = lane >= SLOT_BLK
    pos = pos_ref[0]
    aff = aff_ref[0]
    cols = []
    for e in range(ne):
        at = ((b * ne + e) * nt + k) * 2
        slot = jnp.where(second, lane + (base_ref[at + 1] + 1 - SLOT_BLK), lane + (base_ref[at] + 1))
        cols.append(jnp.where(pos[:, e:e + 1] == slot.astype(F32), aff[:, e:e + 1], 0.0).astype(BF16))
    ywin = jnp.concatenate([r[...] for r in y_refs], axis=0)
    y2 = jnp.dot(jnp.concatenate(cols, axis=1), ywin, preferred_element_type=F32)
    r = y2 * lax.rsqrt(jnp.mean(y2 * y2, axis=-1, keepdims=True) + EPS) * pn_ref[...]
    o_ref[0] = x1_ref[0] + gt_ref[0] * r


def _combine_win(blk, base, pos_t, aff, y, x1, gt2, pn2):
    b, t, d = x1.shape
    ne = aff.shape[-1]
    nt = t // TOK_TILE

    def y_spec(e, j):
        return pl.BlockSpec((SLOT_BLK, d),
                            lambda i, k, blk_ref, base_ref: (blk_ref[((i * ne + e) * nt + k) * 2 + j], 0))

    tile = lambda i, k, blk_ref, base_ref: (i, k, 0)
    y2d = y.reshape(-1, d)
    return pl.pallas_call(
        _combine_win_kernel,
        grid_spec=pltpu.PrefetchScalarGridSpec(
            num_scalar_prefetch=2,
            grid=(b, nt),
            in_specs=[pl.BlockSpec((1, TOK_TILE, ne), tile),
                      pl.BlockSpec((1, TOK_TILE, ne), tile)]
                     + [y_spec(e, j) for e in range(ne) for j in range(2)]
                     + [pl.BlockSpec((1, TOK_TILE, d), tile),
                        pl.BlockSpec((1, 1, d), lambda i, k, blk_ref, base_ref: (i, 0, 0)),
                        pl.BlockSpec((1, d), lambda i, k, blk_ref, base_ref: (0, 0))],
            out_specs=pl.BlockSpec((1, TOK_TILE, d), tile)),
        out_shape=jax.ShapeDtypeStruct((b, t, d), F32),
        compiler_params=_params("parallel", "parallel"),
        name="combine_win",
    )(blk, base, pos_t, aff, *([y2d] * (2 * ne)), x1, gt2, pn2.reshape(1, d))


def _routing_tables(fill, cap):
    b, ne, t = fill.shape
    nblk = cap // SLOT_BLK
    ends = fill[:, :, TOK_TILE - 1::TOK_TILE].astype(jnp.int32)
    starts = jnp.concatenate([jnp.zeros((b, ne, 1), jnp.int32), ends[:, :, :-1]], axis=-1)
    sparse = jnp.all(ends - starts <= SLOT_BLK)
    first = jnp.minimum(starts // SLOT_BLK, nblk - 1)
    second = first + 1
    group = (jnp.arange(ne)[None, :, None] * b + jnp.arange(b)[:, None, None]) * nblk
    blk = jnp.stack([group + first, group + jnp.minimum(second, nblk - 1)], axis=-1)
    base = jnp.stack([first * SLOT_BLK, jnp.where(second < nblk, second * SLOT_BLK, -(1 << 20))], axis=-1)
    return sparse, (starts // 8 * 8).reshape(-1), blk.reshape(-1), base.reshape(-1)


def _rope_tables(t):
    pos = jnp.arange(t)
    r = (pos // GRID_W).astype(F32)
    cl = (pos % GRID_W).astype(F32)
    quarter = HEAD_DIM // 4
    inv = ROPE_THETA ** (-jnp.arange(quarter, dtype=F32) / quarter)
    ang_r = r[:, None] * inv
    ang_c = cl[:, None] * inv
    ang = jnp.concatenate([ang_r, ang_r, ang_c, ang_c], axis=-1)
    return jnp.cos(ang), jnp.sin(ang)


def _kv_heads(h, w_in, q_width, k_norm, rope):
    k_a = _proj(h, w_in, q_width, ATT_KV_W, mode="norm_rope", out_dtype=BF16, tn=ATT_KV_W,
                rope=rope, gain=k_norm)
    v_a = _proj(h, w_in, q_width + ATT_KV_W, ATT_KV_W, mode="plain", out_dtype=BF16, tn=ATT_KV_W)
    k_r = _proj(h, w_in, q_width + 2 * ATT_KV_W, RET_QK_W, mode="rope", out_dtype=BF16, tn=RET_QK_W,
                rope=rope, pre_scale=RET_QK_DIM ** -0.5)
    v_r = _proj(h, w_in, q_width + 2 * ATT_KV_W + RET_QK_W, RET_V_W, mode="plain", out_dtype=BF16,
                tn=512)
    return k_a, v_a, k_r, v_r


def kernel(x, c, ctx, c_ctx, w_mod, b_mod, pre_norm1, post_norm1, pre_norm2, post_norm2, w_in, q_norm,
           k_norm, ret_decay, ret_gn, w_o_att, w_o_ret, w_out, w_router, w_gate, w_up, w_down):
    b, t, d = x.shape
    n_ctx = ctx.shape[1]
    depth = w_mod.shape[0]
    q_width = ATT_W + RET_QK_W + RET_V_W + 2 * d
    cap = EC_FACTOR * t // N_EXPERTS
    rope_lat = tuple(jnp.tile(tbl, (b, 1)) for tbl in _rope_tables(t))
    rope_ctx = (jnp.ones((b * n_ctx, HEAD_DIM), F32), jnp.zeros((b * n_ctx, HEAD_DIM), F32))
    cs = jnp.zeros((8, d), F32).at[:b].set(c).at[b].set(c_ctx)
    xc = ctx
    for layer in range(depth):
        assert layer == depth - 1, "context-stream update between layers is not implemented"
        mod = _mod_vectors(cs, w_mod[layer], b_mod[layer])
        sh1, sc1, gt1, sh2, sc2, gt2 = [m[:b, None, :] for m in jnp.split(mod, 6, axis=-1)]
        csh1, csc1 = [jnp.broadcast_to(m[b][None, None, :], (b, 1, d)) for m in jnp.split(mod, 6, axis=-1)[:2]]
        dec = -jax.nn.softplus(ret_decay[layer].astype(F32))
        wl = w_in[layer]

        hc = _prenorm(xc, pre_norm1[layer], csh1, csc1).reshape(b * n_ctx, d)
        kc_a, vc_a, kc_r, vc_r = _kv_heads(hc, wl, q_width, k_norm[layer], rope_ctx)
        s0f, s0b = _ctx_states(dec, kc_r.reshape(b, n_ctx, RET_QK_W), vc_r.reshape(b, n_ctx, RET_V_W))

        h = _prenorm(x, pre_norm1[layer], sh1, sc1).reshape(b * t, d)
        q_a = _proj(h, wl, 0, ATT_W, mode="norm_rope", out_dtype=BF16, tn=512, rope=rope_lat,
                    gain=q_norm[layer])
        q_r = _proj(h, wl, ATT_W, RET_QK_W, mode="rope", out_dtype=BF16, tn=512, rope=rope_lat)
        zg = _proj(h, wl, ATT_W + RET_QK_W, RET_V_W + 2 * d, mode="plain", out_dtype=F32, tn=512)
        k_a, v_a, k_r, v_r = _kv_heads(h, wl, q_width, k_norm[layer], rope_lat)

        keys = jnp.concatenate([kc_a.reshape(b, n_ctx, ATT_KV_W), k_a.reshape(b, t, ATT_KV_W)], axis=1)
        vals = jnp.concatenate([vc_a.reshape(b, n_ctx, ATT_KV_W), v_a.reshape(b, t, ATT_KV_W)], axis=1)
        kt = keys.reshape(b, n_ctx + t, ATT_KV_HEADS, HEAD_DIM).transpose(0, 2, 3, 1)
        o_att = _attention(q_a.reshape(b, t, ATT_W), kt, vals)

        o_f, o_b = _retention(dec, q_r.reshape(b, t, RET_QK_W), k_r.reshape(b, t, RET_QK_W),
                              v_r.reshape(b, t, RET_V_W), s0f, s0b)

        y = _merge1(o_att.reshape(b * t, ATT_W), o_f.reshape(b * t, RET_V_W), o_b.reshape(b * t, RET_V_W),
                    zg, ret_gn[layer], w_o_att[layer].astype(BF16), w_o_ret[layer].astype(BF16))
        x1, h2, aff = _merge2(y, w_out[layer].astype(BF16), x, gt1, post_norm1[layer], pre_norm2[layer],
                              sh2, sc2, w_router[layer])

        pos, fill = _topk_slots(aff.transpose(0, 2, 1), cap)
        pos_t = pos.transpose(0, 2, 1)
        sparse, w0, blk, base = _routing_tables(fill, cap)
        xg = lax.cond(sparse,
                      lambda: _gather_win(w0, pos, h2, cap),
                      lambda: _gather(pos, h2, cap))
        yg = _ffn(xg, w_gate[layer], w_up[layer], w_down[layer])
        x = lax.cond(sparse,
                     lambda: _combine_win(blk, base, pos_t, aff, yg, x1, gt2, post_norm2[layer]),
                     lambda: _combine(pos_t, aff, yg, x1, gt2, post_norm2[layer], cap))
    return x
```

```python
import functools

import jax
import jax.numpy as jnp
import numpy as np
from jax import lax
from jax.experimental import pallas as pl
from jax.experimental.pallas import tpu as pltpu

F32 = jnp.float32
BF16 = jnp.bfloat16

GRID_W = 64
HEAD_DIM = 128
ATT_HEADS = 8
ATT_KV_HEADS = 2
ATT_GROUP = ATT_HEADS // ATT_KV_HEADS
RET_HEADS = 4
RET_QK_DIM = 128
RET_V_DIM = 256
ROPE_THETA = 10000.0
N_EXPERTS = 16
EC_FACTOR = 2
EPS = 1e-6

ATT_W = ATT_HEADS * HEAD_DIM
ATT_KV_W = ATT_KV_HEADS * HEAD_DIM
RET_QK_W = RET_HEADS * RET_QK_DIM
RET_V_W = RET_HEADS * RET_V_DIM

LANES = 128
VMEM_LIMIT = 56 * 1024 * 1024
RET_CHUNK = 256
MIN_NORMAL_F32_BITS = 0x00800000
LOG2_E = 1.4426950408889634
TOK_TILE = 256
SLOT_BLK = 64
GATHER_WIN = SLOT_BLK + 16


def _blk(pref, n):
    return pref if n % pref == 0 else n


def _params(*sem):
    return pltpu.CompilerParams(dimension_semantics=sem, vmem_limit_bytes=VMEM_LIMIT)


def _sigmoid(x):
    return 0.5 * jnp.tanh(0.5 * x) + 0.5


def _mod_kernel(s_ref, w_ref, b_ref, o_ref):
    s = s_ref[...]
    s = s * _sigmoid(s)
    o_ref[...] = jnp.dot(s, w_ref[...], precision=lax.Precision.HIGHEST,
                         preferred_element_type=F32) + b_ref[...]


def _mod_vectors(cs, w_mod, b_mod):
    rows, d = cs.shape
    n = w_mod.shape[1]
    tn = _blk(1024, n)
    return pl.pallas_call(
        _mod_kernel,
        grid=(n // tn,),
        in_specs=[pl.BlockSpec((rows, d), lambda j: (0, 0)),
                  pl.BlockSpec((d, tn), lambda j: (0, j)),
                  pl.BlockSpec((1, tn), lambda j: (0, j))],
        out_specs=pl.BlockSpec((rows, tn), lambda j: (0, j)),
        out_shape=jax.ShapeDtypeStruct((rows, n), F32),
        compiler_params=_params("parallel"),
        name="mod",
    )(cs, w_mod, b_mod.reshape(1, n))


def _prenorm_kernel(x_ref, g_ref, sh_ref, sc_ref, o_ref):
    x = x_ref[0]
    y = x * lax.rsqrt(jnp.mean(x * x, axis=-1, keepdims=True) + EPS) * g_ref[...]
    o_ref[0] = (y * (1.0 + sc_ref[0]) + sh_ref[0]).astype(o_ref.dtype)


def _prenorm(x, gain, shift, scale):
    b, t, d = x.shape
    tm = _blk(512, t)
    return pl.pallas_call(
        _prenorm_kernel,
        grid=(b, t // tm),
        in_specs=[pl.BlockSpec((1, tm, d), lambda i, j: (i, j, 0)),
                  pl.BlockSpec((1, d), lambda i, j: (0, 0)),
                  pl.BlockSpec((1, 1, d), lambda i, j: (i, 0, 0)),
                  pl.BlockSpec((1, 1, d), lambda i, j: (i, 0, 0))],
        out_specs=pl.BlockSpec((1, tm, d), lambda i, j: (i, j, 0)),
        out_shape=jax.ShapeDtypeStruct((b, t, d), BF16),
        compiler_params=_params("parallel", "parallel"),
        name="prenorm",
    )(x, gain.reshape(1, d), shift, scale)


def _split_bf16(v):
    hi = v.astype(BF16)
    return [hi, (v - hi.astype(F32)).astype(BF16)]


def _rotate_half_matrix():
    quarter = HEAD_DIM // 4
    r = np.zeros((HEAD_DIM, HEAD_DIM), np.float32)
    for i in range(HEAD_DIM):
        if i % (2 * quarter) < quarter:
            r[i + quarter, i] = -1.0
        else:
            r[i - quarter, i] = 1.0
    return r


def _head_mix_matrix(mode):
    rot = _rotate_half_matrix()
    if mode == "rope":
        return jnp.asarray(np.concatenate([rot, rot], axis=0), BF16)
    ones = np.ones((HEAD_DIM, HEAD_DIM), np.float32)
    zero = np.zeros((HEAD_DIM, HEAD_DIM), np.float32)
    return jnp.asarray(np.block([[ones, zero], [ones, zero], [zero, rot], [zero, rot]]), BF16)


def _proj_kernel(h_ref, w_ref, o_ref):
    o_ref[...] = jnp.dot(h_ref[...], w_ref[...].astype(BF16), preferred_element_type=F32).astype(o_ref.dtype)


def _proj(h, w_in, col0, width, *, out_dtype, tn):
    r, d = h.shape
    tm = _blk(2048, r)
    assert col0 % tn == 0 and width % tn == 0
    c0 = col0 // tn
    return pl.pallas_call(
        _proj_kernel,
        grid=(r // tm, width // tn),
        in_specs=[pl.BlockSpec((tm, d), lambda i, j: (i, 0)),
                  pl.BlockSpec((d, tn), lambda i, j: (0, c0 + j))],
        out_specs=pl.BlockSpec((tm, tn), lambda i, j: (i, j)),
        out_shape=jax.ShapeDtypeStruct((r, width), out_dtype),
        compiler_params=_params("parallel", "parallel"),
        name="proj_plain",
    )(h, w_in)


def _rope_head(xh, mode, gain, mix, cos, sin, pre_scale):
    if mode == "norm_rope":
        xg = xh * gain
        mixed = jnp.dot(jnp.concatenate(_split_bf16(xh * xh) + _split_bf16(xg), axis=1), mix,
                        preferred_element_type=F32)
        inv = lax.rsqrt(mixed[:, :HEAD_DIM] * (1.0 / HEAD_DIM) + EPS)
        return (xg * cos + mixed[:, HEAD_DIM:] * sin) * inv
    xs = xh * pre_scale if pre_scale != 1.0 else xh
    rot = jnp.dot(jnp.concatenate(_split_bf16(xs), axis=1), mix, preferred_element_type=F32)
    return xs * cos + rot * sin


def _proj_heads_kernel(h_ref, w_ref, gain_ref, mixn_ref, mixr_ref, cos_ref, sin_ref, *o_refs, plan):
    acc = jnp.dot(h_ref[...], w_ref[...].astype(BF16), preferred_element_type=F32)
    for jj, pieces in enumerate(plan):
        @pl.when(pl.program_id(1) == jj)
        def _(pieces=pieces):
            for lo, hi, mode, out, pre_scale in pieces:
                o_ref = o_refs[out]
                if mode == "plain":
                    o_ref[...] = acc[:, lo:hi].astype(o_ref.dtype)
                    continue
                mix = mixn_ref[...] if mode == "norm_rope" else mixr_ref[...]
                for hh in range((hi - lo) // HEAD_DIM):
                    xh = acc[:, lo + hh * HEAD_DIM:lo + (hh + 1) * HEAD_DIM]
                    res = _rope_head(xh, mode, gain_ref[...], mix, cos_ref[...], sin_ref[...], pre_scale)
                    o_ref[:, hh * HEAD_DIM:(hh + 1) * HEAD_DIM] = res.astype(o_ref.dtype)


def _proj_heads(h, w_in, col0, plan, outs, gain, rope):
    r, d = h.shape
    tm = _blk(2048, r)
    tn = 512
    assert col0 % tn == 0
    c0 = col0 // tn
    cos, sin = rope
    assert cos.shape == (r, HEAD_DIM)
    mixn, mixr = _head_mix_matrix("norm_rope"), _head_mix_matrix("rope")
    const = lambda i, j: (0, 0)
    rows = lambda i, j: (i, 0)
    return pl.pallas_call(
        functools.partial(_proj_heads_kernel, plan=plan),
        grid=(r // tm, len(plan)),
        in_specs=[pl.BlockSpec((tm, d), rows),
                  pl.BlockSpec((d, tn), lambda i, j: (0, c0 + j)),
                  pl.BlockSpec((1, HEAD_DIM), const),
                  pl.BlockSpec(mixn.shape, const),
                  pl.BlockSpec(mixr.shape, const),
                  pl.BlockSpec((tm, HEAD_DIM), rows),
                  pl.BlockSpec((tm, HEAD_DIM), rows)],
        out_specs=tuple(pl.BlockSpec((tm, bw), functools.partial(lambda i, j, f: (i, f(j)), f=f))
                        for _, bw, f in outs),
        out_shape=tuple(jax.ShapeDtypeStruct((r, width), BF16) for width, _, _ in outs),
        compiler_params=_params("parallel", "arbitrary"),
        name="proj_heads",
    )(h, w_in, gain.reshape(1, HEAD_DIM), mixn, mixr, cos, sin)


def _ctx_state_kernel(dec_ref, k_ref, v_ref, sf_ref, sb_ref):
    hh = pl.program_id(1)
    lf = dec_ref[0, hh]
    lb = dec_ref[1, hh]
    k = k_ref[0].astype(F32)
    v = v_ref[0]
    n = k.shape[0]
    pos = lax.broadcasted_iota(jnp.int32, k.shape, 0).astype(F32)
    kf = (k * jnp.exp((n - 1.0 - pos) * lf)).T.astype(BF16)
    kb = (k * jnp.exp(pos * lb)).T.astype(BF16)
    sf_ref[0, 0] = jnp.dot(kf, v, preferred_element_type=F32)
    sb_ref[0, 0] = jnp.dot(kb, v, preferred_element_type=F32)


def _ctx_states(dec, k_r, v_r):
    b, n, _ = k_r.shape
    spec_s = pl.BlockSpec((1, 1, RET_QK_DIM, RET_V_DIM), lambda i, j: (i, j, 0, 0))
    shape_s = jax.ShapeDtypeStruct((b, RET_HEADS, RET_QK_DIM, RET_V_DIM), F32)
    return pl.pallas_call(
        _ctx_state_kernel,
        grid=(b, RET_HEADS),
        in_specs=[pl.BlockSpec(memory_space=pltpu.SMEM),
                  pl.BlockSpec((1, n, RET_QK_DIM), lambda i, j: (i, 0, j)),
                  pl.BlockSpec((1, n, RET_V_DIM), lambda i, j: (i, 0, j))],
        out_specs=(spec_s, spec_s),
        out_shape=(shape_s, shape_s),
        compiler_params=_params("parallel", "parallel"),
        name="ctx_state",
    )(dec, k_r, v_r)


def _attn_kernel(q_ref, kt_ref, v_ref, o_ref, *, scale):
    kt = kt_ref[0, 0]
    v = v_ref[0]
    for g in range(ATT_GROUP):
        q = q_ref[0, :, g * HEAD_DIM:(g + 1) * HEAD_DIM]
        s = jnp.dot(q, kt, preferred_element_type=F32)
        m = jnp.max(s, axis=-1, keepdims=True)
        p = jnp.exp2((s - m) * (scale * LOG2_E))
        l = jnp.sum(p, axis=-1, keepdims=True)
        o = jnp.dot(p.astype(BF16), v, preferred_element_type=F32)
        o_ref[0, :, g * HEAD_DIM:(g + 1) * HEAD_DIM] = (o / l).astype(o_ref.dtype)


def _attention(q, kt, v):
    b, t, _ = q.shape
    s = kt.shape[-1]
    tq = _blk(256, t)
    gw = ATT_GROUP * HEAD_DIM
    return pl.pallas_call(
        functools.partial(_attn_kernel, scale=HEAD_DIM ** -0.5),
        grid=(b, ATT_KV_HEADS, t // tq),
        in_specs=[pl.BlockSpec((1, tq, gw), lambda i, j, n: (i, n, j)),
                  pl.BlockSpec((1, 1, HEAD_DIM, s), lambda i, j, n: (i, j, 0, 0)),
                  pl.BlockSpec((1, s, HEAD_DIM), lambda i, j, n: (i, 0, j))],
        out_specs=pl.BlockSpec((1, tq, gw), lambda i, j, n: (i, n, j)),
        out_shape=jax.ShapeDtypeStruct((b, t, ATT_W), BF16),
        compiler_params=_params("parallel", "parallel", "parallel"),
        name="attention",
    )(q, kt, v)


def _ret_kernel(dec_ref, qf_ref, kf_ref, vf_ref, qb_ref, kb_ref, vb_ref, s0f_ref, s0b_ref,
                of_ref, ob_ref, sf_ref, sb_ref, mf_ref, mb_ref):
    n = pl.program_id(1)
    c = qf_ref.shape[1]

    @pl.when(n == 0)
    def _():
        sf_ref[...] = s0f_ref[0]
        sb_ref[...] = s0b_ref[0]
        ri = lax.broadcasted_iota(jnp.int32, (c, c), 0)
        ci = lax.broadcasted_iota(jnp.int32, (c, c), 1)
        d = (ri - ci).astype(F32)
        for hh in range(RET_HEADS):
            mf_ref[hh] = jnp.where(d >= 0, jnp.exp(jnp.maximum(d, 0.0) * dec_ref[0, hh]), 0.0)
            mb_ref[hh] = jnp.where(d <= 0, jnp.exp(jnp.maximum(-d, 0.0) * dec_ref[1, hh]), 0.0)

    nt = (((1,), (1,)), ((), ()))
    row_v = lax.broadcasted_iota(jnp.int32, (c, RET_V_DIM), 0).astype(F32)
    row_k = lax.broadcasted_iota(jnp.int32, (c, RET_QK_DIM), 0).astype(F32)

    def sweep(q_ref, k_ref, v_ref, o_ref, s_ref, m_ref, hh, lg, wq_age, wk_age):
        qk = slice(hh * RET_QK_DIM, (hh + 1) * RET_QK_DIM)
        vv = slice(hh * RET_V_DIM, (hh + 1) * RET_V_DIM)
        q = q_ref[0, :, qk]
        k = k_ref[0, :, qk]
        v = v_ref[0, :, vv]
        s = lax.dot_general(q, k, nt, preferred_element_type=F32) * m_ref[hh]
        o_in = jnp.dot(s.astype(BF16), v, preferred_element_type=F32)
        state = s_ref[hh]
        o_x = jnp.dot(q, state.astype(BF16), preferred_element_type=F32) * jnp.exp(wq_age * lg)
        o_ref[0, :, vv] = o_in + o_x
        kw = (k.astype(F32) * jnp.exp(wk_age * lg)).T.astype(BF16)
        g_chunk = jnp.exp(jnp.full((1, RET_V_DIM), c * lg, F32))
        s_ref[hh] = g_chunk * state + jnp.dot(kw, v, preferred_element_type=F32)

    for hh in range(RET_HEADS):
        sweep(qf_ref, kf_ref, vf_ref, of_ref, sf_ref, mf_ref, hh, dec_ref[0, hh], row_v + 1.0, c - 1.0 - row_k)
        sweep(qb_ref, kb_ref, vb_ref, ob_ref, sb_ref, mb_ref, hh, dec_ref[1, hh], c - row_v, row_k)


def _retention(dec, q, k, v, s0f, s0b):
    b, t, _ = q.shape
    c = _blk(RET_CHUNK, t)
    nc = t // c
    fwd = lambda i, n: (i, n, 0)
    bwd = lambda i, n: (i, nc - 1 - n, 0)
    spec_s = pl.BlockSpec((1, RET_HEADS, RET_QK_DIM, RET_V_DIM), lambda i, n: (i, 0, 0, 0))
    out_shape = jax.ShapeDtypeStruct((b, t, RET_V_W), F32)
    return pl.pallas_call(
        _ret_kernel,
        grid=(b, nc),
        in_specs=[pl.BlockSpec(memory_space=pltpu.SMEM),
                  pl.BlockSpec((1, c, RET_QK_W), fwd),
                  pl.BlockSpec((1, c, RET_QK_W), fwd),
                  pl.BlockSpec((1, c, RET_V_W), fwd),
                  pl.BlockSpec((1, c, RET_QK_W), bwd),
                  pl.BlockSpec((1, c, RET_QK_W), bwd),
                  pl.BlockSpec((1, c, RET_V_W), bwd),
                  spec_s, spec_s],
        out_specs=(pl.BlockSpec((1, c, RET_V_W), fwd), pl.BlockSpec((1, c, RET_V_W), bwd)),
        out_shape=(out_shape, out_shape),
        scratch_shapes=[pltpu.VMEM((RET_HEADS, RET_QK_DIM, RET_V_DIM), F32),
                        pltpu.VMEM((RET_HEADS, RET_QK_DIM, RET_V_DIM), F32),
                        pltpu.VMEM((RET_HEADS, c, c), F32),
                        pltpu.VMEM((RET_HEADS, c, c), F32)],
        compiler_params=_params("parallel", "arbitrary"),
        name="retention",
    )(dec, q, k, v, q, k, v, s0f, s0b)


def _merge1_kernel(oa_ref, of_ref, ob_ref, g_ref, *refs, nslab):
    ga_refs, gr_refs = refs[:nslab], refs[nslab:2 * nslab]
    gn_ref, wa_ref, wr_ref, y_ref = refs[2 * nslab:]
    o = of_ref[...] + ob_ref[...]
    heads = []
    for hh in range(RET_HEADS):
        sl = slice(hh * RET_V_DIM, (hh + 1) * RET_V_DIM)
        oh = o[:, sl]
        dlt = oh - jnp.mean(oh, axis=-1, keepdims=True)
        yh = dlt * lax.rsqrt(jnp.mean(dlt * dlt, axis=-1, keepdims=True) + EPS)
        g = g_ref[:, sl]
        heads.append((yh * gn_ref[:, sl] * (g * _sigmoid(g))).astype(BF16))
    o_ret = jnp.concatenate(heads, axis=1)
    tn = ga_refs[0].shape[1]
    for c in range(nslab):
        cols = slice(c * tn, (c + 1) * tn)
        ya = jnp.dot(oa_ref[...], wa_ref[:, cols], preferred_element_type=F32)
        yr = jnp.dot(o_ret, wr_ref[:, cols], preferred_element_type=F32)
        y_ref[:, cols] = (_sigmoid(ga_refs[c][...]) * ya + _sigmoid(gr_refs[c][...]) * yr).astype(y_ref.dtype)


def _merge1(o_att, o_f, o_b, zg, gn, wa, wr):
    r = o_att.shape[0]
    d = wa.shape[1]
    tm = _blk(256, r)
    tn = _blk(1024, d)
    nslab = d // tn
    ga0 = RET_V_W // tn
    gr0 = (RET_V_W + d) // tn
    row = lambda i: (i, 0)
    const = lambda i: (0, 0)
    gate_specs = [pl.BlockSpec((tm, tn), functools.partial(lambda i, c: (i, c), c=c0 + c))
                  for c0 in (ga0, gr0) for c in range(nslab)]
    return pl.pallas_call(
        functools.partial(_merge1_kernel, nslab=nslab),
        grid=(r // tm,),
        in_specs=[pl.BlockSpec((tm, ATT_W), row),
                  pl.BlockSpec((tm, RET_V_W), row),
                  pl.BlockSpec((tm, RET_V_W), row),
                  pl.BlockSpec((tm, RET_V_W), row)]
                 + gate_specs
                 + [pl.BlockSpec((1, RET_V_W), const),
                    pl.BlockSpec((ATT_W, d), const),
                    pl.BlockSpec((RET_V_W, d), const)],
        out_specs=pl.BlockSpec((tm, d), row),
        out_shape=jax.ShapeDtypeStruct((r, d), BF16),
        compiler_params=_params("parallel"),
        name="merge1",
    )(o_att, o_f, o_b, zg, *([zg] * (2 * nslab)), gn.reshape(1, RET_V_W), wa, wr)


def _merge2_kernel(y_ref, w_ref, x_ref, gt_ref, pn1_ref, pn2_ref, sh_ref, sc_ref, wr_ref,
                   x1_ref, h2_ref, aff_ref):
    yy = jnp.dot(y_ref[0], w_ref[...], preferred_element_type=F32)
    r = yy * lax.rsqrt(jnp.mean(yy * yy, axis=-1, keepdims=True) + EPS) * pn1_ref[...]
    x1 = x_ref[0] + gt_ref[0] * r
    x1_ref[0] = x1
    h2 = x1 * lax.rsqrt(jnp.mean(x1 * x1, axis=-1, keepdims=True) + EPS) * pn2_ref[...]
    h2 = h2 * (1.0 + sc_ref[0]) + sh_ref[0]
    h_hi = h2.astype(BF16)
    h2_ref[0] = h_hi
    h_lo = (h2 - h_hi.astype(F32)).astype(BF16)
    wr = wr_ref[...]
    w_hi = wr.astype(BF16)
    w_lo = (wr - w_hi.astype(F32)).astype(BF16)
    ne = wr.shape[1]
    r_hi = jnp.dot(h_hi, jnp.concatenate([w_hi, w_lo], axis=1), preferred_element_type=F32)
    logits = r_hi[:, :ne] + r_hi[:, ne:] + jnp.dot(h_lo, w_hi, preferred_element_type=F32)
    e = jnp.exp(logits - jnp.max(logits, axis=-1, keepdims=True))
    aff_ref[0] = e / jnp.sum(e, axis=-1, keepdims=True)


def _merge2(y, w_out, x, gt1, pn1, pn2, sh2, sc2, w_router):
    b, t, d = x.shape
    tm = _blk(512, t)
    ne = w_router.shape[1]
    tile = lambda i, j: (i, j, 0)
    vec = lambda i, j: (0, 0)
    per_b = lambda i, j: (i, 0, 0)
    return pl.pallas_call(
        _merge2_kernel,
        grid=(b, t // tm),
        in_specs=[pl.BlockSpec((1, tm, d), tile),
                  pl.BlockSpec((d, d), vec),
                  pl.BlockSpec((1, tm, d), tile),
                  pl.BlockSpec((1, 1, d), per_b),
                  pl.BlockSpec((1, d), vec),
                  pl.BlockSpec((1, d), vec),
                  pl.BlockSpec((1, 1, d), per_b),
                  pl.BlockSpec((1, 1, d), per_b),
                  pl.BlockSpec((d, ne), vec)],
        out_specs=(pl.BlockSpec((1, tm, d), tile),
                   pl.BlockSpec((1, tm, d), tile),
                   pl.BlockSpec((1, tm, ne), tile)),
        out_shape=(jax.ShapeDtypeStruct((b, t, d), F32),
                   jax.ShapeDtypeStruct((b, t, d), BF16),
                   jax.ShapeDtypeStruct((b, t, ne), F32)),
        compiler_params=_params("parallel", "parallel"),
        name="merge2",
    )(y.reshape(b, t, d), w_out, x, gt1, pn1.reshape(1, d), pn2.reshape(1, d), sh2, sc2, w_router)


def _lane_cumsum(x, tri):
    off = jnp.zeros((x.shape[0], 1), F32)
    parts = []
    for cidx in range(x.shape[1] // LANES):
        xc = x[:, cidx * LANES:(cidx + 1) * LANES].astype(BF16)
        cs = jnp.dot(xc, tri, preferred_element_type=F32) + off
        parts.append(cs)
        off = cs[:, LANES - 1:LANES]
    return jnp.concatenate(parts, axis=1)


def _topk_kernel(a_ref, pos_ref, fill_ref, *, cap):
    a = a_ref[0]
    bits = jnp.zeros((a.shape[0], 1), jnp.int32)
    for bit in range(30, -1, -1):
        cand = bits | (1 << bit)
        cnt = jnp.sum(jnp.where(a >= lax.bitcast_convert_type(cand, F32), 1.0, 0.0), axis=1, keepdims=True)
        bits = jnp.where(cnt >= cap, cand, bits)
    bits = jnp.where(bits < MIN_NORMAL_F32_BITS, 0, bits)
    thr = lax.bitcast_convert_type(bits, F32)
    ri = lax.broadcasted_iota(jnp.int32, (LANES, LANES), 0)
    ci = lax.broadcasted_iota(jnp.int32, (LANES, LANES), 1)
    tri = jnp.where(ri <= ci, 1.0, 0.0).astype(BF16)
    gt = a > thr
    eq = a == thr
    n_gt = jnp.sum(jnp.where(gt, 1.0, 0.0), axis=1, keepdims=True)
    eq_rank = _lane_cumsum(jnp.where(eq, 1.0, 0.0), tri)
    sel = jnp.where(gt, 1.0, jnp.where(eq, jnp.where(eq_rank <= cap - n_gt, 1.0, 0.0), 0.0))
    filled = _lane_cumsum(sel, tri)
    pos_ref[0] = filled * sel
    fill_ref[0] = filled


def _topk_slots(aff_t, cap):
    b, ne, t = aff_t.shape
    spec = pl.BlockSpec((1, ne, t), lambda i: (i, 0, 0))
    shape = jax.ShapeDtypeStruct((b, ne, t), F32)
    return pl.pallas_call(
        functools.partial(_topk_kernel, cap=cap),
        grid=(b,),
        in_specs=[spec],
        out_specs=(spec, spec),
        out_shape=(shape, shape),
        compiler_params=_params("parallel"),
        name="topk",
    )(aff_t)


def _gather_kernel(pos_ref, h_ref, o_ref):
    cap = o_ref.shape[1]
    t = h_ref.shape[1]
    tc = _blk(1024, t)
    slot = lax.broadcasted_iota(jnp.int32, (cap, tc), 0).astype(F32) + 1.0
    acc = None
    for cidx in range(t // tc):
        row = pos_ref[0, 0, :, cidx * tc:(cidx + 1) * tc]
        onehot = jnp.where(row == slot, 1.0, 0.0).astype(BF16)
        part = jnp.dot(onehot, h_ref[0, cidx * tc:(cidx + 1) * tc, :], preferred_element_type=F32)
        acc = part if acc is None else acc + part
    o_ref[0] = acc.astype(o_ref.dtype)


def _gather(pos, h2, cap):
    b, ne, t = pos.shape
    d = h2.shape[-1]
    dn = _blk(1024, d)
    return pl.pallas_call(
        _gather_kernel,
        grid=(b, d // dn, ne),
        in_specs=[pl.BlockSpec((1, 1, 1, t), lambda i, n, e: (i, e, 0, 0)),
                  pl.BlockSpec((1, t, dn), lambda i, n, e: (i, 0, n))],
        out_specs=pl.BlockSpec((1, cap, dn), lambda i, n, e: (e, i, n)),
        out_shape=jax.ShapeDtypeStruct((ne, b * cap, d), BF16),
        compiler_params=_params("parallel", "parallel", "parallel"),
        name="gather",
    )(pos.reshape(b, ne, 1, t), h2)


def _gather_win_kernel(w0_ref, pos_ref, h_ref, o_ref, acc_ref, lhs_ref):
    b = pl.program_id(0)
    ne, cap = o_ref.shape[0], o_ref.shape[1]
    nt = h_ref.shape[1] // TOK_TILE
    slot1 = lax.broadcasted_iota(jnp.int32, (GATHER_WIN, TOK_TILE), 0) + 1

    @pl.when(pl.program_id(1) == 0)
    def _():
        def build(k, carry):
            for e in range(ne):
                w0 = w0_ref[(b * ne + e) * nt + k]
                row = pos_ref[0, pl.ds(e * nt + k, 1), :]
                onehot = jnp.where(row == (slot1 + w0).astype(F32), 1.0, 0.0)
                lhs_ref[k, e * GATHER_WIN:(e + 1) * GATHER_WIN, :] = onehot.astype(BF16)
            return carry
        lax.fori_loop(0, nt, build, 0)

    acc_ref[...] = jnp.zeros_like(acc_ref)

    def tile(k, carry):
        t0 = pl.multiple_of(k * TOK_TILE, TOK_TILE)
        res = jnp.dot(lhs_ref[k], h_ref[0, pl.ds(t0, TOK_TILE), :], preferred_element_type=F32)
        for e in range(ne):
            w0 = pl.multiple_of(w0_ref[(b * ne + e) * nt + k], 8)
            acc_ref[e, pl.ds(w0, GATHER_WIN), :] += res[e * GATHER_WIN:(e + 1) * GATHER_WIN]
        return carry
    lax.fori_loop(0, nt, tile, 0)
    o_ref[...] = acc_ref[:, :cap, :].astype(o_ref.dtype)


def _gather_win(w0, pos, h2, cap):
    b, ne, t = pos.shape
    d = h2.shape[-1]
    nt = t // TOK_TILE
    dn = _blk(256, d)
    return pl.pallas_call(
        _gather_win_kernel,
        grid_spec=pltpu.PrefetchScalarGridSpec(
            num_scalar_prefetch=1,
            grid=(b, d // dn),
            in_specs=[pl.BlockSpec((1, ne * nt, TOK_TILE), lambda i, n, w: (i, 0, 0)),
                      pl.BlockSpec((1, t, dn), lambda i, n, w: (i, 0, n))],
            out_specs=pl.BlockSpec((ne, cap, dn), lambda i, n, w: (0, i, n)),
            scratch_shapes=[pltpu.VMEM((ne, cap + GATHER_WIN, dn), F32),
                            pltpu.VMEM((nt, ne * GATHER_WIN, TOK_TILE), BF16)]),
        out_shape=jax.ShapeDtypeStruct((ne, b * cap, d), BF16),
        compiler_params=_params("parallel", "arbitrary"),
        name="gather_win",
    )(w0, pos.reshape(b, ne * nt, TOK_TILE), h2)


def _ffn_kernel(x_ref, wg_ref, wu_ref, wd_ref, o_ref, h_ref, wgu_ref, *, nf):
    j = pl.program_id(2)
    tf = wg_ref.shape[2]

    @pl.when(j < nf)
    def _():
        wgu_ref[:, :tf] = wg_ref[0].astype(BF16)
        wgu_ref[:, tf:] = wu_ref[0].astype(BF16)
        au = jnp.dot(x_ref[0], wgu_ref[...], preferred_element_type=F32)
        a = au[:, :tf]
        h_ref[j] = (a * _sigmoid(a) * au[:, tf:]).astype(BF16)

    @pl.when(j >= nf)
    def _():
        acc = None
        for cidx in range(nf):
            part = jnp.dot(h_ref[cidx], wd_ref[0, cidx * tf:(cidx + 1) * tf, :].astype(BF16),
                           preferred_element_type=F32)
            acc = part if acc is None else acc + part
        o_ref[0] = acc.astype(o_ref.dtype)


def _ffn(xg, w_gate, w_up, w_down):
    ne, rows, d = xg.shape
    ff = w_gate.shape[-1]
    tm = _blk(2048, rows)
    tf = _blk(256, ff)
    tn = _blk(256, d)
    nf = ff // tf
    return pl.pallas_call(
        functools.partial(_ffn_kernel, nf=nf),
        grid=(ne, rows // tm, nf + d // tn),
        in_specs=[pl.BlockSpec((1, tm, d), lambda e, m, j: (e, m, 0)),
                  pl.BlockSpec((1, d, tf), lambda e, m, j: (e, 0, jnp.minimum(j, nf - 1))),
                  pl.BlockSpec((1, d, tf), lambda e, m, j: (e, 0, jnp.minimum(j, nf - 1))),
                  pl.BlockSpec((1, ff, tn), lambda e, m, j: (e, 0, jnp.maximum(j - nf, 0)))],
        out_specs=pl.BlockSpec((1, tm, tn), lambda e, m, j: (e, m, jnp.maximum(j - nf, 0))),
        out_shape=jax.ShapeDtypeStruct((ne, rows, d), BF16),
        scratch_shapes=[pltpu.VMEM((nf, tm, tf), BF16), pltpu.VMEM((d, 2 * tf), BF16)],
        compiler_params=_params("parallel", "parallel", "arbitrary"),
        name="ffn",
    )(xg, w_gate, w_up, w_down)


def _combine_kernel(pos_ref, aff_ref, y_ref, x1_ref, gt_ref, pn_ref, o_ref, acc_ref):
    e = pl.program_id(2)
    tt, ne = pos_ref.shape[1], pos_ref.shape[2]
    cap = y_ref.shape[1]
    pick = lax.broadcasted_iota(jnp.int32, (tt, ne), 1) == e
    pcol = jnp.sum(jnp.where(pick, pos_ref[0], 0.0), axis=1, keepdims=True)
    acol = jnp.sum(jnp.where(pick, aff_ref[0], 0.0), axis=1, keepdims=True)
    slot = lax.broadcasted_iota(jnp.int32, (tt, cap), 1).astype(F32) + 1.0
    onehot = jnp.where(pcol == slot, 1.0, 0.0).astype(BF16)
    z = jnp.dot(onehot, y_ref[0], preferred_element_type=F32) * acol

    @pl.when(e == 0)
    def _():
        acc_ref[...] = z

    @pl.when(e > 0)
    def _():
        acc_ref[...] += z

    @pl.when(e == pl.num_programs(2) - 1)
    def _():
        y2 = acc_ref[...]
        r = y2 * lax.rsqrt(jnp.mean(y2 * y2, axis=-1, keepdims=True) + EPS) * pn_ref[...]
        o_ref[0] = x1_ref[0] + gt_ref[0] * r


def _combine(pos_t, aff, y, x1, gt2, pn2, cap):
    b, t, d = x1.shape
    ne = aff.shape[-1]
    tt = _blk(512, t)
    return pl.pallas_call(
        _combine_kernel,
        grid=(b, t // tt, ne),
        in_specs=[pl.BlockSpec((1, tt, ne), lambda i, j, e: (i, j, 0)),
                  pl.BlockSpec((1, tt, ne), lambda i, j, e: (i, j, 0)),
                  pl.BlockSpec((1, cap, d), lambda i, j, e: (e, i, 0)),
                  pl.BlockSpec((1, tt, d), lambda i, j, e: (i, j, 0)),
                  pl.BlockSpec((1, 1, d), lambda i, j, e: (i, 0, 0)),
                  pl.BlockSpec((1, d), lambda i, j, e: (0, 0))],
        out_specs=pl.BlockSpec((1, tt, d), lambda i, j, e: (i, j, 0)),
        out_shape=jax.ShapeDtypeStruct((b, t, d), F32),
        scratch_shapes=[pltpu.VMEM((tt, d), F32)],
        compiler_params=_params("parallel", "parallel", "arbitrary"),
        name="combine",
    )(pos_t, aff, y, x1, gt2, pn2.reshape(1, d))


def _combine_win_kernel(blk_ref, base_ref, pos_ref, aff_ref, *refs):
    del blk_ref
    ne = pos_ref.shape[2]
    y_refs = refs[:2 * ne]
    x1_ref, gt_ref, pn_ref, o_ref = refs[2 * ne:]
    b = pl.program_id(0)
    k = pl.program_id(1)
    nt = pl.num_programs(1)
    lane = lax.broadcasted_iota(jnp.int32, (TOK_TILE, 2 * SLOT_BLK), 1)
    second = lane >= SLOT_BLK
    pos = pos_ref[0]
    aff = aff_ref[0]
    cols = []
    for e in range(ne):
        at = ((b * ne + e) * nt + k) * 2
        slot = jnp.where(second, lane + (base_ref[at + 1] + 1 - SLOT_BLK), lane + (base_ref[at] + 1))
        cols.append(jnp.where(pos[:, e:e + 1] == slot.astype(F32), aff[:, e:e + 1], 0.0).astype(BF16))
    ywin = jnp.concatenate([r[...] for r in y_refs], axis=0)
    y2 = jnp.dot(jnp.concatenate(cols, axis=1), ywin, preferred_element_type=F32)
    r = y2 * lax.rsqrt(jnp.mean(y2 * y2, axis=-1, keepdims=True) + EPS) * pn_ref[...]
    o_ref[0] = x1_ref[0] + gt_ref[0] * r


def _combine_win(blk, base, pos_t, aff, y, x1, gt2, pn2):
    b, t, d = x1.shape
    ne = aff.shape[-1]
    nt = t // TOK_TILE

    def y_spec(e, j):
        return pl.BlockSpec((SLOT_BLK, d),
                            lambda i, k, blk_ref, base_ref: (blk_ref[((i * ne + e) * nt + k) * 2 + j], 0))

    tile = lambda i, k, blk_ref, base_ref: (i, k, 0)
    y2d = y.reshape(-1, d)
    return pl.pallas_call(
        _combine_win_kernel,
        grid_spec=pltpu.PrefetchScalarGridSpec(
            num_scalar_prefetch=2,
            grid=(b, nt),
            in_specs=[pl.BlockSpec((1, TOK_TILE, ne), tile),
                      pl.BlockSpec((1, TOK_TILE, ne), tile)]
                     + [y_spec(e, j) for e in range(ne) for j in range(2)]
                     + [pl.BlockSpec((1, TOK_TILE, d), tile),
                        pl.BlockSpec((1, 1, d), lambda i, k, blk_ref, base_ref: (i, 0, 0)),
                        pl.BlockSpec((1, d), lambda i, k, blk_ref, base_ref: (0, 0))],
            out_specs=pl.BlockSpec((1, TOK_TILE, d), tile)),
        out_shape=jax.ShapeDtypeStruct((b, t, d), F32),
        compiler_params=_params("parallel", "parallel"),
        name="combine_win",
    )(blk, base, pos_t, aff, *([y2d] * (2 * ne)), x1, gt2, pn2.reshape(1, d))


def _routing_tables(fill, cap):
    b, ne, t = fill.shape
    nblk = cap // SLOT_BLK
    ends = fill[:, :, TOK_TILE - 1::TOK_TILE].astype(jnp.int32)
    starts = jnp.concatenate([jnp.zeros((b, ne, 1), jnp.int32), ends[:, :, :-1]], axis=-1)
    sparse = jnp.all(ends - starts <= SLOT_BLK)
    first = jnp.minimum(starts // SLOT_BLK, nblk - 1)
    second = first + 1
    group = (jnp.arange(ne)[None, :, None] * b + jnp.arange(b)[:, None, None]) * nblk
    blk = jnp.stack([group + first, group + jnp.minimum(second, nblk - 1)], axis=-1)
    base = jnp.stack([first * SLOT_BLK, jnp.where(second < nblk, second * SLOT_BLK, -(1 << 20))], axis=-1)
    return sparse, (starts // 8 * 8).reshape(-1), blk.reshape(-1), base.reshape(-1)


def _rope_tables(t):
    pos = jnp.arange(t)
    r = (pos // GRID_W).astype(F32)
    cl = (pos % GRID_W).astype(F32)
    quarter = HEAD_DIM // 4
    inv = ROPE_THETA ** (-jnp.arange(quarter, dtype=F32) / quarter)
    ang_r = r[:, None] * inv
    ang_c = cl[:, None] * inv
    ang = jnp.concatenate([ang_r, ang_r, ang_c, ang_c], axis=-1)
    return jnp.cos(ang), jnp.sin(ang)


def _kv_heads(h, w_in, q_width, k_norm, rope):
    plan = (((0, ATT_KV_W, "norm_rope", 0, 1.0), (ATT_KV_W, 2 * ATT_KV_W, "plain", 1, 1.0)),
            ((0, RET_QK_W, "rope", 2, RET_QK_DIM ** -0.5),),
            ((0, 512, "plain", 3, 1.0),),
            ((0, 512, "plain", 3, 1.0),))
    outs = ((ATT_KV_W, ATT_KV_W, lambda j: 0), (ATT_KV_W, ATT_KV_W, lambda j: 0),
            (RET_QK_W, RET_QK_W, lambda j: 0), (RET_V_W, 512, lambda j: jnp.clip(j - 2, 0, 1)))
    return _proj_heads(h, w_in, q_width, plan, outs, k_norm, rope)


def _q_heads(h, w_in, q_norm, rope):
    plan = (((0, 512, "norm_rope", 0, 1.0),), ((0, 512, "norm_rope", 0, 1.0),), ((0, RET_QK_W, "rope", 1, 1.0),))
    outs = ((ATT_W, 512, lambda j: jnp.minimum(j, 1)), (RET_QK_W, RET_QK_W, lambda j: 0))
    return _proj_heads(h, w_in, 0, plan, outs, q_norm, rope)


def kernel(x, c, ctx, c_ctx, w_mod, b_mod, pre_norm1, post_norm1, pre_norm2, post_norm2, w_in, q_norm,
           k_norm, ret_decay, ret_gn, w_o_att, w_o_ret, w_out, w_router, w_gate, w_up, w_down):
    b, t, d = x.shape
    n_ctx = ctx.shape[1]
    depth = w_mod.shape[0]
    q_width = ATT_W + RET_QK_W + RET_V_W + 2 * d
    cap = EC_FACTOR * t // N_EXPERTS
    rope_lat = tuple(jnp.tile(tbl, (b, 1)) for tbl in _rope_tables(t))
    rope_ctx = (jnp.ones((b * n_ctx, HEAD_DIM), F32), jnp.zeros((b * n_ctx, HEAD_DIM), F32))
    cs = jnp.zeros((8, d), F32).at[:b].set(c).at[b].set(c_ctx)
    xc = ctx
    for layer in range(depth):
        assert layer == depth - 1, "context-stream update between layers is not implemented"
        mod = _mod_vectors(cs, w_mod[layer], b_mod[layer])
        sh1, sc1, gt1, sh2, sc2, gt2 = [m[:b, None, :] for m in jnp.split(mod, 6, axis=-1)]
        csh1, csc1 = [jnp.broadcast_to(m[b][None, None, :], (b, 1, d)) for m in jnp.split(mod, 6, axis=-1)[:2]]
        dec = -jax.nn.softplus(ret_decay[layer].astype(F32))
        wl = w_in[layer]

        hc = _prenorm(xc, pre_norm1[layer], csh1, csc1).reshape(b * n_ctx, d)
        kc_a, vc_a, kc_r, vc_r = _kv_heads(hc, wl, q_width, k_norm[layer], rope_ctx)
        s0f, s0b = _ctx_states(dec, kc_r.reshape(b, n_ctx, RET_QK_W), vc_r.reshape(b, n_ctx, RET_V_W))

        h = _prenorm(x, pre_norm1[layer], sh1, sc1).reshape(b * t, d)
        q_a, q_r = _q_heads(h, wl, q_norm[layer], rope_lat)
        zg = _proj(h, wl, ATT_W + RET_QK_W, RET_V_W + 2 * d, out_dtype=F32, tn=512)
        k_a, v_a, k_r, v_r = _kv_heads(h, wl, q_width, k_norm[layer], rope_lat)

        keys = jnp.concatenate([kc_a.reshape(b, n_ctx, ATT_KV_W), k_a.reshape(b, t, ATT_KV_W)], axis=1)
        vals = jnp.concatenate([vc_a.reshape(b, n_ctx, ATT_KV_W), v_a.reshape(b, t, ATT_KV_W)], axis=1)
        kt = keys.reshape(b, n_ctx + t, ATT_KV_HEADS, HEAD_DIM).transpose(0, 2, 3, 1)
        o_att = _attention(q_a.reshape(b, t, ATT_W), kt, vals)

        o_f, o_b = _retention(dec, q_r.reshape(b, t, RET_QK_W), k_r.reshape(b, t, RET_QK_W),
                              v_r.reshape(b, t, RET_V_W), s0f, s0b)

        y = _merge1(o_att.reshape(b * t, ATT_W), o_f.reshape(b * t, RET_V_W), o_b.reshape(b * t, RET_V_W),
                    zg, ret_gn[layer], w_o_att[layer].astype(BF16), w_o_ret[layer].astype(BF16))
        x1, h2, aff = _merge2(y, w_out[layer].astype(BF16), x, gt1, post_norm1[layer], pre_norm2[layer],
                              sh2, sc2, w_router[layer])

        pos, fill = _topk_slots(aff.transpose(0, 2, 1), cap)
        pos_t = pos.transpose(0, 2, 1)
        sparse, w0, blk, base = _routing_tables(fill, cap)
        xg = lax.cond(sparse,
                      lambda: _gather_win(w0, pos, h2, cap),
                      lambda: _gather(pos, h2, cap))
        yg = _ffn(xg, w_gate[layer], w_up[layer], w_down[layer])
        x = lax.cond(sparse,
                     lambda: _combine_win(blk, base, pos_t, aff, yg, x1, gt2, post_norm2[layer]),
                     lambda: _combine(pos_t, aff, yg, x1, gt2, post_norm2[layer], cap))
    return x
```

```python
import functools

import jax
import jax.numpy as jnp
import numpy as np
from jax import lax
from jax.experimental import pallas as pl
from jax.experimental.pallas import tpu as pltpu

F32 = jnp.float32
BF16 = jnp.bfloat16

GRID_W = 64
HEAD_DIM = 128
ATT_HEADS = 8
ATT_KV_HEADS = 2
ATT_GROUP = ATT_HEADS // ATT_KV_HEADS
RET_HEADS = 4
RET_QK_DIM = 128
RET_V_DIM = 256
ROPE_THETA = 10000.0
N_EXPERTS = 16
EC_FACTOR = 2
EPS = 1e-6

ATT_W = ATT_HEADS * HEAD_DIM
ATT_KV_W = ATT_KV_HEADS * HEAD_DIM
RET_QK_W = RET_HEADS * RET_QK_DIM
RET_V_W = RET_HEADS * RET_V_DIM

LANES = 128
VMEM_LIMIT = 56 * 1024 * 1024
RET_CHUNK = 256
MIN_NORMAL_F32_BITS = 0x00800000
LOG2_E = 1.4426950408889634
TOK_TILE = 256
SLOT_BLK = 64
GATHER_WIN = SLOT_BLK + 16


def _blk(pref, n):
    return pref if n % pref == 0 else n


def _params(*sem):
    return pltpu.CompilerParams(dimension_semantics=sem, vmem_limit_bytes=VMEM_LIMIT)


def _sigmoid(x):
    return 0.5 * jnp.tanh(0.5 * x) + 0.5


def _mod_kernel(s_ref, w_ref, b_ref, o_ref):
    s = s_ref[...]
    s = s * _sigmoid(s)
    o_ref[...] = jnp.dot(s, w_ref[...], precision=lax.Precision.HIGHEST,
                         preferred_element_type=F32) + b_ref[...]


def _mod_vectors(cs, w_mod, b_mod):
    rows, d = cs.shape
    n = w_mod.shape[1]
    tn = _blk(2048, n)
    return pl.pallas_call(
        _mod_kernel,
        grid=(n // tn,),
        in_specs=[pl.BlockSpec((rows, d), lambda j: (0, 0)),
                  pl.BlockSpec((d, tn), lambda j: (0, j)),
                  pl.BlockSpec((1, tn), lambda j: (0, j))],
        out_specs=pl.BlockSpec((rows, tn), lambda j: (0, j)),
        out_shape=jax.ShapeDtypeStruct((rows, n), F32),
        compiler_params=_params("parallel"),
        name="mod",
    )(cs, w_mod, b_mod.reshape(1, n))


def _prenorm_kernel(x_ref, g_ref, sh_ref, sc_ref, o_ref):
    x = x_ref[0]
    y = x * lax.rsqrt(jnp.mean(x * x, axis=-1, keepdims=True) + EPS) * g_ref[...]
    o_ref[0] = (y * (1.0 + sc_ref[0]) + sh_ref[0]).astype(o_ref.dtype)


def _prenorm(x, gain, shift, scale):
    b, t, d = x.shape
    tm = _blk(1024, t)
    return pl.pallas_call(
        _prenorm_kernel,
        grid=(b, t // tm),
        in_specs=[pl.BlockSpec((1, tm, d), lambda i, j: (i, j, 0)),
                  pl.BlockSpec((1, d), lambda i, j: (0, 0)),
                  pl.BlockSpec((1, 1, d), lambda i, j: (i, 0, 0)),
                  pl.BlockSpec((1, 1, d), lambda i, j: (i, 0, 0))],
        out_specs=pl.BlockSpec((1, tm, d), lambda i, j: (i, j, 0)),
        out_shape=jax.ShapeDtypeStruct((b, t, d), BF16),
        compiler_params=_params("parallel", "parallel"),
        name="prenorm",
    )(x, gain.reshape(1, d), shift, scale)


def _split_bf16(v):
    hi = v.astype(BF16)
    return [hi, (v - hi.astype(F32)).astype(BF16)]


def _rotate_half_matrix():
    quarter = HEAD_DIM // 4
    r = np.zeros((HEAD_DIM, HEAD_DIM), np.float32)
    for i in range(HEAD_DIM):
        if i % (2 * quarter) < quarter:
            r[i + quarter, i] = -1.0
        else:
            r[i - quarter, i] = 1.0
    return r


def _head_mix_matrix(mode):
    rot = _rotate_half_matrix()
    if mode == "rope":
        return jnp.asarray(np.concatenate([rot, rot], axis=0), BF16)
    ones = np.ones((HEAD_DIM, HEAD_DIM), np.float32)
    zero = np.zeros((HEAD_DIM, HEAD_DIM), np.float32)
    return jnp.asarray(np.block([[ones, zero], [ones, zero], [zero, rot], [zero, rot]]), BF16)


def _proj_kernel(h_ref, w_ref, o_ref):
    o_ref[...] = jnp.dot(h_ref[...], w_ref[...].astype(BF16), preferred_element_type=F32).astype(o_ref.dtype)


def _proj(h, w_in, col0, width, *, out_dtype, tn):
    r, d = h.shape
    tm = _blk(2048, r)
    assert col0 % tn == 0 and width % tn == 0
    c0 = col0 // tn
    return pl.pallas_call(
        _proj_kernel,
        grid=(r // tm, width // tn),
        in_specs=[pl.BlockSpec((tm, d), lambda i, j: (i, 0)),
                  pl.BlockSpec((d, tn), lambda i, j: (0, c0 + j))],
        out_specs=pl.BlockSpec((tm, tn), lambda i, j: (i, j)),
        out_shape=jax.ShapeDtypeStruct((r, width), out_dtype),
        compiler_params=_params("parallel", "parallel"),
        name="proj_plain",
    )(h, w_in)


def _rope_head(xh, mode, gain, mix, cos, sin, pre_scale):
    if mode == "norm_rope":
        xg = xh * gain
        mixed = jnp.dot(jnp.concatenate(_split_bf16(xh * xh) + _split_bf16(xg), axis=1), mix,
                        preferred_element_type=F32)
        inv = lax.rsqrt(mixed[:, :HEAD_DIM] * (1.0 / HEAD_DIM) + EPS)
        return (xg * cos + mixed[:, HEAD_DIM:] * sin) * inv
    xs = xh * pre_scale if pre_scale != 1.0 else xh
    rot = jnp.dot(jnp.concatenate(_split_bf16(xs), axis=1), mix, preferred_element_type=F32)
    return xs * cos + rot * sin


def _proj_heads_kernel(h_ref, w_ref, gain_ref, mixn_ref, mixr_ref, cos_ref, sin_ref, *o_refs, plan):
    acc = jnp.dot(h_ref[...], w_ref[...].astype(BF16), preferred_element_type=F32)
    for jj, pieces in enumerate(plan):
        @pl.when(pl.program_id(1) == jj)
        def _(pieces=pieces):
            for lo, hi, mode, out, pre_scale in pieces:
                o_ref = o_refs[out]
                if mode == "plain":
                    o_ref[...] = acc[:, lo:hi].astype(o_ref.dtype)
                    continue
                mix = mixn_ref[...] if mode == "norm_rope" else mixr_ref[...]
                for hh in range((hi - lo) // HEAD_DIM):
                    xh = acc[:, lo + hh * HEAD_DIM:lo + (hh + 1) * HEAD_DIM]
                    res = _rope_head(xh, mode, gain_ref[...], mix, cos_ref[...], sin_ref[...], pre_scale)
                    o_ref[:, hh * HEAD_DIM:(hh + 1) * HEAD_DIM] = res.astype(o_ref.dtype)


def _proj_heads(h, w_in, col0, plan, outs, gain, rope):
    r, d = h.shape
    tm = _blk(2048, r)
    tn = 512
    assert col0 % tn == 0
    c0 = col0 // tn
    cos, sin = rope
    assert cos.shape == (r, HEAD_DIM)
    mixn, mixr = _head_mix_matrix("norm_rope"), _head_mix_matrix("rope")
    const = lambda i, j: (0, 0)
    rows = lambda i, j: (i, 0)
    return pl.pallas_call(
        functools.partial(_proj_heads_kernel, plan=plan),
        grid=(r // tm, len(plan)),
        in_specs=[pl.BlockSpec((tm, d), rows),
                  pl.BlockSpec((d, tn), lambda i, j: (0, c0 + j)),
                  pl.BlockSpec((1, HEAD_DIM), const),
                  pl.BlockSpec(mixn.shape, const),
                  pl.BlockSpec(mixr.shape, const),
                  pl.BlockSpec((tm, HEAD_DIM), rows),
                  pl.BlockSpec((tm, HEAD_DIM), rows)],
        out_specs=tuple(pl.BlockSpec((tm, bw), functools.partial(lambda i, j, f: (i, f(j)), f=f))
                        for _, bw, f in outs),
        out_shape=tuple(jax.ShapeDtypeStruct((r, width), BF16) for width, _, _ in outs),
        compiler_params=_params("parallel", "arbitrary"),
        name="proj_heads",
    )(h, w_in, gain.reshape(1, HEAD_DIM), mixn, mixr, cos, sin)


def _ctx_state_kernel(dec_ref, k_ref, v_ref, sf_ref, sb_ref):
    hh = pl.program_id(1)
    lf = dec_ref[0, hh]
    lb = dec_ref[1, hh]
    k = k_ref[0].astype(F32)
    v = v_ref[0]
    n = k.shape[0]
    pos = lax.broadcasted_iota(jnp.int32, k.shape, 0).astype(F32)
    kf = (k * jnp.exp((n - 1.0 - pos) * lf)).T.astype(BF16)
    kb = (k * jnp.exp(pos * lb)).T.astype(BF16)
    sf_ref[0, 0] = jnp.dot(kf, v, preferred_element_type=F32)
    sb_ref[0, 0] = jnp.dot(kb, v, preferred_element_type=F32)


def _ctx_states(dec, k_r, v_r):
    b, n, _ = k_r.shape
    spec_s = pl.BlockSpec((1, 1, RET_QK_DIM, RET_V_DIM), lambda i, j: (i, j, 0, 0))
    shape_s = jax.ShapeDtypeStruct((b, RET_HEADS, RET_QK_DIM, RET_V_DIM), F32)
    return pl.pallas_call(
        _ctx_state_kernel,
        grid=(b, RET_HEADS),
        in_specs=[pl.BlockSpec(memory_space=pltpu.SMEM),
                  pl.BlockSpec((1, n, RET_QK_DIM), lambda i, j: (i, 0, j)),
                  pl.BlockSpec((1, n, RET_V_DIM), lambda i, j: (i, 0, j))],
        out_specs=(spec_s, spec_s),
        out_shape=(shape_s, shape_s),
        compiler_params=_params("parallel", "parallel"),
        name="ctx_state",
    )(dec, k_r, v_r)


def _attn_kernel(q_ref, kt_ref, v_ref, o_ref, *, scale):
    kt = kt_ref[0, 0]
    v = v_ref[0]
    for g in range(ATT_GROUP):
        q = q_ref[0, :, g * HEAD_DIM:(g + 1) * HEAD_DIM]
        s = jnp.dot(q, kt, preferred_element_type=F32)
        m = jnp.max(s, axis=-1, keepdims=True)
        p = jnp.exp2((s - m) * (scale * LOG2_E))
        l = jnp.sum(p, axis=-1, keepdims=True)
        o = jnp.dot(p.astype(BF16), v, preferred_element_type=F32)
        o_ref[0, :, g * HEAD_DIM:(g + 1) * HEAD_DIM] = (o / l).astype(o_ref.dtype)


def _attention(q, kt, v):
    b, t, _ = q.shape
    s = kt.shape[-1]
    tq = _blk(256, t)
    gw = ATT_GROUP * HEAD_DIM
    return pl.pallas_call(
        functools.partial(_attn_kernel, scale=HEAD_DIM ** -0.5),
        grid=(b, ATT_KV_HEADS, t // tq),
        in_specs=[pl.BlockSpec((1, tq, gw), lambda i, j, n: (i, n, j)),
                  pl.BlockSpec((1, 1, HEAD_DIM, s), lambda i, j, n: (i, j, 0, 0)),
                  pl.BlockSpec((1, s, HEAD_DIM), lambda i, j, n: (i, 0, j))],
        out_specs=pl.BlockSpec((1, tq, gw), lambda i, j, n: (i, n, j)),
        out_shape=jax.ShapeDtypeStruct((b, t, ATT_W), BF16),
        compiler_params=_params("parallel", "parallel", "parallel"),
        name="attention",
    )(q, kt, v)


def _ret_kernel(dec_ref, qf_ref, kf_ref, vf_ref, qb_ref, kb_ref, vb_ref, s0f_ref, s0b_ref,
                of_ref, ob_ref, sf_ref, sb_ref, mf_ref, mb_ref):
    n = pl.program_id(1)
    c = qf_ref.shape[1]

    @pl.when(n == 0)
    def _():
        sf_ref[...] = s0f_ref[0]
        sb_ref[...] = s0b_ref[0]
        ri = lax.broadcasted_iota(jnp.int32, (c, c), 0)
        ci = lax.broadcasted_iota(jnp.int32, (c, c), 1)
        d = (ri - ci).astype(F32)
        for hh in range(RET_HEADS):
            mf_ref[hh] = jnp.where(d >= 0, jnp.exp(jnp.maximum(d, 0.0) * dec_ref[0, hh]), 0.0)
            mb_ref[hh] = jnp.where(d <= 0, jnp.exp(jnp.maximum(-d, 0.0) * dec_ref[1, hh]), 0.0)

    nt = (((1,), (1,)), ((), ()))
    row_v = lax.broadcasted_iota(jnp.int32, (c, RET_V_DIM), 0).astype(F32)
    row_k = lax.broadcasted_iota(jnp.int32, (c, RET_QK_DIM), 0).astype(F32)

    def sweep(q_ref, k_ref, v_ref, o_ref, s_ref, m_ref, hh, lg, wq_age, wk_age):
        qk = slice(hh * RET_QK_DIM, (hh + 1) * RET_QK_DIM)
        vv = slice(hh * RET_V_DIM, (hh + 1) * RET_V_DIM)
        q = q_ref[0, :, qk]
        k = k_ref[0, :, qk]
        v = v_ref[0, :, vv]
        s = lax.dot_general(q, k, nt, preferred_element_type=F32) * m_ref[hh]
        o_in = jnp.dot(s.astype(BF16), v, preferred_element_type=F32)
        state = s_ref[hh]
        o_x = jnp.dot(q, state.astype(BF16), preferred_element_type=F32) * jnp.exp(wq_age * lg)
        o_ref[0, :, vv] = o_in + o_x
        kw = (k.astype(F32) * jnp.exp(wk_age * lg)).T.astype(BF16)
        g_chunk = jnp.exp(jnp.full((1, RET_V_DIM), c * lg, F32))
        s_ref[hh] = g_chunk * state + jnp.dot(kw, v, preferred_element_type=F32)

    for hh in range(RET_HEADS):
        sweep(qf_ref, kf_ref, vf_ref, of_ref, sf_ref, mf_ref, hh, dec_ref[0, hh], row_v + 1.0, c - 1.0 - row_k)
        sweep(qb_ref, kb_ref, vb_ref, ob_ref, sb_ref, mb_ref, hh, dec_ref[1, hh], c - row_v, row_k)


def _retention(dec, q, k, v, s0f, s0b):
    b, t, _ = q.shape
    c = _blk(RET_CHUNK, t)
    nc = t // c
    fwd = lambda i, n: (i, n, 0)
    bwd = lambda i, n: (i, nc - 1 - n, 0)
    spec_s = pl.BlockSpec((1, RET_HEADS, RET_QK_DIM, RET_V_DIM), lambda i, n: (i, 0, 0, 0))
    out_shape = jax.ShapeDtypeStruct((b, t, RET_V_W), F32)
    return pl.pallas_call(
        _ret_kernel,
        grid=(b, nc),
        in_specs=[pl.BlockSpec(memory_space=pltpu.SMEM),
                  pl.BlockSpec((1, c, RET_QK_W), fwd),
                  pl.BlockSpec((1, c, RET_QK_W), fwd),
                  pl.BlockSpec((1, c, RET_V_W), fwd),
                  pl.BlockSpec((1, c, RET_QK_W), bwd),
                  pl.BlockSpec((1, c, RET_QK_W), bwd),
                  pl.BlockSpec((1, c, RET_V_W), bwd),
                  spec_s, spec_s],
        out_specs=(pl.BlockSpec((1, c, RET_V_W), fwd), pl.BlockSpec((1, c, RET_V_W), bwd)),
        out_shape=(out_shape, out_shape),
        scratch_shapes=[pltpu.VMEM((RET_HEADS, RET_QK_DIM, RET_V_DIM), F32),
                        pltpu.VMEM((RET_HEADS, RET_QK_DIM, RET_V_DIM), F32),
                        pltpu.VMEM((RET_HEADS, c, c), F32),
                        pltpu.VMEM((RET_HEADS, c, c), F32)],
        compiler_params=_params("parallel", "arbitrary"),
        name="retention",
    )(dec, q, k, v, q, k, v, s0f, s0b)


def _merge1_kernel(oa_ref, of_ref, ob_ref, g_ref, *refs, nslab):
    ga_refs, gr_refs = refs[:nslab], refs[nslab:2 * nslab]
    gn_ref, wa_ref, wr_ref, y_ref = refs[2 * nslab:]
    o = of_ref[...] + ob_ref[...]
    heads = []
    for hh in range(RET_HEADS):
        sl = slice(hh * RET_V_DIM, (hh + 1) * RET_V_DIM)
        oh = o[:, sl]
        dlt = oh - jnp.mean(oh, axis=-1, keepdims=True)
        yh = dlt * lax.rsqrt(jnp.mean(dlt * dlt, axis=-1, keepdims=True) + EPS)
        g = g_ref[:, sl]
        heads.append((yh * gn_ref[:, sl] * (g * _sigmoid(g))).astype(BF16))
    o_ret = jnp.concatenate(heads, axis=1)
    tn = ga_refs[0].shape[1]
    for c in range(nslab):
        cols = slice(c * tn, (c + 1) * tn)
        ya = jnp.dot(oa_ref[...], wa_ref[:, cols], preferred_element_type=F32)
        yr = jnp.dot(o_ret, wr_ref[:, cols], preferred_element_type=F32)
        y_ref[:, cols] = (_sigmoid(ga_refs[c][...]) * ya + _sigmoid(gr_refs[c][...]) * yr).astype(y_ref.dtype)


def _merge1(o_att, o_f, o_b, zg, gn, wa, wr):
    r = o_att.shape[0]
    d = wa.shape[1]
    tm = _blk(256, r)
    tn = _blk(1024, d)
    nslab = d // tn
    ga0 = RET_V_W // tn
    gr0 = (RET_V_W + d) // tn
    row = lambda i: (i, 0)
    const = lambda i: (0, 0)
    gate_specs = [pl.BlockSpec((tm, tn), functools.partial(lambda i, c: (i, c), c=c0 + c))
                  for c0 in (ga0, gr0) for c in range(nslab)]
    return pl.pallas_call(
        functools.partial(_merge1_kernel, nslab=nslab),
        grid=(r // tm,),
        in_specs=[pl.BlockSpec((tm, ATT_W), row),
                  pl.BlockSpec((tm, RET_V_W), row),
                  pl.BlockSpec((tm, RET_V_W), row),
                  pl.BlockSpec((tm, RET_V_W), row)]
                 + gate_specs
                 + [pl.BlockSpec((1, RET_V_W), const),
                    pl.BlockSpec((ATT_W, d), const),
                    pl.BlockSpec((RET_V_W, d), const)],
        out_specs=pl.BlockSpec((tm, d), row),
        out_shape=jax.ShapeDtypeStruct((r, d), BF16),
        compiler_params=_params("parallel"),
        name="merge1",
    )(o_att, o_f, o_b, zg, *([zg] * (2 * nslab)), gn.reshape(1, RET_V_W), wa, wr)


def _merge2_kernel(y_ref, w_ref, x_ref, gt_ref, pn1_ref, pn2_ref, sh_ref, sc_ref, wr_ref,
                   x1_ref, h2_ref, aff_ref):
    yy = jnp.dot(y_ref[0], w_ref[...], preferred_element_type=F32)
    r = yy * lax.rsqrt(jnp.mean(yy * yy, axis=-1, keepdims=True) + EPS) * pn1_ref[...]
    x1 = x_ref[0] + gt_ref[0] * r
    x1_ref[0] = x1
    h2 = x1 * lax.rsqrt(jnp.mean(x1 * x1, axis=-1, keepdims=True) + EPS) * pn2_ref[...]
    h2 = h2 * (1.0 + sc_ref[0]) + sh_ref[0]
    h_hi = h2.astype(BF16)
    h2_ref[0] = h_hi
    h_lo = (h2 - h_hi.astype(F32)).astype(BF16)
    wr = wr_ref[...]
    w_hi = wr.astype(BF16)
    w_lo = (wr - w_hi.astype(F32)).astype(BF16)
    ne = wr.shape[1]
    r_hi = jnp.dot(h_hi, jnp.concatenate([w_hi, w_lo], axis=1), preferred_element_type=F32)
    logits = r_hi[:, :ne] + r_hi[:, ne:] + jnp.dot(h_lo, w_hi, preferred_element_type=F32)
    e = jnp.exp(logits - jnp.max(logits, axis=-1, keepdims=True))
    aff_ref[0] = e / jnp.sum(e, axis=-1, keepdims=True)


def _merge2(y, w_out, x, gt1, pn1, pn2, sh2, sc2, w_router):
    b, t, d = x.shape
    tm = _blk(512, t)
    ne = w_router.shape[1]
    tile = lambda i, j: (i, j, 0)
    vec = lambda i, j: (0, 0)
    per_b = lambda i, j: (i, 0, 0)
    return pl.pallas_call(
        _merge2_kernel,
        grid=(b, t // tm),
        in_specs=[pl.BlockSpec((1, tm, d), tile),
                  pl.BlockSpec((d, d), vec),
                  pl.BlockSpec((1, tm, d), tile),
                  pl.BlockSpec((1, 1, d), per_b),
                  pl.BlockSpec((1, d), vec),
                  pl.BlockSpec((1, d), vec),
                  pl.BlockSpec((1, 1, d), per_b),
                  pl.BlockSpec((1, 1, d), per_b),
                  pl.BlockSpec((d, ne), vec)],
        out_specs=(pl.BlockSpec((1, tm, d), tile),
                   pl.BlockSpec((1, tm, d), tile),
                   pl.BlockSpec((1, tm, ne), tile)),
        out_shape=(jax.ShapeDtypeStruct((b, t, d), F32),
                   jax.ShapeDtypeStruct((b, t, d), BF16),
                   jax.ShapeDtypeStruct((b, t, ne), F32)),
        compiler_params=_params("parallel", "parallel"),
        name="merge2",
    )(y.reshape(b, t, d), w_out, x, gt1, pn1.reshape(1, d), pn2.reshape(1, d), sh2, sc2, w_router)


def _lane_cumsum(x, tri):
    off = jnp.zeros((x.shape[0], 1), F32)
    parts = []
    for cidx in range(x.shape[1] // LANES):
        xc = x[:, cidx * LANES:(cidx + 1) * LANES].astype(BF16)
        cs = jnp.dot(xc, tri, preferred_element_type=F32) + off
        parts.append(cs)
        off = cs[:, LANES - 1:LANES]
    return jnp.concatenate(parts, axis=1)


def _topk_kernel(a_ref, pos_ref, fill_ref, *, cap):
    a = a_ref[0]
    bits = jnp.zeros((a.shape[0], 1), jnp.int32)
    for bit in range(30, -1, -1):
        cand = bits | (1 << bit)
        cnt = jnp.sum(jnp.where(a >= lax.bitcast_convert_type(cand, F32), 1.0, 0.0), axis=1, keepdims=True)
        bits = jnp.where(cnt >= cap, cand, bits)
    bits = jnp.where(bits < MIN_NORMAL_F32_BITS, 0, bits)
    thr = lax.bitcast_convert_type(bits, F32)
    ri = lax.broadcasted_iota(jnp.int32, (LANES, LANES), 0)
    ci = lax.broadcasted_iota(jnp.int32, (LANES, LANES), 1)
    tri = jnp.where(ri <= ci, 1.0, 0.0).astype(BF16)
    gt = a > thr
    eq = a == thr
    n_gt = jnp.sum(jnp.where(gt, 1.0, 0.0), axis=1, keepdims=True)
    eq_rank = _lane_cumsum(jnp.where(eq, 1.0, 0.0), tri)
    sel = jnp.where(gt, 1.0, jnp.where(eq, jnp.where(eq_rank <= cap - n_gt, 1.0, 0.0), 0.0))
    filled = _lane_cumsum(sel, tri)
    pos_ref[0] = filled * sel
    fill_ref[0] = filled


def _topk_slots(aff_t, cap):
    b, ne, t = aff_t.shape
    spec = pl.BlockSpec((1, ne, t), lambda i: (i, 0, 0))
    shape = jax.ShapeDtypeStruct((b, ne, t), F32)
    return pl.pallas_call(
        functools.partial(_topk_kernel, cap=cap),
        grid=(b,),
        in_specs=[spec],
        out_specs=(spec, spec),
        out_shape=(shape, shape),
        compiler_params=_params("parallel"),
        name="topk",
    )(aff_t)


def _gather_kernel(pos_ref, h_ref, o_ref):
    cap = o_ref.shape[1]
    t = h_ref.shape[1]
    tc = _blk(1024, t)
    slot = lax.broadcasted_iota(jnp.int32, (cap, tc), 0).astype(F32) + 1.0
    acc = None
    for cidx in range(t // tc):
        row = pos_ref[0, 0, :, cidx * tc:(cidx + 1) * tc]
        onehot = jnp.where(row == slot, 1.0, 0.0).astype(BF16)
        part = jnp.dot(onehot, h_ref[0, cidx * tc:(cidx + 1) * tc, :], preferred_element_type=F32)
        acc = part if acc is None else acc + part
    o_ref[0] = acc.astype(o_ref.dtype)


def _gather(pos, h2, cap):
    b, ne, t = pos.shape
    d = h2.shape[-1]
    dn = _blk(1024, d)
    return pl.pallas_call(
        _gather_kernel,
        grid=(b, d // dn, ne),
        in_specs=[pl.BlockSpec((1, 1, 1, t), lambda i, n, e: (i, e, 0, 0)),
                  pl.BlockSpec((1, t, dn), lambda i, n, e: (i, 0, n))],
        out_specs=pl.BlockSpec((1, cap, dn), lambda i, n, e: (e, i, n)),
        out_shape=jax.ShapeDtypeStruct((ne, b * cap, d), BF16),
        compiler_params=_params("parallel", "parallel", "parallel"),
        name="gather",
    )(pos.reshape(b, ne, 1, t), h2)


def _gather_win_kernel(w0_ref, pos_ref, h_ref, o_ref, acc_ref, lhs_ref):
    b = pl.program_id(0)
    ne, cap = o_ref.shape[0], o_ref.shape[1]
    nt = h_ref.shape[1] // TOK_TILE
    slot1 = lax.broadcasted_iota(jnp.int32, (GATHER_WIN, TOK_TILE), 0) + 1

    @pl.when(pl.program_id(1) == 0)
    def _():
        def build(k, carry):
            for e in range(ne):
                w0 = w0_ref[(b * ne + e) * nt + k]
                row = pos_ref[0, pl.ds(e * nt + k, 1), :]
                onehot = jnp.where(row == (slot1 + w0).astype(F32), 1.0, 0.0)
                lhs_ref[k, e * GATHER_WIN:(e + 1) * GATHER_WIN, :] = onehot.astype(BF16)
            return carry
        lax.fori_loop(0, nt, build, 0)

    acc_ref[:, :8, :] = jnp.zeros((ne, 8) + acc_ref.shape[2:], F32)

    def tile(k, carry):
        t0 = pl.multiple_of(k * TOK_TILE, TOK_TILE)
        res = jnp.dot(lhs_ref[k], h_ref[0, pl.ds(t0, TOK_TILE), :], preferred_element_type=F32)
        for e in range(ne):
            w0 = pl.multiple_of(w0_ref[(b * ne + e) * nt + k], 8)
            acc_ref[e, pl.ds(w0, 8), :] += res[e * GATHER_WIN:e * GATHER_WIN + 8]
            acc_ref[e, pl.ds(w0 + 8, GATHER_WIN - 8), :] = res[e * GATHER_WIN + 8:(e + 1) * GATHER_WIN]
        return carry
    lax.fori_loop(0, nt, tile, 0)
    o_ref[...] = acc_ref[:, :cap, :].astype(o_ref.dtype)


def _gather_win(w0, pos, h2, cap):
    b, ne, t = pos.shape
    d = h2.shape[-1]
    nt = t // TOK_TILE
    dn = _blk(256, d)
    return pl.pallas_call(
        _gather_win_kernel,
        grid_spec=pltpu.PrefetchScalarGridSpec(
            num_scalar_prefetch=1,
            grid=(b, d // dn),
            in_specs=[pl.BlockSpec((1, ne * nt, TOK_TILE), lambda i, n, w: (i, 0, 0)),
                      pl.BlockSpec((1, t, dn), lambda i, n, w: (i, 0, n))],
            out_specs=pl.BlockSpec((ne, cap, dn), lambda i, n, w: (0, i, n)),
            scratch_shapes=[pltpu.VMEM((ne, cap + GATHER_WIN, dn), F32),
                            pltpu.VMEM((nt, ne * GATHER_WIN, TOK_TILE), BF16)]),
        out_shape=jax.ShapeDtypeStruct((ne, b * cap, d), BF16),
        compiler_params=_params("parallel", "arbitrary"),
        name="gather_win",
    )(w0, pos.reshape(b, ne * nt, TOK_TILE), h2)


def _ffn_kernel(x_ref, wg_ref, wu_ref, wd_ref, o_ref, h_ref, wgu_ref, *, nf):
    j = pl.program_id(2)
    tf = wg_ref.shape[2]

    @pl.when(j < nf)
    def _():
        wgu_ref[:, :tf] = wg_ref[0].astype(BF16)
        wgu_ref[:, tf:] = wu_ref[0].astype(BF16)
        au = jnp.dot(x_ref[0], wgu_ref[...], preferred_element_type=F32)
        a = au[:, :tf]
        h_ref[j] = (a * _sigmoid(a) * au[:, tf:]).astype(BF16)

    @pl.when(j >= nf)
    def _():
        acc = None
        for cidx in range(nf):
            part = jnp.dot(h_ref[cidx], wd_ref[0, cidx * tf:(cidx + 1) * tf, :].astype(BF16),
                           preferred_element_type=F32)
            acc = part if acc is None else acc + part
        o_ref[0] = acc.astype(o_ref.dtype)


def _ffn(xg, w_gate, w_up, w_down):
    ne, rows, d = xg.shape
    ff = w_gate.shape[-1]
    tm = _blk(2048, rows)
    tf = _blk(256, ff)
    tn = _blk(512, d)
    nf = ff // tf
    return pl.pallas_call(
        functools.partial(_ffn_kernel, nf=nf),
        grid=(ne, rows // tm, nf + d // tn),
        in_specs=[pl.BlockSpec((1, tm, d), lambda e, m, j: (e, m, 0)),
                  pl.BlockSpec((1, d, tf), lambda e, m, j: (e, 0, jnp.minimum(j, nf - 1))),
                  pl.BlockSpec((1, d, tf), lambda e, m, j: (e, 0, jnp.minimum(j, nf - 1))),
                  pl.BlockSpec((1, ff, tn), lambda e, m, j: (e, 0, jnp.maximum(j - nf, 0)))],
        out_specs=pl.BlockSpec((1, tm, tn), lambda e, m, j: (e, m, jnp.maximum(j - nf, 0))),
        out_shape=jax.ShapeDtypeStruct((ne, rows, d), BF16),
        scratch_shapes=[pltpu.VMEM((nf, tm, tf), BF16), pltpu.VMEM((d, 2 * tf), BF16)],
        compiler_params=_params("parallel", "parallel", "arbitrary"),
        name="ffn",
    )(xg, w_gate, w_up, w_down)


def _combine_kernel(pos_ref, aff_ref, y_ref, x1_ref, gt_ref, pn_ref, o_ref, acc_ref):
    e = pl.program_id(2)
    tt, ne = pos_ref.shape[1], pos_ref.shape[2]
    cap = y_ref.shape[1]
    pick = lax.broadcasted_iota(jnp.int32, (tt, ne), 1) == e
    pcol = jnp.sum(jnp.where(pick, pos_ref[0], 0.0), axis=1, keepdims=True)
    acol = jnp.sum(jnp.where(pick, aff_ref[0], 0.0), axis=1, keepdims=True)
    slot = lax.broadcasted_iota(jnp.int32, (tt, cap), 1).astype(F32) + 1.0
    onehot = jnp.where(pcol == slot, 1.0, 0.0).astype(BF16)
    z = jnp.dot(onehot, y_ref[0], preferred_element_type=F32) * acol

    @pl.when(e == 0)
    def _():
        acc_ref[...] = z

    @pl.when(e > 0)
    def _():
        acc_ref[...] += z

    @pl.when(e == pl.num_programs(2) - 1)
    def _():
        y2 = acc_ref[...]
        r = y2 * lax.rsqrt(jnp.mean(y2 * y2, axis=-1, keepdims=True) + EPS) * pn_ref[...]
        o_ref[0] = x1_ref[0] + gt_ref[0] * r


def _combine(pos_t, aff, y, x1, gt2, pn2, cap):
    b, t, d = x1.shape
    ne = aff.shape[-1]
    tt = _blk(512, t)
    return pl.pallas_call(
        _combine_kernel,
        grid=(b, t // tt, ne),
        in_specs=[pl.BlockSpec((1, tt, ne), lambda i, j, e: (i, j, 0)),
                  pl.BlockSpec((1, tt, ne), lambda i, j, e: (i, j, 0)),
                  pl.BlockSpec((1, cap, d), lambda i, j, e: (e, i, 0)),
                  pl.BlockSpec((1, tt, d), lambda i, j, e: (i, j, 0)),
                  pl.BlockSpec((1, 1, d), lambda i, j, e: (i, 0, 0)),
                  pl.BlockSpec((1, d), lambda i, j, e: (0, 0))],
        out_specs=pl.BlockSpec((1, tt, d), lambda i, j, e: (i, j, 0)),
        out_shape=jax.ShapeDtypeStruct((b, t, d), F32),
        scratch_shapes=[pltpu.VMEM((tt, d), F32)],
        compiler_params=_params("parallel", "parallel", "arbitrary"),
        name="combine",
    )(pos_t, aff, y, x1, gt2, pn2.reshape(1, d))


def _combine_win_kernel(blk_ref, base_ref, pos_ref, aff_ref, *refs):
    del blk_ref
    ne = pos_ref.shape[2]
    y_refs = refs[:2 * ne]
    x1_ref, gt_ref, pn_ref, o_ref = refs[2 * ne:]
    b = pl.program_id(0)
    k = pl.program_id(1)
    nt = pl.num_programs(1)
    lane = lax.broadcasted_iota(jnp.int32, (TOK_TILE, 2 * SLOT_BLK), 1)
    second = lane >= SLOT_BLK
    pos = pos_ref[0]
    aff = aff_ref[0]
    cols = []
    for e in range(ne):
        at = ((b * ne + e) * nt + k) * 2
        slot = jnp.where(second, lane + (base_ref[at + 1] + 1 - SLOT_BLK), lane + (base_ref[at] + 1))
        cols.append(jnp.where(pos[:, e:e + 1] == slot.astype(F32), aff[:, e:e + 1], 0.0).astype(BF16))
    ywin = jnp.concatenate([r[...] for r in y_refs], axis=0)
    y2 = jnp.dot(jnp.concatenate(cols, axis=1), ywin, preferred_element_type=F32)
    r = y2 * lax.rsqrt(jnp.mean(y2 * y2, axis=-1, keepdims=True) + EPS) * pn_ref[...]
    o_ref[0] = x1_ref[0] + gt_ref[0] * r


def _combine_win(blk, base, pos_t, aff, y, x1, gt2, pn2):
    b, t, d = x1.shape
    ne = aff.shape[-1]
    nt = t // TOK_TILE

    def y_spec(e, j):
        return pl.BlockSpec((SLOT_BLK, d),
                            lambda i, k, blk_ref, base_ref: (blk_ref[((i * ne + e) * nt + k) * 2 + j], 0))

    tile = lambda i, k, blk_ref, base_ref: (i, k, 0)
    y2d = y.reshape(-1, d)
    return pl.pallas_call(
        _combine_win_kernel,
        grid_spec=pltpu.PrefetchScalarGridSpec(
            num_scalar_prefetch=2,
            grid=(b, nt),
            in_specs=[pl.BlockSpec((1, TOK_TILE, ne), tile),
                      pl.BlockSpec((1, TOK_TILE, ne), tile)]
                     + [y_spec(e, j) for e in range(ne) for j in range(2)]
                     + [pl.BlockSpec((1, TOK_TILE, d), tile),
                        pl.BlockSpec((1, 1, d), lambda i, k, blk_ref, base_ref: (i, 0, 0)),
                        pl.BlockSpec((1, d), lambda i, k, blk_ref, base_ref: (0, 0))],
            out_specs=pl.BlockSpec((1, TOK_TILE, d), tile)),
        out_shape=jax.ShapeDtypeStruct((b, t, d), F32),
        compiler_params=_params("parallel", "parallel"),
        name="combine_win",
    )(blk, base, pos_t, aff, *([y2d] * (2 * ne)), x1, gt2, pn2.reshape(1, d))


def _routing_tables(fill, cap):
    b, ne, t = fill.shape
    nblk = cap // SLOT_BLK
    ends = fill[:, :, TOK_TILE - 1::TOK_TILE].astype(jnp.int32)
    starts = jnp.concatenate([jnp.zeros((b, ne, 1), jnp.int32), ends[:, :, :-1]], axis=-1)
    sparse = jnp.all(ends - starts <= SLOT_BLK)
    first = jnp.minimum(starts // SLOT_BLK, nblk - 1)
    second = first + 1
    group = (jnp.arange(ne)[None, :, None] * b + jnp.arange(b)[:, None, None]) * nblk
    blk = jnp.stack([group + first, group + jnp.minimum(second, nblk - 1)], axis=-1)
    base = jnp.stack([first * SLOT_BLK, jnp.where(second < nblk, second * SLOT_BLK, -(1 << 20))], axis=-1)
    return sparse, (starts // 8 * 8).reshape(-1), blk.reshape(-1), base.reshape(-1)


def _rope_tables(t):
    pos = jnp.arange(t)
    r = (pos // GRID_W).astype(F32)
    cl = (pos % GRID_W).astype(F32)
    quarter = HEAD_DIM // 4
    inv = ROPE_THETA ** (-jnp.arange(quarter, dtype=F32) / quarter)
    ang_r = r[:, None] * inv
    ang_c = cl[:, None] * inv
    ang = jnp.concatenate([ang_r, ang_r, ang_c, ang_c], axis=-1)
    return jnp.cos(ang), jnp.sin(ang)


def _kv_heads(h, w_in, q_width, k_norm, rope):
    plan = (((0, ATT_KV_W, "norm_rope", 0, 1.0), (ATT_KV_W, 2 * ATT_KV_W, "plain", 1, 1.0)),
            ((0, RET_QK_W, "rope", 2, RET_QK_DIM ** -0.5),),
            ((0, 512, "plain", 3, 1.0),),
            ((0, 512, "plain", 3, 1.0),))
    outs = ((ATT_KV_W, ATT_KV_W, lambda j: 0), (ATT_KV_W, ATT_KV_W, lambda j: 0),
            (RET_QK_W, RET_QK_W, lambda j: 0), (RET_V_W, 512, lambda j: jnp.clip(j - 2, 0, 1)))
    return _proj_heads(h, w_in, q_width, plan, outs, k_norm, rope)


def _q_heads(h, w_in, q_norm, rope):
    plan = (((0, 512, "norm_rope", 0, 1.0),), ((0, 512, "norm_rope", 0, 1.0),), ((0, RET_QK_W, "rope", 1, 1.0),))
    outs = ((ATT_W, 512, lambda j: jnp.minimum(j, 1)), (RET_QK_W, RET_QK_W, lambda j: 0))
    return _proj_heads(h, w_in, 0, plan, outs, q_norm, rope)


def kernel(x, c, ctx, c_ctx, w_mod, b_mod, pre_norm1, post_norm1, pre_norm2, post_norm2, w_in, q_norm,
           k_norm, ret_decay, ret_gn, w_o_att, w_o_ret, w_out, w_router, w_gate, w_up, w_down):
    b, t, d = x.shape
    n_ctx = ctx.shape[1]
    depth = w_mod.shape[0]
    q_width = ATT_W + RET_QK_W + RET_V_W + 2 * d
    cap = EC_FACTOR * t // N_EXPERTS
    rope_lat = tuple(jnp.tile(tbl, (b, 1)) for tbl in _rope_tables(t))
    rope_ctx = (jnp.ones((b * n_ctx, HEAD_DIM), F32), jnp.zeros((b * n_ctx, HEAD_DIM), F32))
    cs = jnp.zeros((8, d), F32).at[:b].set(c).at[b].set(c_ctx)
    xc = ctx
    for layer in range(depth):
        assert layer == depth - 1, "context-stream update between layers is not implemented"
        mod = _mod_vectors(cs, w_mod[layer], b_mod[layer])
        sh1, sc1, gt1, sh2, sc2, gt2 = [m[:b, None, :] for m in jnp.split(mod, 6, axis=-1)]
        csh1, csc1 = [jnp.broadcast_to(m[b][None, None, :], (b, 1, d)) for m in jnp.split(mod, 6, axis=-1)[:2]]
        dec = -jax.nn.softplus(ret_decay[layer].astype(F32))
        wl = w_in[layer]

        hc = _prenorm(xc, pre_norm1[layer], csh1, csc1).reshape(b * n_ctx, d)
        kc_a, vc_a, kc_r, vc_r = _kv_heads(hc, wl, q_width, k_norm[layer], rope_ctx)
        s0f, s0b = _ctx_states(dec, kc_r.reshape(b, n_ctx, RET_QK_W), vc_r.reshape(b, n_ctx, RET_V_W))

        h = _prenorm(x, pre_norm1[layer], sh1, sc1).reshape(b * t, d)
        q_a, q_r = _q_heads(h, wl, q_norm[layer], rope_lat)
        zg = _proj(h, wl, ATT_W + RET_QK_W, RET_V_W + 2 * d, out_dtype=F32, tn=512)
        k_a, v_a, k_r, v_r = _kv_heads(h, wl, q_width, k_norm[layer], rope_lat)

        keys = jnp.concatenate([kc_a.reshape(b, n_ctx, ATT_KV_W), k_a.reshape(b, t, ATT_KV_W)], axis=1)
        vals = jnp.concatenate([vc_a.reshape(b, n_ctx, ATT_KV_W), v_a.reshape(b, t, ATT_KV_W)], axis=1)
        kt = keys.reshape(b, n_ctx + t, ATT_KV_HEADS, HEAD_DIM).transpose(0, 2, 3, 1)
        o_att = _attention(q_a.reshape(b, t, ATT_W), kt, vals)

        o_f, o_b = _retention(dec, q_r.reshape(b, t, RET_QK_W), k_r.reshape(b, t, RET_QK_W),
                              v_r.reshape(b, t, RET_V_W), s0f, s0b)

        y = _merge1(o_att.reshape(b * t, ATT_W), o_f.reshape(b * t, RET_V_W), o_b.reshape(b * t, RET_V_W),
                    zg, ret_gn[layer], w_o_att[layer].astype(BF16), w_o_ret[layer].astype(BF16))
        x1, h2, aff = _merge2(y, w_out[layer].astype(BF16), x, gt1, post_norm1[layer], pre_norm2[layer],
                              sh2, sc2, w_router[layer])

        pos, fill = _topk_slots(aff.transpose(0, 2, 1), cap)
        pos_t = pos.transpose(0, 2, 1)
        sparse, w0, blk, base = _routing_tables(fill, cap)
        xg = lax.cond(sparse,
                      lambda: _gather_win(w0, pos, h2, cap),
                      lambda: _gather(pos, h2, cap))
        yg = _ffn(xg, w_gate[layer], w_up[layer], w_down[layer])
        x = lax.cond(sparse,
                     lambda: _combine_win(blk, base, pos_t, aff, yg, x1, gt2, post_norm2[layer]),
                     lambda: _combine(pos_t, aff, yg, x1, gt2, post_norm2[layer], cap))
    return x
```

```python
import functools

import jax
import jax.numpy as jnp
import numpy as np
from jax import lax
from jax.experimental import pallas as pl
from jax.experimental.pallas import tpu as pltpu

F32 = jnp.float32
BF16 = jnp.bfloat16

GRID_W = 64
HEAD_DIM = 128
ATT_HEADS = 8
ATT_KV_HEADS = 2
ATT_GROUP = ATT_HEADS // ATT_KV_HEADS
RET_HEADS = 4
RET_QK_DIM = 128
RET_V_DIM = 256
ROPE_THETA = 10000.0
N_EXPERTS = 16
EC_FACTOR = 2
EPS = 1e-6

ATT_W = ATT_HEADS * HEAD_DIM
ATT_KV_W = ATT_KV_HEADS * HEAD_DIM
RET_QK_W = RET_HEADS * RET_QK_DIM
RET_V_W = RET_HEADS * RET_V_DIM

LANES = 128
VMEM_LIMIT = 56 * 1024 * 1024
RET_CHUNK = 256
MIN_NORMAL_F32_BITS = 0x00800000
LOG2_E = 1.4426950408889634
TOK_TILE = 256
SLOT_BLK = 64
GATHER_WIN = SLOT_BLK + 16
COMBINE_GROUP = 4


def _blk(pref, n):
    return pref if n % pref == 0 else n


def _params(*sem):
    return pltpu.CompilerParams(dimension_semantics=sem, vmem_limit_bytes=VMEM_LIMIT)


def _sigmoid(x):
    return 0.5 * jnp.tanh(0.5 * x) + 0.5


def _mod_kernel(s_ref, w_ref, b_ref, o_ref):
    s = s_ref[...]
    s = s * _sigmoid(s)
    o_ref[...] = jnp.dot(s, w_ref[...], precision=lax.Precision.HIGHEST,
                         preferred_element_type=F32) + b_ref[...]


def _mod_vectors(cs, w_mod, b_mod):
    rows, d = cs.shape
    n = w_mod.shape[1]
    tn = _blk(2048, n)
    return pl.pallas_call(
        _mod_kernel,
        grid=(n // tn,),
        in_specs=[pl.BlockSpec((rows, d), lambda j: (0, 0)),
                  pl.BlockSpec((d, tn), lambda j: (0, j)),
                  pl.BlockSpec((1, tn), lambda j: (0, j))],
        out_specs=pl.BlockSpec((rows, tn), lambda j: (0, j)),
        out_shape=jax.ShapeDtypeStruct((rows, n), F32),
        compiler_params=_params("parallel"),
        name="mod",
    )(cs, w_mod, b_mod.reshape(1, n))


def _prenorm_kernel(x_ref, g_ref, sh_ref, sc_ref, o_ref):
    x = x_ref[0]
    y = x * lax.rsqrt(jnp.mean(x * x, axis=-1, keepdims=True) + EPS) * g_ref[...]
    o_ref[0] = (y * (1.0 + sc_ref[0]) + sh_ref[0]).astype(o_ref.dtype)


def _prenorm(x, gain, shift, scale):
    b, t, d = x.shape
    tm = _blk(1024, t)
    return pl.pallas_call(
        _prenorm_kernel,
        grid=(b, t // tm),
        in_specs=[pl.BlockSpec((1, tm, d), lambda i, j: (i, j, 0)),
                  pl.BlockSpec((1, d), lambda i, j: (0, 0)),
                  pl.BlockSpec((1, 1, d), lambda i, j: (i, 0, 0)),
                  pl.BlockSpec((1, 1, d), lambda i, j: (i, 0, 0))],
        out_specs=pl.BlockSpec((1, tm, d), lambda i, j: (i, j, 0)),
        out_shape=jax.ShapeDtypeStruct((b, t, d), BF16),
        compiler_params=_params("parallel", "parallel"),
        name="prenorm",
    )(x, gain.reshape(1, d), shift, scale)


def _split_bf16(v):
    hi = v.astype(BF16)
    return [hi, (v - hi.astype(F32)).astype(BF16)]


def _rotate_half_matrix():
    quarter = HEAD_DIM // 4
    r = np.zeros((HEAD_DIM, HEAD_DIM), np.float32)
    for i in range(HEAD_DIM):
        if i % (2 * quarter) < quarter:
            r[i + quarter, i] = -1.0
        else:
            r[i - quarter, i] = 1.0
    return r


def _head_mix_matrix(mode):
    rot = _rotate_half_matrix()
    if mode == "rope":
        return jnp.asarray(np.concatenate([rot, rot], axis=0), BF16)
    ones = np.ones((HEAD_DIM, HEAD_DIM), np.float32)
    zero = np.zeros((HEAD_DIM, HEAD_DIM), np.float32)
    return jnp.asarray(np.block([[ones, zero], [ones, zero], [zero, rot], [zero, rot]]), BF16)


def _proj_kernel(h_ref, w_ref, o_ref):
    o_ref[...] = jnp.dot(h_ref[...], w_ref[...].astype(BF16), preferred_element_type=F32).astype(o_ref.dtype)


def _proj(h, w_in, col0, width, *, out_dtype, tn):
    r, d = h.shape
    tm = _blk(2048, r)
    assert col0 % tn == 0 and width % tn == 0
    c0 = col0 // tn
    return pl.pallas_call(
        _proj_kernel,
        grid=(r // tm, width // tn),
        in_specs=[pl.BlockSpec((tm, d), lambda i, j: (i, 0)),
                  pl.BlockSpec((d, tn), lambda i, j: (0, c0 + j))],
        out_specs=pl.BlockSpec((tm, tn), lambda i, j: (i, j)),
        out_shape=jax.ShapeDtypeStruct((r, width), out_dtype),
        compiler_params=_params("parallel", "parallel"),
        name="proj_plain",
    )(h, w_in)


def _rope_head(xh, mode, gain, mix, cos, sin, pre_scale):
    if mode == "norm_rope":
        xg = xh * gain
        mixed = jnp.dot(jnp.concatenate(_split_bf16(xh * xh) + _split_bf16(xg), axis=1), mix,
                        preferred_element_type=F32)
        inv = lax.rsqrt(mixed[:, :HEAD_DIM] * (1.0 / HEAD_DIM) + EPS)
        return (xg * cos + mixed[:, HEAD_DIM:] * sin) * inv
    xs = xh * pre_scale if pre_scale != 1.0 else xh
    rot = jnp.dot(jnp.concatenate(_split_bf16(xs), axis=1), mix, preferred_element_type=F32)
    return xs * cos + rot * sin


def _proj_heads_kernel(h_ref, w_ref, gain_ref, mixn_ref, mixr_ref, cos_ref, sin_ref, *o_refs, plan):
    acc = jnp.dot(h_ref[...], w_ref[...].astype(BF16), preferred_element_type=F32)
    for jj, pieces in enumerate(plan):
        @pl.when(pl.program_id(1) == jj)
        def _(pieces=pieces):
            for lo, hi, mode, out, pre_scale in pieces:
                o_ref = o_refs[out]
                if mode == "plain":
                    o_ref[...] = acc[:, lo:hi].astype(o_ref.dtype)
                    continue
                mix = mixn_ref[...] if mode == "norm_rope" else mixr_ref[...]
                for hh in range((hi - lo) // HEAD_DIM):
                    xh = acc[:, lo + hh * HEAD_DIM:lo + (hh + 1) * HEAD_DIM]
                    res = _rope_head(xh, mode, gain_ref[...], mix, cos_ref[...], sin_ref[...], pre_scale)
                    o_ref[:, hh * HEAD_DIM:(hh + 1) * HEAD_DIM] = res.astype(o_ref.dtype)


def _proj_heads(h, w_in, col0, plan, outs, gain, rope):
    r, d = h.shape
    tm = _blk(2048, r)
    tn = 512
    assert col0 % tn == 0
    c0 = col0 // tn
    cos, sin = rope
    assert cos.shape == (r, HEAD_DIM)
    mixn, mixr = _head_mix_matrix("norm_rope"), _head_mix_matrix("rope")
    const = lambda i, j: (0, 0)
    rows = lambda i, j: (i, 0)
    return pl.pallas_call(
        functools.partial(_proj_heads_kernel, plan=plan),
        grid=(r // tm, len(plan)),
        in_specs=[pl.BlockSpec((tm, d), rows),
                  pl.BlockSpec((d, tn), lambda i, j: (0, c0 + j)),
                  pl.BlockSpec((1, HEAD_DIM), const),
                  pl.BlockSpec(mixn.shape, const),
                  pl.BlockSpec(mixr.shape, const),
                  pl.BlockSpec((tm, HEAD_DIM), rows),
                  pl.BlockSpec((tm, HEAD_DIM), rows)],
        out_specs=tuple(pl.BlockSpec((tm, bw), functools.partial(lambda i, j, f: (i, f(j)), f=f))
                        for _, bw, f in outs),
        out_shape=tuple(jax.ShapeDtypeStruct((r, width), BF16) for width, _, _ in outs),
        compiler_params=_params("parallel", "arbitrary"),
        name="proj_heads",
    )(h, w_in, gain.reshape(1, HEAD_DIM), mixn, mixr, cos, sin)


def _ctx_state_kernel(dec_ref, k_ref, v_ref, sf_ref, sb_ref):
    hh = pl.program_id(1)
    lf = dec_ref[0, hh]
    lb = dec_ref[1, hh]
    k = k_ref[0].astype(F32)
    v = v_ref[0]
    n = k.shape[0]
    pos = lax.broadcasted_iota(jnp.int32, k.shape, 0).astype(F32)
    kf = (k * jnp.exp((n - 1.0 - pos) * lf)).T.astype(BF16)
    kb = (k * jnp.exp(pos * lb)).T.astype(BF16)
    sf_ref[0, 0] = jnp.dot(kf, v, preferred_element_type=F32)
    sb_ref[0, 0] = jnp.dot(kb, v, preferred_element_type=F32)


def _ctx_states(dec, k_r, v_r):
    b, n, _ = k_r.shape
    spec_s = pl.BlockSpec((1, 1, RET_QK_DIM, RET_V_DIM), lambda i, j: (i, j, 0, 0))
    shape_s = jax.ShapeDtypeStruct((b, RET_HEADS, RET_QK_DIM, RET_V_DIM), F32)
    return pl.pallas_call(
        _ctx_state_kernel,
        grid=(b, RET_HEADS),
        in_specs=[pl.BlockSpec(memory_space=pltpu.SMEM),
                  pl.BlockSpec((1, n, RET_QK_DIM), lambda i, j: (i, 0, j)),
                  pl.BlockSpec((1, n, RET_V_DIM), lambda i, j: (i, 0, j))],
        out_specs=(spec_s, spec_s),
        out_shape=(shape_s, shape_s),
        compiler_params=_params("parallel", "parallel"),
        name="ctx_state",
    )(dec, k_r, v_r)


def _attn_kernel(q_ref, kt_ref, v_ref, o_ref, *, scale):
    kt = kt_ref[0, 0]
    v = v_ref[0]
    for g in range(ATT_GROUP):
        q = q_ref[0, :, g * HEAD_DIM:(g + 1) * HEAD_DIM]
        s = jnp.dot(q, kt, preferred_element_type=F32)
        m = jnp.max(s, axis=-1, keepdims=True)
        p = jnp.exp2((s - m) * (scale * LOG2_E))
        l = jnp.sum(p, axis=-1, keepdims=True)
        o = jnp.dot(p.astype(BF16), v, preferred_element_type=F32)
        o_ref[0, :, g * HEAD_DIM:(g + 1) * HEAD_DIM] = (o / l).astype(o_ref.dtype)


def _attention(q, kt, v):
    b, t, _ = q.shape
    s = kt.shape[-1]
    tq = _blk(256, t)
    gw = ATT_GROUP * HEAD_DIM
    return pl.pallas_call(
        functools.partial(_attn_kernel, scale=HEAD_DIM ** -0.5),
        grid=(b, ATT_KV_HEADS, t // tq),
        in_specs=[pl.BlockSpec((1, tq, gw), lambda i, j, n: (i, n, j)),
                  pl.BlockSpec((1, 1, HEAD_DIM, s), lambda i, j, n: (i, j, 0, 0)),
                  pl.BlockSpec((1, s, HEAD_DIM), lambda i, j, n: (i, 0, j))],
        out_specs=pl.BlockSpec((1, tq, gw), lambda i, j, n: (i, n, j)),
        out_shape=jax.ShapeDtypeStruct((b, t, ATT_W), BF16),
        compiler_params=_params("parallel", "parallel", "parallel"),
        name="attention",
    )(q, kt, v)


def _ret_kernel(dec_ref, qf_ref, kf_ref, vf_ref, qb_ref, kb_ref, vb_ref, s0f_ref, s0b_ref,
                of_ref, ob_ref, sf_ref, sb_ref, mf_ref, mb_ref):
    n = pl.program_id(1)
    c = qf_ref.shape[1]

    @pl.when(n == 0)
    def _():
        sf_ref[...] = s0f_ref[0]
        sb_ref[...] = s0b_ref[0]
        ri = lax.broadcasted_iota(jnp.int32, (c, c), 0)
        ci = lax.broadcasted_iota(jnp.int32, (c, c), 1)
        d = (ri - ci).astype(F32)
        for hh in range(RET_HEADS):
            mf_ref[hh] = jnp.where(d >= 0, jnp.exp(jnp.maximum(d, 0.0) * dec_ref[0, hh]), 0.0)
            mb_ref[hh] = jnp.where(d <= 0, jnp.exp(jnp.maximum(-d, 0.0) * dec_ref[1, hh]), 0.0)

    nt = (((1,), (1,)), ((), ()))
    row_v = lax.broadcasted_iota(jnp.int32, (c, RET_V_DIM), 0).astype(F32)
    row_k = lax.broadcasted_iota(jnp.int32, (c, RET_QK_DIM), 0).astype(F32)

    def sweep(q_ref, k_ref, v_ref, o_ref, s_ref, m_ref, hh, lg, wq_age, wk_age):
        qk = slice(hh * RET_QK_DIM, (hh + 1) * RET_QK_DIM)
        vv = slice(hh * RET_V_DIM, (hh + 1) * RET_V_DIM)
        q = q_ref[0, :, qk]
        k = k_ref[0, :, qk]
        v = v_ref[0, :, vv]
        s = lax.dot_general(q, k, nt, preferred_element_type=F32) * m_ref[hh]
        o_in = jnp.dot(s.astype(BF16), v, preferred_element_type=F32)
        state = s_ref[hh]
        o_x = jnp.dot(q, state.astype(BF16), preferred_element_type=F32) * jnp.exp(wq_age * lg)
        o_ref[0, :, vv] = o_in + o_x
        kw = (k.astype(F32) * jnp.exp(wk_age * lg)).T.astype(BF16)
        g_chunk = jnp.exp(jnp.full((1, RET_V_DIM), c * lg, F32))
        s_ref[hh] = g_chunk * state + jnp.dot(kw, v, preferred_element_type=F32)

    for hh in range(RET_HEADS):
        sweep(qf_ref, kf_ref, vf_ref, of_ref, sf_ref, mf_ref, hh, dec_ref[0, hh], row_v + 1.0, c - 1.0 - row_k)
        sweep(qb_ref, kb_ref, vb_ref, ob_ref, sb_ref, mb_ref, hh, dec_ref[1, hh], c - row_v, row_k)


def _retention(dec, q, k, v, s0f, s0b):
    b, t, _ = q.shape
    c = _blk(RET_CHUNK, t)
    nc = t // c
    fwd = lambda i, n: (i, n, 0)
    bwd = lambda i, n: (i, nc - 1 - n, 0)
    spec_s = pl.BlockSpec((1, RET_HEADS, RET_QK_DIM, RET_V_DIM), lambda i, n: (i, 0, 0, 0))
    out_shape = jax.ShapeDtypeStruct((b, t, RET_V_W), F32)
    return pl.pallas_call(
        _ret_kernel,
        grid=(b, nc),
        in_specs=[pl.BlockSpec(memory_space=pltpu.SMEM),
                  pl.BlockSpec((1, c, RET_QK_W), fwd),
                  pl.BlockSpec((1, c, RET_QK_W), fwd),
                  pl.BlockSpec((1, c, RET_V_W), fwd),
                  pl.BlockSpec((1, c, RET_QK_W), bwd),
                  pl.BlockSpec((1, c, RET_QK_W), bwd),
                  pl.BlockSpec((1, c, RET_V_W), bwd),
                  spec_s, spec_s],
        out_specs=(pl.BlockSpec((1, c, RET_V_W), fwd), pl.BlockSpec((1, c, RET_V_W), bwd)),
        out_shape=(out_shape, out_shape),
        scratch_shapes=[pltpu.VMEM((RET_HEADS, RET_QK_DIM, RET_V_DIM), F32),
                        pltpu.VMEM((RET_HEADS, RET_QK_DIM, RET_V_DIM), F32),
                        pltpu.VMEM((RET_HEADS, c, c), F32),
                        pltpu.VMEM((RET_HEADS, c, c), F32)],
        compiler_params=_params("parallel", "arbitrary"),
        name="retention",
    )(dec, q, k, v, q, k, v, s0f, s0b)


def _merge1_kernel(oa_ref, of_ref, ob_ref, g_ref, *refs, nslab):
    ga_refs, gr_refs = refs[:nslab], refs[nslab:2 * nslab]
    gn_ref, wa_ref, wr_ref, y_ref = refs[2 * nslab:]
    o = of_ref[...] + ob_ref[...]
    heads = []
    for hh in range(RET_HEADS):
        sl = slice(hh * RET_V_DIM, (hh + 1) * RET_V_DIM)
        oh = o[:, sl]
        dlt = oh - jnp.mean(oh, axis=-1, keepdims=True)
        yh = dlt * lax.rsqrt(jnp.mean(dlt * dlt, axis=-1, keepdims=True) + EPS)
        g = g_ref[:, sl]
        heads.append((yh * gn_ref[:, sl] * (g * _sigmoid(g))).astype(BF16))
    o_ret = jnp.concatenate(heads, axis=1)
    tn = ga_refs[0].shape[1]
    for c in range(nslab):
        cols = slice(c * tn, (c + 1) * tn)
        ya = jnp.dot(oa_ref[...], wa_ref[:, cols], preferred_element_type=F32)
        yr = jnp.dot(o_ret, wr_ref[:, cols], preferred_element_type=F32)
        y_ref[:, cols] = (_sigmoid(ga_refs[c][...]) * ya + _sigmoid(gr_refs[c][...]) * yr).astype(y_ref.dtype)


def _merge1(o_att, o_f, o_b, zg, gn, wa, wr):
    r = o_att.shape[0]
    d = wa.shape[1]
    tm = _blk(256, r)
    tn = _blk(1024, d)
    nslab = d // tn
    ga0 = RET_V_W // tn
    gr0 = (RET_V_W + d) // tn
    row = lambda i: (i, 0)
    const = lambda i: (0, 0)
    gate_specs = [pl.BlockSpec((tm, tn), functools.partial(lambda i, c: (i, c), c=c0 + c))
                  for c0 in (ga0, gr0) for c in range(nslab)]
    return pl.pallas_call(
        functools.partial(_merge1_kernel, nslab=nslab),
        grid=(r // tm,),
        in_specs=[pl.BlockSpec((tm, ATT_W), row),
                  pl.BlockSpec((tm, RET_V_W), row),
                  pl.BlockSpec((tm, RET_V_W), row),
                  pl.BlockSpec((tm, RET_V_W), row)]
                 + gate_specs
                 + [pl.BlockSpec((1, RET_V_W), const),
                    pl.BlockSpec((ATT_W, d), const),
                    pl.BlockSpec((RET_V_W, d), const)],
        out_specs=pl.BlockSpec((tm, d), row),
        out_shape=jax.ShapeDtypeStruct((r, d), BF16),
        compiler_params=_params("parallel"),
        name="merge1",
    )(o_att, o_f, o_b, zg, *([zg] * (2 * nslab)), gn.reshape(1, RET_V_W), wa, wr)


def _merge2_kernel(y_ref, w_ref, x_ref, gt_ref, pn1_ref, pn2_ref, sh_ref, sc_ref, wr_ref,
                   x1_ref, h2_ref, aff_ref):
    yy = jnp.dot(y_ref[0], w_ref[...], preferred_element_type=F32)
    r = yy * lax.rsqrt(jnp.mean(yy * yy, axis=-1, keepdims=True) + EPS) * pn1_ref[...]
    x1 = x_ref[0] + gt_ref[0] * r
    x1_ref[0] = x1
    h2 = x1 * lax.rsqrt(jnp.mean(x1 * x1, axis=-1, keepdims=True) + EPS) * pn2_ref[...]
    h2 = h2 * (1.0 + sc_ref[0]) + sh_ref[0]
    h_hi = h2.astype(BF16)
    h2_ref[0] = h_hi
    h_lo = (h2 - h_hi.astype(F32)).astype(BF16)
    wr = wr_ref[...]
    w_hi = wr.astype(BF16)
    w_lo = (wr - w_hi.astype(F32)).astype(BF16)
    ne = wr.shape[1]
    r_hi = jnp.dot(h_hi, jnp.concatenate([w_hi, w_lo], axis=1), preferred_element_type=F32)
    logits = r_hi[:, :ne] + r_hi[:, ne:] + jnp.dot(h_lo, w_hi, preferred_element_type=F32)
    e = jnp.exp(logits - jnp.max(logits, axis=-1, keepdims=True))
    aff_ref[0] = e / jnp.sum(e, axis=-1, keepdims=True)


def _merge2(y, w_out, x, gt1, pn1, pn2, sh2, sc2, w_router):
    b, t, d = x.shape
    tm = _blk(512, t)
    ne = w_router.shape[1]
    tile = lambda i, j: (i, j, 0)
    vec = lambda i, j: (0, 0)
    per_b = lambda i, j: (i, 0, 0)
    return pl.pallas_call(
        _merge2_kernel,
        grid=(b, t // tm),
        in_specs=[pl.BlockSpec((1, tm, d), tile),
                  pl.BlockSpec((d, d), vec),
                  pl.BlockSpec((1, tm, d), tile),
                  pl.BlockSpec((1, 1, d), per_b),
                  pl.BlockSpec((1, d), vec),
                  pl.BlockSpec((1, d), vec),
                  pl.BlockSpec((1, 1, d), per_b),
                  pl.BlockSpec((1, 1, d), per_b),
                  pl.BlockSpec((d, ne), vec)],
        out_specs=(pl.BlockSpec((1, tm, d), tile),
                   pl.BlockSpec((1, tm, d), tile),
                   pl.BlockSpec((1, tm, ne), tile)),
        out_shape=(jax.ShapeDtypeStruct((b, t, d), F32),
                   jax.ShapeDtypeStruct((b, t, d), BF16),
                   jax.ShapeDtypeStruct((b, t, ne), F32)),
        compiler_params=_params("parallel", "parallel"),
        name="merge2",
    )(y.reshape(b, t, d), w_out, x, gt1, pn1.reshape(1, d), pn2.reshape(1, d), sh2, sc2, w_router)


def _lane_cumsum(x, tri):
    off = jnp.zeros((x.shape[0], 1), F32)
    parts = []
    for cidx in range(x.shape[1] // LANES):
        xc = x[:, cidx * LANES:(cidx + 1) * LANES].astype(BF16)
        cs = jnp.dot(xc, tri, preferred_element_type=F32) + off
        parts.append(cs)
        off = cs[:, LANES - 1:LANES]
    return jnp.concatenate(parts, axis=1)


def _topk_kernel(a_ref, pos_ref, fill_ref, *, cap):
    a = a_ref[0]
    bits = jnp.zeros((a.shape[0], 1), jnp.int32)
    for bit in range(30, -1, -1):
        cand = bits | (1 << bit)
        cnt = jnp.sum(jnp.where(a >= lax.bitcast_convert_type(cand, F32), 1.0, 0.0), axis=1, keepdims=True)
        bits = jnp.where(cnt >= cap, cand, bits)
    bits = jnp.where(bits < MIN_NORMAL_F32_BITS, 0, bits)
    thr = lax.bitcast_convert_type(bits, F32)
    ri = lax.broadcasted_iota(jnp.int32, (LANES, LANES), 0)
    ci = lax.broadcasted_iota(jnp.int32, (LANES, LANES), 1)
    tri = jnp.where(ri <= ci, 1.0, 0.0).astype(BF16)
    gt = a > thr
    eq = a == thr
    n_gt = jnp.sum(jnp.where(gt, 1.0, 0.0), axis=1, keepdims=True)
    eq_rank = _lane_cumsum(jnp.where(eq, 1.0, 0.0), tri)
    sel = jnp.where(gt, 1.0, jnp.where(eq, jnp.where(eq_rank <= cap - n_gt, 1.0, 0.0), 0.0))
    filled = _lane_cumsum(sel, tri)
    pos_ref[0] = filled * sel
    fill_ref[0] = filled


def _topk_slots(aff_t, cap):
    b, ne, t = aff_t.shape
    spec = pl.BlockSpec((1, ne, t), lambda i: (i, 0, 0))
    shape = jax.ShapeDtypeStruct((b, ne, t), F32)
    return pl.pallas_call(
        functools.partial(_topk_kernel, cap=cap),
        grid=(b,),
        in_specs=[spec],
        out_specs=(spec, spec),
        out_shape=(shape, shape),
        compiler_params=_params("parallel"),
        name="topk",
    )(aff_t)


def _gather_kernel(pos_ref, h_ref, o_ref):
    cap = o_ref.shape[1]
    t = h_ref.shape[1]
    tc = _blk(1024, t)
    slot = lax.broadcasted_iota(jnp.int32, (cap, tc), 0).astype(F32) + 1.0
    acc = None
    for cidx in range(t // tc):
        row = pos_ref[0, 0, :, cidx * tc:(cidx + 1) * tc]
        onehot = jnp.where(row == slot, 1.0, 0.0).astype(BF16)
        part = jnp.dot(onehot, h_ref[0, cidx * tc:(cidx + 1) * tc, :], preferred_element_type=F32)
        acc = part if acc is None else acc + part
    o_ref[0] = acc.astype(o_ref.dtype)


def _gather(pos, h2, cap):
    b, ne, t = pos.shape
    d = h2.shape[-1]
    dn = _blk(1024, d)
    return pl.pallas_call(
        _gather_kernel,
        grid=(b, d // dn, ne),
        in_specs=[pl.BlockSpec((1, 1, 1, t), lambda i, n, e: (i, e, 0, 0)),
                  pl.BlockSpec((1, t, dn), lambda i, n, e: (i, 0, n))],
        out_specs=pl.BlockSpec((1, cap, dn), lambda i, n, e: (e, i, n)),
        out_shape=jax.ShapeDtypeStruct((ne, b * cap, d), BF16),
        compiler_params=_params("parallel", "parallel", "parallel"),
        name="gather",
    )(pos.reshape(b, ne, 1, t), h2)


def _gather_win_kernel(w0_ref, pos_ref, h_ref, o_ref, acc_ref, lhs_ref):
    b = pl.program_id(0)
    ne, cap = o_ref.shape[0], o_ref.shape[1]
    nt = h_ref.shape[1] // TOK_TILE
    slot1 = lax.broadcasted_iota(jnp.int32, (GATHER_WIN, TOK_TILE), 0) + 1

    @pl.when(pl.program_id(1) == 0)
    def _():
        def build(k, carry):
            for e in range(ne):
                w0 = w0_ref[(b * ne + e) * nt + k]
                row = pos_ref[0, pl.ds(e * nt + k, 1), :]
                onehot = jnp.where(row == (slot1 + w0).astype(F32), 1.0, 0.0)
                lhs_ref[k, e * GATHER_WIN:(e + 1) * GATHER_WIN, :] = onehot.astype(BF16)
            return carry
        lax.fori_loop(0, nt, build, 0)

    acc_ref[:, :8, :] = jnp.zeros((ne, 8) + acc_ref.shape[2:], F32)

    def tile(k, carry):
        t0 = pl.multiple_of(k * TOK_TILE, TOK_TILE)
        res = jnp.dot(lhs_ref[k], h_ref[0, pl.ds(t0, TOK_TILE), :], preferred_element_type=F32)
        for e in range(ne):
            w0 = pl.multiple_of(w0_ref[(b * ne + e) * nt + k], 8)
            acc_ref[e, pl.ds(w0, 8), :] += res[e * GATHER_WIN:e * GATHER_WIN + 8]
            acc_ref[e, pl.ds(w0 + 8, GATHER_WIN - 8), :] = res[e * GATHER_WIN + 8:(e + 1) * GATHER_WIN]
        return carry
    lax.fori_loop(0, nt, tile, 0, unroll=8)
    o_ref[...] = acc_ref[:, :cap, :].astype(o_ref.dtype)


def _gather_win(w0, pos, h2, cap):
    b, ne, t = pos.shape
    d = h2.shape[-1]
    nt = t // TOK_TILE
    dn = _blk(256, d)
    return pl.pallas_call(
        _gather_win_kernel,
        grid_spec=pltpu.PrefetchScalarGridSpec(
            num_scalar_prefetch=1,
            grid=(b, d // dn),
            in_specs=[pl.BlockSpec((1, ne * nt, TOK_TILE), lambda i, n, w: (i, 0, 0)),
                      pl.BlockSpec((1, t, dn), lambda i, n, w: (i, 0, n))],
            out_specs=pl.BlockSpec((ne, cap, dn), lambda i, n, w: (0, i, n)),
            scratch_shapes=[pltpu.VMEM((ne, cap + GATHER_WIN, dn), F32),
                            pltpu.VMEM((nt, ne * GATHER_WIN, TOK_TILE), BF16)]),
        out_shape=jax.ShapeDtypeStruct((ne, b * cap, d), BF16),
        compiler_params=_params("parallel", "arbitrary"),
        name="gather_win",
    )(w0, pos.reshape(b, ne * nt, TOK_TILE), h2)


def _ffn_kernel(x_ref, wg_ref, wu_ref, wd_ref, o_ref, h_ref, wgu_ref, *, nf):
    j = pl.program_id(2)
    tf = wg_ref.shape[2]

    @pl.when(j < nf)
    def _():
        wgu_ref[:, :tf] = wg_ref[0].astype(BF16)
        wgu_ref[:, tf:] = wu_ref[0].astype(BF16)
        au = jnp.dot(x_ref[0], wgu_ref[...], preferred_element_type=F32)
        a = au[:, :tf]
        h_ref[j] = (a * _sigmoid(a) * au[:, tf:]).astype(BF16)

    @pl.when(j >= nf)
    def _():
        acc = None
        for cidx in range(nf):
            part = jnp.dot(h_ref[cidx], wd_ref[0, cidx * tf:(cidx + 1) * tf, :].astype(BF16),
                           preferred_element_type=F32)
            acc = part if acc is None else acc + part
        o_ref[0] = acc.astype(o_ref.dtype)


def _ffn(xg, w_gate, w_up, w_down):
    ne, rows, d = xg.shape
    ff = w_gate.shape[-1]
    tm = _blk(2048, rows)
    tf = _blk(256, ff)
    tn = _blk(512, d)
    nf = ff // tf
    return pl.pallas_call(
        functools.partial(_ffn_kernel, nf=nf),
        grid=(ne, rows // tm, nf + d // tn),
        in_specs=[pl.BlockSpec((1, tm, d), lambda e, m, j: (e, m, 0)),
                  pl.BlockSpec((1, d, tf), lambda e, m, j: (e, 0, jnp.minimum(j, nf - 1))),
                  pl.BlockSpec((1, d, tf), lambda e, m, j: (e, 0, jnp.minimum(j, nf - 1))),
                  pl.BlockSpec((1, ff, tn), lambda e, m, j: (e, 0, jnp.maximum(j - nf, 0)))],
        out_specs=pl.BlockSpec((1, tm, tn), lambda e, m, j: (e, m, jnp.maximum(j - nf, 0))),
        out_shape=jax.ShapeDtypeStruct((ne, rows, d), BF16),
        scratch_shapes=[pltpu.VMEM((nf, tm, tf), BF16), pltpu.VMEM((d, 2 * tf), BF16)],
        compiler_params=_params("parallel", "parallel", "arbitrary"),
        name="ffn",
    )(xg, w_gate, w_up, w_down)


def _combine_kernel(pos_ref, aff_ref, y_ref, x1_ref, gt_ref, pn_ref, o_ref, acc_ref):
    e = pl.program_id(2)
    tt, ne = pos_ref.shape[1], pos_ref.shape[2]
    cap = y_ref.shape[1]
    pick = lax.broadcasted_iota(jnp.int32, (tt, ne), 1) == e
    pcol = jnp.sum(jnp.where(pick, pos_ref[0], 0.0), axis=1, keepdims=True)
    acol = jnp.sum(jnp.where(pick, aff_ref[0], 0.0), axis=1, keepdims=True)
    slot = lax.broadcasted_iota(jnp.int32, (tt, cap), 1).astype(F32) + 1.0
    onehot = jnp.where(pcol == slot, 1.0, 0.0).astype(BF16)
    z = jnp.dot(onehot, y_ref[0], preferred_element_type=F32) * acol

    @pl.when(e == 0)
    def _():
        acc_ref[...] = z

    @pl.when(e > 0)
    def _():
        acc_ref[...] += z

    @pl.when(e == pl.num_programs(2) - 1)
    def _():
        y2 = acc_ref[...]
        r = y2 * lax.rsqrt(jnp.mean(y2 * y2, axis=-1, keepdims=True) + EPS) * pn_ref[...]
        o_ref[0] = x1_ref[0] + gt_ref[0] * r


def _combine(pos_t, aff, y, x1, gt2, pn2, cap):
    b, t, d = x1.shape
    ne = aff.shape[-1]
    tt = _blk(512, t)
    return pl.pallas_call(
        _combine_kernel,
        grid=(b, t // tt, ne),
        in_specs=[pl.BlockSpec((1, tt, ne), lambda i, j, e: (i, j, 0)),
                  pl.BlockSpec((1, tt, ne), lambda i, j, e: (i, j, 0)),
                  pl.BlockSpec((1, cap, d), lambda i, j, e: (e, i, 0)),
                  pl.BlockSpec((1, tt, d), lambda i, j, e: (i, j, 0)),
                  pl.BlockSpec((1, 1, d), lambda i, j, e: (i, 0, 0)),
                  pl.BlockSpec((1, d), lambda i, j, e: (0, 0))],
        out_specs=pl.BlockSpec((1, tt, d), lambda i, j, e: (i, j, 0)),
        out_shape=jax.ShapeDtypeStruct((b, t, d), F32),
        scratch_shapes=[pltpu.VMEM((tt, d), F32)],
        compiler_params=_params("parallel", "parallel", "arbitrary"),
        name="combine",
    )(pos_t, aff, y, x1, gt2, pn2.reshape(1, d))


def _spread_matrix(ne, w):
    s = np.zeros((5 * ne, 2 * ne * w), np.float32)
    for e in range(ne):
        s[e, e * w:(e + 1) * w] = SLOT_BLK
        s[ne + e, e * w:(e + 1) * w] = 1.0
        for part in (2, 3, 4):
            s[part * ne + e, (ne + e) * w:(ne + e + 1) * w] = 1.0
    return jnp.asarray(s, BF16)


def _combine_win_kernel(row_ref, base_ref, pos_ref, aff_ref, spread_ref, *refs):
    del row_ref
    ne = pos_ref.shape[2]
    y_refs = refs[:ne]
    x1_ref, gt_ref, pn_ref, o_ref = refs[ne:]
    b = pl.program_id(0)
    k = pl.program_id(1)
    nt = pl.num_programs(1)
    w = y_refs[0].shape[0]
    pos = pos_ref[0]
    aff = aff_ref[0]
    p_hi = jnp.floor(pos * (1.0 / SLOT_BLK))
    a_parts = _split_bf16(aff)
    a_parts.append((aff - a_parts[0].astype(F32) - a_parts[1].astype(F32)).astype(BF16))
    lhs = jnp.concatenate([p_hi.astype(BF16), (pos - SLOT_BLK * p_hi).astype(BF16)] + a_parts, axis=1)
    spread = jnp.dot(lhs, spread_ref[...], preferred_element_type=F32)
    lane = lax.broadcasted_iota(jnp.int32, (1, ne * w), 1)
    seg = lane // w
    seg_base = jnp.zeros((1, ne * w), jnp.int32)
    for e in range(ne):
        seg_base = jnp.where(seg == e, base_ref[(b * ne + e) * nt + k], seg_base)
    slot = (seg_base + lane % w + 1).astype(F32)
    onehot = jnp.where(spread[:, :ne * w] == slot, spread[:, ne * w:], 0.0).astype(BF16)
    y2 = None
    gw = COMBINE_GROUP * w
    for g in range(ne // COMBINE_GROUP):
        ywin = jnp.concatenate([r[...] for r in y_refs[COMBINE_GROUP * g:COMBINE_GROUP * (g + 1)]], axis=0)
        part = jnp.dot(onehot[:, g * gw:(g + 1) * gw], ywin, preferred_element_type=F32)
        y2 = part if y2 is None else y2 + part
    r = y2 * lax.rsqrt(jnp.mean(y2 * y2, axis=-1, keepdims=True) + EPS) * pn_ref[...]
    o_ref[0] = x1_ref[0] + gt_ref[0] * r


def _combine_win(row, base, pos_t, aff, y, x1, gt2, pn2, w):
    b, t, d = x1.shape
    ne = aff.shape[-1]
    nt = t // TOK_TILE

    def y_spec(e):
        return pl.BlockSpec((pl.Element(w), pl.Element(d)),
                            lambda i, k, row_ref, base_ref: (row_ref[(i * ne + e) * nt + k] * SLOT_BLK, 0))

    tile = lambda i, k, row_ref, base_ref: (i, k, 0)
    const = lambda i, k, row_ref, base_ref: (0, 0)
    spread = _spread_matrix(ne, w)
    return pl.pallas_call(
        _combine_win_kernel,
        grid_spec=pltpu.PrefetchScalarGridSpec(
            num_scalar_prefetch=2,
            grid=(b, nt),
            in_specs=[pl.BlockSpec((1, TOK_TILE, ne), tile),
                      pl.BlockSpec((1, TOK_TILE, ne), tile),
                      pl.BlockSpec(spread.shape, const)]
                     + [y_spec(e) for e in range(ne)]
                     + [pl.BlockSpec((1, TOK_TILE, d), tile),
                        pl.BlockSpec((1, 1, d), lambda i, k, row_ref, base_ref: (i, 0, 0)),
                        pl.BlockSpec((1, d), const)],
            out_specs=pl.BlockSpec((1, TOK_TILE, d), tile)),
        out_shape=jax.ShapeDtypeStruct((b, t, d), F32),
        compiler_params=_params("parallel", "parallel"),
        name="combine_win",
    )(row, base, pos_t, aff, spread, *([y.reshape(-1, d)] * ne), x1, gt2, pn2.reshape(1, d))


def _routing_tables(fill, cap):
    b, ne, t = fill.shape
    w = min(2 * SLOT_BLK, cap)
    ends = fill[:, :, TOK_TILE - 1::TOK_TILE].astype(jnp.int32)
    starts = jnp.concatenate([jnp.zeros((b, ne, 1), jnp.int32), ends[:, :, :-1]], axis=-1)
    sparse = jnp.all(ends - starts <= SLOT_BLK)
    base = jnp.minimum(starts // SLOT_BLK * SLOT_BLK, cap - w)
    group = (jnp.arange(ne)[None, :, None] * b + jnp.arange(b)[:, None, None]) * cap
    row_blk = (group + base) // SLOT_BLK
    return sparse, (starts // 8 * 8).reshape(-1), row_blk.reshape(-1), base.reshape(-1), w


def _rope_tables(t):
    pos = jnp.arange(t)
    r = (pos // GRID_W).astype(F32)
    cl = (pos % GRID_W).astype(F32)
    quarter = HEAD_DIM // 4
    inv = ROPE_THETA ** (-jnp.arange(quarter, dtype=F32) / quarter)
    ang_r = r[:, None] * inv
    ang_c = cl[:, None] * inv
    ang = jnp.concatenate([ang_r, ang_r, ang_c, ang_c], axis=-1)
    return jnp.cos(ang), jnp.sin(ang)


def _kv_heads(h, w_in, q_width, k_norm, rope):
    plan = (((0, ATT_KV_W, "norm_rope", 0, 1.0), (ATT_KV_W, 2 * ATT_KV_W, "plain", 1, 1.0)),
            ((0, RET_QK_W, "rope", 2, RET_QK_DIM ** -0.5),),
            ((0, 512, "plain", 3, 1.0),),
            ((0, 512, "plain", 3, 1.0),))
    outs = ((ATT_KV_W, ATT_KV_W, lambda j: 0), (ATT_KV_W, ATT_KV_W, lambda j: 0),
            (RET_QK_W, RET_QK_W, lambda j: 0), (RET_V_W, 512, lambda j: jnp.clip(j - 2, 0, 1)))
    return _proj_heads(h, w_in, q_width, plan, outs, k_norm, rope)


def _q_heads(h, w_in, q_norm, rope):
    plan = (((0, 512, "norm_rope", 0, 1.0),), ((0, 512, "norm_rope", 0, 1.0),), ((0, RET_QK_W, "rope", 1, 1.0),))
    outs = ((ATT_W, 512, lambda j: jnp.minimum(j, 1)), (RET_QK_W, RET_QK_W, lambda j: 0))
    return _proj_heads(h, w_in, 0, plan, outs, q_norm, rope)


def kernel(x, c, ctx, c_ctx, w_mod, b_mod, pre_norm1, post_norm1, pre_norm2, post_norm2, w_in, q_norm,
           k_norm, ret_decay, ret_gn, w_o_att, w_o_ret, w_out, w_router, w_gate, w_up, w_down):
    b, t, d = x.shape
    n_ctx = ctx.shape[1]
    depth = w_mod.shape[0]
    q_width = ATT_W + RET_QK_W + RET_V_W + 2 * d
    cap = EC_FACTOR * t // N_EXPERTS
    rope_lat = tuple(jnp.tile(tbl, (b, 1)) for tbl in _rope_tables(t))
    rope_ctx = (jnp.ones((b * n_ctx, HEAD_DIM), F32), jnp.zeros((b * n_ctx, HEAD_DIM), F32))
    cs = jnp.zeros((8, d), F32).at[:b].set(c).at[b].set(c_ctx)
    xc = ctx
    for layer in range(depth):
        assert layer == depth - 1, "context-stream update between layers is not implemented"
        mod = _mod_vectors(cs, w_mod[layer], b_mod[layer])
        sh1, sc1, gt1, sh2, sc2, gt2 = [m[:b, None, :] for m in jnp.split(mod, 6, axis=-1)]
        csh1, csc1 = [jnp.broadcast_to(m[b][None, None, :], (b, 1, d)) for m in jnp.split(mod, 6, axis=-1)[:2]]
        dec = -jax.nn.softplus(ret_decay[layer].astype(F32))
        wl = w_in[layer]

        hc = _prenorm(xc, pre_norm1[layer], csh1, csc1).reshape(b * n_ctx, d)
        kc_a, vc_a, kc_r, vc_r = _kv_heads(hc, wl, q_width, k_norm[layer], rope_ctx)
        s0f, s0b = _ctx_states(dec, kc_r.reshape(b, n_ctx, RET_QK_W), vc_r.reshape(b, n_ctx, RET_V_W))

        h = _prenorm(x, pre_norm1[layer], sh1, sc1).reshape(b * t, d)
        q_a, q_r = _q_heads(h, wl, q_norm[layer], rope_lat)
        zg = _proj(h, wl, ATT_W + RET_QK_W, RET_V_W + 2 * d, out_dtype=F32, tn=512)
        k_a, v_a, k_r, v_r = _kv_heads(h, wl, q_width, k_norm[layer], rope_lat)

        keys = jnp.concatenate([kc_a.reshape(b, n_ctx, ATT_KV_W), k_a.reshape(b, t, ATT_KV_W)], axis=1)
        vals = jnp.concatenate([vc_a.reshape(b, n_ctx, ATT_KV_W), v_a.reshape(b, t, ATT_KV_W)], axis=1)
        kt = keys.reshape(b, n_ctx + t, ATT_KV_HEADS, HEAD_DIM).transpose(0, 2, 3, 1)
        o_att = _attention(q_a.reshape(b, t, ATT_W), kt, vals)

        o_f, o_b = _retention(dec, q_r.reshape(b, t, RET_QK_W), k_r.reshape(b, t, RET_QK_W),
                              v_r.reshape(b, t, RET_V_W), s0f, s0b)

        y = _merge1(o_att.reshape(b * t, ATT_W), o_f.reshape(b * t, RET_V_W), o_b.reshape(b * t, RET_V_W),
                    zg, ret_gn[layer], w_o_att[layer].astype(BF16), w_o_ret[layer].astype(BF16))
        x1, h2, aff = _merge2(y, w_out[layer].astype(BF16), x, gt1, post_norm1[layer], pre_norm2[layer],
                              sh2, sc2, w_router[layer])

        pos, fill = _topk_slots(aff.transpose(0, 2, 1), cap)
        pos_t = pos.transpose(0, 2, 1)
        sparse, w0, row, base, win = _routing_tables(fill, cap)
        xg = lax.cond(sparse,
                      lambda: _gather_win(w0, pos, h2, cap),
                      lambda: _gather(pos, h2, cap))
        yg = _ffn(xg, w_gate[layer], w_up[layer], w_down[layer])
        x = lax.cond(sparse,
                     lambda: _combine_win(row, base, pos_t, aff, yg, x1, gt2, post_norm2[layer], win),
                     lambda: _combine(pos_t, aff, yg, x1, gt2, post_norm2[layer], cap))
    return x
```

```python
import functools

import jax
import jax.numpy as jnp
import numpy as np
from jax import lax
from jax.experimental import pallas as pl
from jax.experimental.pallas import tpu as pltpu

F32 = jnp.float32
BF16 = jnp.bfloat16

GRID_W = 64
HEAD_DIM = 128
ATT_HEADS = 8
ATT_KV_HEADS = 2
ATT_GROUP = ATT_HEADS // ATT_KV_HEADS
RET_HEADS = 4
RET_QK_DIM = 128
RET_V_DIM = 256
ROPE_THETA = 10000.0
N_EXPERTS = 16
EC_FACTOR = 2
EPS = 1e-6

ATT_W = ATT_HEADS * HEAD_DIM
ATT_KV_W = ATT_KV_HEADS * HEAD_DIM
RET_QK_W = RET_HEADS * RET_QK_DIM
RET_V_W = RET_HEADS * RET_V_DIM

LANES = 128
VMEM_LIMIT = 56 * 1024 * 1024
RET_CHUNK = 256
MIN_NORMAL_F32_BITS = 0x00800000
LOG2_E = 1.4426950408889634
TOK_TILE = 256
SLOT_BLK = 64
GATHER_WIN = SLOT_BLK + 16
COMBINE_GROUP = 4


def _blk(pref, n):
    return pref if n % pref == 0 else n


def _params(*sem):
    return pltpu.CompilerParams(dimension_semantics=sem, vmem_limit_bytes=VMEM_LIMIT)


def _sigmoid(x):
    return 0.5 * jnp.tanh(0.5 * x) + 0.5


def _mod_kernel(s_ref, w_ref, b_ref, o_ref):
    s = s_ref[...]
    s = s * _sigmoid(s)
    o_ref[...] = jnp.dot(s, w_ref[...], precision=lax.Precision.HIGHEST,
                         preferred_element_type=F32) + b_ref[...]


def _mod_vectors(cs, w_mod, b_mod):
    rows, d = cs.shape
    n = w_mod.shape[1]
    tn = _blk(2048, n)
    return pl.pallas_call(
        _mod_kernel,
        grid=(n // tn,),
        in_specs=[pl.BlockSpec((rows, d), lambda j: (0, 0)),
                  pl.BlockSpec((d, tn), lambda j: (0, j)),
                  pl.BlockSpec((1, tn), lambda j: (0, j))],
        out_specs=pl.BlockSpec((rows, tn), lambda j: (0, j)),
        out_shape=jax.ShapeDtypeStruct((rows, n), F32),
        compiler_params=_params("parallel"),
        name="mod",
    )(cs, w_mod, b_mod.reshape(1, n))


def _prenorm_kernel(x_ref, g_ref, sh_ref, sc_ref, o_ref):
    x = x_ref[0]
    y = x * lax.rsqrt(jnp.mean(x * x, axis=-1, keepdims=True) + EPS) * g_ref[...]
    o_ref[0] = (y * (1.0 + sc_ref[0]) + sh_ref[0]).astype(o_ref.dtype)


def _prenorm(x, gain, shift, scale):
    b, t, d = x.shape
    tm = _blk(1024, t)
    return pl.pallas_call(
        _prenorm_kernel,
        grid=(b, t // tm),
        in_specs=[pl.BlockSpec((1, tm, d), lambda i, j: (i, j, 0)),
                  pl.BlockSpec((1, d), lambda i, j: (0, 0)),
                  pl.BlockSpec((1, 1, d), lambda i, j: (i, 0, 0)),
                  pl.BlockSpec((1, 1, d), lambda i, j: (i, 0, 0))],
        out_specs=pl.BlockSpec((1, tm, d), lambda i, j: (i, j, 0)),
        out_shape=jax.ShapeDtypeStruct((b, t, d), BF16),
        compiler_params=_params("parallel", "parallel"),
        name="prenorm",
    )(x, gain.reshape(1, d), shift, scale)


def _split_bf16(v):
    hi = v.astype(BF16)
    return [hi, (v - hi.astype(F32)).astype(BF16)]


def _rotate_half_matrix():
    quarter = HEAD_DIM // 4
    r = np.zeros((HEAD_DIM, HEAD_DIM), np.float32)
    for i in range(HEAD_DIM):
        if i % (2 * quarter) < quarter:
            r[i + quarter, i] = -1.0
        else:
            r[i - quarter, i] = 1.0
    return r


def _head_mix_matrix(mode):
    rot = _rotate_half_matrix()
    if mode == "rope":
        return jnp.asarray(np.concatenate([rot, rot], axis=0), BF16)
    ones = np.ones((HEAD_DIM, HEAD_DIM), np.float32)
    zero = np.zeros((HEAD_DIM, HEAD_DIM), np.float32)
    return jnp.asarray(np.block([[ones, zero], [ones, zero], [zero, rot], [zero, rot]]), BF16)


def _proj_kernel(h_ref, w_ref, o_ref):
    o_ref[...] = jnp.dot(h_ref[...], w_ref[...].astype(BF16), preferred_element_type=F32).astype(o_ref.dtype)


def _proj(h, w_in, col0, width, *, out_dtype, tn):
    r, d = h.shape
    tm = _blk(2048, r)
    assert col0 % tn == 0 and width % tn == 0
    c0 = col0 // tn
    return pl.pallas_call(
        _proj_kernel,
        grid=(r // tm, width // tn),
        in_specs=[pl.BlockSpec((tm, d), lambda i, j: (i, 0)),
                  pl.BlockSpec((d, tn), lambda i, j: (0, c0 + j))],
        out_specs=pl.BlockSpec((tm, tn), lambda i, j: (i, j)),
        out_shape=jax.ShapeDtypeStruct((r, width), out_dtype),
        compiler_params=_params("parallel", "parallel"),
        name="proj_plain",
    )(h, w_in)


def _rope_head(xh, mode, gain, mix, cos, sin, pre_scale):
    if mode == "norm_rope":
        xg = xh * gain
        mixed = jnp.dot(jnp.concatenate(_split_bf16(xh * xh) + _split_bf16(xg), axis=1), mix,
                        preferred_element_type=F32)
        inv = lax.rsqrt(mixed[:, :HEAD_DIM] * (1.0 / HEAD_DIM) + EPS)
        return (xg * cos + mixed[:, HEAD_DIM:] * sin) * inv
    xs = xh * pre_scale if pre_scale != 1.0 else xh
    rot = jnp.dot(jnp.concatenate(_split_bf16(xs), axis=1), mix, preferred_element_type=F32)
    return xs * cos + rot * sin


def _proj_heads_kernel(h_ref, w_ref, gain_ref, mixn_ref, mixr_ref, cos_ref, sin_ref, *o_refs, plan):
    acc = jnp.dot(h_ref[...], w_ref[...].astype(BF16), preferred_element_type=F32)
    for jj, pieces in enumerate(plan):
        @pl.when(pl.program_id(1) == jj)
        def _(pieces=pieces):
            for lo, hi, mode, out, pre_scale in pieces:
                o_ref = o_refs[out]
                if mode == "plain":
                    o_ref[...] = acc[:, lo:hi].astype(o_ref.dtype)
                    continue
                mix = mixn_ref[...] if mode == "norm_rope" else mixr_ref[...]
                for hh in range((hi - lo) // HEAD_DIM):
                    xh = acc[:, lo + hh * HEAD_DIM:lo + (hh + 1) * HEAD_DIM]
                    res = _rope_head(xh, mode, gain_ref[...], mix, cos_ref[...], sin_ref[...], pre_scale)
                    o_ref[:, hh * HEAD_DIM:(hh + 1) * HEAD_DIM] = res.astype(o_ref.dtype)


def _proj_heads(h, w_in, col0, plan, outs, gain, rope):
    r, d = h.shape
    tm = _blk(2048, r)
    tn = 512
    assert col0 % tn == 0
    c0 = col0 // tn
    cos, sin = rope
    assert cos.shape == (r, HEAD_DIM)
    mixn, mixr = _head_mix_matrix("norm_rope"), _head_mix_matrix("rope")
    const = lambda i, j: (0, 0)
    rows = lambda i, j: (i, 0)
    return pl.pallas_call(
        functools.partial(_proj_heads_kernel, plan=plan),
        grid=(r // tm, len(plan)),
        in_specs=[pl.BlockSpec((tm, d), rows),
                  pl.BlockSpec((d, tn), lambda i, j: (0, c0 + j)),
                  pl.BlockSpec((1, HEAD_DIM), const),
                  pl.BlockSpec(mixn.shape, const),
                  pl.BlockSpec(mixr.shape, const),
                  pl.BlockSpec((tm, HEAD_DIM), rows),
                  pl.BlockSpec((tm, HEAD_DIM), rows)],
        out_specs=tuple(pl.BlockSpec((tm, bw), functools.partial(lambda i, j, f: (i, f(j)), f=f))
                        for _, bw, f in outs),
        out_shape=tuple(jax.ShapeDtypeStruct((r, width), BF16) for width, _, _ in outs),
        compiler_params=_params("parallel", "arbitrary"),
        name="proj_heads",
    )(h, w_in, gain.reshape(1, HEAD_DIM), mixn, mixr, cos, sin)


def _ctx_state_kernel(dec_ref, k_ref, v_ref, sf_ref, sb_ref):
    hh = pl.program_id(1)
    lf = dec_ref[0, hh]
    lb = dec_ref[1, hh]
    k = k_ref[0].astype(F32)
    v = v_ref[0]
    n = k.shape[0]
    pos = lax.broadcasted_iota(jnp.int32, k.shape, 0).astype(F32)
    kf = (k * jnp.exp((n - 1.0 - pos) * lf)).T.astype(BF16)
    kb = (k * jnp.exp(pos * lb)).T.astype(BF16)
    sf_ref[0, 0] = jnp.dot(kf, v, preferred_element_type=F32)
    sb_ref[0, 0] = jnp.dot(kb, v, preferred_element_type=F32)


def _ctx_states(dec, k_r, v_r):
    b, n, _ = k_r.shape
    spec_s = pl.BlockSpec((1, 1, RET_QK_DIM, RET_V_DIM), lambda i, j: (i, j, 0, 0))
    shape_s = jax.ShapeDtypeStruct((b, RET_HEADS, RET_QK_DIM, RET_V_DIM), F32)
    return pl.pallas_call(
        _ctx_state_kernel,
        grid=(b, RET_HEADS),
        in_specs=[pl.BlockSpec(memory_space=pltpu.SMEM),
                  pl.BlockSpec((1, n, RET_QK_DIM), lambda i, j: (i, 0, j)),
                  pl.BlockSpec((1, n, RET_V_DIM), lambda i, j: (i, 0, j))],
        out_specs=(spec_s, spec_s),
        out_shape=(shape_s, shape_s),
        compiler_params=_params("parallel", "parallel"),
        name="ctx_state",
    )(dec, k_r, v_r)


def _attn_kernel(q_ref, kt_ref, v_ref, o_ref, *, scale):
    for hk in range(ATT_KV_HEADS):
        kt = kt_ref[0, hk]
        v = v_ref[0, :, hk * HEAD_DIM:(hk + 1) * HEAD_DIM]
        for g in range(ATT_GROUP):
            cols = slice((hk * ATT_GROUP + g) * HEAD_DIM, (hk * ATT_GROUP + g + 1) * HEAD_DIM)
            s = jnp.dot(q_ref[0, :, cols], kt, preferred_element_type=F32)
            m = jnp.max(s, axis=-1, keepdims=True)
            p = jnp.exp2((s - m) * (scale * LOG2_E))
            l = jnp.sum(p, axis=-1, keepdims=True)
            o = jnp.dot(p.astype(BF16), v, preferred_element_type=F32)
            o_ref[0, :, cols] = (o / l).astype(o_ref.dtype)


def _attention(q, kt, v):
    b, t, _ = q.shape
    s = kt.shape[-1]
    tq = _blk(256, t)
    return pl.pallas_call(
        functools.partial(_attn_kernel, scale=HEAD_DIM ** -0.5),
        grid=(b, t // tq),
        in_specs=[pl.BlockSpec((1, tq, ATT_W), lambda i, n: (i, n, 0)),
                  pl.BlockSpec((1, ATT_KV_HEADS, HEAD_DIM, s), lambda i, n: (i, 0, 0, 0)),
                  pl.BlockSpec((1, s, ATT_KV_W), lambda i, n: (i, 0, 0))],
        out_specs=pl.BlockSpec((1, tq, ATT_W), lambda i, n: (i, n, 0)),
        out_shape=jax.ShapeDtypeStruct((b, t, ATT_W), BF16),
        compiler_params=_params("parallel", "parallel"),
        name="attention",
    )(q, kt, v)


def _ret_kernel(dec_ref, qf_ref, kf_ref, vf_ref, qb_ref, kb_ref, vb_ref, s0f_ref, s0b_ref,
                of_ref, ob_ref, sf_ref, sb_ref, mf_ref, mb_ref):
    n = pl.program_id(1)
    c = qf_ref.shape[1]

    @pl.when(n == 0)
    def _():
        sf_ref[...] = s0f_ref[0]
        sb_ref[...] = s0b_ref[0]
        ri = lax.broadcasted_iota(jnp.int32, (c, c), 0)
        ci = lax.broadcasted_iota(jnp.int32, (c, c), 1)
        d = (ri - ci).astype(F32)
        for hh in range(RET_HEADS):
            mf_ref[hh] = jnp.where(d >= 0, jnp.exp(jnp.maximum(d, 0.0) * dec_ref[0, hh]), 0.0)
            mb_ref[hh] = jnp.where(d <= 0, jnp.exp(jnp.maximum(-d, 0.0) * dec_ref[1, hh]), 0.0)

    nt = (((1,), (1,)), ((), ()))
    row_v = lax.broadcasted_iota(jnp.int32, (c, RET_V_DIM), 0).astype(F32)
    row_k = lax.broadcasted_iota(jnp.int32, (c, RET_QK_DIM), 0).astype(F32)

    def sweep(q_ref, k_ref, v_ref, o_ref, s_ref, m_ref, hh, lg, wq_age, wk_age):
        qk = slice(hh * RET_QK_DIM, (hh + 1) * RET_QK_DIM)
        vv = slice(hh * RET_V_DIM, (hh + 1) * RET_V_DIM)
        q = q_ref[0, :, qk]
        k = k_ref[0, :, qk]
        v = v_ref[0, :, vv]
        s = lax.dot_general(q, k, nt, preferred_element_type=F32) * m_ref[hh]
        o_in = jnp.dot(s.astype(BF16), v, preferred_element_type=F32)
        state = s_ref[hh]
        o_x = jnp.dot(q, state.astype(BF16), preferred_element_type=F32) * jnp.exp(wq_age * lg)
        o_ref[0, :, vv] = o_in + o_x
        kw = (k.astype(F32) * jnp.exp(wk_age * lg)).T.astype(BF16)
        g_chunk = jnp.exp(jnp.full((1, RET_V_DIM), c * lg, F32))
        s_ref[hh] = g_chunk * state + jnp.dot(kw, v, preferred_element_type=F32)

    for hh in range(RET_HEADS):
        sweep(qf_ref, kf_ref, vf_ref, of_ref, sf_ref, mf_ref, hh, dec_ref[0, hh], row_v + 1.0, c - 1.0 - row_k)
        sweep(qb_ref, kb_ref, vb_ref, ob_ref, sb_ref, mb_ref, hh, dec_ref[1, hh], c - row_v, row_k)


def _retention(dec, q, k, v, s0f, s0b):
    b, t, _ = q.shape
    c = _blk(RET_CHUNK, t)
    nc = t // c
    fwd = lambda i, n: (i, n, 0)
    bwd = lambda i, n: (i, nc - 1 - n, 0)
    spec_s = pl.BlockSpec((1, RET_HEADS, RET_QK_DIM, RET_V_DIM), lambda i, n: (i, 0, 0, 0))
    out_shape = jax.ShapeDtypeStruct((b, t, RET_V_W), F32)
    return pl.pallas_call(
        _ret_kernel,
        grid=(b, nc),
        in_specs=[pl.BlockSpec(memory_space=pltpu.SMEM),
                  pl.BlockSpec((1, c, RET_QK_W), fwd),
                  pl.BlockSpec((1, c, RET_QK_W), fwd),
                  pl.BlockSpec((1, c, RET_V_W), fwd),
                  pl.BlockSpec((1, c, RET_QK_W), bwd),
                  pl.BlockSpec((1, c, RET_QK_W), bwd),
                  pl.BlockSpec((1, c, RET_V_W), bwd),
                  spec_s, spec_s],
        out_specs=(pl.BlockSpec((1, c, RET_V_W), fwd), pl.BlockSpec((1, c, RET_V_W), bwd)),
        out_shape=(out_shape, out_shape),
        scratch_shapes=[pltpu.VMEM((RET_HEADS, RET_QK_DIM, RET_V_DIM), F32),
                        pltpu.VMEM((RET_HEADS, RET_QK_DIM, RET_V_DIM), F32),
                        pltpu.VMEM((RET_HEADS, c, c), F32),
                        pltpu.VMEM((RET_HEADS, c, c), F32)],
        compiler_params=_params("parallel", "arbitrary"),
        name="retention",
    )(dec, q, k, v, q, k, v, s0f, s0b)


def _merge1_kernel(oa_ref, of_ref, ob_ref, g_ref, *refs, nslab):
    ga_refs, gr_refs = refs[:nslab], refs[nslab:2 * nslab]
    gn_ref, wa_ref, wr_ref, y_ref = refs[2 * nslab:]
    o = of_ref[...] + ob_ref[...]
    heads = []
    for hh in range(RET_HEADS):
        sl = slice(hh * RET_V_DIM, (hh + 1) * RET_V_DIM)
        oh = o[:, sl]
        dlt = oh - jnp.mean(oh, axis=-1, keepdims=True)
        yh = dlt * lax.rsqrt(jnp.mean(dlt * dlt, axis=-1, keepdims=True) + EPS)
        g = g_ref[:, sl]
        heads.append((yh * gn_ref[:, sl] * (g * _sigmoid(g))).astype(BF16))
    o_ret = jnp.concatenate(heads, axis=1)
    tn = ga_refs[0].shape[1]
    for c in range(nslab):
        cols = slice(c * tn, (c + 1) * tn)
        ya = jnp.dot(oa_ref[...], wa_ref[:, cols], preferred_element_type=F32)
        yr = jnp.dot(o_ret, wr_ref[:, cols], preferred_element_type=F32)
        y_ref[:, cols] = (_sigmoid(ga_refs[c][...]) * ya + _sigmoid(gr_refs[c][...]) * yr).astype(y_ref.dtype)


def _merge1(o_att, o_f, o_b, zg, gn, wa, wr):
    r = o_att.shape[0]
    d = wa.shape[1]
    tm = _blk(256, r)
    tn = _blk(1024, d)
    nslab = d // tn
    ga0 = RET_V_W // tn
    gr0 = (RET_V_W + d) // tn
    row = lambda i: (i, 0)
    const = lambda i: (0, 0)
    gate_specs = [pl.BlockSpec((tm, tn), functools.partial(lambda i, c: (i, c), c=c0 + c))
                  for c0 in (ga0, gr0) for c in range(nslab)]
    return pl.pallas_call(
        functools.partial(_merge1_kernel, nslab=nslab),
        grid=(r // tm,),
        in_specs=[pl.BlockSpec((tm, ATT_W), row),
                  pl.BlockSpec((tm, RET_V_W), row),
                  pl.BlockSpec((tm, RET_V_W), row),
                  pl.BlockSpec((tm, RET_V_W), row)]
                 + gate_specs
                 + [pl.BlockSpec((1, RET_V_W), const),
                    pl.BlockSpec((ATT_W, d), const),
                    pl.BlockSpec((RET_V_W, d), const)],
        out_specs=pl.BlockSpec((tm, d), row),
        out_shape=jax.ShapeDtypeStruct((r, d), BF16),
        compiler_params=_params("parallel"),
        name="merge1",
    )(o_att, o_f, o_b, zg, *([zg] * (2 * nslab)), gn.reshape(1, RET_V_W), wa, wr)


def _merge2_kernel(y_ref, w_ref, x_ref, gt_ref, pn1_ref, pn2_ref, sh_ref, sc_ref, wr_ref,
                   x1_ref, h2_ref, aff_ref):
    yy = jnp.dot(y_ref[0], w_ref[...], preferred_element_type=F32)
    r = yy * lax.rsqrt(jnp.mean(yy * yy, axis=-1, keepdims=True) + EPS) * pn1_ref[...]
    x1 = x_ref[0] + gt_ref[0] * r
    x1_ref[0] = x1
    h2 = x1 * lax.rsqrt(jnp.mean(x1 * x1, axis=-1, keepdims=True) + EPS) * pn2_ref[...]
    h2 = h2 * (1.0 + sc_ref[0]) + sh_ref[0]
    h_hi = h2.astype(BF16)
    h2_ref[0] = h_hi
    h_lo = (h2 - h_hi.astype(F32)).astype(BF16)
    wr = wr_ref[...]
    w_hi = wr.astype(BF16)
    w_lo = (wr - w_hi.astype(F32)).astype(BF16)
    ne = wr.shape[1]
    r_hi = jnp.dot(h_hi, jnp.concatenate([w_hi, w_lo], axis=1), preferred_element_type=F32)
    logits = r_hi[:, :ne] + r_hi[:, ne:] + jnp.dot(h_lo, w_hi, preferred_element_type=F32)
    e = jnp.exp(logits - jnp.max(logits, axis=-1, keepdims=True))
    aff_ref[0] = e / jnp.sum(e, axis=-1, keepdims=True)


def _merge2(y, w_out, x, gt1, pn1, pn2, sh2, sc2, w_router):
    b, t, d = x.shape
    tm = _blk(512, t)
    ne = w_router.shape[1]
    tile = lambda i, j: (i, j, 0)
    vec = lambda i, j: (0, 0)
    per_b = lambda i, j: (i, 0, 0)
    return pl.pallas_call(
        _merge2_kernel,
        grid=(b, t // tm),
        in_specs=[pl.BlockSpec((1, tm, d), tile),
                  pl.BlockSpec((d, d), vec),
                  pl.BlockSpec((1, tm, d), tile),
                  pl.BlockSpec((1, 1, d), per_b),
                  pl.BlockSpec((1, d), vec),
                  pl.BlockSpec((1, d), vec),
                  pl.BlockSpec((1, 1, d), per_b),
                  pl.BlockSpec((1, 1, d), per_b),
                  pl.BlockSpec((d, ne), vec)],
        out_specs=(pl.BlockSpec((1, tm, d), tile),
                   pl.BlockSpec((1, tm, d), tile),
                   pl.BlockSpec((1, tm, ne), tile)),
        out_shape=(jax.ShapeDtypeStruct((b, t, d), F32),
                   jax.ShapeDtypeStruct((b, t, d), BF16),
                   jax.ShapeDtypeStruct((b, t, ne), F32)),
        compiler_params=_params("parallel", "parallel"),
        name="merge2",
    )(y.reshape(b, t, d), w_out, x, gt1, pn1.reshape(1, d), pn2.reshape(1, d), sh2, sc2, w_router)


def _lane_cumsum(x, tri):
    off = jnp.zeros((x.shape[0], 1), F32)
    parts = []
    for cidx in range(x.shape[1] // LANES):
        xc = x[:, cidx * LANES:(cidx + 1) * LANES].astype(BF16)
        cs = jnp.dot(xc, tri, preferred_element_type=F32) + off
        parts.append(cs)
        off = cs[:, LANES - 1:LANES]
    return jnp.concatenate(parts, axis=1)


def _topk_kernel(a_ref, pos_ref, fill_ref, *, cap):
    a = a_ref[0]
    bits = jnp.zeros((a.shape[0], 1), jnp.int32)
    for bit in range(30, -1, -1):
        cand = bits | (1 << bit)
        cnt = jnp.sum(jnp.where(a >= lax.bitcast_convert_type(cand, F32), 1.0, 0.0), axis=1, keepdims=True)
        bits = jnp.where(cnt >= cap, cand, bits)
    bits = jnp.where(bits < MIN_NORMAL_F32_BITS, 0, bits)
    thr = lax.bitcast_convert_type(bits, F32)
    ri = lax.broadcasted_iota(jnp.int32, (LANES, LANES), 0)
    ci = lax.broadcasted_iota(jnp.int32, (LANES, LANES), 1)
    tri = jnp.where(ri <= ci, 1.0, 0.0).astype(BF16)
    gt = a > thr
    eq = a == thr
    n_gt = jnp.sum(jnp.where(gt, 1.0, 0.0), axis=1, keepdims=True)
    eq_rank = _lane_cumsum(jnp.where(eq, 1.0, 0.0), tri)
    sel = jnp.where(gt, 1.0, jnp.where(eq, jnp.where(eq_rank <= cap - n_gt, 1.0, 0.0), 0.0))
    filled = _lane_cumsum(sel, tri)
    pos_ref[0] = filled * sel
    fill_ref[0] = filled


def _topk_slots(aff_t, cap):
    b, ne, t = aff_t.shape
    spec = pl.BlockSpec((1, ne, t), lambda i: (i, 0, 0))
    shape = jax.ShapeDtypeStruct((b, ne, t), F32)
    return pl.pallas_call(
        functools.partial(_topk_kernel, cap=cap),
        grid=(b,),
        in_specs=[spec],
        out_specs=(spec, spec),
        out_shape=(shape, shape),
        compiler_params=_params("parallel"),
        name="topk",
    )(aff_t)


def _gather_kernel(pos_ref, h_ref, o_ref):
    cap = o_ref.shape[1]
    t = h_ref.shape[1]
    tc = _blk(1024, t)
    slot = lax.broadcasted_iota(jnp.int32, (cap, tc), 0).astype(F32) + 1.0
    acc = None
    for cidx in range(t // tc):
        row = pos_ref[0, 0, :, cidx * tc:(cidx + 1) * tc]
        onehot = jnp.where(row == slot, 1.0, 0.0).astype(BF16)
        part = jnp.dot(onehot, h_ref[0, cidx * tc:(cidx + 1) * tc, :], preferred_element_type=F32)
        acc = part if acc is None else acc + part
    o_ref[0] = acc.astype(o_ref.dtype)


def _gather(pos, h2, cap):
    b, ne, t = pos.shape
    d = h2.shape[-1]
    dn = _blk(1024, d)
    return pl.pallas_call(
        _gather_kernel,
        grid=(b, d // dn, ne),
        in_specs=[pl.BlockSpec((1, 1, 1, t), lambda i, n, e: (i, e, 0, 0)),
                  pl.BlockSpec((1, t, dn), lambda i, n, e: (i, 0, n))],
        out_specs=pl.BlockSpec((1, cap, dn), lambda i, n, e: (e, i, n)),
        out_shape=jax.ShapeDtypeStruct((ne, b * cap, d), BF16),
        compiler_params=_params("parallel", "parallel", "parallel"),
        name="gather",
    )(pos.reshape(b, ne, 1, t), h2)


def _gather_win_kernel(w0_ref, pos_ref, h_ref, o_ref, acc_ref, lhs_ref):
    b = pl.program_id(0)
    ne, cap = o_ref.shape[0], o_ref.shape[1]
    nt = h_ref.shape[1] // TOK_TILE
    slot1 = lax.broadcasted_iota(jnp.int32, (GATHER_WIN, TOK_TILE), 0) + 1

    @pl.when(pl.program_id(1) == 0)
    def _():
        def build(k, carry):
            for e in range(ne):
                w0 = w0_ref[(b * ne + e) * nt + k]
                row = pos_ref[0, pl.ds(e * nt + k, 1), :]
                onehot = jnp.where(row == (slot1 + w0).astype(F32), 1.0, 0.0)
                lhs_ref[k, e * GATHER_WIN:(e + 1) * GATHER_WIN, :] = onehot.astype(BF16)
            return carry
        lax.fori_loop(0, nt, build, 0)

    acc_ref[:, :8, :] = jnp.zeros((ne, 8) + acc_ref.shape[2:], F32)

    def tile(k, carry):
        t0 = pl.multiple_of(k * TOK_TILE, TOK_TILE)
        res = jnp.dot(lhs_ref[k], h_ref[0, pl.ds(t0, TOK_TILE), :], preferred_element_type=F32)
        for e in range(ne):
            w0 = pl.multiple_of(w0_ref[(b * ne + e) * nt + k], 8)
            acc_ref[e, pl.ds(w0, 8), :] += res[e * GATHER_WIN:e * GATHER_WIN + 8]
            acc_ref[e, pl.ds(w0 + 8, GATHER_WIN - 8), :] = res[e * GATHER_WIN + 8:(e + 1) * GATHER_WIN]
        return carry
    lax.fori_loop(0, nt, tile, 0, unroll=8)
    o_ref[...] = acc_ref[:, :cap, :].astype(o_ref.dtype)


def _gather_win(w0, pos, h2, cap):
    b, ne, t = pos.shape
    d = h2.shape[-1]
    nt = t // TOK_TILE
    dn = _blk(256, d)
    return pl.pallas_call(
        _gather_win_kernel,
        grid_spec=pltpu.PrefetchScalarGridSpec(
            num_scalar_prefetch=1,
            grid=(b, d // dn),
            in_specs=[pl.BlockSpec((1, ne * nt, TOK_TILE), lambda i, n, w: (i, 0, 0)),
                      pl.BlockSpec((1, t, dn), lambda i, n, w: (i, 0, n))],
            out_specs=pl.BlockSpec((ne, cap, dn), lambda i, n, w: (0, i, n)),
            scratch_shapes=[pltpu.VMEM((ne, cap + GATHER_WIN, dn), F32),
                            pltpu.VMEM((nt, ne * GATHER_WIN, TOK_TILE), BF16)]),
        out_shape=jax.ShapeDtypeStruct((ne, b * cap, d), BF16),
        compiler_params=_params("parallel", "arbitrary"),
        name="gather_win",
    )(w0, pos.reshape(b, ne * nt, TOK_TILE), h2)


def _ffn_kernel(x_ref, wg_ref, wu_ref, wd_ref, o_ref, h_ref, wgu_ref, *, nf):
    j = pl.program_id(2)
    tf = wg_ref.shape[2]

    @pl.when(j < nf)
    def _():
        wgu_ref[:, :tf] = wg_ref[0].astype(BF16)
        wgu_ref[:, tf:] = wu_ref[0].astype(BF16)
        au = jnp.dot(x_ref[0], wgu_ref[...], preferred_element_type=F32)
        a = au[:, :tf]
        h_ref[j] = (a * _sigmoid(a) * au[:, tf:]).astype(BF16)

    @pl.when(j >= nf)
    def _():
        acc = None
        for cidx in range(nf):
            part = jnp.dot(h_ref[cidx], wd_ref[0, cidx * tf:(cidx + 1) * tf, :].astype(BF16),
                           preferred_element_type=F32)
            acc = part if acc is None else acc + part
        o_ref[0] = acc.astype(o_ref.dtype)


def _ffn(xg, w_gate, w_up, w_down):
    ne, rows, d = xg.shape
    ff = w_gate.shape[-1]
    tm = _blk(2048, rows)
    tf = _blk(256, ff)
    tn = _blk(512, d)
    nf = ff // tf
    return pl.pallas_call(
        functools.partial(_ffn_kernel, nf=nf),
        grid=(ne, rows // tm, nf + d // tn),
        in_specs=[pl.BlockSpec((1, tm, d), lambda e, m, j: (e, m, 0)),
                  pl.BlockSpec((1, d, tf), lambda e, m, j: (e, 0, jnp.minimum(j, nf - 1))),
                  pl.BlockSpec((1, d, tf), lambda e, m, j: (e, 0, jnp.minimum(j, nf - 1))),
                  pl.BlockSpec((1, ff, tn), lambda e, m, j: (e, 0, jnp.maximum(j - nf, 0)))],
        out_specs=pl.BlockSpec((1, tm, tn), lambda e, m, j: (e, m, jnp.maximum(j - nf, 0))),
        out_shape=jax.ShapeDtypeStruct((ne, rows, d), BF16),
        scratch_shapes=[pltpu.VMEM((nf, tm, tf), BF16), pltpu.VMEM((d, 2 * tf), BF16)],
        compiler_params=_params("parallel", "parallel", "arbitrary"),
        name="ffn",
    )(xg, w_gate, w_up, w_down)


def _combine_kernel(pos_ref, aff_ref, y_ref, x1_ref, gt_ref, pn_ref, o_ref, acc_ref):
    e = pl.program_id(2)
    tt, ne = pos_ref.shape[1], pos_ref.shape[2]
    cap = y_ref.shape[1]
    pick = lax.broadcasted_iota(jnp.int32, (tt, ne), 1) == e
    pcol = jnp.sum(jnp.where(pick, pos_ref[0], 0.0), axis=1, keepdims=True)
    acol = jnp.sum(jnp.where(pick, aff_ref[0], 0.0), axis=1, keepdims=True)
    slot = lax.broadcasted_iota(jnp.int32, (tt, cap), 1).astype(F32) + 1.0
    onehot = jnp.where(pcol == slot, 1.0, 0.0).astype(BF16)
    z = jnp.dot(onehot, y_ref[0], preferred_element_type=F32) * acol

    @pl.when(e == 0)
    def _():
        acc_ref[...] = z

    @pl.when(e > 0)
    def _():
        acc_ref[...] += z

    @pl.when(e == pl.num_programs(2) - 1)
    def _():
        y2 = acc_ref[...]
        r = y2 * lax.rsqrt(jnp.mean(y2 * y2, axis=-1, keepdims=True) + EPS) * pn_ref[...]
        o_ref[0] = x1_ref[0] + gt_ref[0] * r


def _combine(pos_t, aff, y, x1, gt2, pn2, cap):
    b, t, d = x1.shape
    ne = aff.shape[-1]
    tt = _blk(512, t)
    return pl.pallas_call(
        _combine_kernel,
        grid=(b, t // tt, ne),
        in_specs=[pl.BlockSpec((1, tt, ne), lambda i, j, e: (i, j, 0)),
                  pl.BlockSpec((1, tt, ne), lambda i, j, e: (i, j, 0)),
                  pl.BlockSpec((1, cap, d), lambda i, j, e: (e, i, 0)),
                  pl.BlockSpec((1, tt, d), lambda i, j, e: (i, j, 0)),
                  pl.BlockSpec((1, 1, d), lambda i, j, e: (i, 0, 0)),
                  pl.BlockSpec((1, d), lambda i, j, e: (0, 0))],
        out_specs=pl.BlockSpec((1, tt, d), lambda i, j, e: (i, j, 0)),
        out_shape=jax.ShapeDtypeStruct((b, t, d), F32),
        scratch_shapes=[pltpu.VMEM((tt, d), F32)],
        compiler_params=_params("parallel", "parallel", "arbitrary"),
        name="combine",
    )(pos_t, aff, y, x1, gt2, pn2.reshape(1, d))


def _spread_matrix(ne, w):
    s = np.zeros((5 * ne, 2 * ne * w), np.float32)
    for e in range(ne):
        s[e, e * w:(e + 1) * w] = SLOT_BLK
        s[ne + e, e * w:(e + 1) * w] = 1.0
        for part in (2, 3, 4):
            s[part * ne + e, (ne + e) * w:(ne + e + 1) * w] = 1.0
    return jnp.asarray(s, BF16)


def _combine_win_kernel(row_ref, base_ref, pos_ref, aff_ref, spread_ref, *refs):
    del row_ref
    ne = pos_ref.shape[2]
    y_refs = refs[:ne]
    x1_ref, gt_ref, pn_ref, o_ref = refs[ne:]
    b = pl.program_id(0)
    k = pl.program_id(1)
    nt = pl.num_programs(1)
    w = y_refs[0].shape[0]
    pos = pos_ref[0]
    aff = aff_ref[0]
    p_hi = jnp.floor(pos * (1.0 / SLOT_BLK))
    a_parts = _split_bf16(aff)
    a_parts.append((aff - a_parts[0].astype(F32) - a_parts[1].astype(F32)).astype(BF16))
    lhs = jnp.concatenate([p_hi.astype(BF16), (pos - SLOT_BLK * p_hi).astype(BF16)] + a_parts, axis=1)
    spread = jnp.dot(lhs, spread_ref[...], preferred_element_type=F32)
    lane = lax.broadcasted_iota(jnp.int32, (1, ne * w), 1)
    seg = lane // w
    seg_base = jnp.zeros((1, ne * w), jnp.int32)
    for e in range(ne):
        seg_base = jnp.where(seg == e, base_ref[(b * ne + e) * nt + k], seg_base)
    slot = (seg_base + lane % w + 1).astype(F32)
    onehot = jnp.where(spread[:, :ne * w] == slot, spread[:, ne * w:], 0.0).astype(BF16)
    y2 = None
    gw = COMBINE_GROUP * w
    for g in range(ne // COMBINE_GROUP):
        ywin = jnp.concatenate([r[...] for r in y_refs[COMBINE_GROUP * g:COMBINE_GROUP * (g + 1)]], axis=0)
        part = jnp.dot(onehot[:, g * gw:(g + 1) * gw], ywin, preferred_element_type=F32)
        y2 = part if y2 is None else y2 + part
    r = y2 * lax.rsqrt(jnp.mean(y2 * y2, axis=-1, keepdims=True) + EPS) * pn_ref[...]
    o_ref[0] = x1_ref[0] + gt_ref[0] * r


def _combine_win(row, base, pos_t, aff, y, x1, gt2, pn2, w):
    b, t, d = x1.shape
    ne = aff.shape[-1]
    nt = t // TOK_TILE

    def y_spec(e):
        return pl.BlockSpec((pl.Element(w), pl.Element(d)),
                            lambda i, k, row_ref, base_ref: (row_ref[(i * ne + e) * nt + k] * SLOT_BLK, 0))

    tile = lambda i, k, row_ref, base_ref: (i, k, 0)
    const = lambda i, k, row_ref, base_ref: (0, 0)
    spread = _spread_matrix(ne, w)
    return pl.pallas_call(
        _combine_win_kernel,
        grid_spec=pltpu.PrefetchScalarGridSpec(
            num_scalar_prefetch=2,
            grid=(b, nt),
            in_specs=[pl.BlockSpec((1, TOK_TILE, ne), tile),
                      pl.BlockSpec((1, TOK_TILE, ne), tile),
                      pl.BlockSpec(spread.shape, const)]
                     + [y_spec(e) for e in range(ne)]
                     + [pl.BlockSpec((1, TOK_TILE, d), tile),
                        pl.BlockSpec((1, 1, d), lambda i, k, row_ref, base_ref: (i, 0, 0)),
                        pl.BlockSpec((1, d), const)],
            out_specs=pl.BlockSpec((1, TOK_TILE, d), tile)),
        out_shape=jax.ShapeDtypeStruct((b, t, d), F32),
        compiler_params=_params("parallel", "parallel"),
        name="combine_win",
    )(row, base, pos_t, aff, spread, *([y.reshape(-1, d)] * ne), x1, gt2, pn2.reshape(1, d))


def _routing_tables(fill, cap):
    b, ne, t = fill.shape
    w = min(2 * SLOT_BLK, cap)
    ends = fill[:, :, TOK_TILE - 1::TOK_TILE].astype(jnp.int32)
    starts = jnp.concatenate([jnp.zeros((b, ne, 1), jnp.int32), ends[:, :, :-1]], axis=-1)
    sparse = jnp.all(ends - starts <= SLOT_BLK)
    base = jnp.minimum(starts // SLOT_BLK * SLOT_BLK, cap - w)
    group = (jnp.arange(ne)[None, :, None] * b + jnp.arange(b)[:, None, None]) * cap
    row_blk = (group + base) // SLOT_BLK
    return sparse, (starts // 8 * 8).reshape(-1), row_blk.reshape(-1), base.reshape(-1), w


def _rope_tables(t):
    pos = jnp.arange(t)
    r = (pos // GRID_W).astype(F32)
    cl = (pos % GRID_W).astype(F32)
    quarter = HEAD_DIM // 4
    inv = ROPE_THETA ** (-jnp.arange(quarter, dtype=F32) / quarter)
    ang_r = r[:, None] * inv
    ang_c = cl[:, None] * inv
    ang = jnp.concatenate([ang_r, ang_r, ang_c, ang_c], axis=-1)
    return jnp.cos(ang), jnp.sin(ang)


def _kv_heads(h, w_in, q_width, k_norm, rope):
    plan = (((0, ATT_KV_W, "norm_rope", 0, 1.0), (ATT_KV_W, 2 * ATT_KV_W, "plain", 1, 1.0)),
            ((0, RET_QK_W, "rope", 2, RET_QK_DIM ** -0.5),),
            ((0, 512, "plain", 3, 1.0),),
            ((0, 512, "plain", 3, 1.0),))
    outs = ((ATT_KV_W, ATT_KV_W, lambda j: 0), (ATT_KV_W, ATT_KV_W, lambda j: 0),
            (RET_QK_W, RET_QK_W, lambda j: 0), (RET_V_W, 512, lambda j: jnp.clip(j - 2, 0, 1)))
    return _proj_heads(h, w_in, q_width, plan, outs, k_norm, rope)


def _q_heads(h, w_in, q_norm, rope):
    plan = (((0, 512, "norm_rope", 0, 1.0),), ((0, 512, "norm_rope", 0, 1.0),), ((0, RET_QK_W, "rope", 1, 1.0),))
    outs = ((ATT_W, 512, lambda j: jnp.minimum(j, 1)), (RET_QK_W, RET_QK_W, lambda j: 0))
    return _proj_heads(h, w_in, 0, plan, outs, q_norm, rope)


def kernel(x, c, ctx, c_ctx, w_mod, b_mod, pre_norm1, post_norm1, pre_norm2, post_norm2, w_in, q_norm,
           k_norm, ret_decay, ret_gn, w_o_att, w_o_ret, w_out, w_router, w_gate, w_up, w_down):
    b, t, d = x.shape
    n_ctx = ctx.shape[1]
    depth = w_mod.shape[0]
    q_width = ATT_W + RET_QK_W + RET_V_W + 2 * d
    cap = EC_FACTOR * t // N_EXPERTS
    rope_lat = tuple(jnp.tile(tbl, (b, 1)) for tbl in _rope_tables(t))
    rope_ctx = (jnp.ones((b * n_ctx, HEAD_DIM), F32), jnp.zeros((b * n_ctx, HEAD_DIM), F32))
    cs = jnp.zeros((8, d), F32).at[:b].set(c).at[b].set(c_ctx)
    xc = ctx
    for layer in range(depth):
        assert layer == depth - 1, "context-stream update between layers is not implemented"
        mod = _mod_vectors(cs, w_mod[layer], b_mod[layer])
        sh1, sc1, gt1, sh2, sc2, gt2 = [m[:b, None, :] for m in jnp.split(mod, 6, axis=-1)]
        csh1, csc1 = [jnp.broadcast_to(m[b][None, None, :], (b, 1, d)) for m in jnp.split(mod, 6, axis=-1)[:2]]
        dec = -jax.nn.softplus(ret_decay[layer].astype(F32))
        wl = w_in[layer]

        hc = _prenorm(xc, pre_norm1[layer], csh1, csc1).reshape(b * n_ctx, d)
        kc_a, vc_a, kc_r, vc_r = _kv_heads(hc, wl, q_width, k_norm[layer], rope_ctx)
        s0f, s0b = _ctx_states(dec, kc_r.reshape(b, n_ctx, RET_QK_W), vc_r.reshape(b, n_ctx, RET_V_W))

        h = _prenorm(x, pre_norm1[layer], sh1, sc1).reshape(b * t, d)
        q_a, q_r = _q_heads(h, wl, q_norm[layer], rope_lat)
        zg = _proj(h, wl, ATT_W + RET_QK_W, RET_V_W + 2 * d, out_dtype=F32, tn=512)
        k_a, v_a, k_r, v_r = _kv_heads(h, wl, q_width, k_norm[layer], rope_lat)

        keys = jnp.concatenate([kc_a.reshape(b, n_ctx, ATT_KV_W), k_a.reshape(b, t, ATT_KV_W)], axis=1)
        vals = jnp.concatenate([vc_a.reshape(b, n_ctx, ATT_KV_W), v_a.reshape(b, t, ATT_KV_W)], axis=1)
        kt = keys.reshape(b, n_ctx + t, ATT_KV_HEADS, HEAD_DIM).transpose(0, 2, 3, 1)
        o_att = _attention(q_a.reshape(b, t, ATT_W), kt, vals)

        o_f, o_b = _retention(dec, q_r.reshape(b, t, RET_QK_W), k_r.reshape(b, t, RET_QK_W),
                              v_r.reshape(b, t, RET_V_W), s0f, s0b)

        y = _merge1(o_att.reshape(b * t, ATT_W), o_f.reshape(b * t, RET_V_W), o_b.reshape(b * t, RET_V_W),
                    zg, ret_gn[layer], w_o_att[layer].astype(BF16), w_o_ret[layer].astype(BF16))
        x1, h2, aff = _merge2(y, w_out[layer].astype(BF16), x, gt1, post_norm1[layer], pre_norm2[layer],
                              sh2, sc2, w_router[layer])

        pos, fill = _topk_slots(aff.transpose(0, 2, 1), cap)
        pos_t = pos.transpose(0, 2, 1)
        sparse, w0, row, base, win = _routing_tables(fill, cap)
        xg = lax.cond(sparse,
                      lambda: _gather_win(w0, pos, h2, cap),
                      lambda: _gather(pos, h2, cap))
        yg = _ffn(xg, w_gate[layer], w_up[layer], w_down[layer])
        x = lax.cond(sparse,
                     lambda: _combine_win(row, base, pos_t, aff, yg, x1, gt2, post_norm2[layer], win),
                     lambda: _combine(pos_t, aff, yg, x1, gt2, post_norm2[layer], cap))
    return x
```

```python
import functools

import jax
import jax.numpy as jnp
import numpy as np
from jax import lax
from jax.experimental import pallas as pl
from jax.experimental.pallas import tpu as pltpu

F32 = jnp.float32
BF16 = jnp.bfloat16

GRID_W = 64
HEAD_DIM = 128
ATT_HEADS = 8
ATT_KV_HEADS = 2
ATT_GROUP = ATT_HEADS // ATT_KV_HEADS
RET_HEADS = 4
RET_QK_DIM = 128
RET_V_DIM = 256
ROPE_THETA = 10000.0
N_EXPERTS = 16
EC_FACTOR = 2
EPS = 1e-6

ATT_W = ATT_HEADS * HEAD_DIM
ATT_KV_W = ATT_KV_HEADS * HEAD_DIM
RET_QK_W = RET_HEADS * RET_QK_DIM
RET_V_W = RET_HEADS * RET_V_DIM

LANES = 128
VMEM_LIMIT = 56 * 1024 * 1024
RET_CHUNK = 256
MIN_NORMAL_F32_BITS = 0x00800000
LOG2_E = 1.4426950408889634
TOK_TILE = 256
SLOT_BLK = 64
GATHER_WIN = SLOT_BLK + 16
COMBINE_GROUP = 4


def _blk(pref, n):
    return pref if n % pref == 0 else n


def _params(*sem):
    return pltpu.CompilerParams(dimension_semantics=sem, vmem_limit_bytes=VMEM_LIMIT)


def _sigmoid(x):
    return 0.5 * jnp.tanh(0.5 * x) + 0.5


def _mod_kernel(s_ref, w_ref, b_ref, o_ref):
    s = s_ref[...]
    s = s * _sigmoid(s)
    o_ref[...] = jnp.dot(s, w_ref[...], precision=lax.Precision.HIGHEST,
                         preferred_element_type=F32) + b_ref[...]


def _mod_vectors(cs, w_mod, b_mod):
    rows, d = cs.shape
    n = w_mod.shape[1]
    tn = _blk(2048, n)
    return pl.pallas_call(
        _mod_kernel,
        grid=(n // tn,),
        in_specs=[pl.BlockSpec((rows, d), lambda j: (0, 0)),
                  pl.BlockSpec((d, tn), lambda j: (0, j)),
                  pl.BlockSpec((1, tn), lambda j: (0, j))],
        out_specs=pl.BlockSpec((rows, tn), lambda j: (0, j)),
        out_shape=jax.ShapeDtypeStruct((rows, n), F32),
        compiler_params=_params("parallel"),
        name="mod",
    )(cs, w_mod, b_mod.reshape(1, n))


def _prenorm_kernel(x_ref, g_ref, sh_ref, sc_ref, o_ref):
    x = x_ref[0]
    y = x * lax.rsqrt(jnp.mean(x * x, axis=-1, keepdims=True) + EPS) * g_ref[...]
    o_ref[0] = (y * (1.0 + sc_ref[0]) + sh_ref[0]).astype(o_ref.dtype)


def _prenorm(x, gain, shift, scale):
    b, t, d = x.shape
    tm = _blk(1024, t)
    return pl.pallas_call(
        _prenorm_kernel,
        grid=(b, t // tm),
        in_specs=[pl.BlockSpec((1, tm, d), lambda i, j: (i, j, 0)),
                  pl.BlockSpec((1, d), lambda i, j: (0, 0)),
                  pl.BlockSpec((1, 1, d), lambda i, j: (i, 0, 0)),
                  pl.BlockSpec((1, 1, d), lambda i, j: (i, 0, 0))],
        out_specs=pl.BlockSpec((1, tm, d), lambda i, j: (i, j, 0)),
        out_shape=jax.ShapeDtypeStruct((b, t, d), BF16),
        compiler_params=_params("parallel", "parallel"),
        name="prenorm",
    )(x, gain.reshape(1, d), shift, scale)


def _split_bf16(v):
    hi = v.astype(BF16)
    return [hi, (v - hi.astype(F32)).astype(BF16)]


def _rotate_half_matrix():
    quarter = HEAD_DIM // 4
    r = np.zeros((HEAD_DIM, HEAD_DIM), np.float32)
    for i in range(HEAD_DIM):
        if i % (2 * quarter) < quarter:
            r[i + quarter, i] = -1.0
        else:
            r[i - quarter, i] = 1.0
    return r


def _head_mix_matrix(mode):
    rot = _rotate_half_matrix()
    if mode == "rope":
        return jnp.asarray(np.concatenate([rot, rot], axis=0), BF16)
    ones = np.ones((HEAD_DIM, HEAD_DIM), np.float32)
    zero = np.zeros((HEAD_DIM, HEAD_DIM), np.float32)
    return jnp.asarray(np.block([[ones, zero], [ones, zero], [zero, rot], [zero, rot]]), BF16)


def _proj_kernel(h_ref, w_ref, o_ref):
    o_ref[...] = jnp.dot(h_ref[...], w_ref[...].astype(BF16), preferred_element_type=F32).astype(o_ref.dtype)


def _proj(h, w_in, col0, width, *, out_dtype, tn):
    r, d = h.shape
    tm = _blk(2048, r)
    assert col0 % tn == 0 and width % tn == 0
    c0 = col0 // tn
    return pl.pallas_call(
        _proj_kernel,
        grid=(r // tm, width // tn),
        in_specs=[pl.BlockSpec((tm, d), lambda i, j: (i, 0)),
                  pl.BlockSpec((d, tn), lambda i, j: (0, c0 + j))],
        out_specs=pl.BlockSpec((tm, tn), lambda i, j: (i, j)),
        out_shape=jax.ShapeDtypeStruct((r, width), out_dtype),
        compiler_params=_params("parallel", "parallel"),
        name="proj_plain",
    )(h, w_in)


def _rope_head(xh, mode, gain, mix, cos, sin, pre_scale):
    if mode == "norm_rope":
        xg = xh * gain
        mixed = jnp.dot(jnp.concatenate(_split_bf16(xh * xh) + _split_bf16(xg), axis=1), mix,
                        preferred_element_type=F32)
        inv = lax.rsqrt(mixed[:, :HEAD_DIM] * (1.0 / HEAD_DIM) + EPS)
        return (xg * cos + mixed[:, HEAD_DIM:] * sin) * inv
    xs = xh * pre_scale if pre_scale != 1.0 else xh
    rot = jnp.dot(jnp.concatenate(_split_bf16(xs), axis=1), mix, preferred_element_type=F32)
    return xs * cos + rot * sin


def _proj_heads_kernel(h_ref, w_ref, gain_ref, mixn_ref, mixr_ref, cos_ref, sin_ref, *o_refs, plan):
    acc = jnp.dot(h_ref[...], w_ref[...].astype(BF16), preferred_element_type=F32)
    for jj, pieces in enumerate(plan):
        @pl.when(pl.program_id(1) == jj)
        def _(pieces=pieces):
            for lo, hi, mode, out, pre_scale in pieces:
                o_ref = o_refs[out]
                if mode == "plain":
                    o_ref[...] = acc[:, lo:hi].astype(o_ref.dtype)
                    continue
                mix = mixn_ref[...] if mode == "norm_rope" else mixr_ref[...]
                for hh in range((hi - lo) // HEAD_DIM):
                    xh = acc[:, lo + hh * HEAD_DIM:lo + (hh + 1) * HEAD_DIM]
                    res = _rope_head(xh, mode, gain_ref[...], mix, cos_ref[...], sin_ref[...], pre_scale)
                    o_ref[:, hh * HEAD_DIM:(hh + 1) * HEAD_DIM] = res.astype(o_ref.dtype)


def _proj_heads(h, w_in, col0, plan, outs, gain, rope):
    r, d = h.shape
    cos, sin = rope
    tm = min(_blk(2048, r), cos.shape[0])
    tn = 512
    assert col0 % tn == 0
    c0 = col0 // tn
    period = cos.shape[0] // tm
    assert cos.shape == (period * tm, HEAD_DIM) and (r // tm) % period == 0
    mixn, mixr = _head_mix_matrix("norm_rope"), _head_mix_matrix("rope")
    const = lambda i, j: (0, 0)
    rows = lambda i, j: (i, 0)
    table = lambda i, j: (i % period, 0)
    return pl.pallas_call(
        functools.partial(_proj_heads_kernel, plan=plan),
        grid=(r // tm, len(plan)),
        in_specs=[pl.BlockSpec((tm, d), rows),
                  pl.BlockSpec((d, tn), lambda i, j: (0, c0 + j)),
                  pl.BlockSpec((1, HEAD_DIM), const),
                  pl.BlockSpec(mixn.shape, const),
                  pl.BlockSpec(mixr.shape, const),
                  pl.BlockSpec((tm, HEAD_DIM), table),
                  pl.BlockSpec((tm, HEAD_DIM), table)],
        out_specs=tuple(pl.BlockSpec((tm, bw), functools.partial(lambda i, j, f: (i, f(j)), f=f))
                        for _, bw, f in outs),
        out_shape=tuple(jax.ShapeDtypeStruct((r, width), BF16) for width, _, _ in outs),
        compiler_params=_params("parallel", "arbitrary"),
        name="proj_heads",
    )(h, w_in, gain.reshape(1, HEAD_DIM), mixn, mixr, cos, sin)


def _ctx_state_kernel(dec_ref, k_ref, v_ref, sf_ref, sb_ref):
    hh = pl.program_id(1)
    lf = dec_ref[0, hh]
    lb = dec_ref[1, hh]
    k = k_ref[0].astype(F32)
    v = v_ref[0]
    n = k.shape[0]
    pos = lax.broadcasted_iota(jnp.int32, k.shape, 0).astype(F32)
    kf = (k * jnp.exp((n - 1.0 - pos) * lf)).T.astype(BF16)
    kb = (k * jnp.exp(pos * lb)).T.astype(BF16)
    sf_ref[0, 0] = jnp.dot(kf, v, preferred_element_type=F32)
    sb_ref[0, 0] = jnp.dot(kb, v, preferred_element_type=F32)


def _ctx_states(dec, k_r, v_r):
    b, n, _ = k_r.shape
    spec_s = pl.BlockSpec((1, 1, RET_QK_DIM, RET_V_DIM), lambda i, j: (i, j, 0, 0))
    shape_s = jax.ShapeDtypeStruct((b, RET_HEADS, RET_QK_DIM, RET_V_DIM), F32)
    return pl.pallas_call(
        _ctx_state_kernel,
        grid=(b, RET_HEADS),
        in_specs=[pl.BlockSpec(memory_space=pltpu.SMEM),
                  pl.BlockSpec((1, n, RET_QK_DIM), lambda i, j: (i, 0, j)),
                  pl.BlockSpec((1, n, RET_V_DIM), lambda i, j: (i, 0, j))],
        out_specs=(spec_s, spec_s),
        out_shape=(shape_s, shape_s),
        compiler_params=_params("parallel", "parallel"),
        name="ctx_state",
    )(dec, k_r, v_r)


def _attn_kernel(q_ref, kt_ref, v_ref, o_ref, *, scale):
    for hk in range(ATT_KV_HEADS):
        kt = kt_ref[0, hk]
        v = v_ref[0, :, hk * HEAD_DIM:(hk + 1) * HEAD_DIM]
        for g in range(ATT_GROUP):
            cols = slice((hk * ATT_GROUP + g) * HEAD_DIM, (hk * ATT_GROUP + g + 1) * HEAD_DIM)
            s = jnp.dot(q_ref[0, :, cols], kt, preferred_element_type=F32)
            m = jnp.max(s, axis=-1, keepdims=True)
            p = jnp.exp2((s - m) * (scale * LOG2_E))
            l = jnp.sum(p, axis=-1, keepdims=True)
            o = jnp.dot(p.astype(BF16), v, preferred_element_type=F32)
            o_ref[0, :, cols] = (o / l).astype(o_ref.dtype)


def _attention(q, kt, v):
    b, t, _ = q.shape
    s = kt.shape[-1]
    tq = _blk(256, t)
    return pl.pallas_call(
        functools.partial(_attn_kernel, scale=HEAD_DIM ** -0.5),
        grid=(b, t // tq),
        in_specs=[pl.BlockSpec((1, tq, ATT_W), lambda i, n: (i, n, 0)),
                  pl.BlockSpec((1, ATT_KV_HEADS, HEAD_DIM, s), lambda i, n: (i, 0, 0, 0)),
                  pl.BlockSpec((1, s, ATT_KV_W), lambda i, n: (i, 0, 0))],
        out_specs=pl.BlockSpec((1, tq, ATT_W), lambda i, n: (i, n, 0)),
        out_shape=jax.ShapeDtypeStruct((b, t, ATT_W), BF16),
        compiler_params=_params("parallel", "parallel"),
        name="attention",
    )(q, kt, v)


def _ret_kernel(dec_ref, qf_ref, kf_ref, vf_ref, qb_ref, kb_ref, vb_ref, s0f_ref, s0b_ref,
                of_ref, ob_ref, sf_ref, sb_ref, mf_ref, mb_ref):
    n = pl.program_id(1)
    c = qf_ref.shape[1]

    @pl.when(n == 0)
    def _():
        sf_ref[...] = s0f_ref[0]
        sb_ref[...] = s0b_ref[0]
        ri = lax.broadcasted_iota(jnp.int32, (c, c), 0)
        ci = lax.broadcasted_iota(jnp.int32, (c, c), 1)
        d = (ri - ci).astype(F32)
        for hh in range(RET_HEADS):
            mf_ref[hh] = jnp.where(d >= 0, jnp.exp(jnp.maximum(d, 0.0) * dec_ref[0, hh]), 0.0)
            mb_ref[hh] = jnp.where(d <= 0, jnp.exp(jnp.maximum(-d, 0.0) * dec_ref[1, hh]), 0.0)

    nt = (((1,), (1,)), ((), ()))
    row_v = lax.broadcasted_iota(jnp.int32, (c, RET_V_DIM), 0).astype(F32)
    row_k = lax.broadcasted_iota(jnp.int32, (c, RET_QK_DIM), 0).astype(F32)

    def sweep(q_ref, k_ref, v_ref, o_ref, s_ref, m_ref, hh, lg, wq_age, wk_age):
        qk = slice(hh * RET_QK_DIM, (hh + 1) * RET_QK_DIM)
        vv = slice(hh * RET_V_DIM, (hh + 1) * RET_V_DIM)
        q = q_ref[0, :, qk]
        k = k_ref[0, :, qk]
        v = v_ref[0, :, vv]
        s = lax.dot_general(q, k, nt, preferred_element_type=F32) * m_ref[hh]
        o_in = jnp.dot(s.astype(BF16), v, preferred_element_type=F32)
        state = s_ref[hh]
        o_x = jnp.dot(q, state.astype(BF16), preferred_element_type=F32) * jnp.exp(wq_age * lg)
        o_ref[0, :, vv] = o_in + o_x
        kw = (k.astype(F32) * jnp.exp(wk_age * lg)).T.astype(BF16)
        g_chunk = jnp.exp(jnp.full((1, RET_V_DIM), c * lg, F32))
        s_ref[hh] = g_chunk * state + jnp.dot(kw, v, preferred_element_type=F32)

    for hh in range(RET_HEADS):
        sweep(qf_ref, kf_ref, vf_ref, of_ref, sf_ref, mf_ref, hh, dec_ref[0, hh], row_v + 1.0, c - 1.0 - row_k)
        sweep(qb_ref, kb_ref, vb_ref, ob_ref, sb_ref, mb_ref, hh, dec_ref[1, hh], c - row_v, row_k)


def _retention(dec, q, k, v, s0f, s0b):
    b, t, _ = q.shape
    c = _blk(RET_CHUNK, t)
    nc = t // c
    fwd = lambda i, n: (i, n, 0)
    bwd = lambda i, n: (i, nc - 1 - n, 0)
    spec_s = pl.BlockSpec((1, RET_HEADS, RET_QK_DIM, RET_V_DIM), lambda i, n: (i, 0, 0, 0))
    out_shape = jax.ShapeDtypeStruct((b, t, RET_V_W), F32)
    return pl.pallas_call(
        _ret_kernel,
        grid=(b, nc),
        in_specs=[pl.BlockSpec(memory_space=pltpu.SMEM),
                  pl.BlockSpec((1, c, RET_QK_W), fwd),
                  pl.BlockSpec((1, c, RET_QK_W), fwd),
                  pl.BlockSpec((1, c, RET_V_W), fwd),
                  pl.BlockSpec((1, c, RET_QK_W), bwd),
                  pl.BlockSpec((1, c, RET_QK_W), bwd),
                  pl.BlockSpec((1, c, RET_V_W), bwd),
                  spec_s, spec_s],
        out_specs=(pl.BlockSpec((1, c, RET_V_W), fwd), pl.BlockSpec((1, c, RET_V_W), bwd)),
        out_shape=(out_shape, out_shape),
        scratch_shapes=[pltpu.VMEM((RET_HEADS, RET_QK_DIM, RET_V_DIM), F32),
                        pltpu.VMEM((RET_HEADS, RET_QK_DIM, RET_V_DIM), F32),
                        pltpu.VMEM((RET_HEADS, c, c), F32),
                        pltpu.VMEM((RET_HEADS, c, c), F32)],
        compiler_params=_params("parallel", "arbitrary"),
        name="retention",
    )(dec, q, k, v, q, k, v, s0f, s0b)


def _merge1_kernel(oa_ref, of_ref, ob_ref, g_ref, *refs, nslab):
    ga_refs, gr_refs = refs[:nslab], refs[nslab:2 * nslab]
    gn_ref, wa_ref, wr_ref, y_ref = refs[2 * nslab:]
    o = of_ref[...] + ob_ref[...]
    heads = []
    for hh in range(RET_HEADS):
        sl = slice(hh * RET_V_DIM, (hh + 1) * RET_V_DIM)
        oh = o[:, sl]
        dlt = oh - jnp.mean(oh, axis=-1, keepdims=True)
        yh = dlt * lax.rsqrt(jnp.mean(dlt * dlt, axis=-1, keepdims=True) + EPS)
        g = g_ref[:, sl]
        heads.append((yh * gn_ref[:, sl] * (g * _sigmoid(g))).astype(BF16))
    o_ret = jnp.concatenate(heads, axis=1)
    tn = ga_refs[0].shape[1]
    for c in range(nslab):
        cols = slice(c * tn, (c + 1) * tn)
        ya = jnp.dot(oa_ref[...], wa_ref[:, cols], preferred_element_type=F32)
        yr = jnp.dot(o_ret, wr_ref[:, cols], preferred_element_type=F32)
        y_ref[:, cols] = (_sigmoid(ga_refs[c][...]) * ya + _sigmoid(gr_refs[c][...]) * yr).astype(y_ref.dtype)


def _merge1(o_att, o_f, o_b, zg, gn, wa, wr):
    r = o_att.shape[0]
    d = wa.shape[1]
    tm = _blk(256, r)
    tn = _blk(1024, d)
    nslab = d // tn
    ga0 = RET_V_W // tn
    gr0 = (RET_V_W + d) // tn
    row = lambda i: (i, 0)
    const = lambda i: (0, 0)
    gate_specs = [pl.BlockSpec((tm, tn), functools.partial(lambda i, c: (i, c), c=c0 + c))
                  for c0 in (ga0, gr0) for c in range(nslab)]
    return pl.pallas_call(
        functools.partial(_merge1_kernel, nslab=nslab),
        grid=(r // tm,),
        in_specs=[pl.BlockSpec((tm, ATT_W), row),
                  pl.BlockSpec((tm, RET_V_W), row),
                  pl.BlockSpec((tm, RET_V_W), row),
                  pl.BlockSpec((tm, RET_V_W), row)]
                 + gate_specs
                 + [pl.BlockSpec((1, RET_V_W), const),
                    pl.BlockSpec((ATT_W, d), const),
                    pl.BlockSpec((RET_V_W, d), const)],
        out_specs=pl.BlockSpec((tm, d), row),
        out_shape=jax.ShapeDtypeStruct((r, d), BF16),
        compiler_params=_params("parallel"),
        name="merge1",
    )(o_att, o_f, o_b, zg, *([zg] * (2 * nslab)), gn.reshape(1, RET_V_W), wa, wr)


def _merge2_kernel(y_ref, w_ref, x_ref, gt_ref, pn1_ref, pn2_ref, sh_ref, sc_ref, wr_ref,
                   x1_ref, h2_ref, aff_ref):
    yy = jnp.dot(y_ref[0], w_ref[...], preferred_element_type=F32)
    r = yy * lax.rsqrt(jnp.mean(yy * yy, axis=-1, keepdims=True) + EPS) * pn1_ref[...]
    x1 = x_ref[0] + gt_ref[0] * r
    x1_ref[0] = x1
    h2 = x1 * lax.rsqrt(jnp.mean(x1 * x1, axis=-1, keepdims=True) + EPS) * pn2_ref[...]
    h2 = h2 * (1.0 + sc_ref[0]) + sh_ref[0]
    h_hi = h2.astype(BF16)
    h2_ref[0] = h_hi
    h_lo = (h2 - h_hi.astype(F32)).astype(BF16)
    wr = wr_ref[...]
    w_hi = wr.astype(BF16)
    w_lo = (wr - w_hi.astype(F32)).astype(BF16)
    ne = wr.shape[1]
    r_hi = jnp.dot(h_hi, jnp.concatenate([w_hi, w_lo], axis=1), preferred_element_type=F32)
    logits = r_hi[:, :ne] + r_hi[:, ne:] + jnp.dot(h_lo, w_hi, preferred_element_type=F32)
    e = jnp.exp(logits - jnp.max(logits, axis=-1, keepdims=True))
    aff_ref[0] = e / jnp.sum(e, axis=-1, keepdims=True)


def _merge2(y, w_out, x, gt1, pn1, pn2, sh2, sc2, w_router):
    b, t, d = x.shape
    tm = _blk(512, t)
    ne = w_router.shape[1]
    tile = lambda i, j: (i, j, 0)
    vec = lambda i, j: (0, 0)
    per_b = lambda i, j: (i, 0, 0)
    return pl.pallas_call(
        _merge2_kernel,
        grid=(b, t // tm),
        in_specs=[pl.BlockSpec((1, tm, d), tile),
                  pl.BlockSpec((d, d), vec),
                  pl.BlockSpec((1, tm, d), tile),
                  pl.BlockSpec((1, 1, d), per_b),
                  pl.BlockSpec((1, d), vec),
                  pl.BlockSpec((1, d), vec),
                  pl.BlockSpec((1, 1, d), per_b),
                  pl.BlockSpec((1, 1, d), per_b),
                  pl.BlockSpec((d, ne), vec)],
        out_specs=(pl.BlockSpec((1, tm, d), tile),
                   pl.BlockSpec((1, tm, d), tile),
                   pl.BlockSpec((1, tm, ne), tile)),
        out_shape=(jax.ShapeDtypeStruct((b, t, d), F32),
                   jax.ShapeDtypeStruct((b, t, d), BF16),
                   jax.ShapeDtypeStruct((b, t, ne), F32)),
        compiler_params=_params("parallel", "parallel"),
        name="merge2",
    )(y.reshape(b, t, d), w_out, x, gt1, pn1.reshape(1, d), pn2.reshape(1, d), sh2, sc2, w_router)


def _lane_cumsum(x, tri):
    off = jnp.zeros((x.shape[0], 1), F32)
    parts = []
    for cidx in range(x.shape[1] // LANES):
        xc = x[:, cidx * LANES:(cidx + 1) * LANES].astype(BF16)
        cs = jnp.dot(xc, tri, preferred_element_type=F32) + off
        parts.append(cs)
        off = cs[:, LANES - 1:LANES]
    return jnp.concatenate(parts, axis=1)


def _topk_kernel(a_ref, pos_ref, fill_ref, *, cap):
    a = a_ref[0]
    bits = jnp.zeros((a.shape[0], 1), jnp.int32)
    for bit in range(30, -1, -1):
        cand = bits | (1 << bit)
        cnt = jnp.sum(jnp.where(a >= lax.bitcast_convert_type(cand, F32), 1.0, 0.0), axis=1, keepdims=True)
        bits = jnp.where(cnt >= cap, cand, bits)
    bits = jnp.where(bits < MIN_NORMAL_F32_BITS, 0, bits)
    thr = lax.bitcast_convert_type(bits, F32)
    ri = lax.broadcasted_iota(jnp.int32, (LANES, LANES), 0)
    ci = lax.broadcasted_iota(jnp.int32, (LANES, LANES), 1)
    tri = jnp.where(ri <= ci, 1.0, 0.0).astype(BF16)
    gt = a > thr
    eq = a == thr
    n_gt = jnp.sum(jnp.where(gt, 1.0, 0.0), axis=1, keepdims=True)
    eq_rank = _lane_cumsum(jnp.where(eq, 1.0, 0.0), tri)
    sel = jnp.where(gt, 1.0, jnp.where(eq, jnp.where(eq_rank <= cap - n_gt, 1.0, 0.0), 0.0))
    filled = _lane_cumsum(sel, tri)
    pos_ref[0] = filled * sel
    fill_ref[0] = filled


def _topk_slots(aff_t, cap):
    b, ne, t = aff_t.shape
    spec = pl.BlockSpec((1, ne, t), lambda i: (i, 0, 0))
    shape = jax.ShapeDtypeStruct((b, ne, t), F32)
    return pl.pallas_call(
        functools.partial(_topk_kernel, cap=cap),
        grid=(b,),
        in_specs=[spec],
        out_specs=(spec, spec),
        out_shape=(shape, shape),
        compiler_params=_params("parallel"),
        name="topk",
    )(aff_t)


def _gather_kernel(pos_ref, h_ref, o_ref):
    cap = o_ref.shape[1]
    t = h_ref.shape[1]
    tc = _blk(1024, t)
    slot = lax.broadcasted_iota(jnp.int32, (cap, tc), 0).astype(F32) + 1.0
    acc = None
    for cidx in range(t // tc):
        row = pos_ref[0, 0, :, cidx * tc:(cidx + 1) * tc]
        onehot = jnp.where(row == slot, 1.0, 0.0).astype(BF16)
        part = jnp.dot(onehot, h_ref[0, cidx * tc:(cidx + 1) * tc, :], preferred_element_type=F32)
        acc = part if acc is None else acc + part
    o_ref[0] = acc.astype(o_ref.dtype)


def _gather(pos, h2, cap):
    b, ne, t = pos.shape
    d = h2.shape[-1]
    dn = _blk(1024, d)
    return pl.pallas_call(
        _gather_kernel,
        grid=(b, d // dn, ne),
        in_specs=[pl.BlockSpec((1, 1, 1, t), lambda i, n, e: (i, e, 0, 0)),
                  pl.BlockSpec((1, t, dn), lambda i, n, e: (i, 0, n))],
        out_specs=pl.BlockSpec((1, cap, dn), lambda i, n, e: (e, i, n)),
        out_shape=jax.ShapeDtypeStruct((ne, b * cap, d), BF16),
        compiler_params=_params("parallel", "parallel", "parallel"),
        name="gather",
    )(pos.reshape(b, ne, 1, t), h2)


def _gather_win_kernel(w0_ref, pos_ref, h_ref, o_ref, acc_ref, lhs_ref):
    b = pl.program_id(0)
    ne, cap = o_ref.shape[0], o_ref.shape[1]
    nt = h_ref.shape[1] // TOK_TILE
    slot1 = lax.broadcasted_iota(jnp.int32, (GATHER_WIN, TOK_TILE), 0) + 1

    @pl.when(pl.program_id(1) == 0)
    def _():
        def build(k, carry):
            for e in range(ne):
                w0 = w0_ref[(b * ne + e) * nt + k]
                row = pos_ref[0, pl.ds(e * nt + k, 1), :]
                onehot = jnp.where(row == (slot1 + w0).astype(F32), 1.0, 0.0)
                lhs_ref[k, e * GATHER_WIN:(e + 1) * GATHER_WIN, :] = onehot.astype(BF16)
            return carry
        lax.fori_loop(0, nt, build, 0)

    acc_ref[:, :8, :] = jnp.zeros((ne, 8) + acc_ref.shape[2:], F32)

    def tile(k, carry):
        t0 = pl.multiple_of(k * TOK_TILE, TOK_TILE)
        res = jnp.dot(lhs_ref[k], h_ref[0, pl.ds(t0, TOK_TILE), :], preferred_element_type=F32)
        for e in range(ne):
            w0 = pl.multiple_of(w0_ref[(b * ne + e) * nt + k], 8)
            acc_ref[e, pl.ds(w0, 8), :] += res[e * GATHER_WIN:e * GATHER_WIN + 8]
            acc_ref[e, pl.ds(w0 + 8, GATHER_WIN - 8), :] = res[e * GATHER_WIN + 8:(e + 1) * GATHER_WIN]
        return carry
    lax.fori_loop(0, nt, tile, 0, unroll=8)
    o_ref[...] = acc_ref[:, :cap, :].astype(o_ref.dtype)


def _gather_win(w0, pos, h2, cap):
    b, ne, t = pos.shape
    d = h2.shape[-1]
    nt = t // TOK_TILE
    dn = _blk(256, d)
    return pl.pallas_call(
        _gather_win_kernel,
        grid_spec=pltpu.PrefetchScalarGridSpec(
            num_scalar_prefetch=1,
            grid=(b, d // dn),
            in_specs=[pl.BlockSpec((1, ne * nt, TOK_TILE), lambda i, n, w: (i, 0, 0)),
                      pl.BlockSpec((1, t, dn), lambda i, n, w: (i, 0, n))],
            out_specs=pl.BlockSpec((ne, cap, dn), lambda i, n, w: (0, i, n)),
            scratch_shapes=[pltpu.VMEM((ne, cap + GATHER_WIN, dn), F32),
                            pltpu.VMEM((nt, ne * GATHER_WIN, TOK_TILE), BF16)]),
        out_shape=jax.ShapeDtypeStruct((ne, b * cap, d), BF16),
        compiler_params=_params("parallel", "arbitrary"),
        name="gather_win",
    )(w0, pos.reshape(b, ne * nt, TOK_TILE), h2)


def _ffn_kernel(x_ref, wg_ref, wu_ref, wd_ref, o_ref, h_ref, wgu_ref, *, nf):
    j = pl.program_id(2)
    tf = wg_ref.shape[2]

    @pl.when(j < nf)
    def _():
        wgu_ref[:, :tf] = wg_ref[0].astype(BF16)
        wgu_ref[:, tf:] = wu_ref[0].astype(BF16)
        au = jnp.dot(x_ref[0], wgu_ref[...], preferred_element_type=F32)
        a = au[:, :tf]
        h_ref[j] = (a * _sigmoid(a) * au[:, tf:]).astype(BF16)

    @pl.when(j >= nf)
    def _():
        acc = None
        for cidx in range(nf):
            part = jnp.dot(h_ref[cidx], wd_ref[0, cidx * tf:(cidx + 1) * tf, :].astype(BF16),
                           preferred_element_type=F32)
            acc = part if acc is None else acc + part
        o_ref[0] = acc.astype(o_ref.dtype)


def _ffn(xg, w_gate, w_up, w_down):
    ne, rows, d = xg.shape
    ff = w_gate.shape[-1]
    tm = _blk(2048, rows)
    tf = _blk(256, ff)
    tn = _blk(512, d)
    nf = ff // tf
    return pl.pallas_call(
        functools.partial(_ffn_kernel, nf=nf),
        grid=(ne, rows // tm, nf + d // tn),
        in_specs=[pl.BlockSpec((1, tm, d), lambda e, m, j: (e, m, 0)),
                  pl.BlockSpec((1, d, tf), lambda e, m, j: (e, 0, jnp.minimum(j, nf - 1))),
                  pl.BlockSpec((1, d, tf), lambda e, m, j: (e, 0, jnp.minimum(j, nf - 1))),
                  pl.BlockSpec((1, ff, tn), lambda e, m, j: (e, 0, jnp.maximum(j - nf, 0)))],
        out_specs=pl.BlockSpec((1, tm, tn), lambda e, m, j: (e, m, jnp.maximum(j - nf, 0))),
        out_shape=jax.ShapeDtypeStruct((ne, rows, d), BF16),
        scratch_shapes=[pltpu.VMEM((nf, tm, tf), BF16), pltpu.VMEM((d, 2 * tf), BF16)],
        compiler_params=_params("parallel", "parallel", "arbitrary"),
        name="ffn",
    )(xg, w_gate, w_up, w_down)


def _combine_kernel(pos_ref, aff_ref, y_ref, x1_ref, gt_ref, pn_ref, o_ref, acc_ref):
    e = pl.program_id(2)
    tt, ne = pos_ref.shape[1], pos_ref.shape[2]
    cap = y_ref.shape[1]
    pick = lax.broadcasted_iota(jnp.int32, (tt, ne), 1) == e
    pcol = jnp.sum(jnp.where(pick, pos_ref[0], 0.0), axis=1, keepdims=True)
    acol = jnp.sum(jnp.where(pick, aff_ref[0], 0.0), axis=1, keepdims=True)
    slot = lax.broadcasted_iota(jnp.int32, (tt, cap), 1).astype(F32) + 1.0
    onehot = jnp.where(pcol == slot, 1.0, 0.0).astype(BF16)
    z = jnp.dot(onehot, y_ref[0], preferred_element_type=F32) * acol

    @pl.when(e == 0)
    def _():
        acc_ref[...] = z

    @pl.when(e > 0)
    def _():
        acc_ref[...] += z

    @pl.when(e == pl.num_programs(2) - 1)
    def _():
        y2 = acc_ref[...]
        r = y2 * lax.rsqrt(jnp.mean(y2 * y2, axis=-1, keepdims=True) + EPS) * pn_ref[...]
        o_ref[0] = x1_ref[0] + gt_ref[0] * r


def _combine(pos_t, aff, y, x1, gt2, pn2, cap):
    b, t, d = x1.shape
    ne = aff.shape[-1]
    tt = _blk(512, t)
    return pl.pallas_call(
        _combine_kernel,
        grid=(b, t // tt, ne),
        in_specs=[pl.BlockSpec((1, tt, ne), lambda i, j, e: (i, j, 0)),
                  pl.BlockSpec((1, tt, ne), lambda i, j, e: (i, j, 0)),
                  pl.BlockSpec((1, cap, d), lambda i, j, e: (e, i, 0)),
                  pl.BlockSpec((1, tt, d), lambda i, j, e: (i, j, 0)),
                  pl.BlockSpec((1, 1, d), lambda i, j, e: (i, 0, 0)),
                  pl.BlockSpec((1, d), lambda i, j, e: (0, 0))],
        out_specs=pl.BlockSpec((1, tt, d), lambda i, j, e: (i, j, 0)),
        out_shape=jax.ShapeDtypeStruct((b, t, d), F32),
        scratch_shapes=[pltpu.VMEM((tt, d), F32)],
        compiler_params=_params("parallel", "parallel", "arbitrary"),
        name="combine",
    )(pos_t, aff, y, x1, gt2, pn2.reshape(1, d))


def _spread_matrix(ne, w):
    s = np.zeros((5 * ne, 2 * ne * w), np.float32)
    for e in range(ne):
        s[e, e * w:(e + 1) * w] = SLOT_BLK
        s[ne + e, e * w:(e + 1) * w] = 1.0
        for part in (2, 3, 4):
            s[part * ne + e, (ne + e) * w:(ne + e + 1) * w] = 1.0
    return jnp.asarray(s, BF16)


def _combine_win_kernel(row_ref, base_ref, pos_ref, aff_ref, spread_ref, *refs):
    del row_ref
    ne = pos_ref.shape[2]
    y_refs = refs[:ne]
    x1_ref, gt_ref, pn_ref, o_ref = refs[ne:]
    b = pl.program_id(0)
    k = pl.program_id(1)
    nt = pl.num_programs(1)
    w = y_refs[0].shape[0]
    pos = pos_ref[0]
    aff = aff_ref[0]
    p_hi = jnp.floor(pos * (1.0 / SLOT_BLK))
    a_parts = _split_bf16(aff)
    a_parts.append((aff - a_parts[0].astype(F32) - a_parts[1].astype(F32)).astype(BF16))
    lhs = jnp.concatenate([p_hi.astype(BF16), (pos - SLOT_BLK * p_hi).astype(BF16)] + a_parts, axis=1)
    spread = jnp.dot(lhs, spread_ref[...], preferred_element_type=F32)
    lane = lax.broadcasted_iota(jnp.int32, (1, ne * w), 1)
    seg = lane // w
    seg_base = jnp.zeros((1, ne * w), jnp.int32)
    for e in range(ne):
        seg_base = jnp.where(seg == e, base_ref[(b * ne + e) * nt + k], seg_base)
    slot = (seg_base + lane % w + 1).astype(F32)
    onehot = jnp.where(spread[:, :ne * w] == slot, spread[:, ne * w:], 0.0).astype(BF16)
    y2 = None
    gw = COMBINE_GROUP * w
    for g in range(ne // COMBINE_GROUP):
        ywin = jnp.concatenate([r[...] for r in y_refs[COMBINE_GROUP * g:COMBINE_GROUP * (g + 1)]], axis=0)
        part = jnp.dot(onehot[:, g * gw:(g + 1) * gw], ywin, preferred_element_type=F32)
        y2 = part if y2 is None else y2 + part
    r = y2 * lax.rsqrt(jnp.mean(y2 * y2, axis=-1, keepdims=True) + EPS) * pn_ref[...]
    o_ref[0] = x1_ref[0] + gt_ref[0] * r


def _combine_win(row, base, pos_t, aff, y, x1, gt2, pn2, w):
    b, t, d = x1.shape
    ne = aff.shape[-1]
    nt = t // TOK_TILE

    def y_spec(e):
        return pl.BlockSpec((pl.Element(w), pl.Element(d)),
                            lambda i, k, row_ref, base_ref: (row_ref[(i * ne + e) * nt + k] * SLOT_BLK, 0))

    tile = lambda i, k, row_ref, base_ref: (i, k, 0)
    const = lambda i, k, row_ref, base_ref: (0, 0)
    spread = _spread_matrix(ne, w)
    return pl.pallas_call(
        _combine_win_kernel,
        grid_spec=pltpu.PrefetchScalarGridSpec(
            num_scalar_prefetch=2,
            grid=(b, nt),
            in_specs=[pl.BlockSpec((1, TOK_TILE, ne), tile),
                      pl.BlockSpec((1, TOK_TILE, ne), tile),
                      pl.BlockSpec(spread.shape, const)]
                     + [y_spec(e) for e in range(ne)]
                     + [pl.BlockSpec((1, TOK_TILE, d), tile),
                        pl.BlockSpec((1, 1, d), lambda i, k, row_ref, base_ref: (i, 0, 0)),
                        pl.BlockSpec((1, d), const)],
            out_specs=pl.BlockSpec((1, TOK_TILE, d), tile)),
        out_shape=jax.ShapeDtypeStruct((b, t, d), F32),
        compiler_params=_params("parallel", "parallel"),
        name="combine_win",
    )(row, base, pos_t, aff, spread, *([y.reshape(-1, d)] * ne), x1, gt2, pn2.reshape(1, d))


def _routing_tables(fill, cap):
    b, ne, t = fill.shape
    w = min(2 * SLOT_BLK, cap)
    ends = fill[:, :, TOK_TILE - 1::TOK_TILE].astype(jnp.int32)
    starts = jnp.concatenate([jnp.zeros((b, ne, 1), jnp.int32), ends[:, :, :-1]], axis=-1)
    sparse = jnp.all(ends - starts <= SLOT_BLK)
    base = jnp.minimum(starts // SLOT_BLK * SLOT_BLK, cap - w)
    group = (jnp.arange(ne)[None, :, None] * b + jnp.arange(b)[:, None, None]) * cap
    row_blk = (group + base) // SLOT_BLK
    return sparse, (starts // 8 * 8).reshape(-1), row_blk.reshape(-1), base.reshape(-1), w


def _rope_tables(t):
    pos = np.arange(t)
    r = (pos // GRID_W).astype(np.float32)
    cl = (pos % GRID_W).astype(np.float32)
    quarter = HEAD_DIM // 4
    inv = (np.float32(ROPE_THETA) ** (-np.arange(quarter, dtype=np.float32) / np.float32(quarter))).astype(np.float32)
    ang_r = r[:, None] * inv
    ang_c = cl[:, None] * inv
    ang = np.concatenate([ang_r, ang_r, ang_c, ang_c], axis=-1).astype(np.float32)
    return jnp.asarray(np.cos(ang), F32), jnp.asarray(np.sin(ang), F32)


def _kv_heads(h, w_in, q_width, k_norm, rope):
    plan = (((0, ATT_KV_W, "norm_rope", 0, 1.0), (ATT_KV_W, 2 * ATT_KV_W, "plain", 1, 1.0)),
            ((0, RET_QK_W, "rope", 2, RET_QK_DIM ** -0.5),),
            ((0, 512, "plain", 3, 1.0),),
            ((0, 512, "plain", 3, 1.0),))
    outs = ((ATT_KV_W, ATT_KV_W, lambda j: 0), (ATT_KV_W, ATT_KV_W, lambda j: 0),
            (RET_QK_W, RET_QK_W, lambda j: 0), (RET_V_W, 512, lambda j: jnp.clip(j - 2, 0, 1)))
    return _proj_heads(h, w_in, q_width, plan, outs, k_norm, rope)


def _q_heads(h, w_in, q_norm, rope):
    plan = (((0, 512, "norm_rope", 0, 1.0),), ((0, 512, "norm_rope", 0, 1.0),), ((0, RET_QK_W, "rope", 1, 1.0),))
    outs = ((ATT_W, 512, lambda j: jnp.minimum(j, 1)), (RET_QK_W, RET_QK_W, lambda j: 0))
    return _proj_heads(h, w_in, 0, plan, outs, q_norm, rope)


def kernel(x, c, ctx, c_ctx, w_mod, b_mod, pre_norm1, post_norm1, pre_norm2, post_norm2, w_in, q_norm,
           k_norm, ret_decay, ret_gn, w_o_att, w_o_ret, w_out, w_router, w_gate, w_up, w_down):
    b, t, d = x.shape
    n_ctx = ctx.shape[1]
    depth = w_mod.shape[0]
    q_width = ATT_W + RET_QK_W + RET_V_W + 2 * d
    cap = EC_FACTOR * t // N_EXPERTS
    rope_lat = _rope_tables(t)
    rope_ctx = (jnp.ones((b * n_ctx, HEAD_DIM), F32), jnp.zeros((b * n_ctx, HEAD_DIM), F32))
    cs = jnp.zeros((8, d), F32).at[:b].set(c).at[b].set(c_ctx)
    xc = ctx
    for layer in range(depth):
        assert layer == depth - 1, "context-stream update between layers is not implemented"
        mod = _mod_vectors(cs, w_mod[layer], b_mod[layer])
        sh1, sc1, gt1, sh2, sc2, gt2 = [m[:b, None, :] for m in jnp.split(mod, 6, axis=-1)]
        csh1, csc1 = [jnp.broadcast_to(m[b][None, None, :], (b, 1, d)) for m in jnp.split(mod, 6, axis=-1)[:2]]
        dec = -jax.nn.softplus(ret_decay[layer].astype(F32))
        wl = w_in[layer]

        hc = _prenorm(xc, pre_norm1[layer], csh1, csc1).reshape(b * n_ctx, d)
        kc_a, vc_a, kc_r, vc_r = _kv_heads(hc, wl, q_width, k_norm[layer], rope_ctx)
        s0f, s0b = _ctx_states(dec, kc_r.reshape(b, n_ctx, RET_QK_W), vc_r.reshape(b, n_ctx, RET_V_W))

        h = _prenorm(x, pre_norm1[layer], sh1, sc1).reshape(b * t, d)
        q_a, q_r = _q_heads(h, wl, q_norm[layer], rope_lat)
        zg = _proj(h, wl, ATT_W + RET_QK_W, RET_V_W + 2 * d, out_dtype=F32, tn=512)
        k_a, v_a, k_r, v_r = _kv_heads(h, wl, q_width, k_norm[layer], rope_lat)

        keys = jnp.concatenate([kc_a.reshape(b, n_ctx, ATT_KV_W), k_a.reshape(b, t, ATT_KV_W)], axis=1)
        vals = jnp.concatenate([vc_a.reshape(b, n_ctx, ATT_KV_W), v_a.reshape(b, t, ATT_KV_W)], axis=1)
        kt = keys.reshape(b, n_ctx + t, ATT_KV_HEADS, HEAD_DIM).transpose(0, 2, 3, 1)
        o_att = _attention(q_a.reshape(b, t, ATT_W), kt, vals)

        o_f, o_b = _retention(dec, q_r.reshape(b, t, RET_QK_W), k_r.reshape(b, t, RET_QK_W),
                              v_r.reshape(b, t, RET_V_W), s0f, s0b)

        y = _merge1(o_att.reshape(b * t, ATT_W), o_f.reshape(b * t, RET_V_W), o_b.reshape(b * t, RET_V_W),
                    zg, ret_gn[layer], w_o_att[layer].astype(BF16), w_o_ret[layer].astype(BF16))
        x1, h2, aff = _merge2(y, w_out[layer].astype(BF16), x, gt1, post_norm1[layer], pre_norm2[layer],
                              sh2, sc2, w_router[layer])

        pos, fill = _topk_slots(aff.transpose(0, 2, 1), cap)
        pos_t = pos.transpose(0, 2, 1)
        sparse, w0, row, base, win = _routing_tables(fill, cap)
        xg = lax.cond(sparse,
                      lambda: _gather_win(w0, pos, h2, cap),
                      lambda: _gather(pos, h2, cap))
        yg = _ffn(xg, w_gate[layer], w_up[layer], w_down[layer])
        x = lax.cond(sparse,
                     lambda: _combine_win(row, base, pos_t, aff, yg, x1, gt2, post_norm2[layer], win),
                     lambda: _combine(pos_t, aff, yg, x1, gt2, post_norm2[layer], cap))
    return x
```

```python
import functools

import jax
import jax.numpy as jnp
import numpy as np
from jax import lax
from jax.experimental import pallas as pl
from jax.experimental.pallas import tpu as pltpu

F32 = jnp.float32
BF16 = jnp.bfloat16

GRID_W = 64
HEAD_DIM = 128
ATT_HEADS = 8
ATT_KV_HEADS = 2
ATT_GROUP = ATT_HEADS // ATT_KV_HEADS
RET_HEADS = 4
RET_QK_DIM = 128
RET_V_DIM = 256
ROPE_THETA = 10000.0
N_EXPERTS = 16
EC_FACTOR = 2
EPS = 1e-6

ATT_W = ATT_HEADS * HEAD_DIM
ATT_KV_W = ATT_KV_HEADS * HEAD_DIM
RET_QK_W = RET_HEADS * RET_QK_DIM
RET_V_W = RET_HEADS * RET_V_DIM

LANES = 128
VMEM_LIMIT = 56 * 1024 * 1024
RET_CHUNK = 256
MIN_NORMAL_F32_BITS = 0x00800000
LOG2_E = 1.4426950408889634
TOK_TILE = 256
SLOT_BLK = 64
GATHER_WIN = SLOT_BLK + 16
COMBINE_GROUP = 4
MOD_ROWS = 16


def _blk(pref, n):
    return pref if n % pref == 0 else n


def _params(*sem):
    return pltpu.CompilerParams(dimension_semantics=sem, vmem_limit_bytes=VMEM_LIMIT)


def _sigmoid(x):
    return 0.5 * jnp.tanh(0.5 * x) + 0.5


def _mod_kernel(s_ref, w_ref, b_ref, o_ref):
    s = s_ref[...]
    s = s * _sigmoid(s)
    rows = s.shape[0]
    s_hi, s_lo = _split_bf16(s)
    w_hi, w_lo = _split_bf16(w_ref[...])
    both = jnp.dot(jnp.concatenate([s_hi, s_lo], axis=0), w_hi, preferred_element_type=F32)
    o_ref[...] = both[:rows] + both[rows:] + jnp.dot(s_hi, w_lo, preferred_element_type=F32) + b_ref[...]


def _mod_vectors(cs, w_mod, b_mod):
    rows, d = cs.shape
    n = w_mod.shape[1]
    tn = _blk(2048, n)
    return pl.pallas_call(
        _mod_kernel,
        grid=(n // tn,),
        in_specs=[pl.BlockSpec((rows, d), lambda j: (0, 0)),
                  pl.BlockSpec((d, tn), lambda j: (0, j)),
                  pl.BlockSpec((1, tn), lambda j: (0, j))],
        out_specs=pl.BlockSpec((rows, tn), lambda j: (0, j)),
        out_shape=jax.ShapeDtypeStruct((rows, n), F32),
        compiler_params=_params("parallel"),
        name="mod",
    )(cs, w_mod, b_mod.reshape(1, n))


def _prenorm_kernel(x_ref, g_ref, sh_ref, sc_ref, o_ref):
    x = x_ref[0]
    y = x * lax.rsqrt(jnp.mean(x * x, axis=-1, keepdims=True) + EPS) * g_ref[...]
    o_ref[0] = (y * (1.0 + sc_ref[0]) + sh_ref[0]).astype(o_ref.dtype)


def _prenorm(x, gain, shift, scale):
    b, t, d = x.shape
    tm = _blk(1024, t)
    return pl.pallas_call(
        _prenorm_kernel,
        grid=(b, t // tm),
        in_specs=[pl.BlockSpec((1, tm, d), lambda i, j: (i, j, 0)),
                  pl.BlockSpec((1, d), lambda i, j: (0, 0)),
                  pl.BlockSpec((1, 1, d), lambda i, j: (i, 0, 0)),
                  pl.BlockSpec((1, 1, d), lambda i, j: (i, 0, 0))],
        out_specs=pl.BlockSpec((1, tm, d), lambda i, j: (i, j, 0)),
        out_shape=jax.ShapeDtypeStruct((b, t, d), BF16),
        compiler_params=_params("parallel", "parallel"),
        name="prenorm",
    )(x, gain.reshape(1, d), shift, scale)


def _split_bf16(v):
    hi = v.astype(BF16)
    return [hi, (v - hi.astype(F32)).astype(BF16)]


def _rotate_half_matrix():
    quarter = HEAD_DIM // 4
    r = np.zeros((HEAD_DIM, HEAD_DIM), np.float32)
    for i in range(HEAD_DIM):
        if i % (2 * quarter) < quarter:
            r[i + quarter, i] = -1.0
        else:
            r[i - quarter, i] = 1.0
    return r


def _head_mix_matrix(mode):
    rot = _rotate_half_matrix()
    if mode == "rope":
        return jnp.asarray(np.concatenate([rot, rot], axis=0), BF16)
    ones = np.ones((HEAD_DIM, HEAD_DIM), np.float32)
    zero = np.zeros((HEAD_DIM, HEAD_DIM), np.float32)
    return jnp.asarray(np.block([[ones, zero], [ones, zero], [zero, rot], [zero, rot]]), BF16)


def _proj_kernel(h_ref, w_ref, o_ref):
    o_ref[...] = jnp.dot(h_ref[...], w_ref[...].astype(BF16), preferred_element_type=F32).astype(o_ref.dtype)


def _proj(h, w_in, col0, width, *, out_dtype, tn):
    r, d = h.shape
    tm = _blk(2048, r)
    assert col0 % tn == 0 and width % tn == 0
    c0 = col0 // tn
    return pl.pallas_call(
        _proj_kernel,
        grid=(r // tm, width // tn),
        in_specs=[pl.BlockSpec((tm, d), lambda i, j: (i, 0)),
                  pl.BlockSpec((d, tn), lambda i, j: (0, c0 + j))],
        out_specs=pl.BlockSpec((tm, tn), lambda i, j: (i, j)),
        out_shape=jax.ShapeDtypeStruct((r, width), out_dtype),
        compiler_params=_params("parallel", "parallel"),
        name="proj_plain",
    )(h, w_in)


def _rope_head(xh, mode, gain, mix, cos, sin, pre_scale):
    if mode == "norm_rope":
        xg = xh * gain
        mixed = jnp.dot(jnp.concatenate(_split_bf16(xh * xh) + _split_bf16(xg), axis=1), mix,
                        preferred_element_type=F32)
        inv = lax.rsqrt(mixed[:, :HEAD_DIM] * (1.0 / HEAD_DIM) + EPS)
        return (xg * cos + mixed[:, HEAD_DIM:] * sin) * inv
    xs = xh * pre_scale if pre_scale != 1.0 else xh
    rot = jnp.dot(jnp.concatenate(_split_bf16(xs), axis=1), mix, preferred_element_type=F32)
    return xs * cos + rot * sin


def _proj_heads_kernel(h_ref, w_ref, gain_ref, mixn_ref, mixr_ref, cos_ref, sin_ref, *o_refs, plan):
    acc = jnp.dot(h_ref[...], w_ref[...].astype(BF16), preferred_element_type=F32)
    for jj, pieces in enumerate(plan):
        @pl.when(pl.program_id(1) == jj)
        def _(pieces=pieces):
            for lo, hi, mode, out, pre_scale in pieces:
                o_ref = o_refs[out]
                if mode == "plain":
                    o_ref[...] = acc[:, lo:hi].astype(o_ref.dtype)
                    continue
                mix = mixn_ref[...] if mode == "norm_rope" else mixr_ref[...]
                for hh in range((hi - lo) // HEAD_DIM):
                    xh = acc[:, lo + hh * HEAD_DIM:lo + (hh + 1) * HEAD_DIM]
                    res = _rope_head(xh, mode, gain_ref[...], mix, cos_ref[...], sin_ref[...], pre_scale)
                    o_ref[:, hh * HEAD_DIM:(hh + 1) * HEAD_DIM] = res.astype(o_ref.dtype)


def _proj_heads(h, w_in, col0, plan, outs, gain, rope):
    r, d = h.shape
    cos, sin = rope
    tm = min(_blk(2048, r), cos.shape[0])
    tn = 512
    assert col0 % tn == 0
    c0 = col0 // tn
    period = cos.shape[0] // tm
    assert cos.shape == (period * tm, HEAD_DIM) and (r // tm) % period == 0
    mixn, mixr = _head_mix_matrix("norm_rope"), _head_mix_matrix("rope")
    const = lambda i, j: (0, 0)
    rows = lambda i, j: (i, 0)
    table = lambda i, j: (i % period, 0)
    return pl.pallas_call(
        functools.partial(_proj_heads_kernel, plan=plan),
        grid=(r // tm, len(plan)),
        in_specs=[pl.BlockSpec((tm, d), rows),
                  pl.BlockSpec((d, tn), lambda i, j: (0, c0 + j)),
                  pl.BlockSpec((1, HEAD_DIM), const),
                  pl.BlockSpec(mixn.shape, const),
                  pl.BlockSpec(mixr.shape, const),
                  pl.BlockSpec((tm, HEAD_DIM), table),
                  pl.BlockSpec((tm, HEAD_DIM), table)],
        out_specs=tuple(pl.BlockSpec((tm, bw), functools.partial(lambda i, j, f: (i, f(j)), f=f))
                        for _, bw, f in outs),
        out_shape=tuple(jax.ShapeDtypeStruct((r, width), BF16) for width, _, _ in outs),
        compiler_params=_params("parallel", "arbitrary"),
        name="proj_heads",
    )(h, w_in, gain.reshape(1, HEAD_DIM), mixn, mixr, cos, sin)


def _ctx_state_kernel(dec_ref, k_ref, v_ref, sf_ref, sb_ref):
    hh = pl.program_id(1)
    lf = dec_ref[0, hh]
    lb = dec_ref[1, hh]
    k = k_ref[0].astype(F32)
    v = v_ref[0]
    n = k.shape[0]
    pos = lax.broadcasted_iota(jnp.int32, k.shape, 0).astype(F32)
    kf = (k * jnp.exp((n - 1.0 - pos) * lf)).T.astype(BF16)
    kb = (k * jnp.exp(pos * lb)).T.astype(BF16)
    sf_ref[0, 0] = jnp.dot(kf, v, preferred_element_type=F32)
    sb_ref[0, 0] = jnp.dot(kb, v, preferred_element_type=F32)


def _ctx_states(dec, k_r, v_r):
    b, n, _ = k_r.shape
    spec_s = pl.BlockSpec((1, 1, RET_QK_DIM, RET_V_DIM), lambda i, j: (i, j, 0, 0))
    shape_s = jax.ShapeDtypeStruct((b, RET_HEADS, RET_QK_DIM, RET_V_DIM), F32)
    return pl.pallas_call(
        _ctx_state_kernel,
        grid=(b, RET_HEADS),
        in_specs=[pl.BlockSpec(memory_space=pltpu.SMEM),
                  pl.BlockSpec((1, n, RET_QK_DIM), lambda i, j: (i, 0, j)),
                  pl.BlockSpec((1, n, RET_V_DIM), lambda i, j: (i, 0, j))],
        out_specs=(spec_s, spec_s),
        out_shape=(shape_s, shape_s),
        compiler_params=_params("parallel", "parallel"),
        name="ctx_state",
    )(dec, k_r, v_r)


def _attn_kernel(q_ref, kt_ref, v_ref, o_ref, *, scale):
    for hk in range(ATT_KV_HEADS):
        kt = kt_ref[0, hk]
        v = v_ref[0, :, hk * HEAD_DIM:(hk + 1) * HEAD_DIM]
        for g in range(ATT_GROUP):
            cols = slice((hk * ATT_GROUP + g) * HEAD_DIM, (hk * ATT_GROUP + g + 1) * HEAD_DIM)
            s = jnp.dot(q_ref[0, :, cols], kt, preferred_element_type=F32)
            m = jnp.max(s, axis=-1, keepdims=True)
            p = jnp.exp2((s - m) * (scale * LOG2_E))
            l = jnp.sum(p, axis=-1, keepdims=True)
            o = jnp.dot(p.astype(BF16), v, preferred_element_type=F32)
            o_ref[0, :, cols] = (o / l).astype(o_ref.dtype)


def _attention(q, kt, v):
    b, t, _ = q.shape
    s = kt.shape[-1]
    tq = _blk(256, t)
    return pl.pallas_call(
        functools.partial(_attn_kernel, scale=HEAD_DIM ** -0.5),
        grid=(b, t // tq),
        in_specs=[pl.BlockSpec((1, tq, ATT_W), lambda i, n: (i, n, 0)),
                  pl.BlockSpec((1, ATT_KV_HEADS, HEAD_DIM, s), lambda i, n: (i, 0, 0, 0)),
                  pl.BlockSpec((1, s, ATT_KV_W), lambda i, n: (i, 0, 0))],
        out_specs=pl.BlockSpec((1, tq, ATT_W), lambda i, n: (i, n, 0)),
        out_shape=jax.ShapeDtypeStruct((b, t, ATT_W), BF16),
        compiler_params=_params("parallel", "parallel"),
        name="attention",
    )(q, kt, v)


def _ret_kernel(dec_ref, qf_ref, kf_ref, vf_ref, qb_ref, kb_ref, vb_ref, s0f_ref, s0b_ref,
                of_ref, ob_ref, sf_ref, sb_ref, mf_ref, mb_ref):
    n = pl.program_id(1)
    c = qf_ref.shape[1]

    @pl.when(n == 0)
    def _():
        sf_ref[...] = s0f_ref[0]
        sb_ref[...] = s0b_ref[0]
        ri = lax.broadcasted_iota(jnp.int32, (c, c), 0)
        ci = lax.broadcasted_iota(jnp.int32, (c, c), 1)
        d = (ri - ci).astype(F32)
        for hh in range(RET_HEADS):
            mf_ref[hh] = jnp.where(d >= 0, jnp.exp(jnp.maximum(d, 0.0) * dec_ref[0, hh]), 0.0)
            mb_ref[hh] = jnp.where(d <= 0, jnp.exp(jnp.maximum(-d, 0.0) * dec_ref[1, hh]), 0.0)

    nt = (((1,), (1,)), ((), ()))
    row_v = lax.broadcasted_iota(jnp.int32, (c, RET_V_DIM), 0).astype(F32)
    row_k = lax.broadcasted_iota(jnp.int32, (c, RET_QK_DIM), 0).astype(F32)

    def sweep(q_ref, k_ref, v_ref, o_ref, s_ref, m_ref, hh, lg, wq_age, wk_age):
        qk = slice(hh * RET_QK_DIM, (hh + 1) * RET_QK_DIM)
        vv = slice(hh * RET_V_DIM, (hh + 1) * RET_V_DIM)
        q = q_ref[0, :, qk]
        k = k_ref[0, :, qk]
        v = v_ref[0, :, vv]
        s = lax.dot_general(q, k, nt, preferred_element_type=F32) * m_ref[hh]
        o_in = jnp.dot(s.astype(BF16), v, preferred_element_type=F32)
        state = s_ref[hh]
        o_x = jnp.dot(q, state.astype(BF16), preferred_element_type=F32) * jnp.exp(wq_age * lg)
        o_ref[0, :, vv] = o_in + o_x
        kw = (k.astype(F32) * jnp.exp(wk_age * lg)).T.astype(BF16)
        g_chunk = jnp.exp(jnp.full((1, RET_V_DIM), c * lg, F32))
        s_ref[hh] = g_chunk * state + jnp.dot(kw, v, preferred_element_type=F32)

    for hh in range(RET_HEADS):
        sweep(qf_ref, kf_ref, vf_ref, of_ref, sf_ref, mf_ref, hh, dec_ref[0, hh], row_v + 1.0, c - 1.0 - row_k)
        sweep(qb_ref, kb_ref, vb_ref, ob_ref, sb_ref, mb_ref, hh, dec_ref[1, hh], c - row_v, row_k)


def _retention(dec, q, k, v, s0f, s0b):
    b, t, _ = q.shape
    c = _blk(RET_CHUNK, t)
    nc = t // c
    fwd = lambda i, n: (i, n, 0)
    bwd = lambda i, n: (i, nc - 1 - n, 0)
    spec_s = pl.BlockSpec((1, RET_HEADS, RET_QK_DIM, RET_V_DIM), lambda i, n: (i, 0, 0, 0))
    out_shape = jax.ShapeDtypeStruct((b, t, RET_V_W), F32)
    return pl.pallas_call(
        _ret_kernel,
        grid=(b, nc),
        in_specs=[pl.BlockSpec(memory_space=pltpu.SMEM),
                  pl.BlockSpec((1, c, RET_QK_W), fwd),
                  pl.BlockSpec((1, c, RET_QK_W), fwd),
                  pl.BlockSpec((1, c, RET_V_W), fwd),
                  pl.BlockSpec((1, c, RET_QK_W), bwd),
                  pl.BlockSpec((1, c, RET_QK_W), bwd),
                  pl.BlockSpec((1, c, RET_V_W), bwd),
                  spec_s, spec_s],
        out_specs=(pl.BlockSpec((1, c, RET_V_W), fwd), pl.BlockSpec((1, c, RET_V_W), bwd)),
        out_shape=(out_shape, out_shape),
        scratch_shapes=[pltpu.VMEM((RET_HEADS, RET_QK_DIM, RET_V_DIM), F32),
                        pltpu.VMEM((RET_HEADS, RET_QK_DIM, RET_V_DIM), F32),
                        pltpu.VMEM((RET_HEADS, c, c), F32),
                        pltpu.VMEM((RET_HEADS, c, c), F32)],
        compiler_params=_params("parallel", "arbitrary"),
        name="retention",
    )(dec, q, k, v, q, k, v, s0f, s0b)


def _merge1_kernel(oa_ref, of_ref, ob_ref, g_ref, *refs, nslab):
    ga_refs, gr_refs = refs[:nslab], refs[nslab:2 * nslab]
    gn_ref, wa_ref, wr_ref, y_ref = refs[2 * nslab:]
    o = of_ref[...] + ob_ref[...]
    heads = []
    for hh in range(RET_HEADS):
        sl = slice(hh * RET_V_DIM, (hh + 1) * RET_V_DIM)
        oh = o[:, sl]
        dlt = oh - jnp.mean(oh, axis=-1, keepdims=True)
        yh = dlt * lax.rsqrt(jnp.mean(dlt * dlt, axis=-1, keepdims=True) + EPS)
        g = g_ref[:, sl]
        heads.append((yh * gn_ref[:, sl] * (g * _sigmoid(g))).astype(BF16))
    o_ret = jnp.concatenate(heads, axis=1)
    tn = ga_refs[0].shape[1]
    for c in range(nslab):
        cols = slice(c * tn, (c + 1) * tn)
        ya = jnp.dot(oa_ref[...], wa_ref[:, cols], preferred_element_type=F32)
        yr = jnp.dot(o_ret, wr_ref[:, cols], preferred_element_type=F32)
        y_ref[:, cols] = (_sigmoid(ga_refs[c][...]) * ya + _sigmoid(gr_refs[c][...]) * yr).astype(y_ref.dtype)


def _merge1(o_att, o_f, o_b, zg, gn, wa, wr):
    r = o_att.shape[0]
    d = wa.shape[1]
    tm = _blk(256, r)
    tn = _blk(1024, d)
    nslab = d // tn
    ga0 = RET_V_W // tn
    gr0 = (RET_V_W + d) // tn
    row = lambda i: (i, 0)
    const = lambda i: (0, 0)
    gate_specs = [pl.BlockSpec((tm, tn), functools.partial(lambda i, c: (i, c), c=c0 + c))
                  for c0 in (ga0, gr0) for c in range(nslab)]
    return pl.pallas_call(
        functools.partial(_merge1_kernel, nslab=nslab),
        grid=(r // tm,),
        in_specs=[pl.BlockSpec((tm, ATT_W), row),
                  pl.BlockSpec((tm, RET_V_W), row),
                  pl.BlockSpec((tm, RET_V_W), row),
                  pl.BlockSpec((tm, RET_V_W), row)]
                 + gate_specs
                 + [pl.BlockSpec((1, RET_V_W), const),
                    pl.BlockSpec((ATT_W, d), const),
                    pl.BlockSpec((RET_V_W, d), const)],
        out_specs=pl.BlockSpec((tm, d), row),
        out_shape=jax.ShapeDtypeStruct((r, d), BF16),
        compiler_params=_params("parallel"),
        name="merge1",
    )(o_att, o_f, o_b, zg, *([zg] * (2 * nslab)), gn.reshape(1, RET_V_W), wa, wr)


def _merge2_kernel(y_ref, w_ref, x_ref, gt_ref, pn1_ref, pn2_ref, sh_ref, sc_ref, wr_ref,
                   x1_ref, h2_ref, aff_ref):
    yy = jnp.dot(y_ref[0], w_ref[...], preferred_element_type=F32)
    r = yy * lax.rsqrt(jnp.mean(yy * yy, axis=-1, keepdims=True) + EPS) * pn1_ref[...]
    x1 = x_ref[0] + gt_ref[0] * r
    x1_ref[0] = x1
    h2 = x1 * lax.rsqrt(jnp.mean(x1 * x1, axis=-1, keepdims=True) + EPS) * pn2_ref[...]
    h2 = h2 * (1.0 + sc_ref[0]) + sh_ref[0]
    h_hi = h2.astype(BF16)
    h2_ref[0] = h_hi
    h_lo = (h2 - h_hi.astype(F32)).astype(BF16)
    wr = wr_ref[...]
    w_hi = wr.astype(BF16)
    w_lo = (wr - w_hi.astype(F32)).astype(BF16)
    ne = wr.shape[1]
    r_hi = jnp.dot(h_hi, jnp.concatenate([w_hi, w_lo], axis=1), preferred_element_type=F32)
    logits = r_hi[:, :ne] + r_hi[:, ne:] + jnp.dot(h_lo, w_hi, preferred_element_type=F32)
    e = jnp.exp(logits - jnp.max(logits, axis=-1, keepdims=True))
    aff_ref[0] = e / jnp.sum(e, axis=-1, keepdims=True)


def _merge2(y, w_out, x, gt1, pn1, pn2, sh2, sc2, w_router):
    b, t, d = x.shape
    tm = _blk(512, t)
    ne = w_router.shape[1]
    tile = lambda i, j: (i, j, 0)
    vec = lambda i, j: (0, 0)
    per_b = lambda i, j: (i, 0, 0)
    return pl.pallas_call(
        _merge2_kernel,
        grid=(b, t // tm),
        in_specs=[pl.BlockSpec((1, tm, d), tile),
                  pl.BlockSpec((d, d), vec),
                  pl.BlockSpec((1, tm, d), tile),
                  pl.BlockSpec((1, 1, d), per_b),
                  pl.BlockSpec((1, d), vec),
                  pl.BlockSpec((1, d), vec),
                  pl.BlockSpec((1, 1, d), per_b),
                  pl.BlockSpec((1, 1, d), per_b),
                  pl.BlockSpec((d, ne), vec)],
        out_specs=(pl.BlockSpec((1, tm, d), tile),
                   pl.BlockSpec((1, tm, d), tile),
                   pl.BlockSpec((1, tm, ne), tile)),
        out_shape=(jax.ShapeDtypeStruct((b, t, d), F32),
                   jax.ShapeDtypeStruct((b, t, d), BF16),
                   jax.ShapeDtypeStruct((b, t, ne), F32)),
        compiler_params=_params("parallel", "parallel"),
        name="merge2",
    )(y.reshape(b, t, d), w_out, x, gt1, pn1.reshape(1, d), pn2.reshape(1, d), sh2, sc2, w_router)


def _lane_cumsum(x, tri):
    off = jnp.zeros((x.shape[0], 1), F32)
    parts = []
    for cidx in range(x.shape[1] // LANES):
        xc = x[:, cidx * LANES:(cidx + 1) * LANES].astype(BF16)
        cs = jnp.dot(xc, tri, preferred_element_type=F32) + off
        parts.append(cs)
        off = cs[:, LANES - 1:LANES]
    return jnp.concatenate(parts, axis=1)


def _topk_kernel(a_ref, pos_ref, fill_ref, *, cap):
    a = a_ref[0]
    bits = jnp.zeros((a.shape[0], 1), jnp.int32)
    for bit in range(30, -1, -1):
        cand = bits | (1 << bit)
        cnt = jnp.sum(jnp.where(a >= lax.bitcast_convert_type(cand, F32), 1.0, 0.0), axis=1, keepdims=True)
        bits = jnp.where(cnt >= cap, cand, bits)
    bits = jnp.where(bits < MIN_NORMAL_F32_BITS, 0, bits)
    thr = lax.bitcast_convert_type(bits, F32)
    ri = lax.broadcasted_iota(jnp.int32, (LANES, LANES), 0)
    ci = lax.broadcasted_iota(jnp.int32, (LANES, LANES), 1)
    tri = jnp.where(ri <= ci, 1.0, 0.0).astype(BF16)
    gt = a > thr
    eq = a == thr
    n_gt = jnp.sum(jnp.where(gt, 1.0, 0.0), axis=1, keepdims=True)
    eq_rank = _lane_cumsum(jnp.where(eq, 1.0, 0.0), tri)
    sel = jnp.where(gt, 1.0, jnp.where(eq, jnp.where(eq_rank <= cap - n_gt, 1.0, 0.0), 0.0))
    filled = _lane_cumsum(sel, tri)
    pos_ref[0] = filled * sel
    fill_ref[0] = filled


def _topk_slots(aff_t, cap):
    b, ne, t = aff_t.shape
    spec = pl.BlockSpec((1, ne, t), lambda i: (i, 0, 0))
    shape = jax.ShapeDtypeStruct((b, ne, t), F32)
    return pl.pallas_call(
        functools.partial(_topk_kernel, cap=cap),
        grid=(b,),
        in_specs=[spec],
        out_specs=(spec, spec),
        out_shape=(shape, shape),
        compiler_params=_params("parallel"),
        name="topk",
    )(aff_t)


def _gather_kernel(pos_ref, h_ref, o_ref):
    cap = o_ref.shape[1]
    t = h_ref.shape[1]
    tc = _blk(1024, t)
    slot = lax.broadcasted_iota(jnp.int32, (cap, tc), 0).astype(F32) + 1.0
    acc = None
    for cidx in range(t // tc):
        row = pos_ref[0, 0, :, cidx * tc:(cidx + 1) * tc]
        onehot = jnp.where(row == slot, 1.0, 0.0).astype(BF16)
        part = jnp.dot(onehot, h_ref[0, cidx * tc:(cidx + 1) * tc, :], preferred_element_type=F32)
        acc = part if acc is None else acc + part
    o_ref[0] = acc.astype(o_ref.dtype)


def _gather(pos, h2, cap):
    b, ne, t = pos.shape
    d = h2.shape[-1]
    dn = _blk(1024, d)
    return pl.pallas_call(
        _gather_kernel,
        grid=(b, d // dn, ne),
        in_specs=[pl.BlockSpec((1, 1, 1, t), lambda i, n, e: (i, e, 0, 0)),
                  pl.BlockSpec((1, t, dn), lambda i, n, e: (i, 0, n))],
        out_specs=pl.BlockSpec((1, cap, dn), lambda i, n, e: (e, i, n)),
        out_shape=jax.ShapeDtypeStruct((ne, b * cap, d), BF16),
        compiler_params=_params("parallel", "parallel", "parallel"),
        name="gather",
    )(pos.reshape(b, ne, 1, t), h2)


def _gather_win_kernel(w0_ref, pos_ref, h_ref, o_ref, acc_ref, lhs_ref):
    b = pl.program_id(0)
    ne, cap = o_ref.shape[0], o_ref.shape[1]
    nt = h_ref.shape[1] // TOK_TILE
    slot1 = lax.broadcasted_iota(jnp.int32, (GATHER_WIN, TOK_TILE), 0) + 1

    @pl.when(pl.program_id(1) == 0)
    def _():
        def build(k, carry):
            for e in range(ne):
                w0 = w0_ref[(b * ne + e) * nt + k]
                row = pos_ref[0, pl.ds(e * nt + k, 1), :]
                onehot = jnp.where(row == (slot1 + w0).astype(F32), 1.0, 0.0)
                lhs_ref[k, e * GATHER_WIN:(e + 1) * GATHER_WIN, :] = onehot.astype(BF16)
            return carry
        lax.fori_loop(0, nt, build, 0)

    acc_ref[:, :8, :] = jnp.zeros((ne, 8) + acc_ref.shape[2:], F32)

    def tile(k, carry):
        t0 = pl.multiple_of(k * TOK_TILE, TOK_TILE)
        res = jnp.dot(lhs_ref[k], h_ref[0, pl.ds(t0, TOK_TILE), :], preferred_element_type=F32)
        for e in range(ne):
            w0 = pl.multiple_of(w0_ref[(b * ne + e) * nt + k], 8)
            acc_ref[e, pl.ds(w0, 8), :] += res[e * GATHER_WIN:e * GATHER_WIN + 8]
            acc_ref[e, pl.ds(w0 + 8, GATHER_WIN - 8), :] = res[e * GATHER_WIN + 8:(e + 1) * GATHER_WIN]
        return carry
    lax.fori_loop(0, nt, tile, 0, unroll=8)
    o_ref[...] = acc_ref[:, :cap, :].astype(o_ref.dtype)


def _gather_win(w0, pos, h2, cap):
    b, ne, t = pos.shape
    d = h2.shape[-1]
    nt = t // TOK_TILE
    dn = _blk(256, d)
    return pl.pallas_call(
        _gather_win_kernel,
        grid_spec=pltpu.PrefetchScalarGridSpec(
            num_scalar_prefetch=1,
            grid=(b, d // dn),
            in_specs=[pl.BlockSpec((1, ne * nt, TOK_TILE), lambda i, n, w: (i, 0, 0)),
                      pl.BlockSpec((1, t, dn), lambda i, n, w: (i, 0, n))],
            out_specs=pl.BlockSpec((ne, cap, dn), lambda i, n, w: (0, i, n)),
            scratch_shapes=[pltpu.VMEM((ne, cap + GATHER_WIN, dn), F32),
                            pltpu.VMEM((nt, ne * GATHER_WIN, TOK_TILE), BF16)]),
        out_shape=jax.ShapeDtypeStruct((ne, b * cap, d), BF16),
        compiler_params=_params("parallel", "arbitrary"),
        name="gather_win",
    )(w0, pos.reshape(b, ne * nt, TOK_TILE), h2)


def _ffn_kernel(x_ref, wg_ref, wu_ref, wd_ref, o_ref, h_ref, wgu_ref, *, nf):
    j = pl.program_id(2)
    tf = wg_ref.shape[2]

    @pl.when(j < nf)
    def _():
        wgu_ref[:, :tf] = wg_ref[0].astype(BF16)
        wgu_ref[:, tf:] = wu_ref[0].astype(BF16)
        au = jnp.dot(x_ref[0], wgu_ref[...], preferred_element_type=F32)
        a = au[:, :tf]
        h_ref[j] = (a * _sigmoid(a) * au[:, tf:]).astype(BF16)

    @pl.when(j >= nf)
    def _():
        acc = None
        for cidx in range(nf):
            part = jnp.dot(h_ref[cidx], wd_ref[0, cidx * tf:(cidx + 1) * tf, :].astype(BF16),
                           preferred_element_type=F32)
            acc = part if acc is None else acc + part
        o_ref[0] = acc.astype(o_ref.dtype)


def _ffn(xg, w_gate, w_up, w_down):
    ne, rows, d = xg.shape
    ff = w_gate.shape[-1]
    tm = _blk(2048, rows)
    tf = _blk(256, ff)
    tn = _blk(512, d)
    nf = ff // tf
    return pl.pallas_call(
        functools.partial(_ffn_kernel, nf=nf),
        grid=(ne, rows // tm, nf + d // tn),
        in_specs=[pl.BlockSpec((1, tm, d), lambda e, m, j: (e, m, 0)),
                  pl.BlockSpec((1, d, tf), lambda e, m, j: (e, 0, jnp.minimum(j, nf - 1))),
                  pl.BlockSpec((1, d, tf), lambda e, m, j: (e, 0, jnp.minimum(j, nf - 1))),
                  pl.BlockSpec((1, ff, tn), lambda e, m, j: (e, 0, jnp.maximum(j - nf, 0)))],
        out_specs=pl.BlockSpec((1, tm, tn), lambda e, m, j: (e, m, jnp.maximum(j - nf, 0))),
        out_shape=jax.ShapeDtypeStruct((ne, rows, d), BF16),
        scratch_shapes=[pltpu.VMEM((nf, tm, tf), BF16), pltpu.VMEM((d, 2 * tf), BF16)],
        compiler_params=_params("parallel", "parallel", "arbitrary"),
        name="ffn",
    )(xg, w_gate, w_up, w_down)


def _combine_kernel(pos_ref, aff_ref, y_ref, x1_ref, gt_ref, pn_ref, o_ref, acc_ref):
    e = pl.program_id(2)
    tt, ne = pos_ref.shape[1], pos_ref.shape[2]
    cap = y_ref.shape[1]
    pick = lax.broadcasted_iota(jnp.int32, (tt, ne), 1) == e
    pcol = jnp.sum(jnp.where(pick, pos_ref[0], 0.0), axis=1, keepdims=True)
    acol = jnp.sum(jnp.where(pick, aff_ref[0], 0.0), axis=1, keepdims=True)
    slot = lax.broadcasted_iota(jnp.int32, (tt, cap), 1).astype(F32) + 1.0
    onehot = jnp.where(pcol == slot, 1.0, 0.0).astype(BF16)
    z = jnp.dot(onehot, y_ref[0], preferred_element_type=F32) * acol

    @pl.when(e == 0)
    def _():
        acc_ref[...] = z

    @pl.when(e > 0)
    def _():
        acc_ref[...] += z

    @pl.when(e == pl.num_programs(2) - 1)
    def _():
        y2 = acc_ref[...]
        r = y2 * lax.rsqrt(jnp.mean(y2 * y2, axis=-1, keepdims=True) + EPS) * pn_ref[...]
        o_ref[0] = x1_ref[0] + gt_ref[0] * r


def _combine(pos_t, aff, y, x1, gt2, pn2, cap):
    b, t, d = x1.shape
    ne = aff.shape[-1]
    tt = _blk(512, t)
    return pl.pallas_call(
        _combine_kernel,
        grid=(b, t // tt, ne),
        in_specs=[pl.BlockSpec((1, tt, ne), lambda i, j, e: (i, j, 0)),
                  pl.BlockSpec((1, tt, ne), lambda i, j, e: (i, j, 0)),
                  pl.BlockSpec((1, cap, d), lambda i, j, e: (e, i, 0)),
                  pl.BlockSpec((1, tt, d), lambda i, j, e: (i, j, 0)),
                  pl.BlockSpec((1, 1, d), lambda i, j, e: (i, 0, 0)),
                  pl.BlockSpec((1, d), lambda i, j, e: (0, 0))],
        out_specs=pl.BlockSpec((1, tt, d), lambda i, j, e: (i, j, 0)),
        out_shape=jax.ShapeDtypeStruct((b, t, d), F32),
        scratch_shapes=[pltpu.VMEM((tt, d), F32)],
        compiler_params=_params("parallel", "parallel", "arbitrary"),
        name="combine",
    )(pos_t, aff, y, x1, gt2, pn2.reshape(1, d))


def _spread_matrix(ne, w):
    s = np.zeros((5 * ne, 2 * ne * w), np.float32)
    for e in range(ne):
        s[e, e * w:(e + 1) * w] = SLOT_BLK
        s[ne + e, e * w:(e + 1) * w] = 1.0
        for part in (2, 3, 4):
            s[part * ne + e, (ne + e) * w:(ne + e + 1) * w] = 1.0
    return jnp.asarray(s, BF16)


def _combine_win_kernel(row_ref, base_ref, pos_ref, aff_ref, spread_ref, *refs):
    del row_ref
    ne = pos_ref.shape[2]
    y_refs = refs[:ne]
    x1_ref, gt_ref, pn_ref, o_ref = refs[ne:]
    b = pl.program_id(0)
    k = pl.program_id(1)
    nt = pl.num_programs(1)
    w = y_refs[0].shape[0]
    pos = pos_ref[0]
    aff = aff_ref[0]
    p_hi = jnp.floor(pos * (1.0 / SLOT_BLK))
    a_parts = _split_bf16(aff)
    a_parts.append((aff - a_parts[0].astype(F32) - a_parts[1].astype(F32)).astype(BF16))
    lhs = jnp.concatenate([p_hi.astype(BF16), (pos - SLOT_BLK * p_hi).astype(BF16)] + a_parts, axis=1)
    spread = jnp.dot(lhs, spread_ref[...], preferred_element_type=F32)
    lane = lax.broadcasted_iota(jnp.int32, (1, ne * w), 1)
    seg = lane // w
    seg_base = jnp.zeros((1, ne * w), jnp.int32)
    for e in range(ne):
        seg_base = jnp.where(seg == e, base_ref[(b * ne + e) * nt + k], seg_base)
    slot = (seg_base + lane % w + 1).astype(F32)
    onehot = jnp.where(spread[:, :ne * w] == slot, spread[:, ne * w:], 0.0).astype(BF16)
    y2 = None
    gw = COMBINE_GROUP * w
    for g in range(ne // COMBINE_GROUP):
        ywin = jnp.concatenate([r[...] for r in y_refs[COMBINE_GROUP * g:COMBINE_GROUP * (g + 1)]], axis=0)
        part = jnp.dot(onehot[:, g * gw:(g + 1) * gw], ywin, preferred_element_type=F32)
        y2 = part if y2 is None else y2 + part
    r = y2 * lax.rsqrt(jnp.mean(y2 * y2, axis=-1, keepdims=True) + EPS) * pn_ref[...]
    o_ref[0] = x1_ref[0] + gt_ref[0] * r


def _combine_win(row, base, pos_t, aff, y, x1, gt2, pn2, w):
    b, t, d = x1.shape
    ne = aff.shape[-1]
    nt = t // TOK_TILE

    def y_spec(e):
        return pl.BlockSpec((pl.Element(w), pl.Element(d)),
                            lambda i, k, row_ref, base_ref: (row_ref[(i * ne + e) * nt + k] * SLOT_BLK, 0))

    tile = lambda i, k, row_ref, base_ref: (i, k, 0)
    const = lambda i, k, row_ref, base_ref: (0, 0)
    spread = _spread_matrix(ne, w)
    return pl.pallas_call(
        _combine_win_kernel,
        grid_spec=pltpu.PrefetchScalarGridSpec(
            num_scalar_prefetch=2,
            grid=(b, nt),
            in_specs=[pl.BlockSpec((1, TOK_TILE, ne), tile),
                      pl.BlockSpec((1, TOK_TILE, ne), tile),
                      pl.BlockSpec(spread.shape, const)]
                     + [y_spec(e) for e in range(ne)]
                     + [pl.BlockSpec((1, TOK_TILE, d), tile),
                        pl.BlockSpec((1, 1, d), lambda i, k, row_ref, base_ref: (i, 0, 0)),
                        pl.BlockSpec((1, d), const)],
            out_specs=pl.BlockSpec((1, TOK_TILE, d), tile)),
        out_shape=jax.ShapeDtypeStruct((b, t, d), F32),
        compiler_params=_params("parallel", "parallel"),
        name="combine_win",
    )(row, base, pos_t, aff, spread, *([y.reshape(-1, d)] * ne), x1, gt2, pn2.reshape(1, d))


def _routing_tables(fill, cap):
    b, ne, t = fill.shape
    w = min(2 * SLOT_BLK, cap)
    ends = fill[:, :, TOK_TILE - 1::TOK_TILE].astype(jnp.int32)
    starts = jnp.concatenate([jnp.zeros((b, ne, 1), jnp.int32), ends[:, :, :-1]], axis=-1)
    sparse = jnp.all(ends - starts <= SLOT_BLK)
    base = jnp.minimum(starts // SLOT_BLK * SLOT_BLK, cap - w)
    group = (jnp.arange(ne)[None, :, None] * b + jnp.arange(b)[:, None, None]) * cap
    row_blk = (group + base) // SLOT_BLK
    return sparse, (starts // 8 * 8).reshape(-1), row_blk.reshape(-1), base.reshape(-1), w


def _rope_tables(t):
    pos = np.arange(t)
    r = (pos // GRID_W).astype(np.float32)
    cl = (pos % GRID_W).astype(np.float32)
    quarter = HEAD_DIM // 4
    inv = (np.float32(ROPE_THETA) ** (-np.arange(quarter, dtype=np.float32) / np.float32(quarter))).astype(np.float32)
    ang_r = r[:, None] * inv
    ang_c = cl[:, None] * inv
    ang = np.concatenate([ang_r, ang_r, ang_c, ang_c], axis=-1).astype(np.float32)
    return jnp.asarray(np.cos(ang), F32), jnp.asarray(np.sin(ang), F32)


def _kv_heads(h, w_in, q_width, k_norm, rope):
    plan = (((0, ATT_KV_W, "norm_rope", 0, 1.0), (ATT_KV_W, 2 * ATT_KV_W, "plain", 1, 1.0)),
            ((0, RET_QK_W, "rope", 2, RET_QK_DIM ** -0.5),),
            ((0, 512, "plain", 3, 1.0),),
            ((0, 512, "plain", 3, 1.0),))
    outs = ((ATT_KV_W, ATT_KV_W, lambda j: 0), (ATT_KV_W, ATT_KV_W, lambda j: 0),
            (RET_QK_W, RET_QK_W, lambda j: 0), (RET_V_W, 512, lambda j: jnp.clip(j - 2, 0, 1)))
    return _proj_heads(h, w_in, q_width, plan, outs, k_norm, rope)


def _q_heads(h, w_in, q_norm, rope):
    plan = (((0, 512, "norm_rope", 0, 1.0),), ((0, 512, "norm_rope", 0, 1.0),), ((0, RET_QK_W, "rope", 1, 1.0),))
    outs = ((ATT_W, 512, lambda j: jnp.minimum(j, 1)), (RET_QK_W, RET_QK_W, lambda j: 0))
    return _proj_heads(h, w_in, 0, plan, outs, q_norm, rope)


def kernel(x, c, ctx, c_ctx, w_mod, b_mod, pre_norm1, post_norm1, pre_norm2, post_norm2, w_in, q_norm,
           k_norm, ret_decay, ret_gn, w_o_att, w_o_ret, w_out, w_router, w_gate, w_up, w_down):
    b, t, d = x.shape
    n_ctx = ctx.shape[1]
    depth = w_mod.shape[0]
    q_width = ATT_W + RET_QK_W + RET_V_W + 2 * d
    cap = EC_FACTOR * t // N_EXPERTS
    rope_lat = _rope_tables(t)
    rope_ctx = (jnp.ones((b * n_ctx, HEAD_DIM), F32), jnp.zeros((b * n_ctx, HEAD_DIM), F32))
    cs = jnp.zeros((MOD_ROWS, d), F32).at[:b].set(c).at[b].set(c_ctx)
    xc = ctx
    for layer in range(depth):
        assert layer == depth - 1, "context-stream update between layers is not implemented"
        mod = _mod_vectors(cs, w_mod[layer], b_mod[layer])
        sh1, sc1, gt1, sh2, sc2, gt2 = [m[:b, None, :] for m in jnp.split(mod, 6, axis=-1)]
        csh1, csc1 = [jnp.broadcast_to(m[b][None, None, :], (b, 1, d)) for m in jnp.split(mod, 6, axis=-1)[:2]]
        dec = -jax.nn.softplus(ret_decay[layer].astype(F32))
        wl = w_in[layer]

        hc = _prenorm(xc, pre_norm1[layer], csh1, csc1).reshape(b * n_ctx, d)
        kc_a, vc_a, kc_r, vc_r = _kv_heads(hc, wl, q_width, k_norm[layer], rope_ctx)
        s0f, s0b = _ctx_states(dec, kc_r.reshape(b, n_ctx, RET_QK_W), vc_r.reshape(b, n_ctx, RET_V_W))

        h = _prenorm(x, pre_norm1[layer], sh1, sc1).reshape(b * t, d)
        q_a, q_r = _q_heads(h, wl, q_norm[layer], rope_lat)
        zg = _proj(h, wl, ATT_W + RET_QK_W, RET_V_W + 2 * d, out_dtype=F32, tn=512)
        k_a, v_a, k_r, v_r = _kv_heads(h, wl, q_width, k_norm[layer], rope_lat)

        keys = jnp.concatenate([kc_a.reshape(b, n_ctx, ATT_KV_W), k_a.reshape(b, t, ATT_KV_W)], axis=1)
        vals = jnp.concatenate([vc_a.reshape(b, n_ctx, ATT_KV_W), v_a.reshape(b, t, ATT_KV_W)], axis=1)
        kt = keys.reshape(b, n_ctx + t, ATT_KV_HEADS, HEAD_DIM).transpose(0, 2, 3, 1)
        o_att = _attention(q_a.reshape(b, t, ATT_W), kt, vals)

        o_f, o_b = _retention(dec, q_r.reshape(b, t, RET_QK_W), k_r.reshape(b, t, RET_QK_W),
                              v_r.reshape(b, t, RET_V_W), s0f, s0b)

        y = _merge1(o_att.reshape(b * t, ATT_W), o_f.reshape(b * t, RET_V_W), o_b.reshape(b * t, RET_V_W),
                    zg, ret_gn[layer], w_o_att[layer].astype(BF16), w_o_ret[layer].astype(BF16))
        x1, h2, aff = _merge2(y, w_out[layer].astype(BF16), x, gt1, post_norm1[layer], pre_norm2[layer],
                              sh2, sc2, w_router[layer])

        pos, fill = _topk_slots(aff.transpose(0, 2, 1), cap)
        pos_t = pos.transpose(0, 2, 1)
        sparse, w0, row, base, win = _routing_tables(fill, cap)
        xg = lax.cond(sparse,
                      lambda: _gather_win(w0, pos, h2, cap),
                      lambda: _gather(pos, h2, cap))
        yg = _ffn(xg, w_gate[layer], w_up[layer], w_down[layer])
        x = lax.cond(sparse,
                     lambda: _combine_win(row, base, pos_t, aff, yg, x1, gt2, post_norm2[layer], win),
                     lambda: _combine(pos_t, aff, yg, x1, gt2, post_norm2[layer], cap))
    return x
```

```python
import functools

import jax
import jax.numpy as jnp
import numpy as np
from jax import lax
from jax.experimental import pallas as pl
from jax.experimental.pallas import tpu as pltpu

F32 = jnp.float32
BF16 = jnp.bfloat16

GRID_W = 64
HEAD_DIM = 128
ATT_HEADS = 8
ATT_KV_HEADS = 2
ATT_GROUP = ATT_HEADS // ATT_KV_HEADS
RET_HEADS = 4
RET_QK_DIM = 128
RET_V_DIM = 256
ROPE_THETA = 10000.0
N_EXPERTS = 16
EC_FACTOR = 2
EPS = 1e-6

ATT_W = ATT_HEADS * HEAD_DIM
ATT_KV_W = ATT_KV_HEADS * HEAD_DIM
RET_QK_W = RET_HEADS * RET_QK_DIM
RET_V_W = RET_HEADS * RET_V_DIM

LANES = 128
VMEM_LIMIT = 56 * 1024 * 1024
RET_CHUNK = 256
MIN_NORMAL_F32_BITS = 0x00800000
LOG2_E = 1.4426950408889634
TOK_TILE = 256
SLOT_BLK = 64
GATHER_WIN = SLOT_BLK + 16
COMBINE_GROUP = 4
MOD_ROWS = 16

MOD_COLS = 2048
PRENORM_ROWS = 1024
PROJ_ROWS = 2048
PROJ_COLS = 512
ATT_Q_ROWS = 256
MERGE1_ROWS = 256
GATE_COLS = 1024
MERGE2_ROWS = 512
FFN_ROWS = 2048
FFN_HIDDEN_COLS = 256
FFN_OUT_COLS = 512
GATHER_COLS = 256
DENSE_TOKENS = 1024
DENSE_COLS = 1024
DENSE_COMBINE_ROWS = 512


def _blk(pref, n):
    return pref if n % pref == 0 else n


def _params(*sem):
    return pltpu.CompilerParams(dimension_semantics=sem, vmem_limit_bytes=VMEM_LIMIT)


def _sigmoid(x):
    return 0.5 * jnp.tanh(0.5 * x) + 0.5


def _mod_kernel(s_ref, w_ref, b_ref, o_ref):
    s = s_ref[...]
    s = s * _sigmoid(s)
    rows = s.shape[0]
    s_hi, s_lo = _split_bf16(s)
    w_hi, w_lo = _split_bf16(w_ref[...])
    both = jnp.dot(jnp.concatenate([s_hi, s_lo], axis=0), w_hi, preferred_element_type=F32)
    o_ref[...] = both[:rows] + both[rows:] + jnp.dot(s_hi, w_lo, preferred_element_type=F32) + b_ref[...]


def _mod_vectors(cs, w_mod, b_mod):
    rows, d = cs.shape
    n = w_mod.shape[1]
    tn = _blk(MOD_COLS, n)
    return pl.pallas_call(
        _mod_kernel,
        grid=(n // tn,),
        in_specs=[pl.BlockSpec((rows, d), lambda j: (0, 0)),
                  pl.BlockSpec((d, tn), lambda j: (0, j)),
                  pl.BlockSpec((1, tn), lambda j: (0, j))],
        out_specs=pl.BlockSpec((rows, tn), lambda j: (0, j)),
        out_shape=jax.ShapeDtypeStruct((rows, n), F32),
        compiler_params=_params("parallel"),
        name="mod",
    )(cs, w_mod, b_mod.reshape(1, n))


def _prenorm_kernel(x_ref, g_ref, sh_ref, sc_ref, o_ref):
    x = x_ref[0]
    y = x * lax.rsqrt(jnp.mean(x * x, axis=-1, keepdims=True) + EPS) * g_ref[...]
    o_ref[0] = (y * (1.0 + sc_ref[0]) + sh_ref[0]).astype(o_ref.dtype)


def _prenorm(x, gain, shift, scale):
    b, t, d = x.shape
    tm = _blk(PRENORM_ROWS, t)
    return pl.pallas_call(
        _prenorm_kernel,
        grid=(b, t // tm),
        in_specs=[pl.BlockSpec((1, tm, d), lambda i, j: (i, j, 0)),
                  pl.BlockSpec((1, d), lambda i, j: (0, 0)),
                  pl.BlockSpec((1, 1, d), lambda i, j: (i, 0, 0)),
                  pl.BlockSpec((1, 1, d), lambda i, j: (i, 0, 0))],
        out_specs=pl.BlockSpec((1, tm, d), lambda i, j: (i, j, 0)),
        out_shape=jax.ShapeDtypeStruct((b, t, d), BF16),
        compiler_params=_params("parallel", "parallel"),
        name="prenorm",
    )(x, gain.reshape(1, d), shift, scale)


def _split_bf16(v):
    hi = v.astype(BF16)
    return [hi, (v - hi.astype(F32)).astype(BF16)]


def _rotate_half_matrix():
    quarter = HEAD_DIM // 4
    r = np.zeros((HEAD_DIM, HEAD_DIM), np.float32)
    for i in range(HEAD_DIM):
        if i % (2 * quarter) < quarter:
            r[i + quarter, i] = -1.0
        else:
            r[i - quarter, i] = 1.0
    return r


def _head_mix_matrix(mode):
    rot = _rotate_half_matrix()
    if mode == "rope":
        return jnp.asarray(np.concatenate([rot, rot], axis=0), BF16)
    ones = np.ones((HEAD_DIM, HEAD_DIM), np.float32)
    zero = np.zeros((HEAD_DIM, HEAD_DIM), np.float32)
    return jnp.asarray(np.block([[ones, zero], [ones, zero], [zero, rot], [zero, rot]]), BF16)


def _proj_kernel(h_ref, w_ref, o_ref):
    o_ref[...] = jnp.dot(h_ref[...], w_ref[...].astype(BF16), preferred_element_type=F32).astype(o_ref.dtype)


def _proj(h, w_in, col0, width, *, out_dtype, tn):
    r, d = h.shape
    tm = _blk(PROJ_ROWS, r)
    assert col0 % tn == 0 and width % tn == 0
    c0 = col0 // tn
    return pl.pallas_call(
        _proj_kernel,
        grid=(r // tm, width // tn),
        in_specs=[pl.BlockSpec((tm, d), lambda i, j: (i, 0)),
                  pl.BlockSpec((d, tn), lambda i, j: (0, c0 + j))],
        out_specs=pl.BlockSpec((tm, tn), lambda i, j: (i, j)),
        out_shape=jax.ShapeDtypeStruct((r, width), out_dtype),
        compiler_params=_params("parallel", "parallel"),
        name="proj_plain",
    )(h, w_in)


def _rope_head(xh, mode, gain, mix, cos, sin, pre_scale):
    if mode == "norm_rope":
        xg = xh * gain
        mixed = jnp.dot(jnp.concatenate(_split_bf16(xh * xh) + _split_bf16(xg), axis=1), mix,
                        preferred_element_type=F32)
        inv = lax.rsqrt(mixed[:, :HEAD_DIM] * (1.0 / HEAD_DIM) + EPS)
        return (xg * cos + mixed[:, HEAD_DIM:] * sin) * inv
    xs = xh * pre_scale if pre_scale != 1.0 else xh
    rot = jnp.dot(jnp.concatenate(_split_bf16(xs), axis=1), mix, preferred_element_type=F32)
    return xs * cos + rot * sin


def _proj_heads_kernel(h_ref, w_ref, gain_ref, mixn_ref, mixr_ref, cos_ref, sin_ref, *o_refs, plan):
    acc = jnp.dot(h_ref[...], w_ref[...].astype(BF16), preferred_element_type=F32)
    for jj, pieces in enumerate(plan):
        @pl.when(pl.program_id(1) == jj)
        def _(pieces=pieces):
            for lo, hi, mode, out, pre_scale in pieces:
                o_ref = o_refs[out]
                if mode == "plain":
                    o_ref[...] = acc[:, lo:hi].astype(o_ref.dtype)
                    continue
                mix = mixn_ref[...] if mode == "norm_rope" else mixr_ref[...]
                for hh in range((hi - lo) // HEAD_DIM):
                    xh = acc[:, lo + hh * HEAD_DIM:lo + (hh + 1) * HEAD_DIM]
                    res = _rope_head(xh, mode, gain_ref[...], mix, cos_ref[...], sin_ref[...], pre_scale)
                    o_ref[:, hh * HEAD_DIM:(hh + 1) * HEAD_DIM] = res.astype(o_ref.dtype)


def _proj_heads(h, w_in, col0, plan, outs, gain, rope):
    r, d = h.shape
    cos, sin = rope
    tm = min(_blk(PROJ_ROWS, r), cos.shape[0])
    tn = PROJ_COLS
    assert col0 % tn == 0
    c0 = col0 // tn
    period = cos.shape[0] // tm
    assert cos.shape == (period * tm, HEAD_DIM) and (r // tm) % period == 0
    mixn, mixr = _head_mix_matrix("norm_rope"), _head_mix_matrix("rope")
    const = lambda i, j: (0, 0)
    rows = lambda i, j: (i, 0)
    table = lambda i, j: (i % period, 0)
    return pl.pallas_call(
        functools.partial(_proj_heads_kernel, plan=plan),
        grid=(r // tm, len(plan)),
        in_specs=[pl.BlockSpec((tm, d), rows),
                  pl.BlockSpec((d, tn), lambda i, j: (0, c0 + j)),
                  pl.BlockSpec((1, HEAD_DIM), const),
                  pl.BlockSpec(mixn.shape, const),
                  pl.BlockSpec(mixr.shape, const),
                  pl.BlockSpec((tm, HEAD_DIM), table),
                  pl.BlockSpec((tm, HEAD_DIM), table)],
        out_specs=tuple(pl.BlockSpec((tm, bw), functools.partial(lambda i, j, f: (i, f(j)), f=f))
                        for _, bw, f in outs),
        out_shape=tuple(jax.ShapeDtypeStruct((r, width), BF16) for width, _, _ in outs),
        compiler_params=_params("parallel", "arbitrary"),
        name="proj_heads",
    )(h, w_in, gain.reshape(1, HEAD_DIM), mixn, mixr, cos, sin)


def _ctx_state_kernel(dec_ref, k_ref, v_ref, sf_ref, sb_ref):
    hh = pl.program_id(1)
    lf = dec_ref[0, hh]
    lb = dec_ref[1, hh]
    k = k_ref[0].astype(F32)
    v = v_ref[0]
    n = k.shape[0]
    pos = lax.broadcasted_iota(jnp.int32, k.shape, 0).astype(F32)
    kf = (k * jnp.exp((n - 1.0 - pos) * lf)).T.astype(BF16)
    kb = (k * jnp.exp(pos * lb)).T.astype(BF16)
    sf_ref[0, 0] = jnp.dot(kf, v, preferred_element_type=F32)
    sb_ref[0, 0] = jnp.dot(kb, v, preferred_element_type=F32)


def _ctx_states(dec, k_r, v_r):
    b, n, _ = k_r.shape
    spec_s = pl.BlockSpec((1, 1, RET_QK_DIM, RET_V_DIM), lambda i, j: (i, j, 0, 0))
    shape_s = jax.ShapeDtypeStruct((b, RET_HEADS, RET_QK_DIM, RET_V_DIM), F32)
    return pl.pallas_call(
        _ctx_state_kernel,
        grid=(b, RET_HEADS),
        in_specs=[pl.BlockSpec(memory_space=pltpu.SMEM),
                  pl.BlockSpec((1, n, RET_QK_DIM), lambda i, j: (i, 0, j)),
                  pl.BlockSpec((1, n, RET_V_DIM), lambda i, j: (i, 0, j))],
        out_specs=(spec_s, spec_s),
        out_shape=(shape_s, shape_s),
        compiler_params=_params("parallel", "parallel"),
        name="ctx_state",
    )(dec, k_r, v_r)


def _attn_kernel(q_ref, kt_ref, v_ref, o_ref, *, scale):
    for hk in range(ATT_KV_HEADS):
        kt = kt_ref[0, hk]
        v = v_ref[0, :, hk * HEAD_DIM:(hk + 1) * HEAD_DIM]
        for g in range(ATT_GROUP):
            cols = slice((hk * ATT_GROUP + g) * HEAD_DIM, (hk * ATT_GROUP + g + 1) * HEAD_DIM)
            s = jnp.dot(q_ref[0, :, cols], kt, preferred_element_type=F32)
            m = jnp.max(s, axis=-1, keepdims=True)
            p = jnp.exp2((s - m) * (scale * LOG2_E))
            l = jnp.sum(p, axis=-1, keepdims=True)
            o = jnp.dot(p.astype(BF16), v, preferred_element_type=F32)
            o_ref[0, :, cols] = (o / l).astype(o_ref.dtype)


def _attention(q, kt, v):
    b, t, _ = q.shape
    s = kt.shape[-1]
    tq = _blk(ATT_Q_ROWS, t)
    return pl.pallas_call(
        functools.partial(_attn_kernel, scale=HEAD_DIM ** -0.5),
        grid=(b, t // tq),
        in_specs=[pl.BlockSpec((1, tq, ATT_W), lambda i, n: (i, n, 0)),
                  pl.BlockSpec((1, ATT_KV_HEADS, HEAD_DIM, s), lambda i, n: (i, 0, 0, 0)),
                  pl.BlockSpec((1, s, ATT_KV_W), lambda i, n: (i, 0, 0))],
        out_specs=pl.BlockSpec((1, tq, ATT_W), lambda i, n: (i, n, 0)),
        out_shape=jax.ShapeDtypeStruct((b, t, ATT_W), BF16),
        compiler_params=_params("parallel", "parallel"),
        name="attention",
    )(q, kt, v)


def _ret_kernel(dec_ref, qf_ref, kf_ref, vf_ref, qb_ref, kb_ref, vb_ref, s0f_ref, s0b_ref,
                of_ref, ob_ref, sf_ref, sb_ref, mf_ref, mb_ref):
    n = pl.program_id(1)
    c = qf_ref.shape[1]

    @pl.when(n == 0)
    def _():
        sf_ref[...] = s0f_ref[0]
        sb_ref[...] = s0b_ref[0]
        ri = lax.broadcasted_iota(jnp.int32, (c, c), 0)
        ci = lax.broadcasted_iota(jnp.int32, (c, c), 1)
        d = (ri - ci).astype(F32)
        for hh in range(RET_HEADS):
            mf_ref[hh] = jnp.where(d >= 0, jnp.exp(jnp.maximum(d, 0.0) * dec_ref[0, hh]), 0.0)
            mb_ref[hh] = jnp.where(d <= 0, jnp.exp(jnp.maximum(-d, 0.0) * dec_ref[1, hh]), 0.0)

    nt = (((1,), (1,)), ((), ()))
    row_v = lax.broadcasted_iota(jnp.int32, (c, RET_V_DIM), 0).astype(F32)
    row_k = lax.broadcasted_iota(jnp.int32, (c, RET_QK_DIM), 0).astype(F32)

    def sweep(q_ref, k_ref, v_ref, o_ref, s_ref, m_ref, hh, lg, wq_age, wk_age):
        qk = slice(hh * RET_QK_DIM, (hh + 1) * RET_QK_DIM)
        vv = slice(hh * RET_V_DIM, (hh + 1) * RET_V_DIM)
        q = q_ref[0, :, qk]
        k = k_ref[0, :, qk]
        v = v_ref[0, :, vv]
        s = lax.dot_general(q, k, nt, preferred_element_type=F32) * m_ref[hh]
        o_in = jnp.dot(s.astype(BF16), v, preferred_element_type=F32)
        state = s_ref[hh]
        o_x = jnp.dot(q, state.astype(BF16), preferred_element_type=F32) * jnp.exp(wq_age * lg)
        o_ref[0, :, vv] = o_in + o_x
        kw = (k.astype(F32) * jnp.exp(wk_age * lg)).T.astype(BF16)
        g_chunk = jnp.exp(jnp.full((1, RET_V_DIM), c * lg, F32))
        s_ref[hh] = g_chunk * state + jnp.dot(kw, v, preferred_element_type=F32)

    for hh in range(RET_HEADS):
        sweep(qf_ref, kf_ref, vf_ref, of_ref, sf_ref, mf_ref, hh, dec_ref[0, hh], row_v + 1.0, c - 1.0 - row_k)
        sweep(qb_ref, kb_ref, vb_ref, ob_ref, sb_ref, mb_ref, hh, dec_ref[1, hh], c - row_v, row_k)


def _retention(dec, q, k, v, s0f, s0b):
    b, t, _ = q.shape
    c = _blk(RET_CHUNK, t)
    nc = t // c
    fwd = lambda i, n: (i, n, 0)
    bwd = lambda i, n: (i, nc - 1 - n, 0)
    spec_s = pl.BlockSpec((1, RET_HEADS, RET_QK_DIM, RET_V_DIM), lambda i, n: (i, 0, 0, 0))
    out_shape = jax.ShapeDtypeStruct((b, t, RET_V_W), F32)
    return pl.pallas_call(
        _ret_kernel,
        grid=(b, nc),
        in_specs=[pl.BlockSpec(memory_space=pltpu.SMEM),
                  pl.BlockSpec((1, c, RET_QK_W), fwd),
                  pl.BlockSpec((1, c, RET_QK_W), fwd),
                  pl.BlockSpec((1, c, RET_V_W), fwd),
                  pl.BlockSpec((1, c, RET_QK_W), bwd),
                  pl.BlockSpec((1, c, RET_QK_W), bwd),
                  pl.BlockSpec((1, c, RET_V_W), bwd),
                  spec_s, spec_s],
        out_specs=(pl.BlockSpec((1, c, RET_V_W), fwd), pl.BlockSpec((1, c, RET_V_W), bwd)),
        out_shape=(out_shape, out_shape),
        scratch_shapes=[pltpu.VMEM((RET_HEADS, RET_QK_DIM, RET_V_DIM), F32),
                        pltpu.VMEM((RET_HEADS, RET_QK_DIM, RET_V_DIM), F32),
                        pltpu.VMEM((RET_HEADS, c, c), F32),
                        pltpu.VMEM((RET_HEADS, c, c), F32)],
        compiler_params=_params("parallel", "arbitrary"),
        name="retention",
    )(dec, q, k, v, q, k, v, s0f, s0b)


def _merge1_kernel(oa_ref, of_ref, ob_ref, g_ref, *refs, nslab):
    ga_refs, gr_refs = refs[:nslab], refs[nslab:2 * nslab]
    gn_ref, wa_ref, wr_ref, y_ref = refs[2 * nslab:]
    o = of_ref[...] + ob_ref[...]
    heads = []
    for hh in range(RET_HEADS):
        sl = slice(hh * RET_V_DIM, (hh + 1) * RET_V_DIM)
        oh = o[:, sl]
        dlt = oh - jnp.mean(oh, axis=-1, keepdims=True)
        yh = dlt * lax.rsqrt(jnp.mean(dlt * dlt, axis=-1, keepdims=True) + EPS)
        g = g_ref[:, sl]
        heads.append((yh * gn_ref[:, sl] * (g * _sigmoid(g))).astype(BF16))
    o_ret = jnp.concatenate(heads, axis=1)
    tn = ga_refs[0].shape[1]
    for c in range(nslab):
        cols = slice(c * tn, (c + 1) * tn)
        ya = jnp.dot(oa_ref[...], wa_ref[:, cols], preferred_element_type=F32)
        yr = jnp.dot(o_ret, wr_ref[:, cols], preferred_element_type=F32)
        y_ref[:, cols] = (_sigmoid(ga_refs[c][...]) * ya + _sigmoid(gr_refs[c][...]) * yr).astype(y_ref.dtype)


def _merge1(o_att, o_f, o_b, zg, gn, wa, wr):
    r = o_att.shape[0]
    d = wa.shape[1]
    tm = _blk(MERGE1_ROWS, r)
    tn = _blk(GATE_COLS, d)
    nslab = d // tn
    ga0 = RET_V_W // tn
    gr0 = (RET_V_W + d) // tn
    row = lambda i: (i, 0)
    const = lambda i: (0, 0)
    gate_specs = [pl.BlockSpec((tm, tn), functools.partial(lambda i, c: (i, c), c=c0 + c))
                  for c0 in (ga0, gr0) for c in range(nslab)]
    return pl.pallas_call(
        functools.partial(_merge1_kernel, nslab=nslab),
        grid=(r // tm,),
        in_specs=[pl.BlockSpec((tm, ATT_W), row),
                  pl.BlockSpec((tm, RET_V_W), row),
                  pl.BlockSpec((tm, RET_V_W), row),
                  pl.BlockSpec((tm, RET_V_W), row)]
                 + gate_specs
                 + [pl.BlockSpec((1, RET_V_W), const),
                    pl.BlockSpec((ATT_W, d), const),
                    pl.BlockSpec((RET_V_W, d), const)],
        out_specs=pl.BlockSpec((tm, d), row),
        out_shape=jax.ShapeDtypeStruct((r, d), BF16),
        compiler_params=_params("parallel"),
        name="merge1",
    )(o_att, o_f, o_b, zg, *([zg] * (2 * nslab)), gn.reshape(1, RET_V_W), wa, wr)


def _merge2_kernel(y_ref, w_ref, x_ref, gt_ref, pn1_ref, pn2_ref, sh_ref, sc_ref, wr_ref,
                   x1_ref, h2_ref, aff_ref):
    yy = jnp.dot(y_ref[0], w_ref[...], preferred_element_type=F32)
    r = yy * lax.rsqrt(jnp.mean(yy * yy, axis=-1, keepdims=True) + EPS) * pn1_ref[...]
    x1 = x_ref[0] + gt_ref[0] * r
    x1_ref[0] = x1
    h2 = x1 * lax.rsqrt(jnp.mean(x1 * x1, axis=-1, keepdims=True) + EPS) * pn2_ref[...]
    h2 = h2 * (1.0 + sc_ref[0]) + sh_ref[0]
    h_hi = h2.astype(BF16)
    h2_ref[0] = h_hi
    h_lo = (h2 - h_hi.astype(F32)).astype(BF16)
    wr = wr_ref[...]
    w_hi = wr.astype(BF16)
    w_lo = (wr - w_hi.astype(F32)).astype(BF16)
    ne = wr.shape[1]
    r_hi = jnp.dot(h_hi, jnp.concatenate([w_hi, w_lo], axis=1), preferred_element_type=F32)
    logits = r_hi[:, :ne] + r_hi[:, ne:] + jnp.dot(h_lo, w_hi, preferred_element_type=F32)
    e = jnp.exp(logits - jnp.max(logits, axis=-1, keepdims=True))
    aff_ref[0] = e / jnp.sum(e, axis=-1, keepdims=True)


def _merge2(y, w_out, x, gt1, pn1, pn2, sh2, sc2, w_router):
    b, t, d = x.shape
    tm = _blk(MERGE2_ROWS, t)
    ne = w_router.shape[1]
    tile = lambda i, j: (i, j, 0)
    vec = lambda i, j: (0, 0)
    per_b = lambda i, j: (i, 0, 0)
    return pl.pallas_call(
        _merge2_kernel,
        grid=(b, t // tm),
        in_specs=[pl.BlockSpec((1, tm, d), tile),
                  pl.BlockSpec((d, d), vec),
                  pl.BlockSpec((1, tm, d), tile),
                  pl.BlockSpec((1, 1, d), per_b),
                  pl.BlockSpec((1, d), vec),
                  pl.BlockSpec((1, d), vec),
                  pl.BlockSpec((1, 1, d), per_b),
                  pl.BlockSpec((1, 1, d), per_b),
                  pl.BlockSpec((d, ne), vec)],
        out_specs=(pl.BlockSpec((1, tm, d), tile),
                   pl.BlockSpec((1, tm, d), tile),
                   pl.BlockSpec((1, tm, ne), tile)),
        out_shape=(jax.ShapeDtypeStruct((b, t, d), F32),
                   jax.ShapeDtypeStruct((b, t, d), BF16),
                   jax.ShapeDtypeStruct((b, t, ne), F32)),
        compiler_params=_params("parallel", "parallel"),
        name="merge2",
    )(y.reshape(b, t, d), w_out, x, gt1, pn1.reshape(1, d), pn2.reshape(1, d), sh2, sc2, w_router)


def _lane_cumsum(x, tri):
    off = jnp.zeros((x.shape[0], 1), F32)
    parts = []
    for cidx in range(x.shape[1] // LANES):
        xc = x[:, cidx * LANES:(cidx + 1) * LANES].astype(BF16)
        cs = jnp.dot(xc, tri, preferred_element_type=F32) + off
        parts.append(cs)
        off = cs[:, LANES - 1:LANES]
    return jnp.concatenate(parts, axis=1)


def _topk_kernel(a_ref, pos_ref, fill_ref, *, cap):
    a = a_ref[0]
    bits = jnp.zeros((a.shape[0], 1), jnp.int32)
    for bit in range(30, -1, -1):
        cand = bits | (1 << bit)
        cnt = jnp.sum(jnp.where(a >= lax.bitcast_convert_type(cand, F32), 1.0, 0.0), axis=1, keepdims=True)
        bits = jnp.where(cnt >= cap, cand, bits)
    bits = jnp.where(bits < MIN_NORMAL_F32_BITS, 0, bits)
    thr = lax.bitcast_convert_type(bits, F32)
    ri = lax.broadcasted_iota(jnp.int32, (LANES, LANES), 0)
    ci = lax.broadcasted_iota(jnp.int32, (LANES, LANES), 1)
    tri = jnp.where(ri <= ci, 1.0, 0.0).astype(BF16)
    gt = a > thr
    eq = a == thr
    n_gt = jnp.sum(jnp.where(gt, 1.0, 0.0), axis=1, keepdims=True)
    eq_rank = _lane_cumsum(jnp.where(eq, 1.0, 0.0), tri)
    sel = jnp.where(gt, 1.0, jnp.where(eq, jnp.where(eq_rank <= cap - n_gt, 1.0, 0.0), 0.0))
    filled = _lane_cumsum(sel, tri)
    pos_ref[0] = filled * sel
    fill_ref[0] = filled


def _topk_slots(aff_t, cap):
    b, ne, t = aff_t.shape
    spec = pl.BlockSpec((1, ne, t), lambda i: (i, 0, 0))
    shape = jax.ShapeDtypeStruct((b, ne, t), F32)
    return pl.pallas_call(
        functools.partial(_topk_kernel, cap=cap),
        grid=(b,),
        in_specs=[spec],
        out_specs=(spec, spec),
        out_shape=(shape, shape),
        compiler_params=_params("parallel"),
        name="topk",
    )(aff_t)


def _gather_kernel(pos_ref, h_ref, o_ref):
    cap = o_ref.shape[1]
    t = h_ref.shape[1]
    tc = _blk(DENSE_TOKENS, t)
    slot = lax.broadcasted_iota(jnp.int32, (cap, tc), 0).astype(F32) + 1.0
    acc = None
    for cidx in range(t // tc):
        row = pos_ref[0, 0, :, cidx * tc:(cidx + 1) * tc]
        onehot = jnp.where(row == slot, 1.0, 0.0).astype(BF16)
        part = jnp.dot(onehot, h_ref[0, cidx * tc:(cidx + 1) * tc, :], preferred_element_type=F32)
        acc = part if acc is None else acc + part
    o_ref[0] = acc.astype(o_ref.dtype)


def _gather(pos, h2, cap):
    b, ne, t = pos.shape
    d = h2.shape[-1]
    dn = _blk(DENSE_COLS, d)
    return pl.pallas_call(
        _gather_kernel,
        grid=(b, d // dn, ne),
        in_specs=[pl.BlockSpec((1, 1, 1, t), lambda i, n, e: (i, e, 0, 0)),
                  pl.BlockSpec((1, t, dn), lambda i, n, e: (i, 0, n))],
        out_specs=pl.BlockSpec((1, cap, dn), lambda i, n, e: (e, i, n)),
        out_shape=jax.ShapeDtypeStruct((ne, b * cap, d), BF16),
        compiler_params=_params("parallel", "parallel", "parallel"),
        name="gather",
    )(pos.reshape(b, ne, 1, t), h2)


def _gather_win_kernel(w0_ref, pos_ref, h_ref, o_ref, acc_ref, lhs_ref):
    b = pl.program_id(0)
    ne, cap = o_ref.shape[0], o_ref.shape[1]
    nt = h_ref.shape[1] // TOK_TILE
    slot1 = lax.broadcasted_iota(jnp.int32, (GATHER_WIN, TOK_TILE), 0) + 1

    @pl.when(pl.program_id(1) == 0)
    def _():
        def build(k, carry):
            for e in range(ne):
                w0 = w0_ref[(b * ne + e) * nt + k]
                row = pos_ref[0, pl.ds(e * nt + k, 1), :]
                onehot = jnp.where(row == (slot1 + w0).astype(F32), 1.0, 0.0)
                lhs_ref[k, e * GATHER_WIN:(e + 1) * GATHER_WIN, :] = onehot.astype(BF16)
            return carry
        lax.fori_loop(0, nt, build, 0)

    acc_ref[:, :8, :] = jnp.zeros((ne, 8) + acc_ref.shape[2:], F32)

    def tile(k, carry):
        t0 = pl.multiple_of(k * TOK_TILE, TOK_TILE)
        res = jnp.dot(lhs_ref[k], h_ref[0, pl.ds(t0, TOK_TILE), :], preferred_element_type=F32)
        for e in range(ne):
            w0 = pl.multiple_of(w0_ref[(b * ne + e) * nt + k], 8)
            acc_ref[e, pl.ds(w0, 8), :] += res[e * GATHER_WIN:e * GATHER_WIN + 8]
            acc_ref[e, pl.ds(w0 + 8, GATHER_WIN - 8), :] = res[e * GATHER_WIN + 8:(e + 1) * GATHER_WIN]
        return carry
    lax.fori_loop(0, nt, tile, 0, unroll=8)
    o_ref[...] = acc_ref[:, :cap, :].astype(o_ref.dtype)


def _gather_win(w0, pos, h2, cap):
    b, ne, t = pos.shape
    d = h2.shape[-1]
    nt = t // TOK_TILE
    dn = _blk(GATHER_COLS, d)
    return pl.pallas_call(
        _gather_win_kernel,
        grid_spec=pltpu.PrefetchScalarGridSpec(
            num_scalar_prefetch=1,
            grid=(b, d // dn),
            in_specs=[pl.BlockSpec((1, ne * nt, TOK_TILE), lambda i, n, w: (i, 0, 0)),
                      pl.BlockSpec((1, t, dn), lambda i, n, w: (i, 0, n))],
            out_specs=pl.BlockSpec((ne, cap, dn), lambda i, n, w: (0, i, n)),
            scratch_shapes=[pltpu.VMEM((ne, cap + GATHER_WIN, dn), F32),
                            pltpu.VMEM((nt, ne * GATHER_WIN, TOK_TILE), BF16)]),
        out_shape=jax.ShapeDtypeStruct((ne, b * cap, d), BF16),
        compiler_params=_params("parallel", "arbitrary"),
        name="gather_win",
    )(w0, pos.reshape(b, ne * nt, TOK_TILE), h2)


def _ffn_kernel(x_ref, wg_ref, wu_ref, wd_ref, o_ref, h_ref, wgu_ref, *, nf):
    j = pl.program_id(2)
    tf = wg_ref.shape[2]

    @pl.when(j < nf)
    def _():
        wgu_ref[:, :tf] = wg_ref[0].astype(BF16)
        wgu_ref[:, tf:] = wu_ref[0].astype(BF16)
        au = jnp.dot(x_ref[0], wgu_ref[...], preferred_element_type=F32)
        a = au[:, :tf]
        h_ref[j] = (a * _sigmoid(a) * au[:, tf:]).astype(BF16)

    @pl.when(j >= nf)
    def _():
        acc = None
        for cidx in range(nf):
            part = jnp.dot(h_ref[cidx], wd_ref[0, cidx * tf:(cidx + 1) * tf, :].astype(BF16),
                           preferred_element_type=F32)
            acc = part if acc is None else acc + part
        o_ref[0] = acc.astype(o_ref.dtype)


def _ffn(xg, w_gate, w_up, w_down):
    ne, rows, d = xg.shape
    ff = w_gate.shape[-1]
    tm = _blk(FFN_ROWS, rows)
    tf = _blk(FFN_HIDDEN_COLS, ff)
    tn = _blk(FFN_OUT_COLS, d)
    nf = ff // tf
    return pl.pallas_call(
        functools.partial(_ffn_kernel, nf=nf),
        grid=(ne, rows // tm, nf + d // tn),
        in_specs=[pl.BlockSpec((1, tm, d), lambda e, m, j: (e, m, 0)),
                  pl.BlockSpec((1, d, tf), lambda e, m, j: (e, 0, jnp.minimum(j, nf - 1))),
                  pl.BlockSpec((1, d, tf), lambda e, m, j: (e, 0, jnp.minimum(j, nf - 1))),
                  pl.BlockSpec((1, ff, tn), lambda e, m, j: (e, 0, jnp.maximum(j - nf, 0)))],
        out_specs=pl.BlockSpec((1, tm, tn), lambda e, m, j: (e, m, jnp.maximum(j - nf, 0))),
        out_shape=jax.ShapeDtypeStruct((ne, rows, d), BF16),
        scratch_shapes=[pltpu.VMEM((nf, tm, tf), BF16), pltpu.VMEM((d, 2 * tf), BF16)],
        compiler_params=_params("parallel", "parallel", "arbitrary"),
        name="ffn",
    )(xg, w_gate, w_up, w_down)


def _combine_kernel(pos_ref, aff_ref, y_ref, x1_ref, gt_ref, pn_ref, o_ref, acc_ref):
    e = pl.program_id(2)
    tt, ne = pos_ref.shape[1], pos_ref.shape[2]
    cap = y_ref.shape[1]
    pick = lax.broadcasted_iota(jnp.int32, (tt, ne), 1) == e
    pcol = jnp.sum(jnp.where(pick, pos_ref[0], 0.0), axis=1, keepdims=True)
    acol = jnp.sum(jnp.where(pick, aff_ref[0], 0.0), axis=1, keepdims=True)
    slot = lax.broadcasted_iota(jnp.int32, (tt, cap), 1).astype(F32) + 1.0
    onehot = jnp.where(pcol == slot, 1.0, 0.0).astype(BF16)
    z = jnp.dot(onehot, y_ref[0], preferred_element_type=F32) * acol

    @pl.when(e == 0)
    def _():
        acc_ref[...] = z

    @pl.when(e > 0)
    def _():
        acc_ref[...] += z

    @pl.when(e == pl.num_programs(2) - 1)
    def _():
        y2 = acc_ref[...]
        r = y2 * lax.rsqrt(jnp.mean(y2 * y2, axis=-1, keepdims=True) + EPS) * pn_ref[...]
        o_ref[0] = x1_ref[0] + gt_ref[0] * r


def _combine(pos_t, aff, y, x1, gt2, pn2, cap):
    b, t, d = x1.shape
    ne = aff.shape[-1]
    tt = _blk(DENSE_COMBINE_ROWS, t)
    return pl.pallas_call(
        _combine_kernel,
        grid=(b, t // tt, ne),
        in_specs=[pl.BlockSpec((1, tt, ne), lambda i, j, e: (i, j, 0)),
                  pl.BlockSpec((1, tt, ne), lambda i, j, e: (i, j, 0)),
                  pl.BlockSpec((1, cap, d), lambda i, j, e: (e, i, 0)),
                  pl.BlockSpec((1, tt, d), lambda i, j, e: (i, j, 0)),
                  pl.BlockSpec((1, 1, d), lambda i, j, e: (i, 0, 0)),
                  pl.BlockSpec((1, d), lambda i, j, e: (0, 0))],
        out_specs=pl.BlockSpec((1, tt, d), lambda i, j, e: (i, j, 0)),
        out_shape=jax.ShapeDtypeStruct((b, t, d), F32),
        scratch_shapes=[pltpu.VMEM((tt, d), F32)],
        compiler_params=_params("parallel", "parallel", "arbitrary"),
        name="combine",
    )(pos_t, aff, y, x1, gt2, pn2.reshape(1, d))


def _spread_matrix(ne, w):
    s = np.zeros((5 * ne, 2 * ne * w), np.float32)
    for e in range(ne):
        s[e, e * w:(e + 1) * w] = SLOT_BLK
        s[ne + e, e * w:(e + 1) * w] = 1.0
        for part in (2, 3, 4):
            s[part * ne + e, (ne + e) * w:(ne + e + 1) * w] = 1.0
    return jnp.asarray(s, BF16)


def _combine_win_kernel(row_ref, base_ref, pos_ref, aff_ref, spread_ref, *refs):
    del row_ref
    ne = pos_ref.shape[2]
    y_refs = refs[:ne]
    x1_ref, gt_ref, pn_ref, o_ref = refs[ne:]
    b = pl.program_id(0)
    k = pl.program_id(1)
    nt = pl.num_programs(1)
    w = y_refs[0].shape[0]
    pos = pos_ref[0]
    aff = aff_ref[0]
    p_hi = jnp.floor(pos * (1.0 / SLOT_BLK))
    a_parts = _split_bf16(aff)
    a_parts.append((aff - a_parts[0].astype(F32) - a_parts[1].astype(F32)).astype(BF16))
    lhs = jnp.concatenate([p_hi.astype(BF16), (pos - SLOT_BLK * p_hi).astype(BF16)] + a_parts, axis=1)
    spread = jnp.dot(lhs, spread_ref[...], preferred_element_type=F32)
    lane = lax.broadcasted_iota(jnp.int32, (1, ne * w), 1)
    seg = lane // w
    seg_base = jnp.zeros((1, ne * w), jnp.int32)
    for e in range(ne):
        seg_base = jnp.where(seg == e, base_ref[(b * ne + e) * nt + k], seg_base)
    slot = (seg_base + lane % w + 1).astype(F32)
    onehot = jnp.where(spread[:, :ne * w] == slot, spread[:, ne * w:], 0.0).astype(BF16)
    y2 = None
    gw = COMBINE_GROUP * w
    for g in range(ne // COMBINE_GROUP):
        ywin = jnp.concatenate([r[...] for r in y_refs[COMBINE_GROUP * g:COMBINE_GROUP * (g + 1)]], axis=0)
        part = jnp.dot(onehot[:, g * gw:(g + 1) * gw], ywin, preferred_element_type=F32)
        y2 = part if y2 is None else y2 + part
    r = y2 * lax.rsqrt(jnp.mean(y2 * y2, axis=-1, keepdims=True) + EPS) * pn_ref[...]
    o_ref[0] = x1_ref[0] + gt_ref[0] * r


def _combine_win(row, base, pos_t, aff, y, x1, gt2, pn2, w):
    b, t, d = x1.shape
    ne = aff.shape[-1]
    nt = t // TOK_TILE

    def y_spec(e):
        return pl.BlockSpec((pl.Element(w), pl.Element(d)),
                            lambda i, k, row_ref, base_ref: (row_ref[(i * ne + e) * nt + k] * SLOT_BLK, 0))

    tile = lambda i, k, row_ref, base_ref: (i, k, 0)
    const = lambda i, k, row_ref, base_ref: (0, 0)
    spread = _spread_matrix(ne, w)
    return pl.pallas_call(
        _combine_win_kernel,
        grid_spec=pltpu.PrefetchScalarGridSpec(
            num_scalar_prefetch=2,
            grid=(b, nt),
            in_specs=[pl.BlockSpec((1, TOK_TILE, ne), tile),
                      pl.BlockSpec((1, TOK_TILE, ne), tile),
                      pl.BlockSpec(spread.shape, const)]
                     + [y_spec(e) for e in range(ne)]
                     + [pl.BlockSpec((1, TOK_TILE, d), tile),
                        pl.BlockSpec((1, 1, d), lambda i, k, row_ref, base_ref: (i, 0, 0)),
                        pl.BlockSpec((1, d), const)],
            out_specs=pl.BlockSpec((1, TOK_TILE, d), tile)),
        out_shape=jax.ShapeDtypeStruct((b, t, d), F32),
        compiler_params=_params("parallel", "parallel"),
        name="combine_win",
    )(row, base, pos_t, aff, spread, *([y.reshape(-1, d)] * ne), x1, gt2, pn2.reshape(1, d))


def _routing_tables(fill, cap):
    b, ne, t = fill.shape
    w = min(2 * SLOT_BLK, cap)
    ends = fill[:, :, TOK_TILE - 1::TOK_TILE].astype(jnp.int32)
    starts = jnp.concatenate([jnp.zeros((b, ne, 1), jnp.int32), ends[:, :, :-1]], axis=-1)
    sparse = jnp.all(ends - starts <= SLOT_BLK)
    base = jnp.minimum(starts // SLOT_BLK * SLOT_BLK, cap - w)
    group = (jnp.arange(ne)[None, :, None] * b + jnp.arange(b)[:, None, None]) * cap
    row_blk = (group + base) // SLOT_BLK
    return sparse, (starts // 8 * 8).reshape(-1), row_blk.reshape(-1), base.reshape(-1), w


def _rope_tables(t):
    pos = np.arange(t)
    r = (pos // GRID_W).astype(np.float32)
    cl = (pos % GRID_W).astype(np.float32)
    quarter = HEAD_DIM // 4
    inv = (np.float32(ROPE_THETA) ** (-np.arange(quarter, dtype=np.float32) / np.float32(quarter))).astype(np.float32)
    ang_r = r[:, None] * inv
    ang_c = cl[:, None] * inv
    ang = np.concatenate([ang_r, ang_r, ang_c, ang_c], axis=-1).astype(np.float32)
    return jnp.asarray(np.cos(ang), F32), jnp.asarray(np.sin(ang), F32)


def _kv_heads(h, w_in, q_width, k_norm, rope):
    assert 2 * ATT_KV_W == RET_QK_W == PROJ_COLS and RET_V_W == 2 * PROJ_COLS
    plan = (((0, ATT_KV_W, "norm_rope", 0, 1.0), (ATT_KV_W, 2 * ATT_KV_W, "plain", 1, 1.0)),
            ((0, RET_QK_W, "rope", 2, RET_QK_DIM ** -0.5),),
            ((0, PROJ_COLS, "plain", 3, 1.0),),
            ((0, PROJ_COLS, "plain", 3, 1.0),))
    outs = ((ATT_KV_W, ATT_KV_W, lambda j: 0), (ATT_KV_W, ATT_KV_W, lambda j: 0),
            (RET_QK_W, RET_QK_W, lambda j: 0), (RET_V_W, PROJ_COLS, lambda j: jnp.clip(j - 2, 0, 1)))
    return _proj_heads(h, w_in, q_width, plan, outs, k_norm, rope)


def _q_heads(h, w_in, q_norm, rope):
    assert ATT_W == 2 * PROJ_COLS and RET_QK_W == PROJ_COLS
    plan = (((0, PROJ_COLS, "norm_rope", 0, 1.0),), ((0, PROJ_COLS, "norm_rope", 0, 1.0),),
            ((0, RET_QK_W, "rope", 1, 1.0),))
    outs = ((ATT_W, PROJ_COLS, lambda j: jnp.minimum(j, 1)), (RET_QK_W, RET_QK_W, lambda j: 0))
    return _proj_heads(h, w_in, 0, plan, outs, q_norm, rope)


def kernel(x, c, ctx, c_ctx, w_mod, b_mod, pre_norm1, post_norm1, pre_norm2, post_norm2, w_in, q_norm,
           k_norm, ret_decay, ret_gn, w_o_att, w_o_ret, w_out, w_router, w_gate, w_up, w_down):
    b, t, d = x.shape
    n_ctx = ctx.shape[1]
    depth = w_mod.shape[0]
    q_width = ATT_W + RET_QK_W + RET_V_W + 2 * d
    cap = EC_FACTOR * t // N_EXPERTS
    rope_lat = _rope_tables(t)
    rope_ctx = (jnp.ones((b * n_ctx, HEAD_DIM), F32), jnp.zeros((b * n_ctx, HEAD_DIM), F32))
    cs = jnp.zeros((MOD_ROWS, d), F32).at[:b].set(c).at[b].set(c_ctx)
    xc = ctx
    for layer in range(depth):
        assert layer == depth - 1, "context-stream update between layers is not implemented"
        mod = _mod_vectors(cs, w_mod[layer], b_mod[layer])
        sh1, sc1, gt1, sh2, sc2, gt2 = [m[:b, None, :] for m in jnp.split(mod, 6, axis=-1)]
        csh1, csc1 = [jnp.broadcast_to(m[b][None, None, :], (b, 1, d)) for m in jnp.split(mod, 6, axis=-1)[:2]]
        dec = -jax.nn.softplus(ret_decay[layer].astype(F32))
        wl = w_in[layer]

        hc = _prenorm(xc, pre_norm1[layer], csh1, csc1).reshape(b * n_ctx, d)
        kc_a, vc_a, kc_r, vc_r = _kv_heads(hc, wl, q_width, k_norm[layer], rope_ctx)
        s0f, s0b = _ctx_states(dec, kc_r.reshape(b, n_ctx, RET_QK_W), vc_r.reshape(b, n_ctx, RET_V_W))

        h = _prenorm(x, pre_norm1[layer], sh1, sc1).reshape(b * t, d)
        q_a, q_r = _q_heads(h, wl, q_norm[layer], rope_lat)
        zg = _proj(h, wl, ATT_W + RET_QK_W, RET_V_W + 2 * d, out_dtype=F32, tn=PROJ_COLS)
        k_a, v_a, k_r, v_r = _kv_heads(h, wl, q_width, k_norm[layer], rope_lat)

        keys = jnp.concatenate([kc_a.reshape(b, n_ctx, ATT_KV_W), k_a.reshape(b, t, ATT_KV_W)], axis=1)
        vals = jnp.concatenate([vc_a.reshape(b, n_ctx, ATT_KV_W), v_a.reshape(b, t, ATT_KV_W)], axis=1)
        kt = keys.reshape(b, n_ctx + t, ATT_KV_HEADS, HEAD_DIM).transpose(0, 2, 3, 1)
        o_att = _attention(q_a.reshape(b, t, ATT_W), kt, vals)

        o_f, o_b = _retention(dec, q_r.reshape(b, t, RET_QK_W), k_r.reshape(b, t, RET_QK_W),
                              v_r.reshape(b, t, RET_V_W), s0f, s0b)

        y = _merge1(o_att.reshape(b * t, ATT_W), o_f.reshape(b * t, RET_V_W), o_b.reshape(b * t, RET_V_W),
                    zg, ret_gn[layer], w_o_att[layer].astype(BF16), w_o_ret[layer].astype(BF16))
        x1, h2, aff = _merge2(y, w_out[layer].astype(BF16), x, gt1, post_norm1[layer], pre_norm2[layer],
                              sh2, sc2, w_router[layer])

        pos, fill = _topk_slots(aff.transpose(0, 2, 1), cap)
        pos_t = pos.transpose(0, 2, 1)
        sparse, w0, row, base, win = _routing_tables(fill, cap)
        xg = lax.cond(sparse,
                      lambda: _gather_win(w0, pos, h2, cap),
                      lambda: _gather(pos, h2, cap))
        yg = _ffn(xg, w_gate[layer], w_up[layer], w_down[layer])
        x = lax.cond(sparse,
                     lambda: _combine_win(row, base, pos_t, aff, yg, x1, gt2, post_norm2[layer], win),
                     lambda: _combine(pos_t, aff, yg, x1, gt2, post_norm2[layer], cap))
    return x
```

```python
import functools

import jax
import jax.numpy as jnp
import numpy as np
from jax import lax
from jax.experimental import pallas as pl
from jax.experimental.pallas import tpu as pltpu

F32 = jnp.float32
BF16 = jnp.bfloat16

GRID_W = 64
HEAD_DIM = 128
ATT_HEADS = 8
ATT_KV_HEADS = 2
ATT_GROUP = ATT_HEADS // ATT_KV_HEADS
RET_HEADS = 4
RET_QK_DIM = 128
RET_V_DIM = 256
ROPE_THETA = 10000.0
N_EXPERTS = 16
EC_FACTOR = 2
EPS = 1e-6

ATT_W = ATT_HEADS * HEAD_DIM
ATT_KV_W = ATT_KV_HEADS * HEAD_DIM
RET_QK_W = RET_HEADS * RET_QK_DIM
RET_V_W = RET_HEADS * RET_V_DIM

LANES = 128
VMEM_LIMIT = 56 * 1024 * 1024
RET_CHUNK = 256
MIN_NORMAL_F32_BITS = 0x00800000
LOG2_E = 1.4426950408889634
TOK_TILE = 256
SLOT_BLK = 64
GATHER_WIN = SLOT_BLK + 16
COMBINE_GROUP = 4
MOD_ROWS = 16

MOD_COLS = 2048
PRENORM_ROWS = 1024
PROJ_ROWS = 2048
PROJ_COLS = 512
ATT_Q_ROWS = 256
MERGE1_ROWS = 256
GATE_COLS = 1024
MERGE2_ROWS = 512
FFN_ROWS = 2048
FFN_HIDDEN_COLS = 256
FFN_OUT_COLS = 512
GATHER_COLS = 256
DENSE_TOKENS = 1024
DENSE_COLS = 1024
DENSE_COMBINE_ROWS = 512


def _blk(pref, n):
    return pref if n % pref == 0 else n


def _params(*sem):
    return pltpu.CompilerParams(dimension_semantics=sem, vmem_limit_bytes=VMEM_LIMIT)


def _sigmoid(x):
    return 0.5 * jnp.tanh(0.5 * x) + 0.5


def _mod_kernel(s_ref, w_ref, b_ref, o_ref):
    s = s_ref[...]
    s = s * _sigmoid(s)
    rows = s.shape[0]
    s_hi, s_lo = _split_bf16(s)
    w_hi, w_lo = _split_bf16(w_ref[...])
    both = jnp.dot(jnp.concatenate([s_hi, s_lo], axis=0), w_hi, preferred_element_type=F32)
    o_ref[...] = both[:rows] + both[rows:] + jnp.dot(s_hi, w_lo, preferred_element_type=F32) + b_ref[...]


def _mod_vectors(cs, w_mod, b_mod):
    rows, d = cs.shape
    n = w_mod.shape[1]
    tn = _blk(MOD_COLS, n)
    return pl.pallas_call(
        _mod_kernel,
        grid=(n // tn,),
        in_specs=[pl.BlockSpec((rows, d), lambda j: (0, 0)),
                  pl.BlockSpec((d, tn), lambda j: (0, j)),
                  pl.BlockSpec((1, tn), lambda j: (0, j))],
        out_specs=pl.BlockSpec((rows, tn), lambda j: (0, j)),
        out_shape=jax.ShapeDtypeStruct((rows, n), F32),
        compiler_params=_params("parallel"),
        name="mod",
    )(cs, w_mod, b_mod.reshape(1, n))


def _prenorm_kernel(x_ref, g_ref, sh_ref, sc_ref, o_ref):
    x = x_ref[0]
    y = x * lax.rsqrt(jnp.mean(x * x, axis=-1, keepdims=True) + EPS) * g_ref[...]
    o_ref[0] = (y * (1.0 + sc_ref[0]) + sh_ref[0]).astype(o_ref.dtype)


def _prenorm(x, gain, shift, scale):
    b, t, d = x.shape
    tm = _blk(PRENORM_ROWS, t)
    return pl.pallas_call(
        _prenorm_kernel,
        grid=(b, t // tm),
        in_specs=[pl.BlockSpec((1, tm, d), lambda i, j: (i, j, 0)),
                  pl.BlockSpec((1, d), lambda i, j: (0, 0)),
                  pl.BlockSpec((1, 1, d), lambda i, j: (i, 0, 0)),
                  pl.BlockSpec((1, 1, d), lambda i, j: (i, 0, 0))],
        out_specs=pl.BlockSpec((1, tm, d), lambda i, j: (i, j, 0)),
        out_shape=jax.ShapeDtypeStruct((b, t, d), BF16),
        compiler_params=_params("parallel", "parallel"),
        name="prenorm",
    )(x, gain.reshape(1, d), shift, scale)


def _split_bf16(v):
    hi = v.astype(BF16)
    return [hi, (v - hi.astype(F32)).astype(BF16)]


def _rotate_half_matrix():
    quarter = HEAD_DIM // 4
    r = np.zeros((HEAD_DIM, HEAD_DIM), np.float32)
    for i in range(HEAD_DIM):
        if i % (2 * quarter) < quarter:
            r[i + quarter, i] = -1.0
        else:
            r[i - quarter, i] = 1.0
    return r


def _head_mix_matrix(mode):
    rot = _rotate_half_matrix()
    if mode == "rope":
        return jnp.asarray(np.concatenate([rot, rot], axis=0), BF16)
    ones = np.ones((HEAD_DIM, HEAD_DIM), np.float32)
    zero = np.zeros((HEAD_DIM, HEAD_DIM), np.float32)
    return jnp.asarray(np.block([[ones, zero], [ones, zero], [zero, rot], [zero, rot]]), BF16)


def _proj_kernel(h_ref, w_ref, o_ref):
    o_ref[...] = jnp.dot(h_ref[...], w_ref[...].astype(BF16), preferred_element_type=F32).astype(o_ref.dtype)


def _proj(h, w_in, col0, width, *, out_dtype, tn):
    r, d = h.shape
    tm = _blk(PROJ_ROWS, r)
    assert col0 % tn == 0 and width % tn == 0
    c0 = col0 // tn
    return pl.pallas_call(
        _proj_kernel,
        grid=(r // tm, width // tn),
        in_specs=[pl.BlockSpec((tm, d), lambda i, j: (i, 0)),
                  pl.BlockSpec((d, tn), lambda i, j: (0, c0 + j))],
        out_specs=pl.BlockSpec((tm, tn), lambda i, j: (i, j)),
        out_shape=jax.ShapeDtypeStruct((r, width), out_dtype),
        compiler_params=_params("parallel", "parallel"),
        name="proj_plain",
    )(h, w_in)


def _rope_head(xh, mode, gain, mix, cos, sin, pre_scale):
    if mode == "norm_rope":
        xg = xh * gain
        mixed = jnp.dot(jnp.concatenate(_split_bf16(xh * xh) + _split_bf16(xg), axis=1), mix,
                        preferred_element_type=F32)
        inv = lax.rsqrt(mixed[:, :HEAD_DIM] * (1.0 / HEAD_DIM) + EPS)
        return (xg * cos + mixed[:, HEAD_DIM:] * sin) * inv
    xs = xh * pre_scale if pre_scale != 1.0 else xh
    rot = jnp.dot(jnp.concatenate(_split_bf16(xs), axis=1), mix, preferred_element_type=F32)
    return xs * cos + rot * sin


def _proj_heads_kernel(h_ref, w_ref, gain_ref, mixn_ref, mixr_ref, cos_ref, sin_ref, *o_refs, plan):
    acc = jnp.dot(h_ref[...], w_ref[...].astype(BF16), preferred_element_type=F32)
    for jj, pieces in enumerate(plan):
        @pl.when(pl.program_id(1) == jj)
        def _(pieces=pieces):
            for lo, hi, mode, out, pre_scale in pieces:
                o_ref = o_refs[out]
                if mode == "plain":
                    o_ref[...] = acc[:, lo:hi].astype(o_ref.dtype)
                    continue
                mix = mixn_ref[...] if mode == "norm_rope" else mixr_ref[...]
                for hh in range((hi - lo) // HEAD_DIM):
                    xh = acc[:, lo + hh * HEAD_DIM:lo + (hh + 1) * HEAD_DIM]
                    res = _rope_head(xh, mode, gain_ref[...], mix, cos_ref[...], sin_ref[...], pre_scale)
                    o_ref[:, hh * HEAD_DIM:(hh + 1) * HEAD_DIM] = res.astype(o_ref.dtype)


def _proj_heads(h, w_in, col0, plan, outs, gain, rope):
    r, d = h.shape
    cos, sin = rope
    tm = min(_blk(PROJ_ROWS, r), cos.shape[0])
    tn = PROJ_COLS
    assert col0 % tn == 0
    c0 = col0 // tn
    period = cos.shape[0] // tm
    assert cos.shape == (period * tm, HEAD_DIM) and (r // tm) % period == 0
    mixn, mixr = _head_mix_matrix("norm_rope"), _head_mix_matrix("rope")
    const = lambda i, j: (0, 0)
    rows = lambda i, j: (i, 0)
    table = lambda i, j: (i % period, 0)
    return pl.pallas_call(
        functools.partial(_proj_heads_kernel, plan=plan),
        grid=(r // tm, len(plan)),
        in_specs=[pl.BlockSpec((tm, d), rows),
                  pl.BlockSpec((d, tn), lambda i, j: (0, c0 + j)),
                  pl.BlockSpec((1, HEAD_DIM), const),
                  pl.BlockSpec(mixn.shape, const),
                  pl.BlockSpec(mixr.shape, const),
                  pl.BlockSpec((tm, HEAD_DIM), table),
                  pl.BlockSpec((tm, HEAD_DIM), table)],
        out_specs=tuple(pl.BlockSpec((tm, bw), functools.partial(lambda i, j, f: (i, f(j)), f=f))
                        for _, bw, f in outs),
        out_shape=tuple(jax.ShapeDtypeStruct((r, width), BF16) for width, _, _ in outs),
        compiler_params=_params("parallel", "arbitrary"),
        name="proj_heads",
    )(h, w_in, gain.reshape(1, HEAD_DIM), mixn, mixr, cos, sin)


def _ctx_state_kernel(dec_ref, k_ref, v_ref, sf_ref, sb_ref):
    hh = pl.program_id(1)
    lf = dec_ref[0, hh]
    lb = dec_ref[1, hh]
    k = k_ref[0].astype(F32)
    v = v_ref[0]
    n = k.shape[0]
    pos = lax.broadcasted_iota(jnp.int32, k.shape, 0).astype(F32)
    kf = (k * jnp.exp((n - 1.0 - pos) * lf)).T.astype(BF16)
    kb = (k * jnp.exp(pos * lb)).T.astype(BF16)
    sf_ref[0, 0] = jnp.dot(kf, v, preferred_element_type=F32)
    sb_ref[0, 0] = jnp.dot(kb, v, preferred_element_type=F32)


def _ctx_states(dec, k_r, v_r):
    b, n, _ = k_r.shape
    spec_s = pl.BlockSpec((1, 1, RET_QK_DIM, RET_V_DIM), lambda i, j: (i, j, 0, 0))
    shape_s = jax.ShapeDtypeStruct((b, RET_HEADS, RET_QK_DIM, RET_V_DIM), F32)
    return pl.pallas_call(
        _ctx_state_kernel,
        grid=(b, RET_HEADS),
        in_specs=[pl.BlockSpec(memory_space=pltpu.SMEM),
                  pl.BlockSpec((1, n, RET_QK_DIM), lambda i, j: (i, 0, j)),
                  pl.BlockSpec((1, n, RET_V_DIM), lambda i, j: (i, 0, j))],
        out_specs=(spec_s, spec_s),
        out_shape=(shape_s, shape_s),
        compiler_params=_params("parallel", "parallel"),
        name="ctx_state",
    )(dec, k_r, v_r)


def _attn_kernel(q_ref, k_ref, v_ref, o_ref, *, scale):
    nt = (((1,), (1,)), ((), ()))
    for hk in range(ATT_KV_HEADS):
        k = k_ref[0, :, hk * HEAD_DIM:(hk + 1) * HEAD_DIM]
        v = v_ref[0, :, hk * HEAD_DIM:(hk + 1) * HEAD_DIM]
        for g in range(ATT_GROUP):
            cols = slice((hk * ATT_GROUP + g) * HEAD_DIM, (hk * ATT_GROUP + g + 1) * HEAD_DIM)
            s = lax.dot_general(q_ref[0, :, cols], k, nt, preferred_element_type=F32)
            m = jnp.max(s, axis=-1, keepdims=True)
            p = jnp.exp2((s - m) * (scale * LOG2_E))
            l = jnp.sum(p, axis=-1, keepdims=True)
            o = jnp.dot(p.astype(BF16), v, preferred_element_type=F32)
            o_ref[0, :, cols] = (o / l).astype(o_ref.dtype)


def _attention(q, k, v):
    b, t, _ = q.shape
    s = k.shape[1]
    tq = _blk(ATT_Q_ROWS, t)
    return pl.pallas_call(
        functools.partial(_attn_kernel, scale=HEAD_DIM ** -0.5),
        grid=(b, t // tq),
        in_specs=[pl.BlockSpec((1, tq, ATT_W), lambda i, n: (i, n, 0)),
                  pl.BlockSpec((1, s, ATT_KV_W), lambda i, n: (i, 0, 0)),
                  pl.BlockSpec((1, s, ATT_KV_W), lambda i, n: (i, 0, 0))],
        out_specs=pl.BlockSpec((1, tq, ATT_W), lambda i, n: (i, n, 0)),
        out_shape=jax.ShapeDtypeStruct((b, t, ATT_W), BF16),
        compiler_params=_params("parallel", "parallel"),
        name="attention",
    )(q, k, v)


def _ret_kernel(dec_ref, qf_ref, kf_ref, vf_ref, qb_ref, kb_ref, vb_ref, s0f_ref, s0b_ref,
                of_ref, ob_ref, sf_ref, sb_ref, mf_ref, mb_ref):
    n = pl.program_id(1)
    c = qf_ref.shape[1]

    @pl.when(n == 0)
    def _():
        sf_ref[...] = s0f_ref[0]
        sb_ref[...] = s0b_ref[0]
        ri = lax.broadcasted_iota(jnp.int32, (c, c), 0)
        ci = lax.broadcasted_iota(jnp.int32, (c, c), 1)
        d = (ri - ci).astype(F32)
        for hh in range(RET_HEADS):
            mf_ref[hh] = jnp.where(d >= 0, jnp.exp(jnp.maximum(d, 0.0) * dec_ref[0, hh]), 0.0)
            mb_ref[hh] = jnp.where(d <= 0, jnp.exp(jnp.maximum(-d, 0.0) * dec_ref[1, hh]), 0.0)

    nt = (((1,), (1,)), ((), ()))
    row_v = lax.broadcasted_iota(jnp.int32, (c, RET_V_DIM), 0).astype(F32)
    row_k = lax.broadcasted_iota(jnp.int32, (c, RET_QK_DIM), 0).astype(F32)

    def sweep(q_ref, k_ref, v_ref, o_ref, s_ref, m_ref, hh, lg, wq_age, wk_age):
        qk = slice(hh * RET_QK_DIM, (hh + 1) * RET_QK_DIM)
        vv = slice(hh * RET_V_DIM, (hh + 1) * RET_V_DIM)
        q = q_ref[0, :, qk]
        k = k_ref[0, :, qk]
        v = v_ref[0, :, vv]
        s = lax.dot_general(q, k, nt, preferred_element_type=F32) * m_ref[hh]
        o_in = jnp.dot(s.astype(BF16), v, preferred_element_type=F32)
        state = s_ref[hh]
        o_x = jnp.dot(q, state.astype(BF16), preferred_element_type=F32) * jnp.exp(wq_age * lg)
        o_ref[0, :, vv] = o_in + o_x
        kw = (k.astype(F32) * jnp.exp(wk_age * lg)).T.astype(BF16)
        g_chunk = jnp.exp(jnp.full((1, RET_V_DIM), c * lg, F32))
        s_ref[hh] = g_chunk * state + jnp.dot(kw, v, preferred_element_type=F32)

    for hh in range(RET_HEADS):
        sweep(qf_ref, kf_ref, vf_ref, of_ref, sf_ref, mf_ref, hh, dec_ref[0, hh], row_v + 1.0, c - 1.0 - row_k)
        sweep(qb_ref, kb_ref, vb_ref, ob_ref, sb_ref, mb_ref, hh, dec_ref[1, hh], c - row_v, row_k)


def _retention(dec, q, k, v, s0f, s0b):
    b, t, _ = q.shape
    c = _blk(RET_CHUNK, t)
    nc = t // c
    fwd = lambda i, n: (i, n, 0)
    bwd = lambda i, n: (i, nc - 1 - n, 0)
    spec_s = pl.BlockSpec((1, RET_HEADS, RET_QK_DIM, RET_V_DIM), lambda i, n: (i, 0, 0, 0))
    out_shape = jax.ShapeDtypeStruct((b, t, RET_V_W), F32)
    return pl.pallas_call(
        _ret_kernel,
        grid=(b, nc),
        in_specs=[pl.BlockSpec(memory_space=pltpu.SMEM),
                  pl.BlockSpec((1, c, RET_QK_W), fwd),
                  pl.BlockSpec((1, c, RET_QK_W), fwd),
                  pl.BlockSpec((1, c, RET_V_W), fwd),
                  pl.BlockSpec((1, c, RET_QK_W), bwd),
                  pl.BlockSpec((1, c, RET_QK_W), bwd),
                  pl.BlockSpec((1, c, RET_V_W), bwd),
                  spec_s, spec_s],
        out_specs=(pl.BlockSpec((1, c, RET_V_W), fwd), pl.BlockSpec((1, c, RET_V_W), bwd)),
        out_shape=(out_shape, out_shape),
        scratch_shapes=[pltpu.VMEM((RET_HEADS, RET_QK_DIM, RET_V_DIM), F32),
                        pltpu.VMEM((RET_HEADS, RET_QK_DIM, RET_V_DIM), F32),
                        pltpu.VMEM((RET_HEADS, c, c), F32),
                        pltpu.VMEM((RET_HEADS, c, c), F32)],
        compiler_params=_params("parallel", "arbitrary"),
        name="retention",
    )(dec, q, k, v, q, k, v, s0f, s0b)


def _merge1_kernel(oa_ref, of_ref, ob_ref, g_ref, *refs, nslab):
    ga_refs, gr_refs = refs[:nslab], refs[nslab:2 * nslab]
    gn_ref, wa_ref, wr_ref, y_ref = refs[2 * nslab:]
    o = of_ref[...] + ob_ref[...]
    heads = []
    for hh in range(RET_HEADS):
        sl = slice(hh * RET_V_DIM, (hh + 1) * RET_V_DIM)
        oh = o[:, sl]
        dlt = oh - jnp.mean(oh, axis=-1, keepdims=True)
        yh = dlt * lax.rsqrt(jnp.mean(dlt * dlt, axis=-1, keepdims=True) + EPS)
        g = g_ref[:, sl]
        heads.append((yh * gn_ref[:, sl] * (g * _sigmoid(g))).astype(BF16))
    o_ret = jnp.concatenate(heads, axis=1)
    tn = ga_refs[0].shape[1]
    for c in range(nslab):
        cols = slice(c * tn, (c + 1) * tn)
        ya = jnp.dot(oa_ref[...], wa_ref[:, cols], preferred_element_type=F32)
        yr = jnp.dot(o_ret, wr_ref[:, cols], preferred_element_type=F32)
        y_ref[:, cols] = (_sigmoid(ga_refs[c][...]) * ya + _sigmoid(gr_refs[c][...]) * yr).astype(y_ref.dtype)


def _merge1(o_att, o_f, o_b, zg, gn, wa, wr):
    r = o_att.shape[0]
    d = wa.shape[1]
    tm = _blk(MERGE1_ROWS, r)
    tn = _blk(GATE_COLS, d)
    nslab = d // tn
    ga0 = RET_V_W // tn
    gr0 = (RET_V_W + d) // tn
    row = lambda i: (i, 0)
    const = lambda i: (0, 0)
    gate_specs = [pl.BlockSpec((tm, tn), functools.partial(lambda i, c: (i, c), c=c0 + c))
                  for c0 in (ga0, gr0) for c in range(nslab)]
    return pl.pallas_call(
        functools.partial(_merge1_kernel, nslab=nslab),
        grid=(r // tm,),
        in_specs=[pl.BlockSpec((tm, ATT_W), row),
                  pl.BlockSpec((tm, RET_V_W), row),
                  pl.BlockSpec((tm, RET_V_W), row),
                  pl.BlockSpec((tm, RET_V_W), row)]
                 + gate_specs
                 + [pl.BlockSpec((1, RET_V_W), const),
                    pl.BlockSpec((ATT_W, d), const),
                    pl.BlockSpec((RET_V_W, d), const)],
        out_specs=pl.BlockSpec((tm, d), row),
        out_shape=jax.ShapeDtypeStruct((r, d), BF16),
        compiler_params=_params("parallel"),
        name="merge1",
    )(o_att, o_f, o_b, zg, *([zg] * (2 * nslab)), gn.reshape(1, RET_V_W), wa, wr)


def _merge2_kernel(y_ref, w_ref, x_ref, gt_ref, pn1_ref, pn2_ref, sh_ref, sc_ref, wr_ref,
                   x1_ref, h2_ref, aff_ref):
    yy = jnp.dot(y_ref[0], w_ref[...], preferred_element_type=F32)
    r = yy * lax.rsqrt(jnp.mean(yy * yy, axis=-1, keepdims=True) + EPS) * pn1_ref[...]
    x1 = x_ref[0] + gt_ref[0] * r
    x1_ref[0] = x1
    h2 = x1 * lax.rsqrt(jnp.mean(x1 * x1, axis=-1, keepdims=True) + EPS) * pn2_ref[...]
    h2 = h2 * (1.0 + sc_ref[0]) + sh_ref[0]
    h_hi = h2.astype(BF16)
    h2_ref[0] = h_hi
    h_lo = (h2 - h_hi.astype(F32)).astype(BF16)
    wr = wr_ref[...]
    w_hi = wr.astype(BF16)
    w_lo = (wr - w_hi.astype(F32)).astype(BF16)
    ne = wr.shape[1]
    r_hi = jnp.dot(h_hi, jnp.concatenate([w_hi, w_lo], axis=1), preferred_element_type=F32)
    logits = r_hi[:, :ne] + r_hi[:, ne:] + jnp.dot(h_lo, w_hi, preferred_element_type=F32)
    e = jnp.exp(logits - jnp.max(logits, axis=-1, keepdims=True))
    aff_ref[0] = e / jnp.sum(e, axis=-1, keepdims=True)


def _merge2(y, w_out, x, gt1, pn1, pn2, sh2, sc2, w_router):
    b, t, d = x.shape
    tm = _blk(MERGE2_ROWS, t)
    ne = w_router.shape[1]
    tile = lambda i, j: (i, j, 0)
    vec = lambda i, j: (0, 0)
    per_b = lambda i, j: (i, 0, 0)
    return pl.pallas_call(
        _merge2_kernel,
        grid=(b, t // tm),
        in_specs=[pl.BlockSpec((1, tm, d), tile),
                  pl.BlockSpec((d, d), vec),
                  pl.BlockSpec((1, tm, d), tile),
                  pl.BlockSpec((1, 1, d), per_b),
                  pl.BlockSpec((1, d), vec),
                  pl.BlockSpec((1, d), vec),
                  pl.BlockSpec((1, 1, d), per_b),
                  pl.BlockSpec((1, 1, d), per_b),
                  pl.BlockSpec((d, ne), vec)],
        out_specs=(pl.BlockSpec((1, tm, d), tile),
                   pl.BlockSpec((1, tm, d), tile),
                   pl.BlockSpec((1, tm, ne), tile)),
        out_shape=(jax.ShapeDtypeStruct((b, t, d), F32),
                   jax.ShapeDtypeStruct((b, t, d), BF16),
                   jax.ShapeDtypeStruct((b, t, ne), F32)),
        compiler_params=_params("parallel", "parallel"),
        name="merge2",
    )(y.reshape(b, t, d), w_out, x, gt1, pn1.reshape(1, d), pn2.reshape(1, d), sh2, sc2, w_router)


def _lane_cumsum(x, tri):
    off = jnp.zeros((x.shape[0], 1), F32)
    parts = []
    for cidx in range(x.shape[1] // LANES):
        xc = x[:, cidx * LANES:(cidx + 1) * LANES].astype(BF16)
        cs = jnp.dot(xc, tri, preferred_element_type=F32) + off
        parts.append(cs)
        off = cs[:, LANES - 1:LANES]
    return jnp.concatenate(parts, axis=1)


def _topk_kernel(a_ref, pos_ref, fill_ref, *, cap):
    a = a_ref[0]
    bits = jnp.zeros((a.shape[0], 1), jnp.int32)
    for bit in range(30, -1, -1):
        cand = bits | (1 << bit)
        cnt = jnp.sum(jnp.where(a >= lax.bitcast_convert_type(cand, F32), 1.0, 0.0), axis=1, keepdims=True)
        bits = jnp.where(cnt >= cap, cand, bits)
    bits = jnp.where(bits < MIN_NORMAL_F32_BITS, 0, bits)
    thr = lax.bitcast_convert_type(bits, F32)
    ri = lax.broadcasted_iota(jnp.int32, (LANES, LANES), 0)
    ci = lax.broadcasted_iota(jnp.int32, (LANES, LANES), 1)
    tri = jnp.where(ri <= ci, 1.0, 0.0).astype(BF16)
    gt = a > thr
    eq = a == thr
    n_gt = jnp.sum(jnp.where(gt, 1.0, 0.0), axis=1, keepdims=True)
    eq_rank = _lane_cumsum(jnp.where(eq, 1.0, 0.0), tri)
    sel = jnp.where(gt, 1.0, jnp.where(eq, jnp.where(eq_rank <= cap - n_gt, 1.0, 0.0), 0.0))
    filled = _lane_cumsum(sel, tri)
    pos_ref[0] = filled * sel
    fill_ref[0] = filled


def _topk_slots(aff_t, cap):
    b, ne, t = aff_t.shape
    spec = pl.BlockSpec((1, ne, t), lambda i: (i, 0, 0))
    shape = jax.ShapeDtypeStruct((b, ne, t), F32)
    return pl.pallas_call(
        functools.partial(_topk_kernel, cap=cap),
        grid=(b,),
        in_specs=[spec],
        out_specs=(spec, spec),
        out_shape=(shape, shape),
        compiler_params=_params("parallel"),
        name="topk",
    )(aff_t)


def _gather_kernel(pos_ref, h_ref, o_ref):
    cap = o_ref.shape[1]
    t = h_ref.shape[1]
    tc = _blk(DENSE_TOKENS, t)
    slot = lax.broadcasted_iota(jnp.int32, (cap, tc), 0).astype(F32) + 1.0
    acc = None
    for cidx in range(t // tc):
        row = pos_ref[0, 0, :, cidx * tc:(cidx + 1) * tc]
        onehot = jnp.where(row == slot, 1.0, 0.0).astype(BF16)
        part = jnp.dot(onehot, h_ref[0, cidx * tc:(cidx + 1) * tc, :], preferred_element_type=F32)
        acc = part if acc is None else acc + part
    o_ref[0] = acc.astype(o_ref.dtype)


def _gather(pos, h2, cap):
    b, ne, t = pos.shape
    d = h2.shape[-1]
    dn = _blk(DENSE_COLS, d)
    return pl.pallas_call(
        _gather_kernel,
        grid=(b, d // dn, ne),
        in_specs=[pl.BlockSpec((1, 1, 1, t), lambda i, n, e: (i, e, 0, 0)),
                  pl.BlockSpec((1, t, dn), lambda i, n, e: (i, 0, n))],
        out_specs=pl.BlockSpec((1, cap, dn), lambda i, n, e: (e, i, n)),
        out_shape=jax.ShapeDtypeStruct((ne, b * cap, d), BF16),
        compiler_params=_params("parallel", "parallel", "parallel"),
        name="gather",
    )(pos.reshape(b, ne, 1, t), h2)


def _gather_win_kernel(w0_ref, pos_ref, h_ref, o_ref, acc_ref, lhs_ref):
    b = pl.program_id(0)
    ne, cap = o_ref.shape[0], o_ref.shape[1]
    nt = h_ref.shape[1] // TOK_TILE
    slot1 = lax.broadcasted_iota(jnp.int32, (GATHER_WIN, TOK_TILE), 0) + 1

    @pl.when(pl.program_id(1) == 0)
    def _():
        def build(k, carry):
            for e in range(ne):
                w0 = w0_ref[(b * ne + e) * nt + k]
                row = pos_ref[0, pl.ds(e * nt + k, 1), :]
                onehot = jnp.where(row == (slot1 + w0).astype(F32), 1.0, 0.0)
                lhs_ref[k, e * GATHER_WIN:(e + 1) * GATHER_WIN, :] = onehot.astype(BF16)
            return carry
        lax.fori_loop(0, nt, build, 0)

    acc_ref[:, :8, :] = jnp.zeros((ne, 8) + acc_ref.shape[2:], F32)

    def tile(k, carry):
        t0 = pl.multiple_of(k * TOK_TILE, TOK_TILE)
        res = jnp.dot(lhs_ref[k], h_ref[0, pl.ds(t0, TOK_TILE), :], preferred_element_type=F32)
        for e in range(ne):
            w0 = pl.multiple_of(w0_ref[(b * ne + e) * nt + k], 8)
            acc_ref[e, pl.ds(w0, 8), :] += res[e * GATHER_WIN:e * GATHER_WIN + 8]
            acc_ref[e, pl.ds(w0 + 8, GATHER_WIN - 8), :] = res[e * GATHER_WIN + 8:(e + 1) * GATHER_WIN]
        return carry
    lax.fori_loop(0, nt, tile, 0, unroll=8)
    o_ref[...] = acc_ref[:, :cap, :].astype(o_ref.dtype)


def _gather_win(w0, pos, h2, cap):
    b, ne, t = pos.shape
    d = h2.shape[-1]
    nt = t // TOK_TILE
    dn = _blk(GATHER_COLS, d)
    return pl.pallas_call(
        _gather_win_kernel,
        grid_spec=pltpu.PrefetchScalarGridSpec(
            num_scalar_prefetch=1,
            grid=(b, d // dn),
            in_specs=[pl.BlockSpec((1, ne * nt, TOK_TILE), lambda i, n, w: (i, 0, 0)),
                      pl.BlockSpec((1, t, dn), lambda i, n, w: (i, 0, n))],
            out_specs=pl.BlockSpec((ne, cap, dn), lambda i, n, w: (0, i, n)),
            scratch_shapes=[pltpu.VMEM((ne, cap + GATHER_WIN, dn), F32),
                            pltpu.VMEM((nt, ne * GATHER_WIN, TOK_TILE), BF16)]),
        out_shape=jax.ShapeDtypeStruct((ne, b * cap, d), BF16),
        compiler_params=_params("parallel", "arbitrary"),
        name="gather_win",
    )(w0, pos.reshape(b, ne * nt, TOK_TILE), h2)


def _ffn_kernel(x_ref, wg_ref, wu_ref, wd_ref, o_ref, h_ref, wgu_ref, *, nf):
    j = pl.program_id(2)
    tf = wg_ref.shape[2]

    @pl.when(j < nf)
    def _():
        wgu_ref[:, :tf] = wg_ref[0].astype(BF16)
        wgu_ref[:, tf:] = wu_ref[0].astype(BF16)
        au = jnp.dot(x_ref[0], wgu_ref[...], preferred_element_type=F32)
        a = au[:, :tf]
        h_ref[j] = (a * _sigmoid(a) * au[:, tf:]).astype(BF16)

    @pl.when(j >= nf)
    def _():
        acc = None
        for cidx in range(nf):
            part = jnp.dot(h_ref[cidx], wd_ref[0, cidx * tf:(cidx + 1) * tf, :].astype(BF16),
                           preferred_element_type=F32)
            acc = part if acc is None else acc + part
        o_ref[0] = acc.astype(o_ref.dtype)


def _ffn(xg, w_gate, w_up, w_down):
    ne, rows, d = xg.shape
    ff = w_gate.shape[-1]
    tm = _blk(FFN_ROWS, rows)
    tf = _blk(FFN_HIDDEN_COLS, ff)
    tn = _blk(FFN_OUT_COLS, d)
    nf = ff // tf
    return pl.pallas_call(
        functools.partial(_ffn_kernel, nf=nf),
        grid=(ne, rows // tm, nf + d // tn),
        in_specs=[pl.BlockSpec((1, tm, d), lambda e, m, j: (e, m, 0)),
                  pl.BlockSpec((1, d, tf), lambda e, m, j: (e, 0, jnp.minimum(j, nf - 1))),
                  pl.BlockSpec((1, d, tf), lambda e, m, j: (e, 0, jnp.minimum(j, nf - 1))),
                  pl.BlockSpec((1, ff, tn), lambda e, m, j: (e, 0, jnp.maximum(j - nf, 0)))],
        out_specs=pl.BlockSpec((1, tm, tn), lambda e, m, j: (e, m, jnp.maximum(j - nf, 0))),
        out_shape=jax.ShapeDtypeStruct((ne, rows, d), BF16),
        scratch_shapes=[pltpu.VMEM((nf, tm, tf), BF16), pltpu.VMEM((d, 2 * tf), BF16)],
        compiler_params=_params("parallel", "parallel", "arbitrary"),
        name="ffn",
    )(xg, w_gate, w_up, w_down)


def _combine_kernel(pos_ref, aff_ref, y_ref, x1_ref, gt_ref, pn_ref, o_ref, acc_ref):
    e = pl.program_id(2)
    tt, ne = pos_ref.shape[1], pos_ref.shape[2]
    cap = y_ref.shape[1]
    pick = lax.broadcasted_iota(jnp.int32, (tt, ne), 1) == e
    pcol = jnp.sum(jnp.where(pick, pos_ref[0], 0.0), axis=1, keepdims=True)
    acol = jnp.sum(jnp.where(pick, aff_ref[0], 0.0), axis=1, keepdims=True)
    slot = lax.broadcasted_iota(jnp.int32, (tt, cap), 1).astype(F32) + 1.0
    onehot = jnp.where(pcol == slot, 1.0, 0.0).astype(BF16)
    z = jnp.dot(onehot, y_ref[0], preferred_element_type=F32) * acol

    @pl.when(e == 0)
    def _():
        acc_ref[...] = z

    @pl.when(e > 0)
    def _():
        acc_ref[...] += z

    @pl.when(e == pl.num_programs(2) - 1)
    def _():
        y2 = acc_ref[...]
        r = y2 * lax.rsqrt(jnp.mean(y2 * y2, axis=-1, keepdims=True) + EPS) * pn_ref[...]
        o_ref[0] = x1_ref[0] + gt_ref[0] * r


def _combine(pos_t, aff, y, x1, gt2, pn2, cap):
    b, t, d = x1.shape
    ne = aff.shape[-1]
    tt = _blk(DENSE_COMBINE_ROWS, t)
    return pl.pallas_call(
        _combine_kernel,
        grid=(b, t // tt, ne),
        in_specs=[pl.BlockSpec((1, tt, ne), lambda i, j, e: (i, j, 0)),
                  pl.BlockSpec((1, tt, ne), lambda i, j, e: (i, j, 0)),
                  pl.BlockSpec((1, cap, d), lambda i, j, e: (e, i, 0)),
                  pl.BlockSpec((1, tt, d), lambda i, j, e: (i, j, 0)),
                  pl.BlockSpec((1, 1, d), lambda i, j, e: (i, 0, 0)),
                  pl.BlockSpec((1, d), lambda i, j, e: (0, 0))],
        out_specs=pl.BlockSpec((1, tt, d), lambda i, j, e: (i, j, 0)),
        out_shape=jax.ShapeDtypeStruct((b, t, d), F32),
        scratch_shapes=[pltpu.VMEM((tt, d), F32)],
        compiler_params=_params("parallel", "parallel", "arbitrary"),
        name="combine",
    )(pos_t, aff, y, x1, gt2, pn2.reshape(1, d))


def _spread_matrix(ne, w):
    s = np.zeros((5 * ne, 2 * ne * w), np.float32)
    for e in range(ne):
        s[e, e * w:(e + 1) * w] = SLOT_BLK
        s[ne + e, e * w:(e + 1) * w] = 1.0
        for part in (2, 3, 4):
            s[part * ne + e, (ne + e) * w:(ne + e + 1) * w] = 1.0
    return jnp.asarray(s, BF16)


def _combine_win_kernel(row_ref, base_ref, pos_ref, aff_ref, spread_ref, *refs):
    del row_ref
    ne = pos_ref.shape[2]
    y_refs = refs[:ne]
    x1_ref, gt_ref, pn_ref, o_ref = refs[ne:]
    b = pl.program_id(0)
    k = pl.program_id(1)
    nt = pl.num_programs(1)
    w = y_refs[0].shape[0]
    pos = pos_ref[0]
    aff = aff_ref[0]
    p_hi = jnp.floor(pos * (1.0 / SLOT_BLK))
    a_parts = _split_bf16(aff)
    a_parts.append((aff - a_parts[0].astype(F32) - a_parts[1].astype(F32)).astype(BF16))
    lhs = jnp.concatenate([p_hi.astype(BF16), (pos - SLOT_BLK * p_hi).astype(BF16)] + a_parts, axis=1)
    spread = jnp.dot(lhs, spread_ref[...], preferred_element_type=F32)
    lane = lax.broadcasted_iota(jnp.int32, (1, ne * w), 1)
    seg = lane // w
    seg_base = jnp.zeros((1, ne * w), jnp.int32)
    for e in range(ne):
        seg_base = jnp.where(seg == e, base_ref[(b * ne + e) * nt + k], seg_base)
    slot = (seg_base + lane % w + 1).astype(F32)
    onehot = jnp.where(spread[:, :ne * w] == slot, spread[:, ne * w:], 0.0).astype(BF16)
    y2 = None
    gw = COMBINE_GROUP * w
    for g in range(ne // COMBINE_GROUP):
        ywin = jnp.concatenate([r[...] for r in y_refs[COMBINE_GROUP * g:COMBINE_GROUP * (g + 1)]], axis=0)
        part = jnp.dot(onehot[:, g * gw:(g + 1) * gw], ywin, preferred_element_type=F32)
        y2 = part if y2 is None else y2 + part
    r = y2 * lax.rsqrt(jnp.mean(y2 * y2, axis=-1, keepdims=True) + EPS) * pn_ref[...]
    o_ref[0] = x1_ref[0] + gt_ref[0] * r


def _combine_win(row, base, pos_t, aff, y, x1, gt2, pn2, w):
    b, t, d = x1.shape
    ne = aff.shape[-1]
    nt = t // TOK_TILE

    def y_spec(e):
        return pl.BlockSpec((pl.Element(w), pl.Element(d)),
                            lambda i, k, row_ref, base_ref: (row_ref[(i * ne + e) * nt + k] * SLOT_BLK, 0))

    tile = lambda i, k, row_ref, base_ref: (i, k, 0)
    const = lambda i, k, row_ref, base_ref: (0, 0)
    spread = _spread_matrix(ne, w)
    return pl.pallas_call(
        _combine_win_kernel,
        grid_spec=pltpu.PrefetchScalarGridSpec(
            num_scalar_prefetch=2,
            grid=(b, nt),
            in_specs=[pl.BlockSpec((1, TOK_TILE, ne), tile),
                      pl.BlockSpec((1, TOK_TILE, ne), tile),
                      pl.BlockSpec(spread.shape, const)]
                     + [y_spec(e) for e in range(ne)]
                     + [pl.BlockSpec((1, TOK_TILE, d), tile),
                        pl.BlockSpec((1, 1, d), lambda i, k, row_ref, base_ref: (i, 0, 0)),
                        pl.BlockSpec((1, d), const)],
            out_specs=pl.BlockSpec((1, TOK_TILE, d), tile)),
        out_shape=jax.ShapeDtypeStruct((b, t, d), F32),
        compiler_params=_params("parallel", "parallel"),
        name="combine_win",
    )(row, base, pos_t, aff, spread, *([y.reshape(-1, d)] * ne), x1, gt2, pn2.reshape(1, d))


def _routing_tables(fill, cap):
    b, ne, t = fill.shape
    w = min(2 * SLOT_BLK, cap)
    ends = fill[:, :, TOK_TILE - 1::TOK_TILE].astype(jnp.int32)
    starts = jnp.concatenate([jnp.zeros((b, ne, 1), jnp.int32), ends[:, :, :-1]], axis=-1)
    sparse = jnp.all(ends - starts <= SLOT_BLK)
    base = jnp.minimum(starts // SLOT_BLK * SLOT_BLK, cap - w)
    group = (jnp.arange(ne)[None, :, None] * b + jnp.arange(b)[:, None, None]) * cap
    row_blk = (group + base) // SLOT_BLK
    return sparse, (starts // 8 * 8).reshape(-1), row_blk.reshape(-1), base.reshape(-1), w


def _rope_tables(t):
    pos = np.arange(t)
    r = (pos // GRID_W).astype(np.float32)
    cl = (pos % GRID_W).astype(np.float32)
    quarter = HEAD_DIM // 4
    inv = (np.float32(ROPE_THETA) ** (-np.arange(quarter, dtype=np.float32) / np.float32(quarter))).astype(np.float32)
    ang_r = r[:, None] * inv
    ang_c = cl[:, None] * inv
    ang = np.concatenate([ang_r, ang_r, ang_c, ang_c], axis=-1).astype(np.float32)
    return jnp.asarray(np.cos(ang), F32), jnp.asarray(np.sin(ang), F32)


def _kv_heads(h, w_in, q_width, k_norm, rope):
    assert 2 * ATT_KV_W == RET_QK_W == PROJ_COLS and RET_V_W == 2 * PROJ_COLS
    plan = (((0, ATT_KV_W, "norm_rope", 0, 1.0), (ATT_KV_W, 2 * ATT_KV_W, "plain", 1, 1.0)),
            ((0, RET_QK_W, "rope", 2, RET_QK_DIM ** -0.5),),
            ((0, PROJ_COLS, "plain", 3, 1.0),),
            ((0, PROJ_COLS, "plain", 3, 1.0),))
    outs = ((ATT_KV_W, ATT_KV_W, lambda j: 0), (ATT_KV_W, ATT_KV_W, lambda j: 0),
            (RET_QK_W, RET_QK_W, lambda j: 0), (RET_V_W, PROJ_COLS, lambda j: jnp.clip(j - 2, 0, 1)))
    return _proj_heads(h, w_in, q_width, plan, outs, k_norm, rope)


def _q_heads(h, w_in, q_norm, rope):
    assert ATT_W == 2 * PROJ_COLS and RET_QK_W == PROJ_COLS
    plan = (((0, PROJ_COLS, "norm_rope", 0, 1.0),), ((0, PROJ_COLS, "norm_rope", 0, 1.0),),
            ((0, RET_QK_W, "rope", 1, 1.0),))
    outs = ((ATT_W, PROJ_COLS, lambda j: jnp.minimum(j, 1)), (RET_QK_W, RET_QK_W, lambda j: 0))
    return _proj_heads(h, w_in, 0, plan, outs, q_norm, rope)


def kernel(x, c, ctx, c_ctx, w_mod, b_mod, pre_norm1, post_norm1, pre_norm2, post_norm2, w_in, q_norm,
           k_norm, ret_decay, ret_gn, w_o_att, w_o_ret, w_out, w_router, w_gate, w_up, w_down):
    b, t, d = x.shape
    n_ctx = ctx.shape[1]
    depth = w_mod.shape[0]
    q_width = ATT_W + RET_QK_W + RET_V_W + 2 * d
    cap = EC_FACTOR * t // N_EXPERTS
    rope_lat = _rope_tables(t)
    rope_ctx = (jnp.ones((b * n_ctx, HEAD_DIM), F32), jnp.zeros((b * n_ctx, HEAD_DIM), F32))
    cs = jnp.zeros((MOD_ROWS, d), F32).at[:b].set(c).at[b].set(c_ctx)
    xc = ctx
    for layer in range(depth):
        assert layer == depth - 1, "context-stream update between layers is not implemented"
        mod = _mod_vectors(cs, w_mod[layer], b_mod[layer])
        sh1, sc1, gt1, sh2, sc2, gt2 = [m[:b, None, :] for m in jnp.split(mod, 6, axis=-1)]
        csh1, csc1 = [jnp.broadcast_to(m[b][None, None, :], (b, 1, d)) for m in jnp.split(mod, 6, axis=-1)[:2]]
        dec = -jax.nn.softplus(ret_decay[layer].astype(F32))
        wl = w_in[layer]

        hc = _prenorm(xc, pre_norm1[layer], csh1, csc1).reshape(b * n_ctx, d)
        kc_a, vc_a, kc_r, vc_r = _kv_heads(hc, wl, q_width, k_norm[layer], rope_ctx)
        s0f, s0b = _ctx_states(dec, kc_r.reshape(b, n_ctx, RET_QK_W), vc_r.reshape(b, n_ctx, RET_V_W))

        h = _prenorm(x, pre_norm1[layer], sh1, sc1).reshape(b * t, d)
        q_a, q_r = _q_heads(h, wl, q_norm[layer], rope_lat)
        zg = _proj(h, wl, ATT_W + RET_QK_W, RET_V_W + 2 * d, out_dtype=F32, tn=PROJ_COLS)
        k_a, v_a, k_r, v_r = _kv_heads(h, wl, q_width, k_norm[layer], rope_lat)

        keys = jnp.concatenate([kc_a.reshape(b, n_ctx, ATT_KV_W), k_a.reshape(b, t, ATT_KV_W)], axis=1)
        vals = jnp.concatenate([vc_a.reshape(b, n_ctx, ATT_KV_W), v_a.reshape(b, t, ATT_KV_W)], axis=1)
        o_att = _attention(q_a.reshape(b, t, ATT_W), keys, vals)

        o_f, o_b = _retention(dec, q_r.reshape(b, t, RET_QK_W), k_r.reshape(b, t, RET_QK_W),
                              v_r.reshape(b, t, RET_V_W), s0f, s0b)

        y = _merge1(o_att.reshape(b * t, ATT_W), o_f.reshape(b * t, RET_V_W), o_b.reshape(b * t, RET_V_W),
                    zg, ret_gn[layer], w_o_att[layer].astype(BF16), w_o_ret[layer].astype(BF16))
        x1, h2, aff = _merge2(y, w_out[layer].astype(BF16), x, gt1, post_norm1[layer], pre_norm2[layer],
                              sh2, sc2, w_router[layer])

        pos, fill = _topk_slots(aff.transpose(0, 2, 1), cap)
        pos_t = pos.transpose(0, 2, 1)
        sparse, w0, row, base, win = _routing_tables(fill, cap)
        xg = lax.cond(sparse,
                      lambda: _gather_win(w0, pos, h2, cap),
                      lambda: _gather(pos, h2, cap))
        yg = _ffn(xg, w_gate[layer], w_up[layer], w_down[layer])
        x = lax.cond(sparse,
                     lambda: _combine_win(row, base, pos_t, aff, yg, x1, gt2, post_norm2[layer], win),
                     lambda: _combine(pos_t, aff, yg, x1, gt2, post_norm2[layer], cap))
    return x
```

```python
import functools

import jax
import jax.numpy as jnp
import numpy as np
from jax import lax
from jax.experimental import pallas as pl
from jax.experimental.pallas import tpu as pltpu

F32 = jnp.float32
BF16 = jnp.bfloat16

GRID_W = 64
HEAD_DIM = 128
ATT_HEADS = 8
ATT_KV_HEADS = 2
ATT_GROUP = ATT_HEADS // ATT_KV_HEADS
RET_HEADS = 4
RET_QK_DIM = 128
RET_V_DIM = 256
ROPE_THETA = 10000.0
N_EXPERTS = 16
EC_FACTOR = 2
EPS = 1e-6

ATT_W = ATT_HEADS * HEAD_DIM
ATT_KV_W = ATT_KV_HEADS * HEAD_DIM
RET_QK_W = RET_HEADS * RET_QK_DIM
RET_V_W = RET_HEADS * RET_V_DIM

LANES = 128
VMEM_LIMIT = 56 * 1024 * 1024
RET_CHUNK = 256
MIN_NORMAL_F32_BITS = 0x00800000
LOG2_E = 1.4426950408889634
TOK_TILE = 256
SLOT_BLK = 64
GATHER_WIN = SLOT_BLK + 16
COMBINE_GROUP = 4
MOD_ROWS = 16

MOD_COLS = 2048
PRENORM_ROWS = 1024
PROJ_ROWS = 2048
PROJ_COLS = 512
ATT_Q_ROWS = 256
MERGE1_ROWS = 256
GATE_COLS = 1024
MERGE2_ROWS = 512
FFN_ROWS = 2048
FFN_HIDDEN_COLS = 256
FFN_OUT_COLS = 512
GATHER_COLS = 512
DENSE_TOKENS = 1024
DENSE_COLS = 1024
DENSE_COMBINE_ROWS = 512


def _blk(pref, n):
    return pref if n % pref == 0 else n


def _params(*sem):
    return pltpu.CompilerParams(dimension_semantics=sem, vmem_limit_bytes=VMEM_LIMIT)


def _sigmoid(x):
    return 0.5 * jnp.tanh(0.5 * x) + 0.5


def _mod_kernel(s_ref, w_ref, b_ref, o_ref):
    s = s_ref[...]
    s = s * _sigmoid(s)
    rows = s.shape[0]
    s_hi, s_lo = _split_bf16(s)
    w_hi, w_lo = _split_bf16(w_ref[...])
    both = jnp.dot(jnp.concatenate([s_hi, s_lo], axis=0), w_hi, preferred_element_type=F32)
    o_ref[...] = both[:rows] + both[rows:] + jnp.dot(s_hi, w_lo, preferred_element_type=F32) + b_ref[...]


def _mod_vectors(cs, w_mod, b_mod):
    rows, d = cs.shape
    n = w_mod.shape[1]
    tn = _blk(MOD_COLS, n)
    return pl.pallas_call(
        _mod_kernel,
        grid=(n // tn,),
        in_specs=[pl.BlockSpec((rows, d), lambda j: (0, 0)),
                  pl.BlockSpec((d, tn), lambda j: (0, j)),
                  pl.BlockSpec((1, tn), lambda j: (0, j))],
        out_specs=pl.BlockSpec((rows, tn), lambda j: (0, j)),
        out_shape=jax.ShapeDtypeStruct((rows, n), F32),
        compiler_params=_params("parallel"),
        name="mod",
    )(cs, w_mod, b_mod.reshape(1, n))


def _prenorm_kernel(x_ref, g_ref, sh_ref, sc_ref, o_ref):
    x = x_ref[0]
    y = x * lax.rsqrt(jnp.mean(x * x, axis=-1, keepdims=True) + EPS) * g_ref[...]
    o_ref[0] = (y * (1.0 + sc_ref[0]) + sh_ref[0]).astype(o_ref.dtype)


def _prenorm(x, gain, shift, scale):
    b, t, d = x.shape
    tm = _blk(PRENORM_ROWS, t)
    return pl.pallas_call(
        _prenorm_kernel,
        grid=(b, t // tm),
        in_specs=[pl.BlockSpec((1, tm, d), lambda i, j: (i, j, 0)),
                  pl.BlockSpec((1, d), lambda i, j: (0, 0)),
                  pl.BlockSpec((1, 1, d), lambda i, j: (i, 0, 0)),
                  pl.BlockSpec((1, 1, d), lambda i, j: (i, 0, 0))],
        out_specs=pl.BlockSpec((1, tm, d), lambda i, j: (i, j, 0)),
        out_shape=jax.ShapeDtypeStruct((b, t, d), BF16),
        compiler_params=_params("parallel", "parallel"),
        name="prenorm",
    )(x, gain.reshape(1, d), shift, scale)


def _split_bf16(v):
    hi = v.astype(BF16)
    return [hi, (v - hi.astype(F32)).astype(BF16)]


def _rotate_half_matrix():
    quarter = HEAD_DIM // 4
    r = np.zeros((HEAD_DIM, HEAD_DIM), np.float32)
    for i in range(HEAD_DIM):
        if i % (2 * quarter) < quarter:
            r[i + quarter, i] = -1.0
        else:
            r[i - quarter, i] = 1.0
    return r


def _head_mix_matrix(mode):
    rot = _rotate_half_matrix()
    if mode == "rope":
        return jnp.asarray(np.concatenate([rot, rot], axis=0), BF16)
    ones = np.ones((HEAD_DIM, HEAD_DIM), np.float32)
    zero = np.zeros((HEAD_DIM, HEAD_DIM), np.float32)
    return jnp.asarray(np.block([[ones, zero], [ones, zero], [zero, rot], [zero, rot]]), BF16)


def _proj_kernel(h_ref, w_ref, o_ref):
    o_ref[...] = jnp.dot(h_ref[...], w_ref[...].astype(BF16), preferred_element_type=F32).astype(o_ref.dtype)


def _proj(h, w_in, col0, width, *, out_dtype, tn):
    r, d = h.shape
    tm = _blk(PROJ_ROWS, r)
    assert col0 % tn == 0 and width % tn == 0
    c0 = col0 // tn
    return pl.pallas_call(
        _proj_kernel,
        grid=(r // tm, width // tn),
        in_specs=[pl.BlockSpec((tm, d), lambda i, j: (i, 0)),
                  pl.BlockSpec((d, tn), lambda i, j: (0, c0 + j))],
        out_specs=pl.BlockSpec((tm, tn), lambda i, j: (i, j)),
        out_shape=jax.ShapeDtypeStruct((r, width), out_dtype),
        compiler_params=_params("parallel", "parallel"),
        name="proj_plain",
    )(h, w_in)


def _rope_head(xh, mode, gain, mix, cos, sin, pre_scale):
    if mode == "norm_rope":
        xg = xh * gain
        mixed = jnp.dot(jnp.concatenate(_split_bf16(xh * xh) + _split_bf16(xg), axis=1), mix,
                        preferred_element_type=F32)
        inv = lax.rsqrt(mixed[:, :HEAD_DIM] * (1.0 / HEAD_DIM) + EPS)
        return (xg * cos + mixed[:, HEAD_DIM:] * sin) * inv
    xs = xh * pre_scale if pre_scale != 1.0 else xh
    rot = jnp.dot(jnp.concatenate(_split_bf16(xs), axis=1), mix, preferred_element_type=F32)
    return xs * cos + rot * sin


def _proj_heads_kernel(h_ref, w_ref, gain_ref, mixn_ref, mixr_ref, cos_ref, sin_ref, *o_refs, plan):
    acc = jnp.dot(h_ref[...], w_ref[...].astype(BF16), preferred_element_type=F32)
    for jj, pieces in enumerate(plan):
        @pl.when(pl.program_id(1) == jj)
        def _(pieces=pieces):
            for lo, hi, mode, out, pre_scale in pieces:
                o_ref = o_refs[out]
                if mode == "plain":
                    o_ref[...] = acc[:, lo:hi].astype(o_ref.dtype)
                    continue
                mix = mixn_ref[...] if mode == "norm_rope" else mixr_ref[...]
                for hh in range((hi - lo) // HEAD_DIM):
                    xh = acc[:, lo + hh * HEAD_DIM:lo + (hh + 1) * HEAD_DIM]
                    res = _rope_head(xh, mode, gain_ref[...], mix, cos_ref[...], sin_ref[...], pre_scale)
                    o_ref[:, hh * HEAD_DIM:(hh + 1) * HEAD_DIM] = res.astype(o_ref.dtype)


def _proj_heads(h, w_in, col0, plan, outs, gain, rope):
    r, d = h.shape
    cos, sin = rope
    tm = min(_blk(PROJ_ROWS, r), cos.shape[0])
    tn = PROJ_COLS
    assert col0 % tn == 0
    c0 = col0 // tn
    period = cos.shape[0] // tm
    assert cos.shape == (period * tm, HEAD_DIM) and (r // tm) % period == 0
    mixn, mixr = _head_mix_matrix("norm_rope"), _head_mix_matrix("rope")
    const = lambda i, j: (0, 0)
    rows = lambda i, j: (i, 0)
    table = lambda i, j: (i % period, 0)
    return pl.pallas_call(
        functools.partial(_proj_heads_kernel, plan=plan),
        grid=(r // tm, len(plan)),
        in_specs=[pl.BlockSpec((tm, d), rows),
                  pl.BlockSpec((d, tn), lambda i, j: (0, c0 + j)),
                  pl.BlockSpec((1, HEAD_DIM), const),
                  pl.BlockSpec(mixn.shape, const),
                  pl.BlockSpec(mixr.shape, const),
                  pl.BlockSpec((tm, HEAD_DIM), table),
                  pl.BlockSpec((tm, HEAD_DIM), table)],
        out_specs=tuple(pl.BlockSpec((tm, bw), functools.partial(lambda i, j, f: (i, f(j)), f=f))
                        for _, bw, f in outs),
        out_shape=tuple(jax.ShapeDtypeStruct((r, width), BF16) for width, _, _ in outs),
        compiler_params=_params("parallel", "arbitrary"),
        name="proj_heads",
    )(h, w_in, gain.reshape(1, HEAD_DIM), mixn, mixr, cos, sin)


def _ctx_state_kernel(dec_ref, k_ref, v_ref, sf_ref, sb_ref):
    hh = pl.program_id(1)
    lf = dec_ref[0, hh]
    lb = dec_ref[1, hh]
    k = k_ref[0].astype(F32)
    v = v_ref[0]
    n = k.shape[0]
    pos = lax.broadcasted_iota(jnp.int32, k.shape, 0).astype(F32)
    kf = (k * jnp.exp((n - 1.0 - pos) * lf)).T.astype(BF16)
    kb = (k * jnp.exp(pos * lb)).T.astype(BF16)
    sf_ref[0, 0] = jnp.dot(kf, v, preferred_element_type=F32)
    sb_ref[0, 0] = jnp.dot(kb, v, preferred_element_type=F32)


def _ctx_states(dec, k_r, v_r):
    b, n, _ = k_r.shape
    spec_s = pl.BlockSpec((1, 1, RET_QK_DIM, RET_V_DIM), lambda i, j: (i, j, 0, 0))
    shape_s = jax.ShapeDtypeStruct((b, RET_HEADS, RET_QK_DIM, RET_V_DIM), F32)
    return pl.pallas_call(
        _ctx_state_kernel,
        grid=(b, RET_HEADS),
        in_specs=[pl.BlockSpec(memory_space=pltpu.SMEM),
                  pl.BlockSpec((1, n, RET_QK_DIM), lambda i, j: (i, 0, j)),
                  pl.BlockSpec((1, n, RET_V_DIM), lambda i, j: (i, 0, j))],
        out_specs=(spec_s, spec_s),
        out_shape=(shape_s, shape_s),
        compiler_params=_params("parallel", "parallel"),
        name="ctx_state",
    )(dec, k_r, v_r)


def _attn_kernel(q_ref, k_ref, v_ref, o_ref, *, scale):
    nt = (((1,), (1,)), ((), ()))
    for hk in range(ATT_KV_HEADS):
        k = k_ref[0, :, hk * HEAD_DIM:(hk + 1) * HEAD_DIM]
        v = v_ref[0, :, hk * HEAD_DIM:(hk + 1) * HEAD_DIM]
        for g in range(ATT_GROUP):
            cols = slice((hk * ATT_GROUP + g) * HEAD_DIM, (hk * ATT_GROUP + g + 1) * HEAD_DIM)
            s = lax.dot_general(q_ref[0, :, cols], k, nt, preferred_element_type=F32)
            m = jnp.max(s, axis=-1, keepdims=True)
            p = jnp.exp2((s - m) * (scale * LOG2_E))
            l = jnp.sum(p, axis=-1, keepdims=True)
            o = jnp.dot(p.astype(BF16), v, preferred_element_type=F32)
            o_ref[0, :, cols] = (o / l).astype(o_ref.dtype)


def _attention(q, k, v):
    b, t, _ = q.shape
    s = k.shape[1]
    tq = _blk(ATT_Q_ROWS, t)
    return pl.pallas_call(
        functools.partial(_attn_kernel, scale=HEAD_DIM ** -0.5),
        grid=(b, t // tq),
        in_specs=[pl.BlockSpec((1, tq, ATT_W), lambda i, n: (i, n, 0)),
                  pl.BlockSpec((1, s, ATT_KV_W), lambda i, n: (i, 0, 0)),
                  pl.BlockSpec((1, s, ATT_KV_W), lambda i, n: (i, 0, 0))],
        out_specs=pl.BlockSpec((1, tq, ATT_W), lambda i, n: (i, n, 0)),
        out_shape=jax.ShapeDtypeStruct((b, t, ATT_W), BF16),
        compiler_params=_params("parallel", "parallel"),
        name="attention",
    )(q, k, v)


def _ret_kernel(dec_ref, qf_ref, kf_ref, vf_ref, qb_ref, kb_ref, vb_ref, s0f_ref, s0b_ref,
                of_ref, ob_ref, sf_ref, sb_ref, mf_ref, mb_ref):
    n = pl.program_id(1)
    c = qf_ref.shape[1]

    @pl.when(n == 0)
    def _():
        sf_ref[...] = s0f_ref[0]
        sb_ref[...] = s0b_ref[0]
        ri = lax.broadcasted_iota(jnp.int32, (c, c), 0)
        ci = lax.broadcasted_iota(jnp.int32, (c, c), 1)
        d = (ri - ci).astype(F32)
        for hh in range(RET_HEADS):
            mf_ref[hh] = jnp.where(d >= 0, jnp.exp(jnp.maximum(d, 0.0) * dec_ref[0, hh]), 0.0)
            mb_ref[hh] = jnp.where(d <= 0, jnp.exp(jnp.maximum(-d, 0.0) * dec_ref[1, hh]), 0.0)

    nt = (((1,), (1,)), ((), ()))
    row_v = lax.broadcasted_iota(jnp.int32, (c, RET_V_DIM), 0).astype(F32)
    row_k = lax.broadcasted_iota(jnp.int32, (c, RET_QK_DIM), 0).astype(F32)

    def sweep(q_ref, k_ref, v_ref, o_ref, s_ref, m_ref, hh, lg, wq_age, wk_age):
        qk = slice(hh * RET_QK_DIM, (hh + 1) * RET_QK_DIM)
        vv = slice(hh * RET_V_DIM, (hh + 1) * RET_V_DIM)
        q = q_ref[0, :, qk]
        k = k_ref[0, :, qk]
        v = v_ref[0, :, vv]
        s = lax.dot_general(q, k, nt, preferred_element_type=F32) * m_ref[hh]
        o_in = jnp.dot(s.astype(BF16), v, preferred_element_type=F32)
        state = s_ref[hh]
        o_x = jnp.dot(q, state.astype(BF16), preferred_element_type=F32) * jnp.exp(wq_age * lg)
        o_ref[0, :, vv] = o_in + o_x
        kw = (k.astype(F32) * jnp.exp(wk_age * lg)).T.astype(BF16)
        g_chunk = jnp.exp(jnp.full((1, RET_V_DIM), c * lg, F32))
        s_ref[hh] = g_chunk * state + jnp.dot(kw, v, preferred_element_type=F32)

    for hh in range(RET_HEADS):
        sweep(qf_ref, kf_ref, vf_ref, of_ref, sf_ref, mf_ref, hh, dec_ref[0, hh], row_v + 1.0, c - 1.0 - row_k)
        sweep(qb_ref, kb_ref, vb_ref, ob_ref, sb_ref, mb_ref, hh, dec_ref[1, hh], c - row_v, row_k)


def _retention(dec, q, k, v, s0f, s0b):
    b, t, _ = q.shape
    c = _blk(RET_CHUNK, t)
    nc = t // c
    fwd = lambda i, n: (i, n, 0)
    bwd = lambda i, n: (i, nc - 1 - n, 0)
    spec_s = pl.BlockSpec((1, RET_HEADS, RET_QK_DIM, RET_V_DIM), lambda i, n: (i, 0, 0, 0))
    out_shape = jax.ShapeDtypeStruct((b, t, RET_V_W), F32)
    return pl.pallas_call(
        _ret_kernel,
        grid=(b, nc),
        in_specs=[pl.BlockSpec(memory_space=pltpu.SMEM),
                  pl.BlockSpec((1, c, RET_QK_W), fwd),
                  pl.BlockSpec((1, c, RET_QK_W), fwd),
                  pl.BlockSpec((1, c, RET_V_W), fwd),
                  pl.BlockSpec((1, c, RET_QK_W), bwd),
                  pl.BlockSpec((1, c, RET_QK_W), bwd),
                  pl.BlockSpec((1, c, RET_V_W), bwd),
                  spec_s, spec_s],
        out_specs=(pl.BlockSpec((1, c, RET_V_W), fwd), pl.BlockSpec((1, c, RET_V_W), bwd)),
        out_shape=(out_shape, out_shape),
        scratch_shapes=[pltpu.VMEM((RET_HEADS, RET_QK_DIM, RET_V_DIM), F32),
                        pltpu.VMEM((RET_HEADS, RET_QK_DIM, RET_V_DIM), F32),
                        pltpu.VMEM((RET_HEADS, c, c), F32),
                        pltpu.VMEM((RET_HEADS, c, c), F32)],
        compiler_params=_params("parallel", "arbitrary"),
        name="retention",
    )(dec, q, k, v, q, k, v, s0f, s0b)


def _merge1_kernel(oa_ref, of_ref, ob_ref, g_ref, *refs, nslab):
    ga_refs, gr_refs = refs[:nslab], refs[nslab:2 * nslab]
    gn_ref, wa_ref, wr_ref, y_ref = refs[2 * nslab:]
    o = of_ref[...] + ob_ref[...]
    heads = []
    for hh in range(RET_HEADS):
        sl = slice(hh * RET_V_DIM, (hh + 1) * RET_V_DIM)
        oh = o[:, sl]
        dlt = oh - jnp.mean(oh, axis=-1, keepdims=True)
        yh = dlt * lax.rsqrt(jnp.mean(dlt * dlt, axis=-1, keepdims=True) + EPS)
        g = g_ref[:, sl]
        heads.append((yh * gn_ref[:, sl] * (g * _sigmoid(g))).astype(BF16))
    o_ret = jnp.concatenate(heads, axis=1)
    tn = ga_refs[0].shape[1]
    for c in range(nslab):
        cols = slice(c * tn, (c + 1) * tn)
        ya = jnp.dot(oa_ref[...], wa_ref[:, cols], preferred_element_type=F32)
        yr = jnp.dot(o_ret, wr_ref[:, cols], preferred_element_type=F32)
        y_ref[:, cols] = (_sigmoid(ga_refs[c][...]) * ya + _sigmoid(gr_refs[c][...]) * yr).astype(y_ref.dtype)


def _merge1(o_att, o_f, o_b, zg, gn, wa, wr):
    r = o_att.shape[0]
    d = wa.shape[1]
    tm = _blk(MERGE1_ROWS, r)
    tn = _blk(GATE_COLS, d)
    nslab = d // tn
    ga0 = RET_V_W // tn
    gr0 = (RET_V_W + d) // tn
    row = lambda i: (i, 0)
    const = lambda i: (0, 0)
    gate_specs = [pl.BlockSpec((tm, tn), functools.partial(lambda i, c: (i, c), c=c0 + c))
                  for c0 in (ga0, gr0) for c in range(nslab)]
    return pl.pallas_call(
        functools.partial(_merge1_kernel, nslab=nslab),
        grid=(r // tm,),
        in_specs=[pl.BlockSpec((tm, ATT_W), row),
                  pl.BlockSpec((tm, RET_V_W), row),
                  pl.BlockSpec((tm, RET_V_W), row),
                  pl.BlockSpec((tm, RET_V_W), row)]
                 + gate_specs
                 + [pl.BlockSpec((1, RET_V_W), const),
                    pl.BlockSpec((ATT_W, d), const),
                    pl.BlockSpec((RET_V_W, d), const)],
        out_specs=pl.BlockSpec((tm, d), row),
        out_shape=jax.ShapeDtypeStruct((r, d), BF16),
        compiler_params=_params("parallel"),
        name="merge1",
    )(o_att, o_f, o_b, zg, *([zg] * (2 * nslab)), gn.reshape(1, RET_V_W), wa, wr)


def _merge2_kernel(y_ref, w_ref, x_ref, gt_ref, pn1_ref, pn2_ref, sh_ref, sc_ref, wr_ref,
                   x1_ref, h2_ref, aff_ref):
    yy = jnp.dot(y_ref[0], w_ref[...], preferred_element_type=F32)
    r = yy * lax.rsqrt(jnp.mean(yy * yy, axis=-1, keepdims=True) + EPS) * pn1_ref[...]
    x1 = x_ref[0] + gt_ref[0] * r
    x1_ref[0] = x1
    h2 = x1 * lax.rsqrt(jnp.mean(x1 * x1, axis=-1, keepdims=True) + EPS) * pn2_ref[...]
    h2 = h2 * (1.0 + sc_ref[0]) + sh_ref[0]
    h_hi = h2.astype(BF16)
    h2_ref[0] = h_hi
    h_lo = (h2 - h_hi.astype(F32)).astype(BF16)
    wr = wr_ref[...]
    w_hi = wr.astype(BF16)
    w_lo = (wr - w_hi.astype(F32)).astype(BF16)
    ne = wr.shape[1]
    r_hi = jnp.dot(h_hi, jnp.concatenate([w_hi, w_lo], axis=1), preferred_element_type=F32)
    logits = r_hi[:, :ne] + r_hi[:, ne:] + jnp.dot(h_lo, w_hi, preferred_element_type=F32)
    e = jnp.exp(logits - jnp.max(logits, axis=-1, keepdims=True))
    aff_ref[0] = e / jnp.sum(e, axis=-1, keepdims=True)


def _merge2(y, w_out, x, gt1, pn1, pn2, sh2, sc2, w_router):
    b, t, d = x.shape
    tm = _blk(MERGE2_ROWS, t)
    ne = w_router.shape[1]
    tile = lambda i, j: (i, j, 0)
    vec = lambda i, j: (0, 0)
    per_b = lambda i, j: (i, 0, 0)
    return pl.pallas_call(
        _merge2_kernel,
        grid=(b, t // tm),
        in_specs=[pl.BlockSpec((1, tm, d), tile),
                  pl.BlockSpec((d, d), vec),
                  pl.BlockSpec((1, tm, d), tile),
                  pl.BlockSpec((1, 1, d), per_b),
                  pl.BlockSpec((1, d), vec),
                  pl.BlockSpec((1, d), vec),
                  pl.BlockSpec((1, 1, d), per_b),
                  pl.BlockSpec((1, 1, d), per_b),
                  pl.BlockSpec((d, ne), vec)],
        out_specs=(pl.BlockSpec((1, tm, d), tile),
                   pl.BlockSpec((1, tm, d), tile),
                   pl.BlockSpec((1, tm, ne), tile)),
        out_shape=(jax.ShapeDtypeStruct((b, t, d), F32),
                   jax.ShapeDtypeStruct((b, t, d), BF16),
                   jax.ShapeDtypeStruct((b, t, ne), F32)),
        compiler_params=_params("parallel", "parallel"),
        name="merge2",
    )(y.reshape(b, t, d), w_out, x, gt1, pn1.reshape(1, d), pn2.reshape(1, d), sh2, sc2, w_router)


def _lane_cumsum(x, tri):
    off = jnp.zeros((x.shape[0], 1), F32)
    parts = []
    for cidx in range(x.shape[1] // LANES):
        xc = x[:, cidx * LANES:(cidx + 1) * LANES].astype(BF16)
        cs = jnp.dot(xc, tri, preferred_element_type=F32) + off
        parts.append(cs)
        off = cs[:, LANES - 1:LANES]
    return jnp.concatenate(parts, axis=1)


def _topk_kernel(a_ref, pos_ref, fill_ref, *, cap):
    a = a_ref[0]
    bits = jnp.zeros((a.shape[0], 1), jnp.int32)
    for bit in range(30, -1, -1):
        cand = bits | (1 << bit)
        cnt = jnp.sum(jnp.where(a >= lax.bitcast_convert_type(cand, F32), 1.0, 0.0), axis=1, keepdims=True)
        bits = jnp.where(cnt >= cap, cand, bits)
    bits = jnp.where(bits < MIN_NORMAL_F32_BITS, 0, bits)
    thr = lax.bitcast_convert_type(bits, F32)
    ri = lax.broadcasted_iota(jnp.int32, (LANES, LANES), 0)
    ci = lax.broadcasted_iota(jnp.int32, (LANES, LANES), 1)
    tri = jnp.where(ri <= ci, 1.0, 0.0).astype(BF16)
    gt = a > thr
    eq = a == thr
    n_gt = jnp.sum(jnp.where(gt, 1.0, 0.0), axis=1, keepdims=True)
    eq_rank = _lane_cumsum(jnp.where(eq, 1.0, 0.0), tri)
    sel = jnp.where(gt, 1.0, jnp.where(eq, jnp.where(eq_rank <= cap - n_gt, 1.0, 0.0), 0.0))
    filled = _lane_cumsum(sel, tri)
    pos_ref[0] = filled * sel
    fill_ref[0] = filled


def _topk_slots(aff_t, cap):
    b, ne, t = aff_t.shape
    spec = pl.BlockSpec((1, ne, t), lambda i: (i, 0, 0))
    shape = jax.ShapeDtypeStruct((b, ne, t), F32)
    return pl.pallas_call(
        functools.partial(_topk_kernel, cap=cap),
        grid=(b,),
        in_specs=[spec],
        out_specs=(spec, spec),
        out_shape=(shape, shape),
        compiler_params=_params("parallel"),
        name="topk",
    )(aff_t)


def _gather_kernel(pos_ref, h_ref, o_ref):
    cap = o_ref.shape[1]
    t = h_ref.shape[1]
    tc = _blk(DENSE_TOKENS, t)
    slot = lax.broadcasted_iota(jnp.int32, (cap, tc), 0).astype(F32) + 1.0
    acc = None
    for cidx in range(t // tc):
        row = pos_ref[0, 0, :, cidx * tc:(cidx + 1) * tc]
        onehot = jnp.where(row == slot, 1.0, 0.0).astype(BF16)
        part = jnp.dot(onehot, h_ref[0, cidx * tc:(cidx + 1) * tc, :], preferred_element_type=F32)
        acc = part if acc is None else acc + part
    o_ref[0] = acc.astype(o_ref.dtype)


def _gather(pos, h2, cap):
    b, ne, t = pos.shape
    d = h2.shape[-1]
    dn = _blk(DENSE_COLS, d)
    return pl.pallas_call(
        _gather_kernel,
        grid=(b, d // dn, ne),
        in_specs=[pl.BlockSpec((1, 1, 1, t), lambda i, n, e: (i, e, 0, 0)),
                  pl.BlockSpec((1, t, dn), lambda i, n, e: (i, 0, n))],
        out_specs=pl.BlockSpec((1, cap, dn), lambda i, n, e: (e, i, n)),
        out_shape=jax.ShapeDtypeStruct((ne, b * cap, d), BF16),
        compiler_params=_params("parallel", "parallel", "parallel"),
        name="gather",
    )(pos.reshape(b, ne, 1, t), h2)


def _gather_win_kernel(w0_ref, pos_ref, h_ref, o_ref, acc_ref):
    b = pl.program_id(0)
    ne, cap = o_ref.shape[0], o_ref.shape[1]
    nt = h_ref.shape[1] // TOK_TILE
    slot1 = lax.broadcasted_iota(jnp.int32, (GATHER_WIN, TOK_TILE), 0) + 1
    acc_ref[:, :8, :] = jnp.zeros((ne, 8) + acc_ref.shape[2:], F32)

    def tile(k, carry):
        t0 = pl.multiple_of(k * TOK_TILE, TOK_TILE)
        rows = []
        for e in range(ne):
            row = pos_ref[0, pl.ds(e * nt + k, 1), :]
            rows.append(jnp.where(row == (slot1 + w0_ref[(b * ne + e) * nt + k]).astype(F32), 1.0, 0.0))
        lhs = jnp.concatenate(rows, axis=0).astype(BF16)
        res = jnp.dot(lhs, h_ref[0, pl.ds(t0, TOK_TILE), :], preferred_element_type=F32)
        for e in range(ne):
            w0 = pl.multiple_of(w0_ref[(b * ne + e) * nt + k], 8)
            acc_ref[e, pl.ds(w0, 8), :] += res[e * GATHER_WIN:e * GATHER_WIN + 8]
            acc_ref[e, pl.ds(w0 + 8, GATHER_WIN - 8), :] = res[e * GATHER_WIN + 8:(e + 1) * GATHER_WIN]
        return carry
    lax.fori_loop(0, nt, tile, 0, unroll=8)
    o_ref[...] = acc_ref[:, :cap, :].astype(o_ref.dtype)


def _gather_win(w0, pos, h2, cap):
    b, ne, t = pos.shape
    d = h2.shape[-1]
    nt = t // TOK_TILE
    dn = _blk(GATHER_COLS, d)
    return pl.pallas_call(
        _gather_win_kernel,
        grid_spec=pltpu.PrefetchScalarGridSpec(
            num_scalar_prefetch=1,
            grid=(b, d // dn),
            in_specs=[pl.BlockSpec((1, ne * nt, TOK_TILE), lambda i, n, w: (i, 0, 0)),
                      pl.BlockSpec((1, t, dn), lambda i, n, w: (i, 0, n))],
            out_specs=pl.BlockSpec((ne, cap, dn), lambda i, n, w: (0, i, n)),
            scratch_shapes=[pltpu.VMEM((ne, cap + GATHER_WIN, dn), F32)]),
        out_shape=jax.ShapeDtypeStruct((ne, b * cap, d), BF16),
        compiler_params=_params("parallel", "arbitrary"),
        name="gather_win",
    )(w0, pos.reshape(b, ne * nt, TOK_TILE), h2)


def _ffn_kernel(x_ref, wg_ref, wu_ref, wd_ref, o_ref, h_ref, wgu_ref, *, nf):
    j = pl.program_id(2)
    tf = wg_ref.shape[2]

    @pl.when(j < nf)
    def _():
        wgu_ref[:, :tf] = wg_ref[0].astype(BF16)
        wgu_ref[:, tf:] = wu_ref[0].astype(BF16)
        au = jnp.dot(x_ref[0], wgu_ref[...], preferred_element_type=F32)
        a = au[:, :tf]
        h_ref[j] = (a * _sigmoid(a) * au[:, tf:]).astype(BF16)

    @pl.when(j >= nf)
    def _():
        acc = None
        for cidx in range(nf):
            part = jnp.dot(h_ref[cidx], wd_ref[0, cidx * tf:(cidx + 1) * tf, :].astype(BF16),
                           preferred_element_type=F32)
            acc = part if acc is None else acc + part
        o_ref[0] = acc.astype(o_ref.dtype)


def _ffn(xg, w_gate, w_up, w_down):
    ne, rows, d = xg.shape
    ff = w_gate.shape[-1]
    tm = _blk(FFN_ROWS, rows)
    tf = _blk(FFN_HIDDEN_COLS, ff)
    tn = _blk(FFN_OUT_COLS, d)
    nf = ff // tf
    return pl.pallas_call(
        functools.partial(_ffn_kernel, nf=nf),
        grid=(ne, rows // tm, nf + d // tn),
        in_specs=[pl.BlockSpec((1, tm, d), lambda e, m, j: (e, m, 0)),
                  pl.BlockSpec((1, d, tf), lambda e, m, j: (e, 0, jnp.minimum(j, nf - 1))),
                  pl.BlockSpec((1, d, tf), lambda e, m, j: (e, 0, jnp.minimum(j, nf - 1))),
                  pl.BlockSpec((1, ff, tn), lambda e, m, j: (e, 0, jnp.maximum(j - nf, 0)))],
        out_specs=pl.BlockSpec((1, tm, tn), lambda e, m, j: (e, m, jnp.maximum(j - nf, 0))),
        out_shape=jax.ShapeDtypeStruct((ne, rows, d), BF16),
        scratch_shapes=[pltpu.VMEM((nf, tm, tf), BF16), pltpu.VMEM((d, 2 * tf), BF16)],
        compiler_params=_params("parallel", "parallel", "arbitrary"),
        name="ffn",
    )(xg, w_gate, w_up, w_down)


def _combine_kernel(pos_ref, aff_ref, y_ref, x1_ref, gt_ref, pn_ref, o_ref, acc_ref):
    e = pl.program_id(2)
    tt, ne = pos_ref.shape[1], pos_ref.shape[2]
    cap = y_ref.shape[1]
    pick = lax.broadcasted_iota(jnp.int32, (tt, ne), 1) == e
    pcol = jnp.sum(jnp.where(pick, pos_ref[0], 0.0), axis=1, keepdims=True)
    acol = jnp.sum(jnp.where(pick, aff_ref[0], 0.0), axis=1, keepdims=True)
    slot = lax.broadcasted_iota(jnp.int32, (tt, cap), 1).astype(F32) + 1.0
    onehot = jnp.where(pcol == slot, 1.0, 0.0).astype(BF16)
    z = jnp.dot(onehot, y_ref[0], preferred_element_type=F32) * acol

    @pl.when(e == 0)
    def _():
        acc_ref[...] = z

    @pl.when(e > 0)
    def _():
        acc_ref[...] += z

    @pl.when(e == pl.num_programs(2) - 1)
    def _():
        y2 = acc_ref[...]
        r = y2 * lax.rsqrt(jnp.mean(y2 * y2, axis=-1, keepdims=True) + EPS) * pn_ref[...]
        o_ref[0] = x1_ref[0] + gt_ref[0] * r


def _combine(pos_t, aff, y, x1, gt2, pn2, cap):
    b, t, d = x1.shape
    ne = aff.shape[-1]
    tt = _blk(DENSE_COMBINE_ROWS, t)
    return pl.pallas_call(
        _combine_kernel,
        grid=(b, t // tt, ne),
        in_specs=[pl.BlockSpec((1, tt, ne), lambda i, j, e: (i, j, 0)),
                  pl.BlockSpec((1, tt, ne), lambda i, j, e: (i, j, 0)),
                  pl.BlockSpec((1, cap, d), lambda i, j, e: (e, i, 0)),
                  pl.BlockSpec((1, tt, d), lambda i, j, e: (i, j, 0)),
                  pl.BlockSpec((1, 1, d), lambda i, j, e: (i, 0, 0)),
                  pl.BlockSpec((1, d), lambda i, j, e: (0, 0))],
        out_specs=pl.BlockSpec((1, tt, d), lambda i, j, e: (i, j, 0)),
        out_shape=jax.ShapeDtypeStruct((b, t, d), F32),
        scratch_shapes=[pltpu.VMEM((tt, d), F32)],
        compiler_params=_params("parallel", "parallel", "arbitrary"),
        name="combine",
    )(pos_t, aff, y, x1, gt2, pn2.reshape(1, d))


def _spread_matrix(ne, w):
    s = np.zeros((5 * ne, 2 * ne * w), np.float32)
    for e in range(ne):
        s[e, e * w:(e + 1) * w] = SLOT_BLK
        s[ne + e, e * w:(e + 1) * w] = 1.0
        for part in (2, 3, 4):
            s[part * ne + e, (ne + e) * w:(ne + e + 1) * w] = 1.0
    return jnp.asarray(s, BF16)


def _combine_win_kernel(row_ref, base_ref, pos_ref, aff_ref, spread_ref, *refs):
    del row_ref
    ne = pos_ref.shape[2]
    y_refs = refs[:ne]
    x1_ref, gt_ref, pn_ref, o_ref = refs[ne:]
    b = pl.program_id(0)
    k = pl.program_id(1)
    nt = pl.num_programs(1)
    w = y_refs[0].shape[0]
    pos = pos_ref[0]
    aff = aff_ref[0]
    p_hi = jnp.floor(pos * (1.0 / SLOT_BLK))
    a_parts = _split_bf16(aff)
    a_parts.append((aff - a_parts[0].astype(F32) - a_parts[1].astype(F32)).astype(BF16))
    lhs = jnp.concatenate([p_hi.astype(BF16), (pos - SLOT_BLK * p_hi).astype(BF16)] + a_parts, axis=1)
    spread = jnp.dot(lhs, spread_ref[...], preferred_element_type=F32)
    lane = lax.broadcasted_iota(jnp.int32, (1, ne * w), 1)
    seg = lane // w
    seg_base = jnp.zeros((1, ne * w), jnp.int32)
    for e in range(ne):
        seg_base = jnp.where(seg == e, base_ref[(b * ne + e) * nt + k], seg_base)
    slot = (seg_base + lane % w + 1).astype(F32)
    onehot = jnp.where(spread[:, :ne * w] == slot, spread[:, ne * w:], 0.0).astype(BF16)
    y2 = None
    gw = COMBINE_GROUP * w
    for g in range(ne // COMBINE_GROUP):
        ywin = jnp.concatenate([r[...] for r in y_refs[COMBINE_GROUP * g:COMBINE_GROUP * (g + 1)]], axis=0)
        part = jnp.dot(onehot[:, g * gw:(g + 1) * gw], ywin, preferred_element_type=F32)
        y2 = part if y2 is None else y2 + part
    r = y2 * lax.rsqrt(jnp.mean(y2 * y2, axis=-1, keepdims=True) + EPS) * pn_ref[...]
    o_ref[0] = x1_ref[0] + gt_ref[0] * r


def _combine_win(row, base, pos_t, aff, y, x1, gt2, pn2, w):
    b, t, d = x1.shape
    ne = aff.shape[-1]
    nt = t // TOK_TILE

    def y_spec(e):
        return pl.BlockSpec((pl.Element(w), pl.Element(d)),
                            lambda i, k, row_ref, base_ref: (row_ref[(i * ne + e) * nt + k] * SLOT_BLK, 0))

    tile = lambda i, k, row_ref, base_ref: (i, k, 0)
    const = lambda i, k, row_ref, base_ref: (0, 0)
    spread = _spread_matrix(ne, w)
    return pl.pallas_call(
        _combine_win_kernel,
        grid_spec=pltpu.PrefetchScalarGridSpec(
            num_scalar_prefetch=2,
            grid=(b, nt),
            in_specs=[pl.BlockSpec((1, TOK_TILE, ne), tile),
                      pl.BlockSpec((1, TOK_TILE, ne), tile),
                      pl.BlockSpec(spread.shape, const)]
                     + [y_spec(e) for e in range(ne)]
                     + [pl.BlockSpec((1, TOK_TILE, d), tile),
                        pl.BlockSpec((1, 1, d), lambda i, k, row_ref, base_ref: (i, 0, 0)),
                        pl.BlockSpec((1, d), const)],
            out_specs=pl.BlockSpec((1, TOK_TILE, d), tile)),
        out_shape=jax.ShapeDtypeStruct((b, t, d), F32),
        compiler_params=_params("parallel", "parallel"),
        name="combine_win",
    )(row, base, pos_t, aff, spread, *([y.reshape(-1, d)] * ne), x1, gt2, pn2.reshape(1, d))


def _routing_tables(fill, cap):
    b, ne, t = fill.shape
    w = min(2 * SLOT_BLK, cap)
    ends = fill[:, :, TOK_TILE - 1::TOK_TILE].astype(jnp.int32)
    starts = jnp.concatenate([jnp.zeros((b, ne, 1), jnp.int32), ends[:, :, :-1]], axis=-1)
    sparse = jnp.all(ends - starts <= SLOT_BLK)
    base = jnp.minimum(starts // SLOT_BLK * SLOT_BLK, cap - w)
    group = (jnp.arange(ne)[None, :, None] * b + jnp.arange(b)[:, None, None]) * cap
    row_blk = (group + base) // SLOT_BLK
    return sparse, (starts // 8 * 8).reshape(-1), row_blk.reshape(-1), base.reshape(-1), w


def _rope_tables(t):
    pos = np.arange(t)
    r = (pos // GRID_W).astype(np.float32)
    cl = (pos % GRID_W).astype(np.float32)
    quarter = HEAD_DIM // 4
    inv = (np.float32(ROPE_THETA) ** (-np.arange(quarter, dtype=np.float32) / np.float32(quarter))).astype(np.float32)
    ang_r = r[:, None] * inv
    ang_c = cl[:, None] * inv
    ang = np.concatenate([ang_r, ang_r, ang_c, ang_c], axis=-1).astype(np.float32)
    return jnp.asarray(np.cos(ang), F32), jnp.asarray(np.sin(ang), F32)


def _kv_heads(h, w_in, q_width, k_norm, rope):
    assert 2 * ATT_KV_W == RET_QK_W == PROJ_COLS and RET_V_W == 2 * PROJ_COLS
    plan = (((0, ATT_KV_W, "norm_rope", 0, 1.0), (ATT_KV_W, 2 * ATT_KV_W, "plain", 1, 1.0)),
            ((0, RET_QK_W, "rope", 2, RET_QK_DIM ** -0.5),),
            ((0, PROJ_COLS, "plain", 3, 1.0),),
            ((0, PROJ_COLS, "plain", 3, 1.0),))
    outs = ((ATT_KV_W, ATT_KV_W, lambda j: 0), (ATT_KV_W, ATT_KV_W, lambda j: 0),
            (RET_QK_W, RET_QK_W, lambda j: 0), (RET_V_W, PROJ_COLS, lambda j: jnp.clip(j - 2, 0, 1)))
    return _proj_heads(h, w_in, q_width, plan, outs, k_norm, rope)


def _q_heads(h, w_in, q_norm, rope):
    assert ATT_W == 2 * PROJ_COLS and RET_QK_W == PROJ_COLS
    plan = (((0, PROJ_COLS, "norm_rope", 0, 1.0),), ((0, PROJ_COLS, "norm_rope", 0, 1.0),),
            ((0, RET_QK_W, "rope", 1, 1.0),))
    outs = ((ATT_W, PROJ_COLS, lambda j: jnp.minimum(j, 1)), (RET_QK_W, RET_QK_W, lambda j: 0))
    return _proj_heads(h, w_in, 0, plan, outs, q_norm, rope)


def kernel(x, c, ctx, c_ctx, w_mod, b_mod, pre_norm1, post_norm1, pre_norm2, post_norm2, w_in, q_norm,
           k_norm, ret_decay, ret_gn, w_o_att, w_o_ret, w_out, w_router, w_gate, w_up, w_down):
    b, t, d = x.shape
    n_ctx = ctx.shape[1]
    depth = w_mod.shape[0]
    q_width = ATT_W + RET_QK_W + RET_V_W + 2 * d
    cap = EC_FACTOR * t // N_EXPERTS
    rope_lat = _rope_tables(t)
    rope_ctx = (jnp.ones((b * n_ctx, HEAD_DIM), F32), jnp.zeros((b * n_ctx, HEAD_DIM), F32))
    cs = jnp.zeros((MOD_ROWS, d), F32).at[:b].set(c).at[b].set(c_ctx)
    xc = ctx
    for layer in range(depth):
        assert layer == depth - 1, "context-stream update between layers is not implemented"
        mod = _mod_vectors(cs, w_mod[layer], b_mod[layer])
        sh1, sc1, gt1, sh2, sc2, gt2 = [m[:b, None, :] for m in jnp.split(mod, 6, axis=-1)]
        csh1, csc1 = [jnp.broadcast_to(m[b][None, None, :], (b, 1, d)) for m in jnp.split(mod, 6, axis=-1)[:2]]
        dec = -jax.nn.softplus(ret_decay[layer].astype(F32))
        wl = w_in[layer]

        hc = _prenorm(xc, pre_norm1[layer], csh1, csc1).reshape(b * n_ctx, d)
        kc_a, vc_a, kc_r, vc_r = _kv_heads(hc, wl, q_width, k_norm[layer], rope_ctx)
        s0f, s0b = _ctx_states(dec, kc_r.reshape(b, n_ctx, RET_QK_W), vc_r.reshape(b, n_ctx, RET_V_W))

        h = _prenorm(x, pre_norm1[layer], sh1, sc1).reshape(b * t, d)
        q_a, q_r = _q_heads(h, wl, q_norm[layer], rope_lat)
        zg = _proj(h, wl, ATT_W + RET_QK_W, RET_V_W + 2 * d, out_dtype=F32, tn=PROJ_COLS)
        k_a, v_a, k_r, v_r = _kv_heads(h, wl, q_width, k_norm[layer], rope_lat)

        keys = jnp.concatenate([kc_a.reshape(b, n_ctx, ATT_KV_W), k_a.reshape(b, t, ATT_KV_W)], axis=1)
        vals = jnp.concatenate([vc_a.reshape(b, n_ctx, ATT_KV_W), v_a.reshape(b, t, ATT_KV_W)], axis=1)
        o_att = _attention(q_a.reshape(b, t, ATT_W), keys, vals)

        o_f, o_b = _retention(dec, q_r.reshape(b, t, RET_QK_W), k_r.reshape(b, t, RET_QK_W),
                              v_r.reshape(b, t, RET_V_W), s0f, s0b)

        y = _merge1(o_att.reshape(b * t, ATT_W), o_f.reshape(b * t, RET_V_W), o_b.reshape(b * t, RET_V_W),
                    zg, ret_gn[layer], w_o_att[layer].astype(BF16), w_o_ret[layer].astype(BF16))
        x1, h2, aff = _merge2(y, w_out[layer].astype(BF16), x, gt1, post_norm1[layer], pre_norm2[layer],
                              sh2, sc2, w_router[layer])

        pos, fill = _topk_slots(aff.transpose(0, 2, 1), cap)
        pos_t = pos.transpose(0, 2, 1)
        sparse, w0, row, base, win = _routing_tables(fill, cap)
        xg = lax.cond(sparse,
                      lambda: _gather_win(w0, pos, h2, cap),
                      lambda: _gather(pos, h2, cap))
        yg = _ffn(xg, w_gate[layer], w_up[layer], w_down[layer])
        x = lax.cond(sparse,
                     lambda: _combine_win(row, base, pos_t, aff, yg, x1, gt2, post_norm2[layer], win),
                     lambda: _combine(pos_t, aff, yg, x1, gt2, post_norm2[layer], cap))
    return x
```

```python
import functools

import jax
import jax.numpy as jnp
import numpy as np
from jax import lax
from jax.experimental import pallas as pl
from jax.experimental.pallas import tpu as pltpu

F32 = jnp.float32
BF16 = jnp.bfloat16

GRID_W = 64
HEAD_DIM = 128
ATT_HEADS = 8
ATT_KV_HEADS = 2
ATT_GROUP = ATT_HEADS // ATT_KV_HEADS
RET_HEADS = 4
RET_QK_DIM = 128
RET_V_DIM = 256
ROPE_THETA = 10000.0
N_EXPERTS = 16
EC_FACTOR = 2
EPS = 1e-6

ATT_W = ATT_HEADS * HEAD_DIM
ATT_KV_W = ATT_KV_HEADS * HEAD_DIM
RET_QK_W = RET_HEADS * RET_QK_DIM
RET_V_W = RET_HEADS * RET_V_DIM

LANES = 128
VMEM_LIMIT = 56 * 1024 * 1024
RET_CHUNK = 256
MIN_NORMAL_F32_BITS = 0x00800000
LOG2_E = 1.4426950408889634
TOK_TILE = 256
SLOT_BLK = 64
GATHER_WIN = SLOT_BLK + 8
COMBINE_GROUP = 4
MOD_ROWS = 16

MOD_COLS = 2048
PRENORM_ROWS = 1024
PROJ_ROWS = 2048
PROJ_COLS = 512
ATT_Q_ROWS = 256
MERGE1_ROWS = 256
GATE_COLS = 1024
MERGE2_ROWS = 512
FFN_ROWS = 2048
FFN_HIDDEN_COLS = 256
FFN_OUT_COLS = 512
GATHER_COLS = 512
DENSE_TOKENS = 1024
DENSE_COLS = 1024
DENSE_COMBINE_ROWS = 512


def _blk(pref, n):
    return pref if n % pref == 0 else n


def _params(*sem):
    return pltpu.CompilerParams(dimension_semantics=sem, vmem_limit_bytes=VMEM_LIMIT)


def _sigmoid(x):
    return 0.5 * jnp.tanh(0.5 * x) + 0.5


def _mod_kernel(s_ref, w_ref, b_ref, o_ref):
    s = s_ref[...]
    s = s * _sigmoid(s)
    rows = s.shape[0]
    s_hi, s_lo = _split_bf16(s)
    w_hi, w_lo = _split_bf16(w_ref[...])
    both = jnp.dot(jnp.concatenate([s_hi, s_lo], axis=0), w_hi, preferred_element_type=F32)
    o_ref[...] = both[:rows] + both[rows:] + jnp.dot(s_hi, w_lo, preferred_element_type=F32) + b_ref[...]


def _mod_vectors(cs, w_mod, b_mod):
    rows, d = cs.shape
    n = w_mod.shape[1]
    tn = _blk(MOD_COLS, n)
    return pl.pallas_call(
        _mod_kernel,
        grid=(n // tn,),
        in_specs=[pl.BlockSpec((rows, d), lambda j: (0, 0)),
                  pl.BlockSpec((d, tn), lambda j: (0, j)),
                  pl.BlockSpec((1, tn), lambda j: (0, j))],
        out_specs=pl.BlockSpec((rows, tn), lambda j: (0, j)),
        out_shape=jax.ShapeDtypeStruct((rows, n), F32),
        compiler_params=_params("parallel"),
        name="mod",
    )(cs, w_mod, b_mod.reshape(1, n))


def _prenorm_kernel(x_ref, g_ref, sh_ref, sc_ref, o_ref):
    x = x_ref[0]
    y = x * lax.rsqrt(jnp.mean(x * x, axis=-1, keepdims=True) + EPS) * g_ref[...]
    o_ref[0] = (y * (1.0 + sc_ref[0]) + sh_ref[0]).astype(o_ref.dtype)


def _prenorm(x, gain, shift, scale):
    b, t, d = x.shape
    tm = _blk(PRENORM_ROWS, t)
    return pl.pallas_call(
        _prenorm_kernel,
        grid=(b, t // tm),
        in_specs=[pl.BlockSpec((1, tm, d), lambda i, j: (i, j, 0)),
                  pl.BlockSpec((1, d), lambda i, j: (0, 0)),
                  pl.BlockSpec((1, 1, d), lambda i, j: (i, 0, 0)),
                  pl.BlockSpec((1, 1, d), lambda i, j: (i, 0, 0))],
        out_specs=pl.BlockSpec((1, tm, d), lambda i, j: (i, j, 0)),
        out_shape=jax.ShapeDtypeStruct((b, t, d), BF16),
        compiler_params=_params("parallel", "parallel"),
        name="prenorm",
    )(x, gain.reshape(1, d), shift, scale)


def _split_bf16(v):
    hi = v.astype(BF16)
    return [hi, (v - hi.astype(F32)).astype(BF16)]


def _rotate_half_matrix():
    quarter = HEAD_DIM // 4
    r = np.zeros((HEAD_DIM, HEAD_DIM), np.float32)
    for i in range(HEAD_DIM):
        if i % (2 * quarter) < quarter:
            r[i + quarter, i] = -1.0
        else:
            r[i - quarter, i] = 1.0
    return r


def _head_mix_matrix(mode):
    rot = _rotate_half_matrix()
    if mode == "rope":
        return jnp.asarray(np.concatenate([rot, rot], axis=0), BF16)
    ones = np.ones((HEAD_DIM, HEAD_DIM), np.float32)
    zero = np.zeros((HEAD_DIM, HEAD_DIM), np.float32)
    return jnp.asarray(np.block([[ones, zero], [ones, zero], [zero, rot], [zero, rot]]), BF16)


def _proj_kernel(h_ref, w_ref, o_ref):
    o_ref[...] = jnp.dot(h_ref[...], w_ref[...].astype(BF16), preferred_element_type=F32).astype(o_ref.dtype)


def _proj(h, w_in, col0, width, *, out_dtype, tn):
    r, d = h.shape
    tm = _blk(PROJ_ROWS, r)
    assert col0 % tn == 0 and width % tn == 0
    c0 = col0 // tn
    return pl.pallas_call(
        _proj_kernel,
        grid=(r // tm, width // tn),
        in_specs=[pl.BlockSpec((tm, d), lambda i, j: (i, 0)),
                  pl.BlockSpec((d, tn), lambda i, j: (0, c0 + j))],
        out_specs=pl.BlockSpec((tm, tn), lambda i, j: (i, j)),
        out_shape=jax.ShapeDtypeStruct((r, width), out_dtype),
        compiler_params=_params("parallel", "parallel"),
        name="proj_plain",
    )(h, w_in)


def _rope_head(xh, mode, gain, mix, cos, sin, pre_scale):
    if mode == "norm_rope":
        xg = xh * gain
        mixed = jnp.dot(jnp.concatenate(_split_bf16(xh * xh) + _split_bf16(xg), axis=1), mix,
                        preferred_element_type=F32)
        inv = lax.rsqrt(mixed[:, :HEAD_DIM] * (1.0 / HEAD_DIM) + EPS)
        return (xg * cos + mixed[:, HEAD_DIM:] * sin) * inv
    xs = xh * pre_scale if pre_scale != 1.0 else xh
    rot = jnp.dot(jnp.concatenate(_split_bf16(xs), axis=1), mix, preferred_element_type=F32)
    return xs * cos + rot * sin


def _proj_heads_kernel(h_ref, w_ref, gain_ref, mixn_ref, mixr_ref, cos_ref, sin_ref, *o_refs, plan):
    acc = jnp.dot(h_ref[...], w_ref[...].astype(BF16), preferred_element_type=F32)
    for jj, pieces in enumerate(plan):
        @pl.when(pl.program_id(1) == jj)
        def _(pieces=pieces):
            for lo, hi, mode, out, pre_scale in pieces:
                o_ref = o_refs[out]
                if mode == "plain":
                    o_ref[...] = acc[:, lo:hi].astype(o_ref.dtype)
                    continue
                mix = mixn_ref[...] if mode == "norm_rope" else mixr_ref[...]
                for hh in range((hi - lo) // HEAD_DIM):
                    xh = acc[:, lo + hh * HEAD_DIM:lo + (hh + 1) * HEAD_DIM]
                    res = _rope_head(xh, mode, gain_ref[...], mix, cos_ref[...], sin_ref[...], pre_scale)
                    o_ref[:, hh * HEAD_DIM:(hh + 1) * HEAD_DIM] = res.astype(o_ref.dtype)


def _proj_heads(h, w_in, col0, plan, outs, gain, rope):
    r, d = h.shape
    cos, sin = rope
    tm = min(_blk(PROJ_ROWS, r), cos.shape[0])
    tn = PROJ_COLS
    assert col0 % tn == 0
    c0 = col0 // tn
    period = cos.shape[0] // tm
    assert cos.shape == (period * tm, HEAD_DIM) and (r // tm) % period == 0
    mixn, mixr = _head_mix_matrix("norm_rope"), _head_mix_matrix("rope")
    const = lambda i, j: (0, 0)
    rows = lambda i, j: (i, 0)
    table = lambda i, j: (i % period, 0)
    return pl.pallas_call(
        functools.partial(_proj_heads_kernel, plan=plan),
        grid=(r // tm, len(plan)),
        in_specs=[pl.BlockSpec((tm, d), rows),
                  pl.BlockSpec((d, tn), lambda i, j: (0, c0 + j)),
                  pl.BlockSpec((1, HEAD_DIM), const),
                  pl.BlockSpec(mixn.shape, const),
                  pl.BlockSpec(mixr.shape, const),
                  pl.BlockSpec((tm, HEAD_DIM), table),
                  pl.BlockSpec((tm, HEAD_DIM), table)],
        out_specs=tuple(pl.BlockSpec((tm, bw), functools.partial(lambda i, j, f: (i, f(j)), f=f))
                        for _, bw, f in outs),
        out_shape=tuple(jax.ShapeDtypeStruct((r, width), BF16) for width, _, _ in outs),
        compiler_params=_params("parallel", "arbitrary"),
        name="proj_heads",
    )(h, w_in, gain.reshape(1, HEAD_DIM), mixn, mixr, cos, sin)


def _ctx_state_kernel(dec_ref, k_ref, v_ref, sf_ref, sb_ref):
    hh = pl.program_id(1)
    lf = dec_ref[0, hh]
    lb = dec_ref[1, hh]
    k = k_ref[0].astype(F32)
    v = v_ref[0]
    n = k.shape[0]
    pos = lax.broadcasted_iota(jnp.int32, k.shape, 0).astype(F32)
    kf = (k * jnp.exp((n - 1.0 - pos) * lf)).T.astype(BF16)
    kb = (k * jnp.exp(pos * lb)).T.astype(BF16)
    sf_ref[0, 0] = jnp.dot(kf, v, preferred_element_type=F32)
    sb_ref[0, 0] = jnp.dot(kb, v, preferred_element_type=F32)


def _ctx_states(dec, k_r, v_r):
    b, n, _ = k_r.shape
    spec_s = pl.BlockSpec((1, 1, RET_QK_DIM, RET_V_DIM), lambda i, j: (i, j, 0, 0))
    shape_s = jax.ShapeDtypeStruct((b, RET_HEADS, RET_QK_DIM, RET_V_DIM), F32)
    return pl.pallas_call(
        _ctx_state_kernel,
        grid=(b, RET_HEADS),
        in_specs=[pl.BlockSpec(memory_space=pltpu.SMEM),
                  pl.BlockSpec((1, n, RET_QK_DIM), lambda i, j: (i, 0, j)),
                  pl.BlockSpec((1, n, RET_V_DIM), lambda i, j: (i, 0, j))],
        out_specs=(spec_s, spec_s),
        out_shape=(shape_s, shape_s),
        compiler_params=_params("parallel", "parallel"),
        name="ctx_state",
    )(dec, k_r, v_r)


def _attn_kernel(q_ref, k_ref, v_ref, o_ref, *, scale):
    nt = (((1,), (1,)), ((), ()))
    for hk in range(ATT_KV_HEADS):
        k = k_ref[0, :, hk * HEAD_DIM:(hk + 1) * HEAD_DIM]
        v = v_ref[0, :, hk * HEAD_DIM:(hk + 1) * HEAD_DIM]
        for g in range(ATT_GROUP):
            cols = slice((hk * ATT_GROUP + g) * HEAD_DIM, (hk * ATT_GROUP + g + 1) * HEAD_DIM)
            s = lax.dot_general(q_ref[0, :, cols], k, nt, preferred_element_type=F32)
            m = jnp.max(s, axis=-1, keepdims=True)
            p = jnp.exp2((s - m) * (scale * LOG2_E))
            l = jnp.sum(p, axis=-1, keepdims=True)
            o = jnp.dot(p.astype(BF16), v, preferred_element_type=F32)
            o_ref[0, :, cols] = (o / l).astype(o_ref.dtype)


def _attention(q, k, v):
    b, t, _ = q.shape
    s = k.shape[1]
    tq = _blk(ATT_Q_ROWS, t)
    return pl.pallas_call(
        functools.partial(_attn_kernel, scale=HEAD_DIM ** -0.5),
        grid=(b, t // tq),
        in_specs=[pl.BlockSpec((1, tq, ATT_W), lambda i, n: (i, n, 0)),
                  pl.BlockSpec((1, s, ATT_KV_W), lambda i, n: (i, 0, 0)),
                  pl.BlockSpec((1, s, ATT_KV_W), lambda i, n: (i, 0, 0))],
        out_specs=pl.BlockSpec((1, tq, ATT_W), lambda i, n: (i, n, 0)),
        out_shape=jax.ShapeDtypeStruct((b, t, ATT_W), BF16),
        compiler_params=_params("parallel", "parallel"),
        name="attention",
    )(q, k, v)


def _ret_kernel(dec_ref, qf_ref, kf_ref, vf_ref, qb_ref, kb_ref, vb_ref, s0f_ref, s0b_ref,
                of_ref, ob_ref, sf_ref, sb_ref, mf_ref, mb_ref):
    n = pl.program_id(1)
    c = qf_ref.shape[1]

    @pl.when(n == 0)
    def _():
        sf_ref[...] = s0f_ref[0]
        sb_ref[...] = s0b_ref[0]
        ri = lax.broadcasted_iota(jnp.int32, (c, c), 0)
        ci = lax.broadcasted_iota(jnp.int32, (c, c), 1)
        d = (ri - ci).astype(F32)
        for hh in range(RET_HEADS):
            mf_ref[hh] = jnp.where(d >= 0, jnp.exp(jnp.maximum(d, 0.0) * dec_ref[0, hh]), 0.0)
            mb_ref[hh] = jnp.where(d <= 0, jnp.exp(jnp.maximum(-d, 0.0) * dec_ref[1, hh]), 0.0)

    nt = (((1,), (1,)), ((), ()))
    row_v = lax.broadcasted_iota(jnp.int32, (c, RET_V_DIM), 0).astype(F32)
    row_k = lax.broadcasted_iota(jnp.int32, (c, RET_QK_DIM), 0).astype(F32)

    def sweep(q_ref, k_ref, v_ref, o_ref, s_ref, m_ref, hh, lg, wq_age, wk_age):
        qk = slice(hh * RET_QK_DIM, (hh + 1) * RET_QK_DIM)
        vv = slice(hh * RET_V_DIM, (hh + 1) * RET_V_DIM)
        q = q_ref[0, :, qk]
        k = k_ref[0, :, qk]
        v = v_ref[0, :, vv]
        s = lax.dot_general(q, k, nt, preferred_element_type=F32) * m_ref[hh]
        o_in = jnp.dot(s.astype(BF16), v, preferred_element_type=F32)
        state = s_ref[hh]
        o_x = jnp.dot(q, state.astype(BF16), preferred_element_type=F32) * jnp.exp(wq_age * lg)
        o_ref[0, :, vv] = o_in + o_x
        kw = (k.astype(F32) * jnp.exp(wk_age * lg)).T.astype(BF16)
        g_chunk = jnp.exp(jnp.full((1, RET_V_DIM), c * lg, F32))
        s_ref[hh] = g_chunk * state + jnp.dot(kw, v, preferred_element_type=F32)

    for hh in range(RET_HEADS):
        sweep(qf_ref, kf_ref, vf_ref, of_ref, sf_ref, mf_ref, hh, dec_ref[0, hh], row_v + 1.0, c - 1.0 - row_k)
        sweep(qb_ref, kb_ref, vb_ref, ob_ref, sb_ref, mb_ref, hh, dec_ref[1, hh], c - row_v, row_k)


def _retention(dec, q, k, v, s0f, s0b):
    b, t, _ = q.shape
    c = _blk(RET_CHUNK, t)
    nc = t // c
    fwd = lambda i, n: (i, n, 0)
    bwd = lambda i, n: (i, nc - 1 - n, 0)
    spec_s = pl.BlockSpec((1, RET_HEADS, RET_QK_DIM, RET_V_DIM), lambda i, n: (i, 0, 0, 0))
    out_shape = jax.ShapeDtypeStruct((b, t, RET_V_W), F32)
    return pl.pallas_call(
        _ret_kernel,
        grid=(b, nc),
        in_specs=[pl.BlockSpec(memory_space=pltpu.SMEM),
                  pl.BlockSpec((1, c, RET_QK_W), fwd),
                  pl.BlockSpec((1, c, RET_QK_W), fwd),
                  pl.BlockSpec((1, c, RET_V_W), fwd),
                  pl.BlockSpec((1, c, RET_QK_W), bwd),
                  pl.BlockSpec((1, c, RET_QK_W), bwd),
                  pl.BlockSpec((1, c, RET_V_W), bwd),
                  spec_s, spec_s],
        out_specs=(pl.BlockSpec((1, c, RET_V_W), fwd), pl.BlockSpec((1, c, RET_V_W), bwd)),
        out_shape=(out_shape, out_shape),
        scratch_shapes=[pltpu.VMEM((RET_HEADS, RET_QK_DIM, RET_V_DIM), F32),
                        pltpu.VMEM((RET_HEADS, RET_QK_DIM, RET_V_DIM), F32),
                        pltpu.VMEM((RET_HEADS, c, c), F32),
                        pltpu.VMEM((RET_HEADS, c, c), F32)],
        compiler_params=_params("parallel", "arbitrary"),
        name="retention",
    )(dec, q, k, v, q, k, v, s0f, s0b)


def _merge1_kernel(oa_ref, of_ref, ob_ref, g_ref, *refs, nslab):
    ga_refs, gr_refs = refs[:nslab], refs[nslab:2 * nslab]
    gn_ref, wa_ref, wr_ref, y_ref = refs[2 * nslab:]
    o = of_ref[...] + ob_ref[...]
    heads = []
    for hh in range(RET_HEADS):
        sl = slice(hh * RET_V_DIM, (hh + 1) * RET_V_DIM)
        oh = o[:, sl]
        dlt = oh - jnp.mean(oh, axis=-1, keepdims=True)
        yh = dlt * lax.rsqrt(jnp.mean(dlt * dlt, axis=-1, keepdims=True) + EPS)
        g = g_ref[:, sl]
        heads.append((yh * gn_ref[:, sl] * (g * _sigmoid(g))).astype(BF16))
    o_ret = jnp.concatenate(heads, axis=1)
    tn = ga_refs[0].shape[1]
    for c in range(nslab):
        cols = slice(c * tn, (c + 1) * tn)
        ya = jnp.dot(oa_ref[...], wa_ref[:, cols], preferred_element_type=F32)
        yr = jnp.dot(o_ret, wr_ref[:, cols], preferred_element_type=F32)
        y_ref[:, cols] = (_sigmoid(ga_refs[c][...]) * ya + _sigmoid(gr_refs[c][...]) * yr).astype(y_ref.dtype)


def _merge1(o_att, o_f, o_b, zg, gn, wa, wr):
    r = o_att.shape[0]
    d = wa.shape[1]
    tm = _blk(MERGE1_ROWS, r)
    tn = _blk(GATE_COLS, d)
    nslab = d // tn
    ga0 = RET_V_W // tn
    gr0 = (RET_V_W + d) // tn
    row = lambda i: (i, 0)
    const = lambda i: (0, 0)
    gate_specs = [pl.BlockSpec((tm, tn), functools.partial(lambda i, c: (i, c), c=c0 + c))
                  for c0 in (ga0, gr0) for c in range(nslab)]
    return pl.pallas_call(
        functools.partial(_merge1_kernel, nslab=nslab),
        grid=(r // tm,),
        in_specs=[pl.BlockSpec((tm, ATT_W), row),
                  pl.BlockSpec((tm, RET_V_W), row),
                  pl.BlockSpec((tm, RET_V_W), row),
                  pl.BlockSpec((tm, RET_V_W), row)]
                 + gate_specs
                 + [pl.BlockSpec((1, RET_V_W), const),
                    pl.BlockSpec((ATT_W, d), const),
                    pl.BlockSpec((RET_V_W, d), const)],
        out_specs=pl.BlockSpec((tm, d), row),
        out_shape=jax.ShapeDtypeStruct((r, d), BF16),
        compiler_params=_params("parallel"),
        name="merge1",
    )(o_att, o_f, o_b, zg, *([zg] * (2 * nslab)), gn.reshape(1, RET_V_W), wa, wr)


def _merge2_kernel(y_ref, w_ref, x_ref, gt_ref, pn1_ref, pn2_ref, sh_ref, sc_ref, wr_ref,
                   x1_ref, h2_ref, aff_ref):
    yy = jnp.dot(y_ref[0], w_ref[...], preferred_element_type=F32)
    r = yy * lax.rsqrt(jnp.mean(yy * yy, axis=-1, keepdims=True) + EPS) * pn1_ref[...]
    x1 = x_ref[0] + gt_ref[0] * r
    x1_ref[0] = x1
    h2 = x1 * lax.rsqrt(jnp.mean(x1 * x1, axis=-1, keepdims=True) + EPS) * pn2_ref[...]
    h2 = h2 * (1.0 + sc_ref[0]) + sh_ref[0]
    h_hi = h2.astype(BF16)
    h2_ref[0] = h_hi
    h_lo = (h2 - h_hi.astype(F32)).astype(BF16)
    wr = wr_ref[...]
    w_hi = wr.astype(BF16)
    w_lo = (wr - w_hi.astype(F32)).astype(BF16)
    ne = wr.shape[1]
    r_hi = jnp.dot(h_hi, jnp.concatenate([w_hi, w_lo], axis=1), preferred_element_type=F32)
    logits = r_hi[:, :ne] + r_hi[:, ne:] + jnp.dot(h_lo, w_hi, preferred_element_type=F32)
    e = jnp.exp(logits - jnp.max(logits, axis=-1, keepdims=True))
    aff_ref[0] = e / jnp.sum(e, axis=-1, keepdims=True)


def _merge2(y, w_out, x, gt1, pn1, pn2, sh2, sc2, w_router):
    b, t, d = x.shape
    tm = _blk(MERGE2_ROWS, t)
    ne = w_router.shape[1]
    tile = lambda i, j: (i, j, 0)
    vec = lambda i, j: (0, 0)
    per_b = lambda i, j: (i, 0, 0)
    return pl.pallas_call(
        _merge2_kernel,
        grid=(b, t // tm),
        in_specs=[pl.BlockSpec((1, tm, d), tile),
                  pl.BlockSpec((d, d), vec),
                  pl.BlockSpec((1, tm, d), tile),
                  pl.BlockSpec((1, 1, d), per_b),
                  pl.BlockSpec((1, d), vec),
                  pl.BlockSpec((1, d), vec),
                  pl.BlockSpec((1, 1, d), per_b),
                  pl.BlockSpec((1, 1, d), per_b),
                  pl.BlockSpec((d, ne), vec)],
        out_specs=(pl.BlockSpec((1, tm, d), tile),
                   pl.BlockSpec((1, tm, d), tile),
                   pl.BlockSpec((1, tm, ne), tile)),
        out_shape=(jax.ShapeDtypeStruct((b, t, d), F32),
                   jax.ShapeDtypeStruct((b, t, d), BF16),
                   jax.ShapeDtypeStruct((b, t, ne), F32)),
        compiler_params=_params("parallel", "parallel"),
        name="merge2",
    )(y.reshape(b, t, d), w_out, x, gt1, pn1.reshape(1, d), pn2.reshape(1, d), sh2, sc2, w_router)


def _lane_cumsum(x, tri):
    off = jnp.zeros((x.shape[0], 1), F32)
    parts = []
    for cidx in range(x.shape[1] // LANES):
        xc = x[:, cidx * LANES:(cidx + 1) * LANES].astype(BF16)
        cs = jnp.dot(xc, tri, preferred_element_type=F32) + off
        parts.append(cs)
        off = cs[:, LANES - 1:LANES]
    return jnp.concatenate(parts, axis=1)


def _topk_kernel(a_ref, pos_ref, fill_ref, *, cap):
    a = a_ref[0]
    bits = jnp.zeros((a.shape[0], 1), jnp.int32)
    for bit in range(30, -1, -1):
        cand = bits | (1 << bit)
        cnt = jnp.sum(jnp.where(a >= lax.bitcast_convert_type(cand, F32), 1.0, 0.0), axis=1, keepdims=True)
        bits = jnp.where(cnt >= cap, cand, bits)
    bits = jnp.where(bits < MIN_NORMAL_F32_BITS, 0, bits)
    thr = lax.bitcast_convert_type(bits, F32)
    ri = lax.broadcasted_iota(jnp.int32, (LANES, LANES), 0)
    ci = lax.broadcasted_iota(jnp.int32, (LANES, LANES), 1)
    tri = jnp.where(ri <= ci, 1.0, 0.0).astype(BF16)
    gt = a > thr
    eq = a == thr
    n_gt = jnp.sum(jnp.where(gt, 1.0, 0.0), axis=1, keepdims=True)
    eq_rank = _lane_cumsum(jnp.where(eq, 1.0, 0.0), tri)
    sel = jnp.where(gt, 1.0, jnp.where(eq, jnp.where(eq_rank <= cap - n_gt, 1.0, 0.0), 0.0))
    filled = _lane_cumsum(sel, tri)
    pos_ref[0] = filled * sel
    fill_ref[0] = filled


def _topk_slots(aff_t, cap):
    b, ne, t = aff_t.shape
    spec = pl.BlockSpec((1, ne, t), lambda i: (i, 0, 0))
    shape = jax.ShapeDtypeStruct((b, ne, t), F32)
    return pl.pallas_call(
        functools.partial(_topk_kernel, cap=cap),
        grid=(b,),
        in_specs=[spec],
        out_specs=(spec, spec),
        out_shape=(shape, shape),
        compiler_params=_params("parallel"),
        name="topk",
    )(aff_t)


def _gather_kernel(pos_ref, h_ref, o_ref):
    cap = o_ref.shape[1]
    t = h_ref.shape[1]
    tc = _blk(DENSE_TOKENS, t)
    slot = lax.broadcasted_iota(jnp.int32, (cap, tc), 0).astype(F32) + 1.0
    acc = None
    for cidx in range(t // tc):
        row = pos_ref[0, 0, :, cidx * tc:(cidx + 1) * tc]
        onehot = jnp.where(row == slot, 1.0, 0.0).astype(BF16)
        part = jnp.dot(onehot, h_ref[0, cidx * tc:(cidx + 1) * tc, :], preferred_element_type=F32)
        acc = part if acc is None else acc + part
    o_ref[0] = acc.astype(o_ref.dtype)


def _gather(pos, h2, cap):
    b, ne, t = pos.shape
    d = h2.shape[-1]
    dn = _blk(DENSE_COLS, d)
    return pl.pallas_call(
        _gather_kernel,
        grid=(b, d // dn, ne),
        in_specs=[pl.BlockSpec((1, 1, 1, t), lambda i, n, e: (i, e, 0, 0)),
                  pl.BlockSpec((1, t, dn), lambda i, n, e: (i, 0, n))],
        out_specs=pl.BlockSpec((1, cap, dn), lambda i, n, e: (e, i, n)),
        out_shape=jax.ShapeDtypeStruct((ne, b * cap, d), BF16),
        compiler_params=_params("parallel", "parallel", "parallel"),
        name="gather",
    )(pos.reshape(b, ne, 1, t), h2)


def _gather_win_kernel(w0_ref, pos_ref, h_ref, o_ref, acc_ref):
    b = pl.program_id(0)
    ne, cap = o_ref.shape[0], o_ref.shape[1]
    nt = h_ref.shape[1] // TOK_TILE
    slot1 = lax.broadcasted_iota(jnp.int32, (GATHER_WIN, TOK_TILE), 0) + 1
    acc_ref[:, :8, :] = jnp.zeros((ne, 8) + acc_ref.shape[2:], F32)

    def tile(k, carry):
        t0 = pl.multiple_of(k * TOK_TILE, TOK_TILE)
        rows = []
        for e in range(ne):
            row = pos_ref[0, pl.ds(e * nt + k, 1), :]
            rows.append(jnp.where(row == (slot1 + w0_ref[(b * ne + e) * nt + k]).astype(F32), 1.0, 0.0))
        lhs = jnp.concatenate(rows, axis=0).astype(BF16)
        res = jnp.dot(lhs, h_ref[0, pl.ds(t0, TOK_TILE), :], preferred_element_type=F32)
        for e in range(ne):
            w0 = pl.multiple_of(w0_ref[(b * ne + e) * nt + k], 8)
            acc_ref[e, pl.ds(w0, 8), :] += res[e * GATHER_WIN:e * GATHER_WIN + 8]
            acc_ref[e, pl.ds(w0 + 8, GATHER_WIN - 8), :] = res[e * GATHER_WIN + 8:(e + 1) * GATHER_WIN]
        return carry
    lax.fori_loop(0, nt, tile, 0, unroll=8)
    o_ref[...] = acc_ref[:, :cap, :].astype(o_ref.dtype)


def _gather_win(w0, pos, h2, cap):
    b, ne, t = pos.shape
    d = h2.shape[-1]
    nt = t // TOK_TILE
    dn = _blk(GATHER_COLS, d)
    return pl.pallas_call(
        _gather_win_kernel,
        grid_spec=pltpu.PrefetchScalarGridSpec(
            num_scalar_prefetch=1,
            grid=(b, d // dn),
            in_specs=[pl.BlockSpec((1, ne * nt, TOK_TILE), lambda i, n, w: (i, 0, 0)),
                      pl.BlockSpec((1, t, dn), lambda i, n, w: (i, 0, n))],
            out_specs=pl.BlockSpec((ne, cap, dn), lambda i, n, w: (0, i, n)),
            scratch_shapes=[pltpu.VMEM((ne, cap + GATHER_WIN, dn), F32)]),
        out_shape=jax.ShapeDtypeStruct((ne, b * cap, d), BF16),
        compiler_params=_params("parallel", "arbitrary"),
        name="gather_win",
    )(w0, pos.reshape(b, ne * nt, TOK_TILE), h2)


def _ffn_kernel(x_ref, wg_ref, wu_ref, wd_ref, o_ref, h_ref, wgu_ref, *, nf):
    j = pl.program_id(2)
    tf = wg_ref.shape[2]

    @pl.when(j < nf)
    def _():
        wgu_ref[:, :tf] = wg_ref[0].astype(BF16)
        wgu_ref[:, tf:] = wu_ref[0].astype(BF16)
        au = jnp.dot(x_ref[0], wgu_ref[...], preferred_element_type=F32)
        a = au[:, :tf]
        h_ref[j] = (a * _sigmoid(a) * au[:, tf:]).astype(BF16)

    @pl.when(j >= nf)
    def _():
        acc = None
        for cidx in range(nf):
            part = jnp.dot(h_ref[cidx], wd_ref[0, cidx * tf:(cidx + 1) * tf, :].astype(BF16),
                           preferred_element_type=F32)
            acc = part if acc is None else acc + part
        o_ref[0] = acc.astype(o_ref.dtype)


def _ffn(xg, w_gate, w_up, w_down):
    ne, rows, d = xg.shape
    ff = w_gate.shape[-1]
    tm = _blk(FFN_ROWS, rows)
    tf = _blk(FFN_HIDDEN_COLS, ff)
    tn = _blk(FFN_OUT_COLS, d)
    nf = ff // tf
    return pl.pallas_call(
        functools.partial(_ffn_kernel, nf=nf),
        grid=(ne, rows // tm, nf + d // tn),
        in_specs=[pl.BlockSpec((1, tm, d), lambda e, m, j: (e, m, 0)),
                  pl.BlockSpec((1, d, tf), lambda e, m, j: (e, 0, jnp.minimum(j, nf - 1))),
                  pl.BlockSpec((1, d, tf), lambda e, m, j: (e, 0, jnp.minimum(j, nf - 1))),
                  pl.BlockSpec((1, ff, tn), lambda e, m, j: (e, 0, jnp.maximum(j - nf, 0)))],
        out_specs=pl.BlockSpec((1, tm, tn), lambda e, m, j: (e, m, jnp.maximum(j - nf, 0))),
        out_shape=jax.ShapeDtypeStruct((ne, rows, d), BF16),
        scratch_shapes=[pltpu.VMEM((nf, tm, tf), BF16), pltpu.VMEM((d, 2 * tf), BF16)],
        compiler_params=_params("parallel", "parallel", "arbitrary"),
        name="ffn",
    )(xg, w_gate, w_up, w_down)


def _combine_kernel(pos_ref, aff_ref, y_ref, x1_ref, gt_ref, pn_ref, o_ref, acc_ref):
    e = pl.program_id(2)
    tt, ne = pos_ref.shape[1], pos_ref.shape[2]
    cap = y_ref.shape[1]
    pick = lax.broadcasted_iota(jnp.int32, (tt, ne), 1) == e
    pcol = jnp.sum(jnp.where(pick, pos_ref[0], 0.0), axis=1, keepdims=True)
    acol = jnp.sum(jnp.where(pick, aff_ref[0], 0.0), axis=1, keepdims=True)
    slot = lax.broadcasted_iota(jnp.int32, (tt, cap), 1).astype(F32) + 1.0
    onehot = jnp.where(pcol == slot, 1.0, 0.0).astype(BF16)
    z = jnp.dot(onehot, y_ref[0], preferred_element_type=F32) * acol

    @pl.when(e == 0)
    def _():
        acc_ref[...] = z

    @pl.when(e > 0)
    def _():
        acc_ref[...] += z

    @pl.when(e == pl.num_programs(2) - 1)
    def _():
        y2 = acc_ref[...]
        r = y2 * lax.rsqrt(jnp.mean(y2 * y2, axis=-1, keepdims=True) + EPS) * pn_ref[...]
        o_ref[0] = x1_ref[0] + gt_ref[0] * r


def _combine(pos_t, aff, y, x1, gt2, pn2, cap):
    b, t, d = x1.shape
    ne = aff.shape[-1]
    tt = _blk(DENSE_COMBINE_ROWS, t)
    return pl.pallas_call(
        _combine_kernel,
        grid=(b, t // tt, ne),
        in_specs=[pl.BlockSpec((1, tt, ne), lambda i, j, e: (i, j, 0)),
                  pl.BlockSpec((1, tt, ne), lambda i, j, e: (i, j, 0)),
                  pl.BlockSpec((1, cap, d), lambda i, j, e: (e, i, 0)),
                  pl.BlockSpec((1, tt, d), lambda i, j, e: (i, j, 0)),
                  pl.BlockSpec((1, 1, d), lambda i, j, e: (i, 0, 0)),
                  pl.BlockSpec((1, d), lambda i, j, e: (0, 0))],
        out_specs=pl.BlockSpec((1, tt, d), lambda i, j, e: (i, j, 0)),
        out_shape=jax.ShapeDtypeStruct((b, t, d), F32),
        scratch_shapes=[pltpu.VMEM((tt, d), F32)],
        compiler_params=_params("parallel", "parallel", "arbitrary"),
        name="combine",
    )(pos_t, aff, y, x1, gt2, pn2.reshape(1, d))


def _spread_matrix(ne, w):
    s = np.zeros((5 * ne, 2 * ne * w), np.float32)
    for e in range(ne):
        s[e, e * w:(e + 1) * w] = SLOT_BLK
        s[ne + e, e * w:(e + 1) * w] = 1.0
        for part in (2, 3, 4):
            s[part * ne + e, (ne + e) * w:(ne + e + 1) * w] = 1.0
    return jnp.asarray(s, BF16)


def _combine_win_kernel(row_ref, base_ref, pos_ref, aff_ref, spread_ref, *refs):
    del row_ref
    ne = pos_ref.shape[2]
    y_refs = refs[:ne]
    x1_ref, gt_ref, pn_ref, o_ref = refs[ne:]
    b = pl.program_id(0)
    k = pl.program_id(1)
    nt = pl.num_programs(1)
    w = y_refs[0].shape[0]
    pos = pos_ref[0]
    aff = aff_ref[0]
    p_hi = jnp.floor(pos * (1.0 / SLOT_BLK))
    a_parts = _split_bf16(aff)
    a_parts.append((aff - a_parts[0].astype(F32) - a_parts[1].astype(F32)).astype(BF16))
    lhs = jnp.concatenate([p_hi.astype(BF16), (pos - SLOT_BLK * p_hi).astype(BF16)] + a_parts, axis=1)
    spread = jnp.dot(lhs, spread_ref[...], preferred_element_type=F32)
    lane = lax.broadcasted_iota(jnp.int32, (1, ne * w), 1)
    seg = lane // w
    seg_base = jnp.zeros((1, ne * w), jnp.int32)
    for e in range(ne):
        seg_base = jnp.where(seg == e, base_ref[(b * ne + e) * nt + k], seg_base)
    slot = (seg_base + lane % w + 1).astype(F32)
    onehot = jnp.where(spread[:, :ne * w] == slot, spread[:, ne * w:], 0.0).astype(BF16)
    y2 = None
    gw = COMBINE_GROUP * w
    for g in range(ne // COMBINE_GROUP):
        ywin = jnp.concatenate([r[...] for r in y_refs[COMBINE_GROUP * g:COMBINE_GROUP * (g + 1)]], axis=0)
        part = jnp.dot(onehot[:, g * gw:(g + 1) * gw], ywin, preferred_element_type=F32)
        y2 = part if y2 is None else y2 + part
    r = y2 * lax.rsqrt(jnp.mean(y2 * y2, axis=-1, keepdims=True) + EPS) * pn_ref[...]
    o_ref[0] = x1_ref[0] + gt_ref[0] * r


def _combine_win(row, base, pos_t, aff, y, x1, gt2, pn2, w):
    b, t, d = x1.shape
    ne = aff.shape[-1]
    nt = t // TOK_TILE

    def y_spec(e):
        return pl.BlockSpec((pl.Element(w), pl.Element(d)),
                            lambda i, k, row_ref, base_ref: (row_ref[(i * ne + e) * nt + k] * SLOT_BLK, 0))

    tile = lambda i, k, row_ref, base_ref: (i, k, 0)
    const = lambda i, k, row_ref, base_ref: (0, 0)
    spread = _spread_matrix(ne, w)
    return pl.pallas_call(
        _combine_win_kernel,
        grid_spec=pltpu.PrefetchScalarGridSpec(
            num_scalar_prefetch=2,
            grid=(b, nt),
            in_specs=[pl.BlockSpec((1, TOK_TILE, ne), tile),
                      pl.BlockSpec((1, TOK_TILE, ne), tile),
                      pl.BlockSpec(spread.shape, const)]
                     + [y_spec(e) for e in range(ne)]
                     + [pl.BlockSpec((1, TOK_TILE, d), tile),
                        pl.BlockSpec((1, 1, d), lambda i, k, row_ref, base_ref: (i, 0, 0)),
                        pl.BlockSpec((1, d), const)],
            out_specs=pl.BlockSpec((1, TOK_TILE, d), tile)),
        out_shape=jax.ShapeDtypeStruct((b, t, d), F32),
        compiler_params=_params("parallel", "parallel"),
        name="combine_win",
    )(row, base, pos_t, aff, spread, *([y.reshape(-1, d)] * ne), x1, gt2, pn2.reshape(1, d))


def _routing_tables(fill, cap):
    b, ne, t = fill.shape
    w = min(2 * SLOT_BLK, cap)
    ends = fill[:, :, TOK_TILE - 1::TOK_TILE].astype(jnp.int32)
    starts = jnp.concatenate([jnp.zeros((b, ne, 1), jnp.int32), ends[:, :, :-1]], axis=-1)
    sparse = jnp.all(ends - starts <= SLOT_BLK)
    base = jnp.minimum(starts // SLOT_BLK * SLOT_BLK, cap - w)
    group = (jnp.arange(ne)[None, :, None] * b + jnp.arange(b)[:, None, None]) * cap
    row_blk = (group + base) // SLOT_BLK
    return sparse, (starts // 8 * 8).reshape(-1), row_blk.reshape(-1), base.reshape(-1), w


def _rope_tables(t):
    pos = np.arange(t)
    r = (pos // GRID_W).astype(np.float32)
    cl = (pos % GRID_W).astype(np.float32)
    quarter = HEAD_DIM // 4
    inv = (np.float32(ROPE_THETA) ** (-np.arange(quarter, dtype=np.float32) / np.float32(quarter))).astype(np.float32)
    ang_r = r[:, None] * inv
    ang_c = cl[:, None] * inv
    ang = np.concatenate([ang_r, ang_r, ang_c, ang_c], axis=-1).astype(np.float32)
    return jnp.asarray(np.cos(ang), F32), jnp.asarray(np.sin(ang), F32)


def _kv_heads(h, w_in, q_width, k_norm, rope):
    assert 2 * ATT_KV_W == RET_QK_W == PROJ_COLS and RET_V_W == 2 * PROJ_COLS
    plan = (((0, ATT_KV_W, "norm_rope", 0, 1.0), (ATT_KV_W, 2 * ATT_KV_W, "plain", 1, 1.0)),
            ((0, RET_QK_W, "rope", 2, RET_QK_DIM ** -0.5),),
            ((0, PROJ_COLS, "plain", 3, 1.0),),
            ((0, PROJ_COLS, "plain", 3, 1.0),))
    outs = ((ATT_KV_W, ATT_KV_W, lambda j: 0), (ATT_KV_W, ATT_KV_W, lambda j: 0),
            (RET_QK_W, RET_QK_W, lambda j: 0), (RET_V_W, PROJ_COLS, lambda j: jnp.clip(j - 2, 0, 1)))
    return _proj_heads(h, w_in, q_width, plan, outs, k_norm, rope)


def _q_heads(h, w_in, q_norm, rope):
    assert ATT_W == 2 * PROJ_COLS and RET_QK_W == PROJ_COLS
    plan = (((0, PROJ_COLS, "norm_rope", 0, 1.0),), ((0, PROJ_COLS, "norm_rope", 0, 1.0),),
            ((0, RET_QK_W, "rope", 1, 1.0),))
    outs = ((ATT_W, PROJ_COLS, lambda j: jnp.minimum(j, 1)), (RET_QK_W, RET_QK_W, lambda j: 0))
    return _proj_heads(h, w_in, 0, plan, outs, q_norm, rope)


def kernel(x, c, ctx, c_ctx, w_mod, b_mod, pre_norm1, post_norm1, pre_norm2, post_norm2, w_in, q_norm,
           k_norm, ret_decay, ret_gn, w_o_att, w_o_ret, w_out, w_router, w_gate, w_up, w_down):
    b, t, d = x.shape
    n_ctx = ctx.shape[1]
    depth = w_mod.shape[0]
    q_width = ATT_W + RET_QK_W + RET_V_W + 2 * d
    cap = EC_FACTOR * t // N_EXPERTS
    rope_lat = _rope_tables(t)
    rope_ctx = (jnp.ones((b * n_ctx, HEAD_DIM), F32), jnp.zeros((b * n_ctx, HEAD_DIM), F32))
    cs = jnp.zeros((MOD_ROWS, d), F32).at[:b].set(c).at[b].set(c_ctx)
    xc = ctx
    for layer in range(depth):
        assert layer == depth - 1, "context-stream update between layers is not implemented"
        mod = _mod_vectors(cs, w_mod[layer], b_mod[layer])
        sh1, sc1, gt1, sh2, sc2, gt2 = [m[:b, None, :] for m in jnp.split(mod, 6, axis=-1)]
        csh1, csc1 = [jnp.broadcast_to(m[b][None, None, :], (b, 1, d)) for m in jnp.split(mod, 6, axis=-1)[:2]]
        dec = -jax.nn.softplus(ret_decay[layer].astype(F32))
        wl = w_in[layer]

        hc = _prenorm(xc, pre_norm1[layer], csh1, csc1).reshape(b * n_ctx, d)
        kc_a, vc_a, kc_r, vc_r = _kv_heads(hc, wl, q_width, k_norm[layer], rope_ctx)
        s0f, s0b = _ctx_states(dec, kc_r.reshape(b, n_ctx, RET_QK_W), vc_r.reshape(b, n_ctx, RET_V_W))

        h = _prenorm(x, pre_norm1[layer], sh1, sc1).reshape(b * t, d)
        q_a, q_r = _q_heads(h, wl, q_norm[layer], rope_lat)
        zg = _proj(h, wl, ATT_W + RET_QK_W, RET_V_W + 2 * d, out_dtype=F32, tn=PROJ_COLS)
        k_a, v_a, k_r, v_r = _kv_heads(h, wl, q_width, k_norm[layer], rope_lat)

        keys = jnp.concatenate([kc_a.reshape(b, n_ctx, ATT_KV_W), k_a.reshape(b, t, ATT_KV_W)], axis=1)
        vals = jnp.concatenate([vc_a.reshape(b, n_ctx, ATT_KV_W), v_a.reshape(b, t, ATT_KV_W)], axis=1)
        o_att = _attention(q_a.reshape(b, t, ATT_W), keys, vals)

        o_f, o_b = _retention(dec, q_r.reshape(b, t, RET_QK_W), k_r.reshape(b, t, RET_QK_W),
                              v_r.reshape(b, t, RET_V_W), s0f, s0b)

        y = _merge1(o_att.reshape(b * t, ATT_W), o_f.reshape(b * t, RET_V_W), o_b.reshape(b * t, RET_V_W),
                    zg, ret_gn[layer], w_o_att[layer].astype(BF16), w_o_ret[layer].astype(BF16))
        x1, h2, aff = _merge2(y, w_out[layer].astype(BF16), x, gt1, post_norm1[layer], pre_norm2[layer],
                              sh2, sc2, w_router[layer])

        pos, fill = _topk_slots(aff.transpose(0, 2, 1), cap)
        pos_t = pos.transpose(0, 2, 1)
        sparse, w0, row, base, win = _routing_tables(fill, cap)
        xg = lax.cond(sparse,
                      lambda: _gather_win(w0, pos, h2, cap),
                      lambda: _gather(pos, h2, cap))
        yg = _ffn(xg, w_gate[layer], w_up[layer], w_down[layer])
        x = lax.cond(sparse,
                     lambda: _combine_win(row, base, pos_t, aff, yg, x1, gt2, post_norm2[layer], win),
                     lambda: _combine(pos_t, aff, yg, x1, gt2, post_norm2[layer], cap))
    return x
```

```python
import functools

import jax
import jax.numpy as jnp
import numpy as np
from jax import lax
from jax.experimental import pallas as pl
from jax.experimental.pallas import tpu as pltpu

F32 = jnp.float32
BF16 = jnp.bfloat16

GRID_W = 64
HEAD_DIM = 128
ATT_HEADS = 8
ATT_KV_HEADS = 2
ATT_GROUP = ATT_HEADS // ATT_KV_HEADS
RET_HEADS = 4
RET_QK_DIM = 128
RET_V_DIM = 256
ROPE_THETA = 10000.0
N_EXPERTS = 16
EC_FACTOR = 2
EPS = 1e-6

ATT_W = ATT_HEADS * HEAD_DIM
ATT_KV_W = ATT_KV_HEADS * HEAD_DIM
RET_QK_W = RET_HEADS * RET_QK_DIM
RET_V_W = RET_HEADS * RET_V_DIM

LANES = 128
VMEM_LIMIT = 56 * 1024 * 1024
RET_CHUNK = 256
MIN_NORMAL_F32_BITS = 0x00800000
LOG2_E = 1.4426950408889634
TOK_TILE = 256
SLOT_BLK = 64
GATHER_WIN = SLOT_BLK + 8
COMBINE_GROUP = 4
MOD_ROWS = 16

MOD_COLS = 2048
PRENORM_ROWS = 1024
PROJ_ROWS = 2048
PROJ_COLS = 512
ATT_Q_ROWS = 256
MERGE1_ROWS = 512
GATE_COLS = 1024
MERGE2_ROWS = 512
FFN_ROWS = 2048
FFN_HIDDEN_COLS = 256
FFN_OUT_COLS = 512
GATHER_COLS = 512
DENSE_TOKENS = 1024
DENSE_COLS = 1024
DENSE_COMBINE_ROWS = 512


def _blk(pref, n):
    return pref if n % pref == 0 else n


def _params(*sem):
    return pltpu.CompilerParams(dimension_semantics=sem, vmem_limit_bytes=VMEM_LIMIT)


def _sigmoid(x):
    return 0.5 * jnp.tanh(0.5 * x) + 0.5


def _mod_kernel(s_ref, w_ref, b_ref, o_ref):
    s = s_ref[...]
    s = s * _sigmoid(s)
    rows = s.shape[0]
    s_hi, s_lo = _split_bf16(s)
    w_hi, w_lo = _split_bf16(w_ref[...])
    both = jnp.dot(jnp.concatenate([s_hi, s_lo], axis=0), w_hi, preferred_element_type=F32)
    o_ref[...] = both[:rows] + both[rows:] + jnp.dot(s_hi, w_lo, preferred_element_type=F32) + b_ref[...]


def _mod_vectors(cs, w_mod, b_mod):
    rows, d = cs.shape
    n = w_mod.shape[1]
    tn = _blk(MOD_COLS, n)
    return pl.pallas_call(
        _mod_kernel,
        grid=(n // tn,),
        in_specs=[pl.BlockSpec((rows, d), lambda j: (0, 0)),
                  pl.BlockSpec((d, tn), lambda j: (0, j)),
                  pl.BlockSpec((1, tn), lambda j: (0, j))],
        out_specs=pl.BlockSpec((rows, tn), lambda j: (0, j)),
        out_shape=jax.ShapeDtypeStruct((rows, n), F32),
        compiler_params=_params("parallel"),
        name="mod",
    )(cs, w_mod, b_mod.reshape(1, n))


def _prenorm_kernel(x_ref, g_ref, sh_ref, sc_ref, o_ref):
    x = x_ref[0]
    y = x * lax.rsqrt(jnp.mean(x * x, axis=-1, keepdims=True) + EPS) * g_ref[...]
    o_ref[0] = (y * (1.0 + sc_ref[0]) + sh_ref[0]).astype(o_ref.dtype)


def _prenorm(x, gain, shift, scale):
    b, t, d = x.shape
    tm = _blk(PRENORM_ROWS, t)
    return pl.pallas_call(
        _prenorm_kernel,
        grid=(b, t // tm),
        in_specs=[pl.BlockSpec((1, tm, d), lambda i, j: (i, j, 0)),
                  pl.BlockSpec((1, d), lambda i, j: (0, 0)),
                  pl.BlockSpec((1, 1, d), lambda i, j: (i, 0, 0)),
                  pl.BlockSpec((1, 1, d), lambda i, j: (i, 0, 0))],
        out_specs=pl.BlockSpec((1, tm, d), lambda i, j: (i, j, 0)),
        out_shape=jax.ShapeDtypeStruct((b, t, d), BF16),
        compiler_params=_params("parallel", "parallel"),
        name="prenorm",
    )(x, gain.reshape(1, d), shift, scale)


def _split_bf16(v):
    hi = v.astype(BF16)
    return [hi, (v - hi.astype(F32)).astype(BF16)]


def _rotate_half_matrix():
    quarter = HEAD_DIM // 4
    r = np.zeros((HEAD_DIM, HEAD_DIM), np.float32)
    for i in range(HEAD_DIM):
        if i % (2 * quarter) < quarter:
            r[i + quarter, i] = -1.0
        else:
            r[i - quarter, i] = 1.0
    return r


def _head_mix_matrix(mode):
    rot = _rotate_half_matrix()
    if mode == "rope":
        return jnp.asarray(np.concatenate([rot, rot], axis=0), BF16)
    ones = np.ones((HEAD_DIM, HEAD_DIM), np.float32)
    zero = np.zeros((HEAD_DIM, HEAD_DIM), np.float32)
    return jnp.asarray(np.block([[ones, zero], [ones, zero], [zero, rot], [zero, rot]]), BF16)


def _proj_kernel(h_ref, w_ref, o_ref):
    o_ref[...] = jnp.dot(h_ref[...], w_ref[...].astype(BF16), preferred_element_type=F32).astype(o_ref.dtype)


def _proj(h, w_in, col0, width, *, out_dtype, tn):
    r, d = h.shape
    tm = _blk(PROJ_ROWS, r)
    assert col0 % tn == 0 and width % tn == 0
    c0 = col0 // tn
    return pl.pallas_call(
        _proj_kernel,
        grid=(r // tm, width // tn),
        in_specs=[pl.BlockSpec((tm, d), lambda i, j: (i, 0)),
                  pl.BlockSpec((d, tn), lambda i, j: (0, c0 + j))],
        out_specs=pl.BlockSpec((tm, tn), lambda i, j: (i, j)),
        out_shape=jax.ShapeDtypeStruct((r, width), out_dtype),
        compiler_params=_params("parallel", "parallel"),
        name="proj_plain",
    )(h, w_in)


def _rope_head(xh, mode, gain, mix, cos, sin, pre_scale):
    if mode == "norm_rope":
        xg = xh * gain
        mixed = jnp.dot(jnp.concatenate(_split_bf16(xh * xh) + _split_bf16(xg), axis=1), mix,
                        preferred_element_type=F32)
        inv = lax.rsqrt(mixed[:, :HEAD_DIM] * (1.0 / HEAD_DIM) + EPS)
        return (xg * cos + mixed[:, HEAD_DIM:] * sin) * inv
    xs = xh * pre_scale if pre_scale != 1.0 else xh
    rot = jnp.dot(jnp.concatenate(_split_bf16(xs), axis=1), mix, preferred_element_type=F32)
    return xs * cos + rot * sin


def _proj_heads_kernel(h_ref, w_ref, gain_ref, mixn_ref, mixr_ref, cos_ref, sin_ref, *o_refs, plan):
    acc = jnp.dot(h_ref[...], w_ref[...].astype(BF16), preferred_element_type=F32)
    for jj, pieces in enumerate(plan):
        @pl.when(pl.program_id(1) == jj)
        def _(pieces=pieces):
            for lo, hi, mode, out, pre_scale in pieces:
                o_ref = o_refs[out]
                if mode == "plain":
                    o_ref[...] = acc[:, lo:hi].astype(o_ref.dtype)
                    continue
                mix = mixn_ref[...] if mode == "norm_rope" else mixr_ref[...]
                for hh in range((hi - lo) // HEAD_DIM):
                    xh = acc[:, lo + hh * HEAD_DIM:lo + (hh + 1) * HEAD_DIM]
                    res = _rope_head(xh, mode, gain_ref[...], mix, cos_ref[...], sin_ref[...], pre_scale)
                    o_ref[:, hh * HEAD_DIM:(hh + 1) * HEAD_DIM] = res.astype(o_ref.dtype)


def _proj_heads(h, w_in, col0, plan, outs, gain, rope):
    r, d = h.shape
    cos, sin = rope
    tm = min(_blk(PROJ_ROWS, r), cos.shape[0])
    tn = PROJ_COLS
    assert col0 % tn == 0
    c0 = col0 // tn
    period = cos.shape[0] // tm
    assert cos.shape == (period * tm, HEAD_DIM) and (r // tm) % period == 0
    mixn, mixr = _head_mix_matrix("norm_rope"), _head_mix_matrix("rope")
    const = lambda i, j: (0, 0)
    rows = lambda i, j: (i, 0)
    table = lambda i, j: (i % period, 0)
    return pl.pallas_call(
        functools.partial(_proj_heads_kernel, plan=plan),
        grid=(r // tm, len(plan)),
        in_specs=[pl.BlockSpec((tm, d), rows),
                  pl.BlockSpec((d, tn), lambda i, j: (0, c0 + j)),
                  pl.BlockSpec((1, HEAD_DIM), const),
                  pl.BlockSpec(mixn.shape, const),
                  pl.BlockSpec(mixr.shape, const),
                  pl.BlockSpec((tm, HEAD_DIM), table),
                  pl.BlockSpec((tm, HEAD_DIM), table)],
        out_specs=tuple(pl.BlockSpec((tm, bw), functools.partial(lambda i, j, f: (i, f(j)), f=f))
                        for _, bw, f in outs),
        out_shape=tuple(jax.ShapeDtypeStruct((r, width), BF16) for width, _, _ in outs),
        compiler_params=_params("parallel", "arbitrary"),
        name="proj_heads",
    )(h, w_in, gain.reshape(1, HEAD_DIM), mixn, mixr, cos, sin)


def _ctx_state_kernel(dec_ref, k_ref, v_ref, sf_ref, sb_ref):
    hh = pl.program_id(1)
    lf = dec_ref[0, hh]
    lb = dec_ref[1, hh]
    k = k_ref[0].astype(F32)
    v = v_ref[0]
    n = k.shape[0]
    pos = lax.broadcasted_iota(jnp.int32, k.shape, 0).astype(F32)
    kf = (k * jnp.exp((n - 1.0 - pos) * lf)).T.astype(BF16)
    kb = (k * jnp.exp(pos * lb)).T.astype(BF16)
    sf_ref[0, 0] = jnp.dot(kf, v, preferred_element_type=F32)
    sb_ref[0, 0] = jnp.dot(kb, v, preferred_element_type=F32)


def _ctx_states(dec, k_r, v_r):
    b, n, _ = k_r.shape
    spec_s = pl.BlockSpec((1, 1, RET_QK_DIM, RET_V_DIM), lambda i, j: (i, j, 0, 0))
    shape_s = jax.ShapeDtypeStruct((b, RET_HEADS, RET_QK_DIM, RET_V_DIM), F32)
    return pl.pallas_call(
        _ctx_state_kernel,
        grid=(b, RET_HEADS),
        in_specs=[pl.BlockSpec(memory_space=pltpu.SMEM),
                  pl.BlockSpec((1, n, RET_QK_DIM), lambda i, j: (i, 0, j)),
                  pl.BlockSpec((1, n, RET_V_DIM), lambda i, j: (i, 0, j))],
        out_specs=(spec_s, spec_s),
        out_shape=(shape_s, shape_s),
        compiler_params=_params("parallel", "parallel"),
        name="ctx_state",
    )(dec, k_r, v_r)


def _attn_kernel(q_ref, k_ref, v_ref, o_ref, *, scale):
    nt = (((1,), (1,)), ((), ()))
    for hk in range(ATT_KV_HEADS):
        k = k_ref[0, :, hk * HEAD_DIM:(hk + 1) * HEAD_DIM]
        v = v_ref[0, :, hk * HEAD_DIM:(hk + 1) * HEAD_DIM]
        for g in range(ATT_GROUP):
            cols = slice((hk * ATT_GROUP + g) * HEAD_DIM, (hk * ATT_GROUP + g + 1) * HEAD_DIM)
            s = lax.dot_general(q_ref[0, :, cols], k, nt, preferred_element_type=F32)
            m = jnp.max(s, axis=-1, keepdims=True)
            p = jnp.exp2((s - m) * (scale * LOG2_E))
            l = jnp.sum(p, axis=-1, keepdims=True)
            o = jnp.dot(p.astype(BF16), v, preferred_element_type=F32)
            o_ref[0, :, cols] = (o / l).astype(o_ref.dtype)


def _attention(q, k, v):
    b, t, _ = q.shape
    s = k.shape[1]
    tq = _blk(ATT_Q_ROWS, t)
    return pl.pallas_call(
        functools.partial(_attn_kernel, scale=HEAD_DIM ** -0.5),
        grid=(b, t // tq),
        in_specs=[pl.BlockSpec((1, tq, ATT_W), lambda i, n: (i, n, 0)),
                  pl.BlockSpec((1, s, ATT_KV_W), lambda i, n: (i, 0, 0)),
                  pl.BlockSpec((1, s, ATT_KV_W), lambda i, n: (i, 0, 0))],
        out_specs=pl.BlockSpec((1, tq, ATT_W), lambda i, n: (i, n, 0)),
        out_shape=jax.ShapeDtypeStruct((b, t, ATT_W), BF16),
        compiler_params=_params("parallel", "parallel"),
        name="attention",
    )(q, k, v)


def _ret_kernel(dec_ref, qf_ref, kf_ref, vf_ref, qb_ref, kb_ref, vb_ref, s0f_ref, s0b_ref,
                of_ref, ob_ref, sf_ref, sb_ref, mf_ref, mb_ref):
    n = pl.program_id(1)
    c = qf_ref.shape[1]

    @pl.when(n == 0)
    def _():
        sf_ref[...] = s0f_ref[0]
        sb_ref[...] = s0b_ref[0]
        ri = lax.broadcasted_iota(jnp.int32, (c, c), 0)
        ci = lax.broadcasted_iota(jnp.int32, (c, c), 1)
        d = (ri - ci).astype(F32)
        for hh in range(RET_HEADS):
            mf_ref[hh] = jnp.where(d >= 0, jnp.exp(jnp.maximum(d, 0.0) * dec_ref[0, hh]), 0.0)
            mb_ref[hh] = jnp.where(d <= 0, jnp.exp(jnp.maximum(-d, 0.0) * dec_ref[1, hh]), 0.0)

    nt = (((1,), (1,)), ((), ()))
    row_v = lax.broadcasted_iota(jnp.int32, (c, RET_V_DIM), 0).astype(F32)
    row_k = lax.broadcasted_iota(jnp.int32, (c, RET_QK_DIM), 0).astype(F32)

    def sweep(q_ref, k_ref, v_ref, o_ref, s_ref, m_ref, hh, lg, wq_age, wk_age):
        qk = slice(hh * RET_QK_DIM, (hh + 1) * RET_QK_DIM)
        vv = slice(hh * RET_V_DIM, (hh + 1) * RET_V_DIM)
        q = q_ref[0, :, qk]
        k = k_ref[0, :, qk]
        v = v_ref[0, :, vv]
        s = lax.dot_general(q, k, nt, preferred_element_type=F32) * m_ref[hh]
        o_in = jnp.dot(s.astype(BF16), v, preferred_element_type=F32)
        state = s_ref[hh]
        o_x = jnp.dot(q, state.astype(BF16), preferred_element_type=F32) * jnp.exp(wq_age * lg)
        o_ref[0, :, vv] = o_in + o_x
        kw = (k.astype(F32) * jnp.exp(wk_age * lg)).T.astype(BF16)
        g_chunk = jnp.exp(jnp.full((1, RET_V_DIM), c * lg, F32))
        s_ref[hh] = g_chunk * state + jnp.dot(kw, v, preferred_element_type=F32)

    for hh in range(RET_HEADS):
        sweep(qf_ref, kf_ref, vf_ref, of_ref, sf_ref, mf_ref, hh, dec_ref[0, hh], row_v + 1.0, c - 1.0 - row_k)
        sweep(qb_ref, kb_ref, vb_ref, ob_ref, sb_ref, mb_ref, hh, dec_ref[1, hh], c - row_v, row_k)


def _retention(dec, q, k, v, s0f, s0b):
    b, t, _ = q.shape
    c = _blk(RET_CHUNK, t)
    nc = t // c
    fwd = lambda i, n: (i, n, 0)
    bwd = lambda i, n: (i, nc - 1 - n, 0)
    spec_s = pl.BlockSpec((1, RET_HEADS, RET_QK_DIM, RET_V_DIM), lambda i, n: (i, 0, 0, 0))
    out_shape = jax.ShapeDtypeStruct((b, t, RET_V_W), F32)
    return pl.pallas_call(
        _ret_kernel,
        grid=(b, nc),
        in_specs=[pl.BlockSpec(memory_space=pltpu.SMEM),
                  pl.BlockSpec((1, c, RET_QK_W), fwd),
                  pl.BlockSpec((1, c, RET_QK_W), fwd),
                  pl.BlockSpec((1, c, RET_V_W), fwd),
                  pl.BlockSpec((1, c, RET_QK_W), bwd),
                  pl.BlockSpec((1, c, RET_QK_W), bwd),
                  pl.BlockSpec((1, c, RET_V_W), bwd),
                  spec_s, spec_s],
        out_specs=(pl.BlockSpec((1, c, RET_V_W), fwd), pl.BlockSpec((1, c, RET_V_W), bwd)),
        out_shape=(out_shape, out_shape),
        scratch_shapes=[pltpu.VMEM((RET_HEADS, RET_QK_DIM, RET_V_DIM), F32),
                        pltpu.VMEM((RET_HEADS, RET_QK_DIM, RET_V_DIM), F32),
                        pltpu.VMEM((RET_HEADS, c, c), F32),
                        pltpu.VMEM((RET_HEADS, c, c), F32)],
        compiler_params=_params("parallel", "arbitrary"),
        name="retention",
    )(dec, q, k, v, q, k, v, s0f, s0b)


def _merge1_kernel(oa_ref, of_ref, ob_ref, g_ref, *refs, nslab):
    ga_refs, gr_refs = refs[:nslab], refs[nslab:2 * nslab]
    gn_ref, wa_ref, wr_ref, y_ref = refs[2 * nslab:]
    o = of_ref[...] + ob_ref[...]
    heads = []
    for hh in range(RET_HEADS):
        sl = slice(hh * RET_V_DIM, (hh + 1) * RET_V_DIM)
        oh = o[:, sl]
        dlt = oh - jnp.mean(oh, axis=-1, keepdims=True)
        yh = dlt * lax.rsqrt(jnp.mean(dlt * dlt, axis=-1, keepdims=True) + EPS)
        g = g_ref[:, sl]
        heads.append((yh * gn_ref[:, sl] * (g * _sigmoid(g))).astype(BF16))
    o_ret = jnp.concatenate(heads, axis=1)
    tn = ga_refs[0].shape[1]
    for c in range(nslab):
        cols = slice(c * tn, (c + 1) * tn)
        ya = jnp.dot(oa_ref[...], wa_ref[:, cols], preferred_element_type=F32)
        yr = jnp.dot(o_ret, wr_ref[:, cols], preferred_element_type=F32)
        y_ref[:, cols] = (_sigmoid(ga_refs[c][...]) * ya + _sigmoid(gr_refs[c][...]) * yr).astype(y_ref.dtype)


def _merge1(o_att, o_f, o_b, zg, gn, wa, wr):
    r = o_att.shape[0]
    d = wa.shape[1]
    tm = _blk(MERGE1_ROWS, r)
    tn = _blk(GATE_COLS, d)
    nslab = d // tn
    ga0 = RET_V_W // tn
    gr0 = (RET_V_W + d) // tn
    row = lambda i: (i, 0)
    const = lambda i: (0, 0)
    deep = None
    once = pl.Buffered(1)
    gate_specs = [pl.BlockSpec((tm, tn), functools.partial(lambda i, c: (i, c), c=c0 + c), pipeline_mode=deep)
                  for c0 in (ga0, gr0) for c in range(nslab)]
    return pl.pallas_call(
        functools.partial(_merge1_kernel, nslab=nslab),
        grid=(r // tm,),
        in_specs=[pl.BlockSpec((tm, ATT_W), row, pipeline_mode=deep),
                  pl.BlockSpec((tm, RET_V_W), row, pipeline_mode=deep),
                  pl.BlockSpec((tm, RET_V_W), row, pipeline_mode=deep),
                  pl.BlockSpec((tm, RET_V_W), row, pipeline_mode=deep)]
                 + gate_specs
                 + [pl.BlockSpec((1, RET_V_W), const),
                    pl.BlockSpec((ATT_W, d), const, pipeline_mode=once),
                    pl.BlockSpec((RET_V_W, d), const, pipeline_mode=once)],
        out_specs=pl.BlockSpec((tm, d), row),
        out_shape=jax.ShapeDtypeStruct((r, d), BF16),
        compiler_params=_params("parallel"),
        name="merge1",
    )(o_att, o_f, o_b, zg, *([zg] * (2 * nslab)), gn.reshape(1, RET_V_W), wa, wr)


def _merge2_kernel(y_ref, w_ref, x_ref, gt_ref, pn1_ref, pn2_ref, sh_ref, sc_ref, wr_ref,
                   x1_ref, h2_ref, aff_ref):
    yy = jnp.dot(y_ref[0], w_ref[...], preferred_element_type=F32)
    r = yy * lax.rsqrt(jnp.mean(yy * yy, axis=-1, keepdims=True) + EPS) * pn1_ref[...]
    x1 = x_ref[0] + gt_ref[0] * r
    x1_ref[0] = x1
    h2 = x1 * lax.rsqrt(jnp.mean(x1 * x1, axis=-1, keepdims=True) + EPS) * pn2_ref[...]
    h2 = h2 * (1.0 + sc_ref[0]) + sh_ref[0]
    h_hi = h2.astype(BF16)
    h2_ref[0] = h_hi
    h_lo = (h2 - h_hi.astype(F32)).astype(BF16)
    wr = wr_ref[...]
    w_hi = wr.astype(BF16)
    w_lo = (wr - w_hi.astype(F32)).astype(BF16)
    ne = wr.shape[1]
    r_hi = jnp.dot(h_hi, jnp.concatenate([w_hi, w_lo], axis=1), preferred_element_type=F32)
    logits = r_hi[:, :ne] + r_hi[:, ne:] + jnp.dot(h_lo, w_hi, preferred_element_type=F32)
    e = jnp.exp(logits - jnp.max(logits, axis=-1, keepdims=True))
    aff_ref[0] = e / jnp.sum(e, axis=-1, keepdims=True)


def _merge2(y, w_out, x, gt1, pn1, pn2, sh2, sc2, w_router):
    b, t, d = x.shape
    tm = _blk(MERGE2_ROWS, t)
    ne = w_router.shape[1]
    tile = lambda i, j: (i, j, 0)
    vec = lambda i, j: (0, 0)
    per_b = lambda i, j: (i, 0, 0)
    return pl.pallas_call(
        _merge2_kernel,
        grid=(b, t // tm),
        in_specs=[pl.BlockSpec((1, tm, d), tile),
                  pl.BlockSpec((d, d), vec),
                  pl.BlockSpec((1, tm, d), tile),
                  pl.BlockSpec((1, 1, d), per_b),
                  pl.BlockSpec((1, d), vec),
                  pl.BlockSpec((1, d), vec),
                  pl.BlockSpec((1, 1, d), per_b),
                  pl.BlockSpec((1, 1, d), per_b),
                  pl.BlockSpec((d, ne), vec)],
        out_specs=(pl.BlockSpec((1, tm, d), tile),
                   pl.BlockSpec((1, tm, d), tile),
                   pl.BlockSpec((1, tm, ne), tile)),
        out_shape=(jax.ShapeDtypeStruct((b, t, d), F32),
                   jax.ShapeDtypeStruct((b, t, d), BF16),
                   jax.ShapeDtypeStruct((b, t, ne), F32)),
        compiler_params=_params("parallel", "parallel"),
        name="merge2",
    )(y.reshape(b, t, d), w_out, x, gt1, pn1.reshape(1, d), pn2.reshape(1, d), sh2, sc2, w_router)


def _lane_cumsum(x, tri):
    off = jnp.zeros((x.shape[0], 1), F32)
    parts = []
    for cidx in range(x.shape[1] // LANES):
        xc = x[:, cidx * LANES:(cidx + 1) * LANES].astype(BF16)
        cs = jnp.dot(xc, tri, preferred_element_type=F32) + off
        parts.append(cs)
        off = cs[:, LANES - 1:LANES]
    return jnp.concatenate(parts, axis=1)


def _topk_kernel(a_ref, pos_ref, fill_ref, *, cap):
    a = a_ref[0]
    bits = jnp.zeros((a.shape[0], 1), jnp.int32)
    for bit in range(30, -1, -1):
        cand = bits | (1 << bit)
        cnt = jnp.sum(jnp.where(a >= lax.bitcast_convert_type(cand, F32), 1.0, 0.0), axis=1, keepdims=True)
        bits = jnp.where(cnt >= cap, cand, bits)
    bits = jnp.where(bits < MIN_NORMAL_F32_BITS, 0, bits)
    thr = lax.bitcast_convert_type(bits, F32)
    ri = lax.broadcasted_iota(jnp.int32, (LANES, LANES), 0)
    ci = lax.broadcasted_iota(jnp.int32, (LANES, LANES), 1)
    tri = jnp.where(ri <= ci, 1.0, 0.0).astype(BF16)
    gt = a > thr
    eq = a == thr
    n_gt = jnp.sum(jnp.where(gt, 1.0, 0.0), axis=1, keepdims=True)
    eq_rank = _lane_cumsum(jnp.where(eq, 1.0, 0.0), tri)
    sel = jnp.where(gt, 1.0, jnp.where(eq, jnp.where(eq_rank <= cap - n_gt, 1.0, 0.0), 0.0))
    filled = _lane_cumsum(sel, tri)
    pos_ref[0] = filled * sel
    fill_ref[0] = filled


def _topk_slots(aff_t, cap):
    b, ne, t = aff_t.shape
    spec = pl.BlockSpec((1, ne, t), lambda i: (i, 0, 0))
    shape = jax.ShapeDtypeStruct((b, ne, t), F32)
    return pl.pallas_call(
        functools.partial(_topk_kernel, cap=cap),
        grid=(b,),
        in_specs=[spec],
        out_specs=(spec, spec),
        out_shape=(shape, shape),
        compiler_params=_params("parallel"),
        name="topk",
    )(aff_t)


def _gather_kernel(pos_ref, h_ref, o_ref):
    cap = o_ref.shape[1]
    t = h_ref.shape[1]
    tc = _blk(DENSE_TOKENS, t)
    slot = lax.broadcasted_iota(jnp.int32, (cap, tc), 0).astype(F32) + 1.0
    acc = None
    for cidx in range(t // tc):
        row = pos_ref[0, 0, :, cidx * tc:(cidx + 1) * tc]
        onehot = jnp.where(row == slot, 1.0, 0.0).astype(BF16)
        part = jnp.dot(onehot, h_ref[0, cidx * tc:(cidx + 1) * tc, :], preferred_element_type=F32)
        acc = part if acc is None else acc + part
    o_ref[0] = acc.astype(o_ref.dtype)


def _gather(pos, h2, cap):
    b, ne, t = pos.shape
    d = h2.shape[-1]
    dn = _blk(DENSE_COLS, d)
    return pl.pallas_call(
        _gather_kernel,
        grid=(b, d // dn, ne),
        in_specs=[pl.BlockSpec((1, 1, 1, t), lambda i, n, e: (i, e, 0, 0)),
                  pl.BlockSpec((1, t, dn), lambda i, n, e: (i, 0, n))],
        out_specs=pl.BlockSpec((1, cap, dn), lambda i, n, e: (e, i, n)),
        out_shape=jax.ShapeDtypeStruct((ne, b * cap, d), BF16),
        compiler_params=_params("parallel", "parallel", "parallel"),
        name="gather",
    )(pos.reshape(b, ne, 1, t), h2)


def _gather_win_kernel(w0_ref, pos_ref, h_ref, o_ref, acc_ref):
    b = pl.program_id(0)
    ne, cap = o_ref.shape[0], o_ref.shape[1]
    nt = h_ref.shape[1] // TOK_TILE
    slot1 = lax.broadcasted_iota(jnp.int32, (GATHER_WIN, TOK_TILE), 0) + 1
    acc_ref[:, :8, :] = jnp.zeros((ne, 8) + acc_ref.shape[2:], F32)

    def tile(k, carry):
        t0 = pl.multiple_of(k * TOK_TILE, TOK_TILE)
        rows = []
        for e in range(ne):
            row = pos_ref[0, pl.ds(e * nt + k, 1), :]
            rows.append(jnp.where(row == (slot1 + w0_ref[(b * ne + e) * nt + k]).astype(F32), 1.0, 0.0))
        lhs = jnp.concatenate(rows, axis=0).astype(BF16)
        res = jnp.dot(lhs, h_ref[0, pl.ds(t0, TOK_TILE), :], preferred_element_type=F32)
        for e in range(ne):
            w0 = pl.multiple_of(w0_ref[(b * ne + e) * nt + k], 8)
            acc_ref[e, pl.ds(w0, 8), :] += res[e * GATHER_WIN:e * GATHER_WIN + 8]
            acc_ref[e, pl.ds(w0 + 8, GATHER_WIN - 8), :] = res[e * GATHER_WIN + 8:(e + 1) * GATHER_WIN]
        return carry
    lax.fori_loop(0, nt, tile, 0, unroll=8)
    o_ref[...] = acc_ref[:, :cap, :].astype(o_ref.dtype)


def _gather_win(w0, pos, h2, cap):
    b, ne, t = pos.shape
    d = h2.shape[-1]
    nt = t // TOK_TILE
    dn = _blk(GATHER_COLS, d)
    return pl.pallas_call(
        _gather_win_kernel,
        grid_spec=pltpu.PrefetchScalarGridSpec(
            num_scalar_prefetch=1,
            grid=(b, d // dn),
            in_specs=[pl.BlockSpec((1, ne * nt, TOK_TILE), lambda i, n, w: (i, 0, 0)),
                      pl.BlockSpec((1, t, dn), lambda i, n, w: (i, 0, n))],
            out_specs=pl.BlockSpec((ne, cap, dn), lambda i, n, w: (0, i, n)),
            scratch_shapes=[pltpu.VMEM((ne, cap + GATHER_WIN, dn), F32)]),
        out_shape=jax.ShapeDtypeStruct((ne, b * cap, d), BF16),
        compiler_params=_params("parallel", "arbitrary"),
        name="gather_win",
    )(w0, pos.reshape(b, ne * nt, TOK_TILE), h2)


def _ffn_kernel(x_ref, wg_ref, wu_ref, wd_ref, o_ref, h_ref, wgu_ref, *, nf):
    j = pl.program_id(2)
    tf = wg_ref.shape[2]

    @pl.when(j < nf)
    def _():
        wgu_ref[:, :tf] = wg_ref[0].astype(BF16)
        wgu_ref[:, tf:] = wu_ref[0].astype(BF16)
        au = jnp.dot(x_ref[0], wgu_ref[...], preferred_element_type=F32)
        a = au[:, :tf]
        h_ref[j] = (a * _sigmoid(a) * au[:, tf:]).astype(BF16)

    @pl.when(j >= nf)
    def _():
        acc = None
        for cidx in range(nf):
            part = jnp.dot(h_ref[cidx], wd_ref[0, cidx * tf:(cidx + 1) * tf, :].astype(BF16),
                           preferred_element_type=F32)
            acc = part if acc is None else acc + part
        o_ref[0] = acc.astype(o_ref.dtype)


def _ffn(xg, w_gate, w_up, w_down):
    ne, rows, d = xg.shape
    ff = w_gate.shape[-1]
    tm = _blk(FFN_ROWS, rows)
    tf = _blk(FFN_HIDDEN_COLS, ff)
    tn = _blk(FFN_OUT_COLS, d)
    nf = ff // tf
    return pl.pallas_call(
        functools.partial(_ffn_kernel, nf=nf),
        grid=(ne, rows // tm, nf + d // tn),
        in_specs=[pl.BlockSpec((1, tm, d), lambda e, m, j: (e, m, 0)),
                  pl.BlockSpec((1, d, tf), lambda e, m, j: (e, 0, jnp.minimum(j, nf - 1))),
                  pl.BlockSpec((1, d, tf), lambda e, m, j: (e, 0, jnp.minimum(j, nf - 1))),
                  pl.BlockSpec((1, ff, tn), lambda e, m, j: (e, 0, jnp.maximum(j - nf, 0)))],
        out_specs=pl.BlockSpec((1, tm, tn), lambda e, m, j: (e, m, jnp.maximum(j - nf, 0))),
        out_shape=jax.ShapeDtypeStruct((ne, rows, d), BF16),
        scratch_shapes=[pltpu.VMEM((nf, tm, tf), BF16), pltpu.VMEM((d, 2 * tf), BF16)],
        compiler_params=_params("parallel", "parallel", "arbitrary"),
        name="ffn",
    )(xg, w_gate, w_up, w_down)


def _combine_kernel(pos_ref, aff_ref, y_ref, x1_ref, gt_ref, pn_ref, o_ref, acc_ref):
    e = pl.program_id(2)
    tt, ne = pos_ref.shape[1], pos_ref.shape[2]
    cap = y_ref.shape[1]
    pick = lax.broadcasted_iota(jnp.int32, (tt, ne), 1) == e
    pcol = jnp.sum(jnp.where(pick, pos_ref[0], 0.0), axis=1, keepdims=True)
    acol = jnp.sum(jnp.where(pick, aff_ref[0], 0.0), axis=1, keepdims=True)
    slot = lax.broadcasted_iota(jnp.int32, (tt, cap), 1).astype(F32) + 1.0
    onehot = jnp.where(pcol == slot, 1.0, 0.0).astype(BF16)
    z = jnp.dot(onehot, y_ref[0], preferred_element_type=F32) * acol

    @pl.when(e == 0)
    def _():
        acc_ref[...] = z

    @pl.when(e > 0)
    def _():
        acc_ref[...] += z

    @pl.when(e == pl.num_programs(2) - 1)
    def _():
        y2 = acc_ref[...]
        r = y2 * lax.rsqrt(jnp.mean(y2 * y2, axis=-1, keepdims=True) + EPS) * pn_ref[...]
        o_ref[0] = x1_ref[0] + gt_ref[0] * r


def _combine(pos_t, aff, y, x1, gt2, pn2, cap):
    b, t, d = x1.shape
    ne = aff.shape[-1]
    tt = _blk(DENSE_COMBINE_ROWS, t)
    return pl.pallas_call(
        _combine_kernel,
        grid=(b, t // tt, ne),
        in_specs=[pl.BlockSpec((1, tt, ne), lambda i, j, e: (i, j, 0)),
                  pl.BlockSpec((1, tt, ne), lambda i, j, e: (i, j, 0)),
                  pl.BlockSpec((1, cap, d), lambda i, j, e: (e, i, 0)),
                  pl.BlockSpec((1, tt, d), lambda i, j, e: (i, j, 0)),
                  pl.BlockSpec((1, 1, d), lambda i, j, e: (i, 0, 0)),
                  pl.BlockSpec((1, d), lambda i, j, e: (0, 0))],
        out_specs=pl.BlockSpec((1, tt, d), lambda i, j, e: (i, j, 0)),
        out_shape=jax.ShapeDtypeStruct((b, t, d), F32),
        scratch_shapes=[pltpu.VMEM((tt, d), F32)],
        compiler_params=_params("parallel", "parallel", "arbitrary"),
        name="combine",
    )(pos_t, aff, y, x1, gt2, pn2.reshape(1, d))


def _spread_matrix(ne, w):
    s = np.zeros((5 * ne, 2 * ne * w), np.float32)
    for e in range(ne):
        s[e, e * w:(e + 1) * w] = SLOT_BLK
        s[ne + e, e * w:(e + 1) * w] = 1.0
        for part in (2, 3, 4):
            s[part * ne + e, (ne + e) * w:(ne + e + 1) * w] = 1.0
    return jnp.asarray(s, BF16)


def _combine_win_kernel(row_ref, base_ref, pos_ref, aff_ref, spread_ref, *refs):
    del row_ref
    ne = pos_ref.shape[2]
    y_refs = refs[:ne]
    x1_ref, gt_ref, pn_ref, o_ref = refs[ne:]
    b = pl.program_id(0)
    k = pl.program_id(1)
    nt = pl.num_programs(1)
    w = y_refs[0].shape[0]
    pos = pos_ref[0]
    aff = aff_ref[0]
    p_hi = jnp.floor(pos * (1.0 / SLOT_BLK))
    a_parts = _split_bf16(aff)
    a_parts.append((aff - a_parts[0].astype(F32) - a_parts[1].astype(F32)).astype(BF16))
    lhs = jnp.concatenate([p_hi.astype(BF16), (pos - SLOT_BLK * p_hi).astype(BF16)] + a_parts, axis=1)
    spread = jnp.dot(lhs, spread_ref[...], preferred_element_type=F32)
    lane = lax.broadcasted_iota(jnp.int32, (1, ne * w), 1)
    seg = lane // w
    seg_base = jnp.zeros((1, ne * w), jnp.int32)
    for e in range(ne):
        seg_base = jnp.where(seg == e, base_ref[(b * ne + e) * nt + k], seg_base)
    slot = (seg_base + lane % w + 1).astype(F32)
    onehot = jnp.where(spread[:, :ne * w] == slot, spread[:, ne * w:], 0.0).astype(BF16)
    y2 = None
    gw = COMBINE_GROUP * w
    for g in range(ne // COMBINE_GROUP):
        ywin = jnp.concatenate([r[...] for r in y_refs[COMBINE_GROUP * g:COMBINE_GROUP * (g + 1)]], axis=0)
        part = jnp.dot(onehot[:, g * gw:(g + 1) * gw], ywin, preferred_element_type=F32)
        y2 = part if y2 is None else y2 + part
    r = y2 * lax.rsqrt(jnp.mean(y2 * y2, axis=-1, keepdims=True) + EPS) * pn_ref[...]
    o_ref[0] = x1_ref[0] + gt_ref[0] * r


def _combine_win(row, base, pos_t, aff, y, x1, gt2, pn2, w):
    b, t, d = x1.shape
    ne = aff.shape[-1]
    nt = t // TOK_TILE

    def y_spec(e):
        return pl.BlockSpec((pl.Element(w), pl.Element(d)),
                            lambda i, k, row_ref, base_ref: (row_ref[(i * ne + e) * nt + k] * SLOT_BLK, 0))

    tile = lambda i, k, row_ref, base_ref: (i, k, 0)
    const = lambda i, k, row_ref, base_ref: (0, 0)
    spread = _spread_matrix(ne, w)
    return pl.pallas_call(
        _combine_win_kernel,
        grid_spec=pltpu.PrefetchScalarGridSpec(
            num_scalar_prefetch=2,
            grid=(b, nt),
            in_specs=[pl.BlockSpec((1, TOK_TILE, ne), tile),
                      pl.BlockSpec((1, TOK_TILE, ne), tile),
                      pl.BlockSpec(spread.shape, const)]
                     + [y_spec(e) for e in range(ne)]
                     + [pl.BlockSpec((1, TOK_TILE, d), tile),
                        pl.BlockSpec((1, 1, d), lambda i, k, row_ref, base_ref: (i, 0, 0)),
                        pl.BlockSpec((1, d), const)],
            out_specs=pl.BlockSpec((1, TOK_TILE, d), tile)),
        out_shape=jax.ShapeDtypeStruct((b, t, d), F32),
        compiler_params=_params("parallel", "parallel"),
        name="combine_win",
    )(row, base, pos_t, aff, spread, *([y.reshape(-1, d)] * ne), x1, gt2, pn2.reshape(1, d))


def _routing_tables(fill, cap):
    b, ne, t = fill.shape
    w = min(2 * SLOT_BLK, cap)
    ends = fill[:, :, TOK_TILE - 1::TOK_TILE].astype(jnp.int32)
    starts = jnp.concatenate([jnp.zeros((b, ne, 1), jnp.int32), ends[:, :, :-1]], axis=-1)
    sparse = jnp.all(ends - starts <= SLOT_BLK)
    base = jnp.minimum(starts // SLOT_BLK * SLOT_BLK, cap - w)
    group = (jnp.arange(ne)[None, :, None] * b + jnp.arange(b)[:, None, None]) * cap
    row_blk = (group + base) // SLOT_BLK
    return sparse, (starts // 8 * 8).reshape(-1), row_blk.reshape(-1), base.reshape(-1), w


def _rope_tables(t):
    pos = np.arange(t)
    r = (pos // GRID_W).astype(np.float32)
    cl = (pos % GRID_W).astype(np.float32)
    quarter = HEAD_DIM // 4
    inv = (np.float32(ROPE_THETA) ** (-np.arange(quarter, dtype=np.float32) / np.float32(quarter))).astype(np.float32)
    ang_r = r[:, None] * inv
    ang_c = cl[:, None] * inv
    ang = np.concatenate([ang_r, ang_r, ang_c, ang_c], axis=-1).astype(np.float32)
    return jnp.asarray(np.cos(ang), F32), jnp.asarray(np.sin(ang), F32)


def _kv_heads(h, w_in, q_width, k_norm, rope):
    assert 2 * ATT_KV_W == RET_QK_W == PROJ_COLS and RET_V_W == 2 * PROJ_COLS
    plan = (((0, ATT_KV_W, "norm_rope", 0, 1.0), (ATT_KV_W, 2 * ATT_KV_W, "plain", 1, 1.0)),
            ((0, RET_QK_W, "rope", 2, RET_QK_DIM ** -0.5),),
            ((0, PROJ_COLS, "plain", 3, 1.0),),
            ((0, PROJ_COLS, "plain", 3, 1.0),))
    outs = ((ATT_KV_W, ATT_KV_W, lambda j: 0), (ATT_KV_W, ATT_KV_W, lambda j: 0),
            (RET_QK_W, RET_QK_W, lambda j: 0), (RET_V_W, PROJ_COLS, lambda j: jnp.clip(j - 2, 0, 1)))
    return _proj_heads(h, w_in, q_width, plan, outs, k_norm, rope)


def _q_heads(h, w_in, q_norm, rope):
    assert ATT_W == 2 * PROJ_COLS and RET_QK_W == PROJ_COLS
    plan = (((0, PROJ_COLS, "norm_rope", 0, 1.0),), ((0, PROJ_COLS, "norm_rope", 0, 1.0),),
            ((0, RET_QK_W, "rope", 1, 1.0),))
    outs = ((ATT_W, PROJ_COLS, lambda j: jnp.minimum(j, 1)), (RET_QK_W, RET_QK_W, lambda j: 0))
    return _proj_heads(h, w_in, 0, plan, outs, q_norm, rope)


def kernel(x, c, ctx, c_ctx, w_mod, b_mod, pre_norm1, post_norm1, pre_norm2, post_norm2, w_in, q_norm,
           k_norm, ret_decay, ret_gn, w_o_att, w_o_ret, w_out, w_router, w_gate, w_up, w_down):
    b, t, d = x.shape
    n_ctx = ctx.shape[1]
    depth = w_mod.shape[0]
    q_width = ATT_W + RET_QK_W + RET_V_W + 2 * d
    cap = EC_FACTOR * t // N_EXPERTS
    rope_lat = _rope_tables(t)
    rope_ctx = (jnp.ones((b * n_ctx, HEAD_DIM), F32), jnp.zeros((b * n_ctx, HEAD_DIM), F32))
    cs = jnp.zeros((MOD_ROWS, d), F32).at[:b].set(c).at[b].set(c_ctx)
    xc = ctx
    for layer in range(depth):
        assert layer == depth - 1, "context-stream update between layers is not implemented"
        mod = _mod_vectors(cs, w_mod[layer], b_mod[layer])
        sh1, sc1, gt1, sh2, sc2, gt2 = [m[:b, None, :] for m in jnp.split(mod, 6, axis=-1)]
        csh1, csc1 = [jnp.broadcast_to(m[b][None, None, :], (b, 1, d)) for m in jnp.split(mod, 6, axis=-1)[:2]]
        dec = -jax.nn.softplus(ret_decay[layer].astype(F32))
        wl = w_in[layer]

        hc = _prenorm(xc, pre_norm1[layer], csh1, csc1).reshape(b * n_ctx, d)
        kc_a, vc_a, kc_r, vc_r = _kv_heads(hc, wl, q_width, k_norm[layer], rope_ctx)
        s0f, s0b = _ctx_states(dec, kc_r.reshape(b, n_ctx, RET_QK_W), vc_r.reshape(b, n_ctx, RET_V_W))

        h = _prenorm(x, pre_norm1[layer], sh1, sc1).reshape(b * t, d)
        q_a, q_r = _q_heads(h, wl, q_norm[layer], rope_lat)
        zg = _proj(h, wl, ATT_W + RET_QK_W, RET_V_W + 2 * d, out_dtype=F32, tn=PROJ_COLS)
        k_a, v_a, k_r, v_r = _kv_heads(h, wl, q_width, k_norm[layer], rope_lat)

        keys = jnp.concatenate([kc_a.reshape(b, n_ctx, ATT_KV_W), k_a.reshape(b, t, ATT_KV_W)], axis=1)
        vals = jnp.concatenate([vc_a.reshape(b, n_ctx, ATT_KV_W), v_a.reshape(b, t, ATT_KV_W)], axis=1)
        o_att = _attention(q_a.reshape(b, t, ATT_W), keys, vals)

        o_f, o_b = _retention(dec, q_r.reshape(b, t, RET_QK_W), k_r.reshape(b, t, RET_QK_W),
                              v_r.reshape(b, t, RET_V_W), s0f, s0b)

        y = _merge1(o_att.reshape(b * t, ATT_W), o_f.reshape(b * t, RET_V_W), o_b.reshape(b * t, RET_V_W),
                    zg, ret_gn[layer], w_o_att[layer].astype(BF16), w_o_ret[layer].astype(BF16))
        x1, h2, aff = _merge2(y, w_out[layer].astype(BF16), x, gt1, post_norm1[layer], pre_norm2[layer],
                              sh2, sc2, w_router[layer])

        pos, fill = _topk_slots(aff.transpose(0, 2, 1), cap)
        pos_t = pos.transpose(0, 2, 1)
        sparse, w0, row, base, win = _routing_tables(fill, cap)
        xg = lax.cond(sparse,
                      lambda: _gather_win(w0, pos, h2, cap),
                      lambda: _gather(pos, h2, cap))
        yg = _ffn(xg, w_gate[layer], w_up[layer], w_down[layer])
        x = lax.cond(sparse,
                     lambda: _combine_win(row, base, pos_t, aff, yg, x1, gt2, post_norm2[layer], win),
                     lambda: _combine(pos_t, aff, yg, x1, gt2, post_norm2[layer], cap))
    return x
```
